```python
import jax, jax.numpy as jnp
from jax import lax
import numpy as np

D_MODEL = 2048
BATCH = 8
SEQ = 2048
DEPTH = 4

N_META = 16
BLOCK = 128
EPS = 1e-6
NEG_INF = -1e30
ROPE_THETA = 10000.0
MLA_HEADS = 8
MLA_Q_RANK = 512
MLA_KV_RANK = 512
MLA_NOPE = 128
MLA_ROPE = 64
MLA_V = 128
CONV_WIDTH = 1024
CONV_K = 3
FOX_HEADS = 8
FOX_HEAD_DIM = 128
FORGET_BIAS_MEAN = 2.0
N_BRANCH = 3
BRANCH_WIDTH = 1024
D_FF = -(-8 * D_MODEL // (3 * 256)) * 256
FOX_WIDTH = FOX_HEADS * FOX_HEAD_DIM
IN_SPLITS = (MLA_Q_RANK, MLA_KV_RANK, MLA_ROPE,
             CONV_WIDTH, CONV_WIDTH, CONV_WIDTH,
             FOX_WIDTH, FOX_WIDTH, FOX_WIDTH, FOX_HEADS,
             N_BRANCH * D_MODEL)
D_IN = sum(IN_SPLITS)

kernel_name = "hybrid_mla_conv_fox_gated_block"


def _split_points():
    return [int(v) for v in np.cumsum(IN_SPLITS)[:-1]]


def rms_norm(x, g):
    xf = x.astype(jnp.float32)
    y = xf * lax.rsqrt(jnp.mean(xf * xf, axis=-1, keepdims=True) + EPS) * g.astype(jnp.float32)
    return y.astype(x.dtype)


def rope_tables(length):
    inv_freq = 1.0 / (ROPE_THETA ** (jnp.arange(0, MLA_ROPE, 2, dtype=jnp.float32) / MLA_ROPE))
    ang = jnp.arange(length, dtype=jnp.float32)[:, None] * inv_freq[None, :]
    return jnp.cos(ang)[:, None, :], jnp.sin(ang)[:, None, :]


def apply_rope(x, cos, sin):
    xf = x.astype(jnp.float32)
    x1, x2 = xf[..., : MLA_ROPE // 2], xf[..., MLA_ROPE // 2:]
    return jnp.concatenate([x1 * cos - x2 * sin, x1 * sin + x2 * cos], axis=-1).astype(x.dtype)


def to_heads(t, n_heads):
    b, l, _ = t.shape
    return t.reshape(b, l, n_heads, -1).transpose(0, 2, 1, 3)


def blocked_causal_attention(q, k, v, scale, decay=None):
    b, h, l, _ = q.shape
    pad = (-l) % BLOCK
    padw = ((0, 0), (0, 0), (pad, 0), (0, 0))
    qp, kp, vp = jnp.pad(q, padw), jnp.pad(k, padw), jnp.pad(v, padw)
    lp = l + pad
    nb = lp // BLOCK
    kpos = jnp.arange(lp)
    key_ok = kpos >= pad
    q_blocks = qp.reshape(b, h, nb, BLOCK, -1).transpose(2, 0, 1, 3, 4)
    dp = None if decay is None else jnp.pad(decay, ((0, 0), (0, 0), (pad, 0)))

    def attend(qi, i, di):
        s = jnp.einsum('bhqd,bhkd->bhqk', qi, kp, preferred_element_type=jnp.float32) * scale
        if di is not None:
            s = s + (di[..., :, None] - dp[:, :, None, :])
        qpos = i * BLOCK + jnp.arange(BLOCK)
        mask = (kpos[None, :] <= qpos[:, None]) & key_ok[None, :]
        p = jax.nn.softmax(jnp.where(mask, s, NEG_INF), axis=-1).astype(vp.dtype)
        return jnp.einsum('bhqk,bhkd->bhqd', p, vp)

    idx = jnp.arange(nb)
    if decay is None:
        out = lax.map(lambda a: attend(a[0], a[1], None), (q_blocks, idx))
    else:
        d_blocks = dp.reshape(b, h, nb, BLOCK).transpose(2, 0, 1, 3)
        out = lax.map(lambda a: attend(a[0], a[1], a[2]), (q_blocks, idx, d_blocks))
    out = out.transpose(1, 2, 0, 3, 4).reshape(b, h, lp, -1)
    return out[:, :, pad:]


def hybrid_mixer(h, w_in, b_forget, g_q_lat, g_kv_lat, w_uq, w_ukv, conv_w, w_branch, w_out, cos, sin):
    b, l, _ = h.shape
    proj = h @ w_in
    (c_q, c_kv, k_pe, conv_b, conv_c, conv_x,
     f_q, f_k, f_v, f_logit, gate_logit) = jnp.split(proj, _split_points(), axis=-1)

    q = (rms_norm(c_q, g_q_lat) @ w_uq).reshape(b, l, MLA_HEADS, MLA_NOPE + MLA_ROPE)
    q_nope, q_pe = q[..., :MLA_NOPE], apply_rope(q[..., MLA_NOPE:], cos, sin)
    kv = (rms_norm(c_kv, g_kv_lat) @ w_ukv).reshape(b, l, MLA_HEADS, MLA_NOPE + MLA_V)
    k_nope, v_a = kv[..., :MLA_NOPE], kv[..., MLA_NOPE:]
    k_pe = apply_rope(k_pe[:, :, None, :], cos, sin)
    q_a = jnp.concatenate([q_nope, q_pe], axis=-1)
    k_a = jnp.concatenate([k_nope, jnp.broadcast_to(k_pe, (b, l, MLA_HEADS, MLA_ROPE))], axis=-1)
    o_a = blocked_causal_attention(q_a.transpose(0, 2, 1, 3), k_a.transpose(0, 2, 1, 3),
                                   v_a.transpose(0, 2, 1, 3), (MLA_NOPE + MLA_ROPE) ** -0.5)
    o_a = o_a.transpose(0, 2, 1, 3).reshape(b, l, MLA_HEADS * MLA_V)

    u = conv_c * conv_x
    u = lax.conv_general_dilated(u, conv_w[:, None, :].astype(u.dtype), window_strides=(1,),
                                 padding=[(CONV_K - 1, 0)], dimension_numbers=('NWC', 'WIO', 'NWC'),
                                 feature_group_count=CONV_WIDTH)
    o_b = conv_b * u

    log_f = jax.nn.log_sigmoid(f_logit.astype(jnp.float32) + b_forget.astype(jnp.float32))
    c = jnp.cumsum(log_f, axis=1).transpose(0, 2, 1)
    o_c = blocked_causal_attention(to_heads(f_q, FOX_HEADS), to_heads(f_k, FOX_HEADS),
                                   to_heads(f_v, FOX_HEADS), FOX_HEAD_DIM ** -0.5, decay=c)
    o_c = o_c.transpose(0, 2, 1, 3).reshape(b, l, FOX_WIDTH)

    o = jnp.stack([o_a, o_b, o_c], axis=2)
    y = jnp.einsum('blnw,nwd->blnd', o, w_branch)
    gates = jax.nn.sigmoid(gate_logit.astype(jnp.float32)).astype(h.dtype).reshape(b, l, N_BRANCH, D_MODEL)
    merged = jnp.sum(gates * y, axis=2)
    return merged @ w_out


def swiglu(h, w_ffn_in, w_ffn_out):
    g, u = jnp.split(h @ w_ffn_in, 2, axis=-1)
    return (jax.nn.silu(g) * u) @ w_ffn_out


def _fwd_setup_inputs(seed: int = 0) -> dict:
    key = jax.random.key(seed)
    ks = jax.random.split(key, 18)
    f32 = jnp.float32

    def dense(k, shape, fan_in):
        return jax.random.normal(k, shape, f32) * fan_in ** -0.5

    def gain(k, shape):
        return 1.0 + 0.05 * jax.random.normal(k, shape, f32)

    return {
        "x": jax.random.normal(ks[0], (BATCH, SEQ, D_MODEL), f32),
        "meta": jax.random.normal(ks[1], (N_META, D_MODEL), f32),
        "w_in": dense(ks[2], (DEPTH, D_MODEL, D_IN), D_MODEL),
        "b_forget": FORGET_BIAS_MEAN + 0.1 * jax.random.normal(ks[3], (DEPTH, FOX_HEADS), f32),
        "g_q_lat": gain(ks[4], (DEPTH, MLA_Q_RANK)),
        "g_kv_lat": gain(ks[5], (DEPTH, MLA_KV_RANK)),
        "w_uq": dense(ks[6], (DEPTH, MLA_Q_RANK, MLA_HEADS * (MLA_NOPE + MLA_ROPE)), MLA_Q_RANK),
        "w_ukv": dense(ks[7], (DEPTH, MLA_KV_RANK, MLA_HEADS * (MLA_NOPE + MLA_V)), MLA_KV_RANK),
        "conv_w": dense(ks[8], (DEPTH, CONV_K, CONV_WIDTH), CONV_K),
        "w_branch": dense(ks[9], (DEPTH, N_BRANCH, BRANCH_WIDTH, D_MODEL), BRANCH_WIDTH),
        "w_out": dense(ks[10], (DEPTH, D_MODEL, D_MODEL), D_MODEL),
        "w_ffn_in": dense(ks[11], (DEPTH, D_MODEL, 2 * D_FF), D_MODEL),
        "w_ffn_out": dense(ks[12], (DEPTH, D_FF, D_MODEL), D_FF),
        "g_mix_pre": gain(ks[13], (DEPTH, D_MODEL)),
        "g_mix_post": gain(ks[14], (DEPTH, D_MODEL)),
        "g_ffn_pre": gain(ks[15], (DEPTH, D_MODEL)),
        "g_ffn_post": gain(ks[16], (DEPTH, D_MODEL)),
    }


def _fwd_reference(x, meta, w_in, b_forget, g_q_lat, g_kv_lat, w_uq, w_ukv, conv_w, w_branch, w_out,
              w_ffn_in, w_ffn_out, g_mix_pre, g_mix_post, g_ffn_pre, g_ffn_post):
    b, s, _ = x.shape
    length = N_META + s
    h = jnp.concatenate([jnp.broadcast_to(meta[None].astype(x.dtype), (b, N_META, D_MODEL)), x], axis=1)
    cos, sin = rope_tables(length)
    for layer in range(DEPTH):
        hn = rms_norm(h, g_mix_pre[layer])
        mix = hybrid_mixer(hn, w_in[layer], b_forget[layer], g_q_lat[layer], g_kv_lat[layer],
                           w_uq[layer], w_ukv[layer], conv_w[layer], w_branch[layer], w_out[layer], cos, sin)
        h = h + rms_norm(mix, g_mix_post[layer])
        hn = rms_norm(h, g_ffn_pre[layer])
        h = h + rms_norm(swiglu(hn, w_ffn_in[layer], w_ffn_out[layer]), g_ffn_post[layer])
    return h[:, N_META:]


import jax as _jax
import jax.numpy as _jnp

TWIN_FORMAT = 'train_step'
FWD_PARAMS = ['x', 'meta', 'w_in', 'b_forget', 'g_q_lat', 'g_kv_lat', 'w_uq', 'w_ukv', 'conv_w', 'w_branch', 'w_out', 'w_ffn_in', 'w_ffn_out', 'g_mix_pre', 'g_mix_post', 'g_ffn_pre', 'g_ffn_post']
TWIN_WEIGHTS = ['meta', 'w_in', 'b_forget', 'g_q_lat', 'g_kv_lat', 'w_uq', 'w_ukv', 'conv_w', 'w_branch', 'w_out', 'w_ffn_in', 'w_ffn_out', 'g_mix_pre', 'g_mix_post', 'g_ffn_pre', 'g_ffn_post']
TWIN_DIFF_INPUT = 'x'
TWIN_INPUTS = ['x', 'meta', 'w_in', 'b_forget', 'g_q_lat', 'g_kv_lat', 'w_uq', 'w_ukv', 'conv_w', 'w_branch', 'w_out', 'w_ffn_in', 'w_ffn_out', 'g_mix_pre', 'g_mix_post', 'g_ffn_pre', 'g_ffn_post', 'loss_target', 'm_meta', 'm_w_in', 'm_b_forget', 'm_g_q_lat', 'm_g_kv_lat', 'm_w_uq', 'm_w_ukv', 'm_conv_w', 'm_w_branch', 'm_w_out', 'm_w_ffn_in', 'm_w_ffn_out', 'm_g_mix_pre', 'm_g_mix_post', 'm_g_ffn_pre', 'm_g_ffn_post', 'v_meta', 'v_w_in', 'v_b_forget', 'v_g_q_lat', 'v_g_kv_lat', 'v_w_uq', 'v_w_ukv', 'v_conv_w', 'v_w_branch', 'v_w_out', 'v_w_ffn_in', 'v_w_ffn_out', 'v_g_mix_pre', 'v_g_mix_post', 'v_g_ffn_pre', 'v_g_ffn_post']
TWIN_OUTPUTS = ['loss', 'grad_x', 'grad_meta', 'grad_w_in', 'grad_b_forget', 'grad_g_q_lat', 'grad_g_kv_lat', 'grad_w_uq', 'grad_w_ukv', 'grad_conv_w', 'grad_w_branch', 'grad_w_out', 'grad_w_ffn_in', 'grad_w_ffn_out', 'grad_g_mix_pre', 'grad_g_mix_post', 'grad_g_ffn_pre', 'grad_g_ffn_post', 'delta_meta', 'delta_w_in', 'delta_b_forget', 'delta_g_q_lat', 'delta_g_kv_lat', 'delta_w_uq', 'delta_w_ukv', 'delta_conv_w', 'delta_w_branch', 'delta_w_out', 'delta_w_ffn_in', 'delta_w_ffn_out', 'delta_g_mix_pre', 'delta_g_mix_post', 'delta_g_ffn_pre', 'delta_g_ffn_post', 'new_m_meta', 'new_m_w_in', 'new_m_b_forget', 'new_m_g_q_lat', 'new_m_g_kv_lat', 'new_m_w_uq', 'new_m_w_ukv', 'new_m_conv_w', 'new_m_w_branch', 'new_m_w_out', 'new_m_w_ffn_in', 'new_m_w_ffn_out', 'new_m_g_mix_pre', 'new_m_g_mix_post', 'new_m_g_ffn_pre', 'new_m_g_ffn_post', 'new_v_meta', 'new_v_w_in', 'new_v_b_forget', 'new_v_g_q_lat', 'new_v_g_kv_lat', 'new_v_w_uq', 'new_v_w_ukv', 'new_v_conv_w', 'new_v_w_branch', 'new_v_w_out', 'new_v_w_ffn_in', 'new_v_w_ffn_out', 'new_v_g_mix_pre', 'new_v_g_mix_post', 'new_v_g_ffn_pre', 'new_v_g_ffn_post']
TWIN_LEAF_KINDS = {'loss': 'loss', 'grad_x': 'grad_x', 'grad_meta': 'grad_w', 'grad_w_in': 'grad_w', 'grad_b_forget': 'grad_w', 'grad_g_q_lat': 'grad_w', 'grad_g_kv_lat': 'grad_w', 'grad_w_uq': 'grad_w', 'grad_w_ukv': 'grad_w', 'grad_conv_w': 'grad_w', 'grad_w_branch': 'grad_w', 'grad_w_out': 'grad_w', 'grad_w_ffn_in': 'grad_w', 'grad_w_ffn_out': 'grad_w', 'grad_g_mix_pre': 'grad_w', 'grad_g_mix_post': 'grad_w', 'grad_g_ffn_pre': 'grad_w', 'grad_g_ffn_post': 'grad_w', 'delta_meta': 'delta_w', 'delta_w_in': 'delta_w', 'delta_b_forget': 'delta_w', 'delta_g_q_lat': 'delta_w', 'delta_g_kv_lat': 'delta_w', 'delta_w_uq': 'delta_w', 'delta_w_ukv': 'delta_w', 'delta_conv_w': 'delta_w', 'delta_w_branch': 'delta_w', 'delta_w_out': 'delta_w', 'delta_w_ffn_in': 'delta_w', 'delta_w_ffn_out': 'delta_w', 'delta_g_mix_pre': 'delta_w', 'delta_g_mix_post': 'delta_w', 'delta_g_ffn_pre': 'delta_w', 'delta_g_ffn_post': 'delta_w', 'new_m_meta': 'new_m', 'new_m_w_in': 'new_m', 'new_m_b_forget': 'new_m', 'new_m_g_q_lat': 'new_m', 'new_m_g_kv_lat': 'new_m', 'new_m_w_uq': 'new_m', 'new_m_w_ukv': 'new_m', 'new_m_conv_w': 'new_m', 'new_m_w_branch': 'new_m', 'new_m_w_out': 'new_m', 'new_m_w_ffn_in': 'new_m', 'new_m_w_ffn_out': 'new_m', 'new_m_g_mix_pre': 'new_m', 'new_m_g_mix_post': 'new_m', 'new_m_g_ffn_pre': 'new_m', 'new_m_g_ffn_post': 'new_m', 'new_v_meta': 'new_v', 'new_v_w_in': 'new_v', 'new_v_b_forget': 'new_v', 'new_v_g_q_lat': 'new_v', 'new_v_g_kv_lat': 'new_v', 'new_v_w_uq': 'new_v', 'new_v_w_ukv': 'new_v', 'new_v_conv_w': 'new_v', 'new_v_w_branch': 'new_v', 'new_v_w_out': 'new_v', 'new_v_w_ffn_in': 'new_v', 'new_v_w_ffn_out': 'new_v', 'new_v_g_mix_pre': 'new_v', 'new_v_g_mix_post': 'new_v', 'new_v_g_ffn_pre': 'new_v', 'new_v_g_ffn_post': 'new_v'}


def _forward(args):
    return _fwd_reference(*[args[k] for k in FWD_PARAMS])


def _output_shape():
    out = _jax.eval_shape(lambda: _forward(_fwd_setup_inputs(0)))
    return out.shape, out.dtype

N_MICROBATCH = 1
ADAM_LR = 0.001
ADAM_B1 = 0.9
ADAM_B2 = 0.999
ADAM_EPS = 1e-08
ADAM_WD = 0.01
ADAM_STEP = 10
PER_EXAMPLE_BATCH_AXIS = {'x': 0, 'loss_target': 0}
SHARED_INPUTS = []
_WEIGHT_DTYPES = {'meta': _jnp.float32, 'w_in': _jnp.float32, 'b_forget': _jnp.float32, 'g_q_lat': _jnp.float32, 'g_kv_lat': _jnp.float32, 'w_uq': _jnp.float32, 'w_ukv': _jnp.float32, 'conv_w': _jnp.float32, 'w_branch': _jnp.float32, 'w_out': _jnp.float32, 'w_ffn_in': _jnp.float32, 'w_ffn_out': _jnp.float32, 'g_mix_pre': _jnp.float32, 'g_mix_post': _jnp.float32, 'g_ffn_pre': _jnp.float32, 'g_ffn_post': _jnp.float32}
MOMENT_SCALE = {'meta': 5.849260e-02, 'w_in': 3.749756e-01, 'b_forget': 2.710178e+00, 'g_q_lat': 1.674747e-01, 'g_kv_lat': 2.510821e-01, 'w_uq': 9.339864e-02, 'w_ukv': 1.231087e-01, 'conv_w': 7.442504e-01, 'w_branch': 3.203442e-01, 'w_out': 5.583004e-01, 'w_ffn_in': 2.545131e-01, 'w_ffn_out': 4.226912e-01, 'g_mix_pre': 9.741247e-01, 'g_mix_post': 7.892376e+00, 'g_ffn_pre': 5.980209e-01, 'g_ffn_post': 7.931708e+00}


def _to_microbatches(a, axis):
    t = _jnp.moveaxis(a, axis, 0)
    t = t.reshape((N_MICROBATCH, t.shape[0] // N_MICROBATCH) + t.shape[1:])
    return _jnp.moveaxis(t, 1, axis + 1)


def setup_inputs(seed: int = 0) -> dict:
    inp = _fwd_setup_inputs(seed)
    key = _jax.random.fold_in(_jax.random.key(seed), 7919)
    shape, _ = _output_shape()
    out = dict(inp)
    out["loss_target"] = _jax.random.normal(_jax.random.fold_in(key, 0), shape, _jnp.float32)
    for i, name in enumerate(TWIN_WEIGHTS):
        w = inp[name].astype(_jnp.float32)
        if MOMENT_SCALE is None:
            s = _jnp.sqrt(_jnp.mean(_jnp.square(w)) + 1e-30)
        else:
            s = MOMENT_SCALE[name]
        km, kv = _jax.random.split(_jax.random.fold_in(key, i + 1))
        out[name] = w
        out["m_" + name] = s * _jax.random.normal(km, w.shape, _jnp.float32)
        out["v_" + name] = (s * s) * _jax.random.uniform(kv, w.shape, _jnp.float32, 0.5, 1.5)
    if N_MICROBATCH > 1:
        for name, axis in PER_EXAMPLE_BATCH_AXIS.items():
            out[name] = _to_microbatches(out[name], axis)
    return {'x': out['x'], 'meta': out['meta'], 'w_in': out['w_in'], 'b_forget': out['b_forget'], 'g_q_lat': out['g_q_lat'], 'g_kv_lat': out['g_kv_lat'], 'w_uq': out['w_uq'], 'w_ukv': out['w_ukv'], 'conv_w': out['conv_w'], 'w_branch': out['w_branch'], 'w_out': out['w_out'], 'w_ffn_in': out['w_ffn_in'], 'w_ffn_out': out['w_ffn_out'], 'g_mix_pre': out['g_mix_pre'], 'g_mix_post': out['g_mix_post'], 'g_ffn_pre': out['g_ffn_pre'], 'g_ffn_post': out['g_ffn_post'], 'loss_target': out['loss_target'], 'm_meta': out['m_meta'], 'm_w_in': out['m_w_in'], 'm_b_forget': out['m_b_forget'], 'm_g_q_lat': out['m_g_q_lat'], 'm_g_kv_lat': out['m_g_kv_lat'], 'm_w_uq': out['m_w_uq'], 'm_w_ukv': out['m_w_ukv'], 'm_conv_w': out['m_conv_w'], 'm_w_branch': out['m_w_branch'], 'm_w_out': out['m_w_out'], 'm_w_ffn_in': out['m_w_ffn_in'], 'm_w_ffn_out': out['m_w_ffn_out'], 'm_g_mix_pre': out['m_g_mix_pre'], 'm_g_mix_post': out['m_g_mix_post'], 'm_g_ffn_pre': out['m_g_ffn_pre'], 'm_g_ffn_post': out['m_g_ffn_post'], 'v_meta': out['v_meta'], 'v_w_in': out['v_w_in'], 'v_b_forget': out['v_b_forget'], 'v_g_q_lat': out['v_g_q_lat'], 'v_g_kv_lat': out['v_g_kv_lat'], 'v_w_uq': out['v_w_uq'], 'v_w_ukv': out['v_w_ukv'], 'v_conv_w': out['v_conv_w'], 'v_w_branch': out['v_w_branch'], 'v_w_out': out['v_w_out'], 'v_w_ffn_in': out['v_w_ffn_in'], 'v_w_ffn_out': out['v_w_ffn_out'], 'v_g_mix_pre': out['v_g_mix_pre'], 'v_g_mix_post': out['v_g_mix_post'], 'v_g_ffn_pre': out['v_g_ffn_pre'], 'v_g_ffn_post': out['v_g_ffn_post']}


def _loss(weights, diff, rest, loss_target):
    with _jax.named_scope("forward"):
        args = {**rest, TWIN_DIFF_INPUT: diff, **{k: w.astype(_WEIGHT_DTYPES[k]) for k, w in weights.items()}}
        y = _forward(args)
    with _jax.named_scope("loss_head"):
        err = _jnp.square(y.astype(_jnp.float32) - loss_target)
        return 0.5 * _jnp.sum(_jnp.mean(err, axis=-1)) if err.ndim else 0.5 * err


def _adamw(w, g, m, v):
    m = ADAM_B1 * m + (1.0 - ADAM_B1) * g
    v = ADAM_B2 * v + (1.0 - ADAM_B2) * _jnp.square(g)
    m_hat = m / (1.0 - ADAM_B1 ** ADAM_STEP)
    v_hat = v / (1.0 - ADAM_B2 ** ADAM_STEP)
    delta = -ADAM_LR * (m_hat / (_jnp.sqrt(v_hat) + ADAM_EPS) + ADAM_WD * w)
    return delta, m, v


def reference(x, meta, w_in, b_forget, g_q_lat, g_kv_lat, w_uq, w_ukv, conv_w, w_branch, w_out, w_ffn_in, w_ffn_out, g_mix_pre, g_mix_post, g_ffn_pre, g_ffn_post, loss_target, m_meta, m_w_in, m_b_forget, m_g_q_lat, m_g_kv_lat, m_w_uq, m_w_ukv, m_conv_w, m_w_branch, m_w_out, m_w_ffn_in, m_w_ffn_out, m_g_mix_pre, m_g_mix_post, m_g_ffn_pre, m_g_ffn_post, v_meta, v_w_in, v_b_forget, v_g_q_lat, v_g_kv_lat, v_w_uq, v_w_ukv, v_conv_w, v_w_branch, v_w_out, v_w_ffn_in, v_w_ffn_out, v_g_mix_pre, v_g_mix_post, v_g_ffn_pre, v_g_ffn_post):
    given = dict(x=x, meta=meta, w_in=w_in, b_forget=b_forget, g_q_lat=g_q_lat, g_kv_lat=g_kv_lat, w_uq=w_uq, w_ukv=w_ukv, conv_w=conv_w, w_branch=w_branch, w_out=w_out, w_ffn_in=w_ffn_in, w_ffn_out=w_ffn_out, g_mix_pre=g_mix_pre, g_mix_post=g_mix_post, g_ffn_pre=g_ffn_pre, g_ffn_post=g_ffn_post, loss_target=loss_target, m_meta=m_meta, m_w_in=m_w_in, m_b_forget=m_b_forget, m_g_q_lat=m_g_q_lat, m_g_kv_lat=m_g_kv_lat, m_w_uq=m_w_uq, m_w_ukv=m_w_ukv, m_conv_w=m_conv_w, m_w_branch=m_w_branch, m_w_out=m_w_out, m_w_ffn_in=m_w_ffn_in, m_w_ffn_out=m_w_ffn_out, m_g_mix_pre=m_g_mix_pre, m_g_mix_post=m_g_mix_post, m_g_ffn_pre=m_g_ffn_pre, m_g_ffn_post=m_g_ffn_post, v_meta=v_meta, v_w_in=v_w_in, v_b_forget=v_b_forget, v_g_q_lat=v_g_q_lat, v_g_kv_lat=v_g_kv_lat, v_w_uq=v_w_uq, v_w_ukv=v_w_ukv, v_conv_w=v_conv_w, v_w_branch=v_w_branch, v_w_out=v_w_out, v_w_ffn_in=v_w_ffn_in, v_w_ffn_out=v_w_ffn_out, v_g_mix_pre=v_g_mix_pre, v_g_mix_post=v_g_mix_post, v_g_ffn_pre=v_g_ffn_pre, v_g_ffn_post=v_g_ffn_post)
    weights = {n: given[n] for n in TWIN_WEIGHTS}
    shared = {n: given[n] for n in SHARED_INPUTS}
    per_example = {n: given[n] for n in ['x']}
    grad_fn = _jax.value_and_grad(_loss, argnums=(0, 1))

    def one_microbatch(ex, loss_target):
        ex = dict(ex)
        diff = ex.pop(TWIN_DIFF_INPUT)
        return grad_fn(weights, diff, {**shared, **ex}, loss_target)

    if N_MICROBATCH == 1:
        loss, (grad_w, grad_x) = one_microbatch(per_example, given["loss_target"])
    else:
        def body(carry, xs):
            loss_sum, grad_sum = carry
            l_k, (gw_k, gx_k) = one_microbatch(xs[0], xs[1])
            with _jax.named_scope("update"):
                return (loss_sum + l_k, _jax.tree.map(_jnp.add, grad_sum, gw_k)), gx_k

        init = (_jnp.zeros((), _jnp.float32), _jax.tree.map(_jnp.zeros_like, weights))
        (loss, grad_w), grad_x = _jax.lax.scan(body, init, (per_example, given["loss_target"]))
    with _jax.named_scope("update"):
        delta_w, new_m, new_v = {}, {}, {}
        for n in TWIN_WEIGHTS:
            delta_w[n], new_m[n], new_v[n] = _adamw(weights[n], grad_w[n], given["m_" + n], given["v_" + n])
    return (loss, grad_x, *[grad_w[n] for n in TWIN_WEIGHTS], *[delta_w[n] for n in TWIN_WEIGHTS],
            *[new_m[n] for n in TWIN_WEIGHTS], *[new_v[n] for n in TWIN_WEIGHTS])
```

```python
import functools
from typing import NamedTuple

import jax
import jax.numpy as jnp
from jax import lax
from jax.experimental import pallas as pl
from jax.experimental.pallas import tpu as pltpu

F32 = jnp.float32
BF16 = jnp.bfloat16
MESH = pl.DeviceIdType.MESH

EPS = 1e-6
NEG_INF = -1e30
ROPE_THETA = 10000.0
LANES = 128
ROPE = 64
N_CHIPS = 4
N_DEV = 8

ADAM_LR = 0.001
ADAM_B1 = 0.9
ADAM_B2 = 0.999
ADAM_EPS = 1e-08
ADAM_WD = 0.01
ADAM_STEP = 10

VMEM_LIMIT_BYTES = 48 * 1024 * 1024


class Cfg(NamedTuple):
    d: int = 2048
    seq: int = 2048
    depth: int = 4
    n_meta: int = 16
    heads: int = 8
    q_rank: int = 512
    kv_rank: int = 512
    d_ff: int = 5632

    @property
    def width(self):
        return self.heads * LANES

    @property
    def pad(self):
        return (-(self.n_meta + self.seq)) % LANES

    @property
    def m(self):
        return self.pad + self.n_meta + self.seq

    @property
    def nat_splits(self):
        w = self.width
        return (self.q_rank, self.kv_rank, ROPE, w, w, w, w, w, w, self.heads, 3 * self.d)

    @property
    def d_in(self):
        return sum(self.nat_splits)

    @property
    def off_cq(self):
        return 3 * self.d

    @property
    def off_ckv(self):
        return self.off_cq + self.q_rank

    @property
    def off_conv(self):
        return self.off_ckv + self.kv_rank

    @property
    def off_fox(self):
        return self.off_conv + 3 * self.width

    @property
    def off_kpe(self):
        return self.off_fox + 3 * self.width

    @property
    def off_fl(self):
        return self.off_kpe + LANES

    @property
    def d_inp(self):
        return self.off_fl + LANES


CFG = Cfg()


def _tile(n, target, mult=LANES):
    best = None
    t = mult
    while t <= min(n, target):
        if n % t == 0:
            best = t
        t += mult
    return best or n


def _cparams(*sem):
    return pltpu.CompilerParams(dimension_semantics=sem, vmem_limit_bytes=VMEM_LIMIT_BYTES)


def pack_w_in(cfg, w):
    cq, ckv, kpe, cb, cc, cx, fq, fk, fv, fl, gate = jnp.split(w, list(_cumsum(cfg.nat_splits))[:-1], axis=1)
    z = lambda n: jnp.zeros((w.shape[0], n), w.dtype)
    return jnp.concatenate([gate, cq, ckv, cb, cc, cx, fq, fk, fv, kpe, z(LANES - ROPE), fl, z(LANES - cfg.heads)], axis=1)


def unpack_w_in(cfg, wp):
    w = cfg.width
    sizes = (3 * cfg.d, cfg.q_rank, cfg.kv_rank, w, w, w, w, w, w, ROPE, LANES - ROPE, cfg.heads, LANES - cfg.heads)
    gate, cq, ckv, cb, cc, cx, fq, fk, fv, kpe, _, fl, _ = jnp.split(wp, list(_cumsum(sizes))[:-1], axis=1)
    return jnp.concatenate([cq, ckv, kpe, cb, cc, cx, fq, fk, fv, fl, gate], axis=1)


def _cumsum(xs):
    out, s = [], 0
    for v in xs:
        s += v
        out.append(s)
    return out


def pack_w_uq(cfg, w):
    r = w.shape[0]
    w3 = w.reshape(r, cfg.heads, LANES + ROPE)
    w3 = jnp.pad(w3, ((0, 0), (0, 0), (0, LANES - ROPE)))
    return w3.reshape(r, cfg.heads * 2 * LANES)


def unpack_w_uq(cfg, wp):
    r = wp.shape[0]
    return wp.reshape(r, cfg.heads, 2 * LANES)[:, :, : LANES + ROPE].reshape(r, cfg.heads * (LANES + ROPE))


def pack_w_ukv(cfg, w):
    r = w.shape[0]
    w4 = w.reshape(r, cfg.heads, 2, LANES)
    return jnp.transpose(w4, (0, 2, 1, 3)).reshape(r, 2 * cfg.heads * LANES)


def unpack_w_ukv(cfg, wp):
    r = wp.shape[0]
    w4 = wp.reshape(r, 2, cfg.heads, LANES)
    return jnp.transpose(w4, (0, 2, 1, 3)).reshape(r, 2 * cfg.heads * LANES)


_DIMS = {
    "nn": (((1,), (0,)), ((), ())),
    "nt": (((1,), (1,)), ((), ())),
    "tn": (((0,), (0,)), ((), ())),
}


def matmul(a, b, mode, out_dtype, *, tm, tn, tk, name):
    batched = a.ndim == 3
    if mode == "nn":
        (m, kc), n = a.shape[-2:], b.shape[-1]
        a_blk, a_idx = (tm, tk), lambda i, j, k: (i, k)
        b_blk, b_idx = (tk, tn), lambda i, j, k: (k, j)
    elif mode == "nt":
        (m, kc), n = a.shape[-2:], b.shape[-2]
        a_blk, a_idx = (tm, tk), lambda i, j, k: (i, k)
        b_blk, b_idx = (tn, tk), lambda i, j, k: (j, k)
    else:
        (kc, m), n = a.shape[-2:], b.shape[-1]
        a_blk, a_idx = (tk, tm), lambda i, j, k: (k, i)
        b_blk, b_idx = (tk, tn), lambda i, j, k: (k, j)
    assert m % tm == 0 and n % tn == 0 and kc % tk == 0, (name, m, n, kc, tm, tn, tk)
    nk = kc // tk
    dims = _DIMS[mode]
    o_blk, o_idx = (tm, tn), lambda i, j, k: (i, j)
    grid = (m // tm, n // tn, nk)
    if batched:
        nb = a.shape[0]
        grid = (nb,) + grid
        wrap = lambda f: (lambda bb, i, j, k: (bb,) + f(i, j, k))
        a_blk, b_blk, o_blk = (None,) + a_blk, (None,) + b_blk, (None,) + o_blk
        a_idx, b_idx, o_idx = wrap(a_idx), wrap(b_idx), wrap(o_idx)
        out_shape = (nb, m, n)
        sem = ("parallel", "parallel", "parallel", "arbitrary")
    else:
        out_shape = (m, n)
        sem = ("parallel", "parallel", "arbitrary")
    k_axis = len(grid) - 1

    def body(a_ref, b_ref, o_ref, *scratch):
        prod = lax.dot_general(a_ref[...], b_ref[...], dims, preferred_element_type=F32)
        if nk == 1:
            o_ref[...] = prod.astype(o_ref.dtype)
        else:
            acc_ref = scratch[0]
            k = pl.program_id(k_axis)

            @pl.when(k == 0)
            def _():
                acc_ref[...] = prod

            @pl.when(k > 0)
            def _():
                acc_ref[...] += prod

            @pl.when(k == nk - 1)
            def _():
                o_ref[...] = acc_ref[...].astype(o_ref.dtype)

    return pl.pallas_call(
        body,
        name=name,
        out_shape=jax.ShapeDtypeStruct(out_shape, out_dtype),
        grid=grid,
        in_specs=[pl.BlockSpec(a_blk, a_idx), pl.BlockSpec(b_blk, b_idx)],
        out_specs=pl.BlockSpec(o_blk, o_idx),
        scratch_shapes=[] if nk == 1 else [pltpu.VMEM((tm, tn), F32)],
        compiler_params=_cparams(*sem),
    )(a, b)


def _row_tile(m):
    return _tile(m, 272, 16)


def rmsnorm_fwd(x, g, out_dtype, *, name, width=None, col_blk=0, res=None):
    m = x.shape[0]
    n = width or x.shape[1]
    tm = _row_tile(m)
    has_res = res is not None

    def body(x_ref, g_ref, *rest):
        o_ref = rest[-1]
        xf = x_ref[...].astype(F32)
        r = lax.rsqrt(jnp.mean(xf * xf, axis=-1, keepdims=True) + EPS)
        y = xf * r * g_ref[...]
        if has_res:
            y = rest[0][...] + y
        o_ref[...] = y.astype(o_ref.dtype)

    in_specs = [pl.BlockSpec((tm, n), lambda i: (i, col_blk)), pl.BlockSpec((1, n), lambda i: (0, 0))]
    args = [x, g.reshape(1, n)]
    if has_res:
        in_specs.append(pl.BlockSpec((tm, n), lambda i: (i, 0)))
        args.append(res)
    return pl.pallas_call(
        body,
        name=name,
        out_shape=jax.ShapeDtypeStruct((m, n), out_dtype),
        grid=(m // tm,),
        in_specs=in_specs,
        out_specs=pl.BlockSpec((tm, n), lambda i: (i, 0)),
        compiler_params=_cparams("parallel"),
    )(*args)


def rmsnorm_bwd(x, g, dy, out_dtype, *, name, width=None, col_blk=0, dres=None):
    m = x.shape[0]
    n = width or x.shape[1]
    tm = _row_tile(m)
    has_res = dres is not None

    def body(x_ref, g_ref, dy_ref, *rest):
        dx_ref, dg_ref = rest[-2:]
        i = pl.program_id(0)
        xf = x_ref[...].astype(F32)
        r = lax.rsqrt(jnp.mean(xf * xf, axis=-1, keepdims=True) + EPS)
        xhat = xf * r
        dyf = dy_ref[...].astype(F32)
        dxh = dyf * g_ref[...]
        dx = r * (dxh - xhat * jnp.mean(dxh * xhat, axis=-1, keepdims=True))
        if has_res:
            dx = dx + rest[0][...]
        dx_ref[...] = dx.astype(dx_ref.dtype)
        part = jnp.sum(dyf * xhat, axis=0, keepdims=True)

        @pl.when(i == 0)
        def _():
            dg_ref[...] = part

        @pl.when(i > 0)
        def _():
            dg_ref[...] += part

    in_specs = [
        pl.BlockSpec((tm, n), lambda i: (i, col_blk)),
        pl.BlockSpec((1, n), lambda i: (0, 0)),
        pl.BlockSpec((tm, n), lambda i: (i, 0)),
    ]
    args = [x, g.reshape(1, n), dy]
    if has_res:
        in_specs.append(pl.BlockSpec((tm, n), lambda i: (i, 0)))
        args.append(dres)
    return pl.pallas_call(
        body,
        name=name,
        out_shape=(jax.ShapeDtypeStruct((m, n), out_dtype), jax.ShapeDtypeStruct((1, n), F32)),
        grid=(m // tm,),
        in_specs=in_specs,
        out_specs=(pl.BlockSpec((tm, n), lambda i: (i, 0)), pl.BlockSpec((1, n), lambda i: (0, 0))),
        compiler_params=_cparams("arbitrary"),
    )(*args)


_NT = (((1,), (1,)), ((), ()))
_NN = (((1,), (0,)), ((), ()))
_TN = (((0,), (0,)), ((), ()))


def _attn_scores(q, k, scale, decay_refs, i, tq, m, pad):
    s = lax.dot_general(q, k, _NT, preferred_element_type=F32) * scale
    if decay_refs is not None:
        cq_ref, ck_ref = decay_refs
        s = s + (cq_ref[0] - ck_ref[0])
    t_idx = i * tq + lax.broadcasted_iota(jnp.int32, (tq, 1), 0)
    s_idx = lax.broadcasted_iota(jnp.int32, (1, m), 1)
    mask = (s_idx <= t_idx) & (s_idx >= pad)
    return s, mask, t_idx


def attn_fwd(q, k, v, *, heads, dk, dv, qblk0, kblk0, vblk0, scale, pad, decay=None, name):
    m = q.shape[0]
    tq = _row_tile(m)
    has_decay = decay is not None

    def body(q_ref, k_ref, v_ref, *rest):
        o_ref, lse_ref = rest[-2:]
        i = pl.program_id(1)
        s, mask, t_idx = _attn_scores(q_ref[...], k_ref[...], scale, rest[:2] if has_decay else None, i, tq, m, pad)
        s = jnp.where(mask, s, NEG_INF)
        mx = jnp.max(s, axis=1, keepdims=True)
        p = jnp.exp(s - mx)
        l = jnp.sum(p, axis=1, keepdims=True)
        o = lax.dot_general(p.astype(BF16), v_ref[...], _NN, preferred_element_type=F32) / l
        o_ref[...] = jnp.where(t_idx >= pad, o, 0.0).astype(o_ref.dtype)
        lse_ref[0] = mx + jnp.log(l)

    in_specs = [
        pl.BlockSpec((tq, dk), lambda h, i: (i, qblk0 + h)),
        pl.BlockSpec((m, dk), lambda h, i: (0, kblk0 + h)),
        pl.BlockSpec((m, dv), lambda h, i: (0, vblk0 + h)),
    ]
    args = [q, k, v]
    if has_decay:
        in_specs += [pl.BlockSpec((1, tq, 1), lambda h, i: (h, i, 0)), pl.BlockSpec((1, 1, m), lambda h, i: (h, 0, 0))]
        args += list(decay)
    return pl.pallas_call(
        body,
        name=name,
        out_shape=(jax.ShapeDtypeStruct((m, heads * dv), BF16), jax.ShapeDtypeStruct((heads, m, 1), F32)),
        grid=(heads, m // tq),
        in_specs=in_specs,
        out_specs=(pl.BlockSpec((tq, dv), lambda h, i: (i, h)), pl.BlockSpec((1, tq, 1), lambda h, i: (h, i, 0))),
        compiler_params=_cparams("parallel", "parallel"),
    )(*args)


def attn_bwd(q, k, v, do, do_sel, lse, *, heads, dk, dv, qblk0, kblk0, vblk0, scale, pad, decay=None, name):
    m = q.shape[0]
    tq = _row_tile(m)
    nq = m // tq
    has_decay = decay is not None

    def body(q_ref, k_ref, v_ref, do_ref, lse_ref, *rest):
        if has_decay:
            cq_ref, ck_ref, dq_ref, dk_ref, dv_ref, dck_ref, dk_acc, dv_acc = rest
            decay_refs = (cq_ref, ck_ref)
        else:
            dq_ref, dk_ref, dv_ref, dk_acc, dv_acc = rest
            decay_refs = None
        i = pl.program_id(1)
        qb, kb, dob = q_ref[...], k_ref[...], do_ref[...]
        s, mask, _ = _attn_scores(qb, kb, scale, decay_refs, i, tq, m, pad)
        p = jnp.where(mask, jnp.exp(s - lse_ref[0]), 0.0)
        dp = lax.dot_general(dob, v_ref[...], _NT, preferred_element_type=F32)
        ds = p * (dp - jnp.sum(p * dp, axis=1, keepdims=True))
        dsb = ds.astype(BF16)
        dq_ref[...] = (lax.dot_general(dsb, kb, _NN, preferred_element_type=F32) * scale).astype(dq_ref.dtype)
        dk_part = lax.dot_general(dsb, qb, _TN, preferred_element_type=F32) * scale
        dv_part = lax.dot_general(p.astype(BF16), dob, _TN, preferred_element_type=F32)
        if has_decay:
            dck_part = -jnp.sum(ds, axis=0, keepdims=True)

        @pl.when(i == 0)
        def _():
            dk_acc[...] = dk_part
            dv_acc[...] = dv_part
            if has_decay:
                dck_ref[0] = dck_part

        @pl.when(i > 0)
        def _():
            dk_acc[...] += dk_part
            dv_acc[...] += dv_part
            if has_decay:
                dck_ref[0] += dck_part

        @pl.when(i == nq - 1)
        def _():
            dk_ref[...] = dk_acc[...].astype(dk_ref.dtype)
            dv_ref[...] = dv_acc[...].astype(dv_ref.dtype)

    in_specs = [
        pl.BlockSpec((tq, dk), lambda h, i: (i, qblk0 + h)),
        pl.BlockSpec((m, dk), lambda h, i: (0, kblk0 + h)),
        pl.BlockSpec((m, dv), lambda h, i: (0, vblk0 + h)),
        pl.BlockSpec((None, tq, dv), lambda h, i: (do_sel, i, h)),
        pl.BlockSpec((1, tq, 1), lambda h, i: (h, i, 0)),
    ]
    args = [q, k, v, do, lse]
    out_shape = [
        jax.ShapeDtypeStruct((m, heads * dk), BF16),
        jax.ShapeDtypeStruct((m, heads * dk), BF16),
        jax.ShapeDtypeStruct((m, heads * dv), BF16),
    ]
    out_specs = [
        pl.BlockSpec((tq, dk), lambda h, i: (i, h)),
        pl.BlockSpec((m, dk), lambda h, i: (0, h)),
        pl.BlockSpec((m, dv), lambda h, i: (0, h)),
    ]
    if has_decay:
        in_specs += [pl.BlockSpec((1, tq, 1), lambda h, i: (h, i, 0)), pl.BlockSpec((1, 1, m), lambda h, i: (h, 0, 0))]
        args += list(decay)
        out_shape.append(jax.ShapeDtypeStruct((heads, 1, m), F32))
        out_specs.append(pl.BlockSpec((1, 1, m), lambda h, i: (h, 0, 0)))
    return pl.pallas_call(
        body,
        name=name,
        out_shape=tuple(out_shape),
        grid=(heads, nq),
        in_specs=in_specs,
        out_specs=tuple(out_specs),
        scratch_shapes=[pltpu.VMEM((m, dk), F32), pltpu.VMEM((m, dv), F32)],
        compiler_params=_cparams("parallel", "arbitrary"),
    )(*args)


def rope_tables(cfg):
    half = ROPE // 2
    inv_freq = 1.0 / (ROPE_THETA ** (jnp.arange(0, ROPE, 2, dtype=F32) / ROPE))
    pos = (jnp.arange(cfg.m, dtype=jnp.int32) - cfg.pad).astype(F32)
    ang = pos[:, None] * inv_freq[None, :]
    cos, sin = jnp.cos(ang), jnp.sin(ang)
    z = jnp.zeros((cfg.m, half), F32)
    zz = jnp.zeros((cfg.m, LANES - ROPE), F32)
    return (
        jnp.concatenate([cos, cos, zz], axis=1),
        jnp.concatenate([-sin, z, zz], axis=1),
        jnp.concatenate([z, sin, zz], axis=1),
    )


def _rope(x, cos, s1, s2):
    return x * cos + pltpu.roll(x, LANES - ROPE // 2, 1) * s1 + pltpu.roll(x, ROPE // 2, 1) * s2


def mla_prep_fwd(cfg, q, kv, proj, tabs, *, name):
    m, h2 = cfg.m, 2 * LANES
    tm = _tile(m, 544, 16)
    kpe_blk = cfg.off_kpe // LANES

    def body(q_ref, kn_ref, kpe_ref, cos_ref, s1_ref, s2_ref, qf_ref, kf_ref):
        cos, s1, s2 = cos_ref[...], s1_ref[...], s2_ref[...]
        qv = q_ref[...]
        qf_ref[:, :LANES] = qv[:, :LANES]
        qf_ref[:, LANES:] = _rope(qv[:, LANES:].astype(F32), cos, s1, s2).astype(qf_ref.dtype)
        kf_ref[:, :LANES] = kn_ref[...]
        kf_ref[:, LANES:] = _rope(kpe_ref[...].astype(F32), cos, s1, s2).astype(kf_ref.dtype)

    tab = pl.BlockSpec((tm, LANES), lambda i, h: (i, 0))
    return pl.pallas_call(
        body,
        name=name,
        out_shape=(jax.ShapeDtypeStruct((m, cfg.heads * h2), BF16), jax.ShapeDtypeStruct((m, cfg.heads * h2), BF16)),
        grid=(m // tm, cfg.heads),
        in_specs=[
            pl.BlockSpec((tm, h2), lambda i, h: (i, h)),
            pl.BlockSpec((tm, LANES), lambda i, h: (i, h)),
            pl.BlockSpec((tm, LANES), lambda i, h: (i, kpe_blk)),
            tab, tab, tab,
        ],
        out_specs=(pl.BlockSpec((tm, h2), lambda i, h: (i, h)), pl.BlockSpec((tm, h2), lambda i, h: (i, h))),
        compiler_params=_cparams("parallel", "parallel"),
    )(q, kv, proj, *tabs)


def mla_prep_bwd(cfg, dqf, dkf, tabs_t, *, name):
    m, h2 = cfg.m, 2 * LANES
    tm = _tile(m, 544, 16)

    def body(dqf_ref, dkf_ref, cos_ref, s1_ref, s2_ref, dq_ref, dkn_ref, dkpe_ref):
        h = pl.program_id(1)
        cos, s1, s2 = cos_ref[...], s1_ref[...], s2_ref[...]
        dqv, dkv = dqf_ref[...], dkf_ref[...]
        dq_ref[:, :LANES] = dqv[:, :LANES]
        dq_ref[:, LANES:] = _rope(dqv[:, LANES:].astype(F32), cos, s1, s2).astype(dq_ref.dtype)
        dkn_ref[...] = dkv[:, :LANES]
        part = _rope(dkv[:, LANES:].astype(F32), cos, s1, s2)

        @pl.when(h == 0)
        def _():
            dkpe_ref[...] = part

        @pl.when(h > 0)
        def _():
            dkpe_ref[...] += part

    tab = pl.BlockSpec((tm, LANES), lambda i, h: (i, 0))
    return pl.pallas_call(
        body,
        name=name,
        out_shape=(
            jax.ShapeDtypeStruct((m, cfg.heads * h2), BF16),
            jax.ShapeDtypeStruct((m, cfg.heads * LANES), BF16),
            jax.ShapeDtypeStruct((m, LANES), F32),
        ),
        grid=(m // tm, cfg.heads),
        in_specs=[pl.BlockSpec((tm, h2), lambda i, h: (i, h)), pl.BlockSpec((tm, h2), lambda i, h: (i, h)), tab, tab, tab],
        out_specs=(
            pl.BlockSpec((tm, h2), lambda i, h: (i, h)),
            pl.BlockSpec((tm, LANES), lambda i, h: (i, h)),
            pl.BlockSpec((tm, LANES), lambda i, h: (i, 0)),
        ),
        compiler_params=_cparams("parallel", "arbitrary"),
    )(dqf, dkf, *tabs_t)


def _conv_parts(b_ref, c_ref, x_ref, w_ref, m):
    b, c, x = b_ref[...].astype(F32), c_ref[...].astype(F32), x_ref[...].astype(F32)
    u = c * x
    row = lax.broadcasted_iota(jnp.int32, (m, 1), 0)
    u1 = jnp.where(row >= 1, pltpu.roll(u, 1, 0), 0.0)
    u2 = jnp.where(row >= 2, pltpu.roll(u, 2, 0), 0.0)
    w0, w1, w2 = w_ref[0:1, :], w_ref[1:2, :], w_ref[2:3, :]
    uc = w0 * u2 + w1 * u1 + w2 * u
    return b, c, x, u, u1, u2, uc, (w0, w1, w2), row


def _conv_specs(cfg, tn):
    m, nb, blk0 = cfg.m, cfg.width // tn, cfg.off_conv // tn
    return [
        pl.BlockSpec((m, tn), lambda j: (0, blk0 + j)),
        pl.BlockSpec((m, tn), lambda j: (0, blk0 + nb + j)),
        pl.BlockSpec((m, tn), lambda j: (0, blk0 + 2 * nb + j)),
        pl.BlockSpec((3, tn), lambda j: (0, j)),
    ]


def conv_fwd(cfg, proj, conv_w, *, name):
    m, tn = cfg.m, LANES

    def body(b_ref, c_ref, x_ref, w_ref, o_ref):
        b, _, _, _, _, _, uc, _, _ = _conv_parts(b_ref, c_ref, x_ref, w_ref, m)
        o_ref[...] = (b * uc).astype(o_ref.dtype)

    return pl.pallas_call(
        body,
        name=name,
        out_shape=jax.ShapeDtypeStruct((m, cfg.width), BF16),
        grid=(cfg.width // tn,),
        in_specs=_conv_specs(cfg, tn),
        out_specs=pl.BlockSpec((m, tn), lambda j: (0, j)),
        compiler_params=_cparams("parallel"),
    )(proj, proj, proj, conv_w)


def conv_bwd(cfg, proj, conv_w, do, do_sel, *, name):
    m, tn = cfg.m, LANES

    def body(b_ref, c_ref, x_ref, w_ref, do_ref, db_ref, dc_ref, dx_ref, dw_ref):
        b, c, x, u, u1, u2, uc, (w0, w1, w2), row = _conv_parts(b_ref, c_ref, x_ref, w_ref, m)
        dob = do_ref[...].astype(F32)
        db_ref[...] = (dob * uc).astype(db_ref.dtype)
        duc = dob * b
        up1 = jnp.where(row <= m - 2, pltpu.roll(duc, m - 1, 0), 0.0)
        up2 = jnp.where(row <= m - 3, pltpu.roll(duc, m - 2, 0), 0.0)
        du = w2 * duc + w1 * up1 + w0 * up2
        dc_ref[...] = (du * x).astype(dc_ref.dtype)
        dx_ref[...] = (du * c).astype(dx_ref.dtype)
        dw_ref[0:1, :] = jnp.sum(duc * u2, axis=0, keepdims=True)
        dw_ref[1:2, :] = jnp.sum(duc * u1, axis=0, keepdims=True)
        dw_ref[2:3, :] = jnp.sum(duc * u, axis=0, keepdims=True)

    act = jax.ShapeDtypeStruct((m, cfg.width), BF16)
    blk = pl.BlockSpec((m, tn), lambda j: (0, j))
    return pl.pallas_call(
        body,
        name=name,
        out_shape=(act, act, act, jax.ShapeDtypeStruct((3, cfg.width), F32)),
        grid=(cfg.width // tn,),
        in_specs=_conv_specs(cfg, tn) + [pl.BlockSpec((None, m, tn), lambda j: (do_sel, 0, j))],
        out_specs=(blk, blk, blk, pl.BlockSpec((3, tn), lambda j: (0, j))),
        compiler_params=_cparams("parallel"),
    )(proj, proj, proj, conv_w, do)


def _tri(lower):
    r = lax.broadcasted_iota(jnp.int32, (LANES, LANES), 0)
    c = lax.broadcasted_iota(jnp.int32, (LANES, LANES), 1)
    return jnp.where((r >= c) if lower else (r <= c), 1.0, 0.0).astype(F32)


def fox_gate_fwd(cfg, fl, b_pad, *, name):
    m = cfg.m
    nblk = m // LANES

    def body(fl_ref, b_ref, c_ref):
        z = fl_ref[...] + b_ref[...]
        logf = jnp.minimum(z, 0.0) - jnp.log(1.0 + jnp.exp(-jnp.abs(z)))
        row = lax.broadcasted_iota(jnp.int32, (m, 1), 0)
        logf = jnp.where(row >= cfg.pad, logf, 0.0)
        tri = _tri(True)
        carry = jnp.zeros((1, LANES), F32)
        for blk in range(nblk):
            cb = jnp.dot(tri, logf[blk * LANES:(blk + 1) * LANES, :], precision=lax.Precision.HIGHEST,
                         preferred_element_type=F32) + carry
            c_ref[blk * LANES:(blk + 1) * LANES, :] = cb
            carry = cb[LANES - 1:LANES, :]

    full = pl.BlockSpec((m, LANES), lambda: (0, 0))
    return pl.pallas_call(
        body,
        name=name,
        out_shape=jax.ShapeDtypeStruct((m, LANES), F32),
        in_specs=[full, pl.BlockSpec((1, LANES), lambda: (0, 0))],
        out_specs=full,
        compiler_params=pltpu.CompilerParams(vmem_limit_bytes=VMEM_LIMIT_BYTES),
    )(fl, b_pad)


def fox_gate_bwd(cfg, fl, b_pad, dc, *, name):
    m = cfg.m
    nblk = m // LANES

    def body(fl_ref, b_ref, dc_ref, dfl_ref, db_ref):
        z = fl_ref[...] + b_ref[...]
        dlogsig = 1.0 / (1.0 + jnp.exp(z))
        row = lax.broadcasted_iota(jnp.int32, (m, 1), 0)
        gate = jnp.where(row >= cfg.pad, dlogsig, 0.0)
        dcv = dc_ref[...]
        tri = _tri(False)
        carry = jnp.zeros((1, LANES), F32)
        db = jnp.zeros((1, LANES), F32)
        for blk in reversed(range(nblk)):
            sl = slice(blk * LANES, (blk + 1) * LANES)
            rb = jnp.dot(tri, dcv[sl, :], precision=lax.Precision.HIGHEST, preferred_element_type=F32) + carry
            carry = rb[0:1, :]
            dfl = rb * gate[sl, :]
            dfl_ref[sl, :] = dfl
            db = db + jnp.sum(dfl, axis=0, keepdims=True)
        db_ref[...] = db

    full = pl.BlockSpec((m, LANES), lambda: (0, 0))
    one = pl.BlockSpec((1, LANES), lambda: (0, 0))
    return pl.pallas_call(
        body,
        name=name,
        out_shape=(jax.ShapeDtypeStruct((m, LANES), F32), jax.ShapeDtypeStruct((1, LANES), F32)),
        in_specs=[full, one, full],
        out_specs=(full, one),
        compiler_params=pltpu.CompilerParams(vmem_limit_bytes=VMEM_LIMIT_BYTES),
    )(fl, b_pad, dc)


def _sigmoid(x):
    return 1.0 / (1.0 + jnp.exp(-x))


def gate_merge_fwd(cfg, y, proj, *, name):
    m, d = cfg.m, cfg.d
    tm, tn = _tile(m, 1088, 16), _tile(d, 512)
    nd = d // tn

    def body(y_ref, g0_ref, g1_ref, g2_ref, o_ref):
        acc = None
        for n, g_ref in enumerate((g0_ref, g1_ref, g2_ref)):
            t = _sigmoid(g_ref[...].astype(F32)) * y_ref[n].astype(F32)
            acc = t if acc is None else acc + t
        o_ref[...] = acc.astype(o_ref.dtype)

    gate = lambda n: pl.BlockSpec((tm, tn), lambda i, j: (i, n * nd + j))
    return pl.pallas_call(
        body,
        name=name,
        out_shape=jax.ShapeDtypeStruct((m, d), BF16),
        grid=(m // tm, nd),
        in_specs=[pl.BlockSpec((3, tm, tn), lambda i, j: (0, i, j)), gate(0), gate(1), gate(2)],
        out_specs=pl.BlockSpec((tm, tn), lambda i, j: (i, j)),
        compiler_params=_cparams("parallel", "parallel"),
    )(y, proj, proj, proj)


def gate_merge_bwd(cfg, dm, y, proj, *, name):
    m, d = cfg.m, cfg.d
    tm, tn = _tile(m, 1088, 16), _tile(d, 512)
    nd = d // tn

    def body(dm_ref, y_ref, g_ref, dy_ref, dg_ref):
        sg = _sigmoid(g_ref[...].astype(F32))
        dmv = dm_ref[...].astype(F32)
        dy_ref[...] = (sg * dmv).astype(dy_ref.dtype)
        dg_ref[...] = (dmv * y_ref[...].astype(F32) * sg * (1.0 - sg)).astype(dg_ref.dtype)

    return pl.pallas_call(
        body,
        name=name,
        out_shape=(jax.ShapeDtypeStruct((3, m, d), BF16), jax.ShapeDtypeStruct((m, 3 * d), BF16)),
        grid=(m // tm, nd, 3),
        in_specs=[
            pl.BlockSpec((tm, tn), lambda i, j, n: (i, j)),
            pl.BlockSpec((None, tm, tn), lambda i, j, n: (n, i, j)),
            pl.BlockSpec((tm, tn), lambda i, j, n: (i, n * nd + j)),
        ],
        out_specs=(
            pl.BlockSpec((None, tm, tn), lambda i, j, n: (n, i, j)),
            pl.BlockSpec((tm, tn), lambda i, j, n: (i, n * nd + j)),
        ),
        compiler_params=_cparams("parallel", "parallel", "parallel"),
    )(dm, y, proj)


def swiglu_fwd(cfg, gu, *, name):
    m, f = cfg.m, cfg.d_ff
    tm, tn = _tile(m, 1088, 16), _tile(f, 512)
    nf = f // tn

    def body(g_ref, u_ref, o_ref):
        g = g_ref[...].astype(F32)
        o_ref[...] = (g * _sigmoid(g) * u_ref[...].astype(F32)).astype(o_ref.dtype)

    return pl.pallas_call(
        body,
        name=name,
        out_shape=jax.ShapeDtypeStruct((m, f), BF16),
        grid=(m // tm, nf),
        in_specs=[pl.BlockSpec((tm, tn), lambda i, j: (i, j)), pl.BlockSpec((tm, tn), lambda i, j: (i, nf + j))],
        out_specs=pl.BlockSpec((tm, tn), lambda i, j: (i, j)),
        compiler_params=_cparams("parallel", "parallel"),
    )(gu, gu)


def swiglu_bwd(cfg, dact, gu, *, name):
    m, f = cfg.m, cfg.d_ff
    tm, tn = _tile(m, 1088, 16), _tile(f, 512)
    nf = f // tn

    def body(da_ref, g_ref, u_ref, o_ref):
        j = pl.program_id(1)
        g, u, da = g_ref[...].astype(F32), u_ref[...].astype(F32), da_ref[...].astype(F32)
        sg = _sigmoid(g)
        dg = da * u * sg * (1.0 + g * (1.0 - sg))
        du = da * g * sg
        o_ref[...] = jnp.where(j < nf, dg, du).astype(o_ref.dtype)

    return pl.pallas_call(
        body,
        name=name,
        out_shape=jax.ShapeDtypeStruct((m, 2 * f), BF16),
        grid=(m // tm, 2 * nf),
        in_specs=[
            pl.BlockSpec((tm, tn), lambda i, j: (i, j % nf)),
            pl.BlockSpec((tm, tn), lambda i, j: (i, j % nf)),
            pl.BlockSpec((tm, tn), lambda i, j: (i, nf + j % nf)),
        ],
        out_specs=pl.BlockSpec((tm, tn), lambda i, j: (i, j)),
        compiler_params=_cparams("parallel", "parallel"),
    )(dact, gu, gu)


def loss_head(cfg, h, target, *, name):
    m, d = cfg.m, cfg.d
    assert cfg.pad + cfg.n_meta == LANES
    tm = LANES
    inv_d = 1.0 / d

    def body(h_ref, t_ref, dh_ref, loss_ref):
        i = pl.program_id(0)

        @pl.when(i == 0)
        def _():
            dh_ref[...] = jnp.zeros_like(dh_ref)
            loss_ref[...] = jnp.zeros_like(loss_ref)

        @pl.when(i > 0)
        def _():
            err = h_ref[...] - t_ref[...]
            dh_ref[...] = err * inv_d
            loss_ref[...] += 0.5 * inv_d * jnp.sum(err * err)

    return pl.pallas_call(
        body,
        name=name,
        out_shape=(jax.ShapeDtypeStruct((m, d), F32), jax.ShapeDtypeStruct((8, LANES), F32)),
        grid=(m // tm,),
        in_specs=[pl.BlockSpec((tm, d), lambda i: (i, 0)), pl.BlockSpec((tm, d), lambda i: (jnp.maximum(i - 1, 0), 0))],
        out_specs=(pl.BlockSpec((tm, d), lambda i: (i, 0)), pl.BlockSpec((8, LANES), lambda i: (0, 0))),
        compiler_params=_cparams("arbitrary"),
    )(h, target)


def adamw(w, g, m_, v_, *, name):
    r, c = w.shape
    c_pad = -(-c // LANES) * LANES
    tr = r
    if r % 8 == 0:
        tr = _tile(r, max(8, (3 << 19) // (4 * c_pad) // 8 * 8), 8)
    bc1 = 1.0 - ADAM_B1 ** ADAM_STEP
    bc2 = 1.0 - ADAM_B2 ** ADAM_STEP

    def body(w_ref, g_ref, m_ref, v_ref, d_ref, nm_ref, nv_ref):
        gv = g_ref[...]
        nm = ADAM_B1 * m_ref[...] + (1.0 - ADAM_B1) * gv
        nv = ADAM_B2 * v_ref[...] + (1.0 - ADAM_B2) * (gv * gv)
        d_ref[...] = -ADAM_LR * ((nm / bc1) / (jnp.sqrt(nv / bc2) + ADAM_EPS) + ADAM_WD * w_ref[...])
        nm_ref[...] = nm
        nv_ref[...] = nv

    blk = pl.BlockSpec((tr, c), lambda i: (i, 0))
    shp = jax.ShapeDtypeStruct((r, c), F32)
    return pl.pallas_call(
        body,
        name=name,
        out_shape=(shp, shp, shp),
        grid=(r // tr,),
        in_specs=[blk, blk, blk, blk],
        out_specs=(blk, blk, blk),
        compiler_params=_cparams("parallel"),
    )(w, g, m_, v_)


_HBM = pl.BlockSpec(memory_space=pltpu.HBM)


def _place():
    x, y, c = lax.axis_index("x"), lax.axis_index("y"), lax.axis_index("c")
    flips = [(1 - x, y), (x, 1 - y), (1 - x, 1 - y)]
    return x, y, c, flips


def gather_weights(shards, *, name):
    nw = len(shards)
    halves = [s.shape[0] // 2 for s in shards]
    assert all(s.shape[0] % 32 == 0 for s in shards)

    def body(*refs):
        ins, outs = refs[:nw], refs[nw:2 * nw]
        send_sems, recv_sems, local_sems = refs[2 * nw:]
        x, y, c, flips = _place()
        mine = 2 * x + y
        sibling = (x, y, 1 - c)

        def half(w, chip, core):
            return outs[w].at[chip, pl.ds(core * halves[w], halves[w]), :]

        def copy(w, k, src, chip, core, to):
            return pltpu.make_async_remote_copy(src_ref=src, dst_ref=half(w, chip, core), send_sem=send_sems.at[6 * w + k],
                                                recv_sem=recv_sems.at[6 * w + k], device_id=to, device_id_type=MESH)

        local = [pltpu.make_async_copy(ins[w], outs[w].at[mine], local_sems.at[w]) for w in range(nw)]
        for cp in local:
            cp.start()
        sent = []
        for w in range(nw):
            src = ins[w].at[pl.ds(c * halves[w], halves[w]), :]
            for k, (fx, fy) in enumerate(flips):
                sent.append(copy(w, k, src, mine, c, (fx, fy, c)))
                sent[-1].start()
        for w in range(nw):
            for k, (fx, fy) in enumerate(flips):
                theirs = 2 * fx + fy
                copy(w, k, half(w, theirs, c), theirs, c, sibling).wait_recv()
                sent.append(copy(w, 3 + k, half(w, theirs, c), theirs, c, sibling))
                sent[-1].start()
        for w in range(nw):
            for k, (fx, fy) in enumerate(flips):
                theirs = 2 * fx + fy
                copy(w, 3 + k, half(w, theirs, 1 - c), theirs, 1 - c, sibling).wait_recv()
        for cp in sent:
            cp.wait_send()
        for cp in local:
            cp.wait()

    return pl.pallas_call(
        body,
        name=name,
        out_shape=tuple(jax.ShapeDtypeStruct((N_CHIPS,) + s.shape, s.dtype) for s in shards),
        in_specs=[_HBM] * nw,
        out_specs=tuple([_HBM] * nw),
        scratch_shapes=[pltpu.SemaphoreType.DMA((6 * nw,)), pltpu.SemaphoreType.DMA((6 * nw,)), pltpu.SemaphoreType.DMA((nw,))],
    )(*shards)


def swap_halves(grads, *, name):
    nw = len(grads)
    halves = [g.shape[1] // 2 for g in grads]

    def body(*refs):
        ins, outs = refs[:nw], refs[nw:2 * nw]
        send_sems, recv_sems = refs[2 * nw:]
        x, y, c, _ = _place()
        copies = [
            pltpu.make_async_remote_copy(src_ref=ins[w].at[:, pl.ds((1 - c) * halves[w], halves[w]), :], dst_ref=outs[w],
                                         send_sem=send_sems.at[w], recv_sem=recv_sems.at[w], device_id=(x, y, 1 - c),
                                         device_id_type=MESH)
            for w in range(nw)
        ]
        for cp in copies:
            cp.start()
        for cp in copies:
            cp.wait()

    return pl.pallas_call(
        body,
        name=name,
        out_shape=tuple(jax.ShapeDtypeStruct((N_CHIPS, h, g.shape[2]), g.dtype) for g, h in zip(grads, halves)),
        in_specs=[_HBM] * nw,
        out_specs=tuple([_HBM] * nw),
        scratch_shapes=[pltpu.SemaphoreType.DMA((nw,)), pltpu.SemaphoreType.DMA((nw,))],
    )(*grads)


def exchange_chips(parts, *, name):
    nw = len(parts)

    def body(*refs):
        ins, outs = refs[:nw], refs[nw:2 * nw]
        send_sems, recv_sems = refs[2 * nw:]
        _, _, c, flips = _place()
        copies = [
            pltpu.make_async_remote_copy(src_ref=ins[w].at[2 * fx + fy], dst_ref=outs[w].at[k], send_sem=send_sems.at[3 * w + k],
                                         recv_sem=recv_sems.at[3 * w + k], device_id=(fx, fy, c), device_id_type=MESH)
            for w in range(nw) for k, (fx, fy) in enumerate(flips)
        ]
        for cp in copies:
            cp.start()
        for cp in copies:
            cp.wait()

    return pl.pallas_call(
        body,
        name=name,
        out_shape=tuple(jax.ShapeDtypeStruct((3,) + p.shape[1:], p.dtype) for p in parts),
        in_specs=[_HBM] * nw,
        out_specs=tuple([_HBM] * nw),
        scratch_shapes=[pltpu.SemaphoreType.DMA((3 * nw,)), pltpu.SemaphoreType.DMA((3 * nw,))],
    )(*parts)


def share_halves(sums, *, name):
    nw = len(sums)
    halves = [s.shape[0] for s in sums]

    def body(*refs):
        ins, outs = refs[:nw], refs[nw:2 * nw]
        send_sems, recv_sems, local_sems = refs[2 * nw:]
        x, y, c, _ = _place()

        def rows(w, core):
            return outs[w].at[pl.ds(core * halves[w], halves[w]), :]

        local = [pltpu.make_async_copy(ins[w], rows(w, c), local_sems.at[w]) for w in range(nw)]
        copies = [
            pltpu.make_async_remote_copy(src_ref=ins[w], dst_ref=rows(w, c), send_sem=send_sems.at[w],
                                         recv_sem=recv_sems.at[w], device_id=(x, y, 1 - c), device_id_type=MESH)
            for w in range(nw)
        ]
        for cp in local + copies:
            cp.start()
        for w in range(nw):
            pltpu.make_async_remote_copy(src_ref=ins[w], dst_ref=rows(w, 1 - c), send_sem=send_sems.at[w],
                                         recv_sem=recv_sems.at[w], device_id=(x, y, 1 - c), device_id_type=MESH).wait_recv()
        for cp in copies:
            cp.wait_send()
        for cp in local:
            cp.wait()

    return pl.pallas_call(
        body,
        name=name,
        out_shape=tuple(jax.ShapeDtypeStruct((2 * s.shape[0], s.shape[1]), s.dtype) for s in sums),
        in_specs=[_HBM] * nw,
        out_specs=tuple([_HBM] * nw),
        scratch_shapes=[pltpu.SemaphoreType.DMA((nw,)), pltpu.SemaphoreType.DMA((nw,)), pltpu.SemaphoreType.DMA((nw,))],
    )(*sums)


def gather_blocks(block, *, reduce, name):
    rows, cols = block.shape

    def body(x_ref, out_ref, *rest):
        if reduce:
            buf_ref, send_sems, recv_sems = rest
        else:
            send_sems, recv_sems = rest
            buf_ref = out_ref
        x, y, c, flips = _place()
        me, sibling = (x, y, c), (x, y, 1 - c)

        def slot(px, py, pc):
            return buf_ref.at[4 * px + 2 * py + pc]

        def copy(k, blk, to, src=None):
            return pltpu.make_async_remote_copy(src_ref=slot(*blk) if src is None else src, dst_ref=slot(*blk),
                                                send_sem=send_sems.at[k], recv_sem=recv_sems.at[k], device_id=to,
                                                device_id_type=MESH)

        buf_ref[4 * x + 2 * y + c] = x_ref[...]
        first = [copy(0, me, sibling, src=x_ref)]
        first += [copy(1 + j, me, (*chip, c), src=x_ref) for j, chip in enumerate(flips)]
        for cp in first:
            cp.start()
        passed = [copy(4 + j, (*chip, c), sibling) for j, chip in enumerate(flips)]
        for j, chip in enumerate(flips):
            copy(1 + j, (*chip, c), me).wait_recv()
            passed[j].start()
        copy(0, sibling, me).wait_recv()
        for j, chip in enumerate(flips):
            copy(4 + j, (*chip, 1 - c), me).wait_recv()
        for cp in first + passed:
            cp.wait_send()
        if reduce:
            acc = buf_ref[0]
            for dev in range(1, N_DEV):
                acc = acc + buf_ref[dev]
            out_ref[...] = acc

    vmem = pl.BlockSpec(memory_space=pltpu.VMEM)
    sems = [pltpu.SemaphoreType.DMA((7,)), pltpu.SemaphoreType.DMA((7,))]
    if reduce:
        out_shape = jax.ShapeDtypeStruct((rows, cols), block.dtype)
        scratch = [pltpu.VMEM((N_DEV, rows, cols), block.dtype)] + sems
    else:
        out_shape = jax.ShapeDtypeStruct((N_DEV, rows, cols), block.dtype)
        scratch = sems
    return pl.pallas_call(
        body,
        name=name,
        out_shape=out_shape,
        in_specs=[vmem],
        out_specs=vmem,
        scratch_shapes=scratch,
        compiler_params=pltpu.CompilerParams(vmem_limit_bytes=VMEM_LIMIT_BYTES),
    )(block)


def add_own_half(grad, recv, core, *, name):
    _, r2, cols = recv.shape
    tr = _tile(r2, max(16, (1 << 20) // (2 * cols) // 16 * 16), 16)
    nr = r2 // tr

    def body(core_ref, g_ref, r_ref, o_ref):
        o_ref[...] = (g_ref[...].astype(F32) + r_ref[...].astype(F32)).astype(o_ref.dtype)

    return pl.pallas_call(
        body,
        name=name,
        out_shape=jax.ShapeDtypeStruct(recv.shape, BF16),
        grid_spec=pltpu.PrefetchScalarGridSpec(
            num_scalar_prefetch=1,
            grid=(N_CHIPS, nr),
            in_specs=[
                pl.BlockSpec((None, tr, cols), lambda k, i, core_ref: (k, core_ref[0] * nr + i, 0)),
                pl.BlockSpec((None, tr, cols), lambda k, i, core_ref: (k, i, 0)),
            ],
            out_specs=pl.BlockSpec((None, tr, cols), lambda k, i, core_ref: (k, i, 0)),
        ),
        compiler_params=_cparams("parallel", "parallel"),
    )(core, grad, recv)


def sum_chips(part, recv, chip, *, name):
    _, r2, cols = part.shape
    tr = _tile(r2, max(16, (1 << 20) // (2 * cols) // 16 * 16), 16)

    def body(chip_ref, p_ref, r_ref, o_ref):
        acc = p_ref[...].astype(F32)
        for k in range(3):
            acc = acc + r_ref[k].astype(F32)
        o_ref[...] = acc

    return pl.pallas_call(
        body,
        name=name,
        out_shape=jax.ShapeDtypeStruct((r2, cols), F32),
        grid_spec=pltpu.PrefetchScalarGridSpec(
            num_scalar_prefetch=1,
            grid=(r2 // tr,),
            in_specs=[
                pl.BlockSpec((None, tr, cols), lambda i, chip_ref: (chip_ref[0], i, 0)),
                pl.BlockSpec((3, tr, cols), lambda i, chip_ref: (0, i, 0)),
            ],
            out_specs=pl.BlockSpec((tr, cols), lambda i, chip_ref: (i, 0)),
        ),
        compiler_params=_cparams("parallel"),
    )(chip, part, recv)


WEIGHTS = ("w_in", "w_uq", "w_ukv", "w_branch", "w_out", "w_ffn_in", "w_ffn_out")
GAINS = ("g_mix_pre", "g_mix_post", "g_ffn_pre", "g_ffn_post")


def layer_fwd(cfg, h, w, s, tabs, tag):
    m, d, hd = cfg.m, cfg.d, cfg.heads
    fox_blk = cfg.off_fox // LANES
    hn = rmsnorm_fwd(h, s["g_mix_pre"], BF16, name=f"norm_mix_pre{tag}")
    proj = matmul(hn, w["w_in"], "nn", BF16, tm=m, tn=_tile(cfg.d_inp, 256), tk=d, name=f"proj{tag}")
    fl = matmul(hn, w["w_in"][:, cfg.off_fl:], "nn", F32, tm=m, tn=LANES, tk=d, name=f"proj_forget{tag}")
    cqn = rmsnorm_fwd(proj, s["g_q_lat"], BF16, width=cfg.q_rank, col_blk=cfg.off_cq // cfg.q_rank, name=f"norm_q{tag}")
    ckvn = rmsnorm_fwd(proj, s["g_kv_lat"], BF16, width=cfg.kv_rank, col_blk=cfg.off_ckv // cfg.kv_rank, name=f"norm_kv{tag}")
    q = matmul(cqn, w["w_uq"], "nn", BF16, tm=m, tn=_tile(2 * cfg.width, 512), tk=cfg.q_rank, name=f"up_q{tag}")
    kv = matmul(ckvn, w["w_ukv"], "nn", BF16, tm=m, tn=_tile(2 * cfg.width, 512), tk=cfg.kv_rank, name=f"up_kv{tag}")
    qf, kf = mla_prep_fwd(cfg, q, kv, proj, tabs[0], name=f"mla_prep{tag}")
    o_a, lse_a = attn_fwd(qf, kf, kv, heads=hd, dk=2 * LANES, dv=LANES, qblk0=0, kblk0=0, vblk0=hd,
                          scale=(LANES + ROPE) ** -0.5, pad=cfg.pad, name=f"mla_attn{tag}")
    o_b = conv_fwd(cfg, proj, s["conv_w"], name=f"conv{tag}")
    b_pad = jnp.pad(s["b_forget"], (0, LANES - hd)).reshape(1, LANES)
    cum = fox_gate_fwd(cfg, fl, b_pad, name=f"fox_gate{tag}")
    cum_t = cum[:, :hd].T
    decay = (cum_t[:, :, None], cum_t[:, None, :])
    o_c, lse_c = attn_fwd(proj, proj, proj, heads=hd, dk=LANES, dv=LANES, qblk0=fox_blk, kblk0=fox_blk + hd,
                          vblk0=fox_blk + 2 * hd, scale=LANES ** -0.5, pad=cfg.pad, decay=decay, name=f"fox_attn{tag}")
    o = jnp.stack([o_a, o_b, o_c])
    y = matmul(o, w["w_branch"], "nn", BF16, tm=m, tn=_tile(d, 512), tk=cfg.width, name=f"branch{tag}")
    merged = gate_merge_fwd(cfg, y, proj, name=f"merge{tag}")
    mix = matmul(merged, w["w_out"], "nn", F32, tm=m, tn=_tile(d, 256), tk=d, name=f"out_proj{tag}")
    h_mid = rmsnorm_fwd(mix, s["g_mix_post"], F32, res=h, name=f"norm_mix_post{tag}")
    hn2 = rmsnorm_fwd(h_mid, s["g_ffn_pre"], BF16, name=f"norm_ffn_pre{tag}")
    gu = matmul(hn2, w["w_ffn_in"], "nn", BF16, tm=m, tn=_tile(2 * cfg.d_ff, 512), tk=d, name=f"ffn_in{tag}")
    act = swiglu_fwd(cfg, gu, name=f"swiglu{tag}")
    f = matmul(act, w["w_ffn_out"], "nn", F32, tm=m, tn=_tile(d, 512), tk=_tile(cfg.d_ff, 512), name=f"ffn_out{tag}")
    h_next = rmsnorm_fwd(f, s["g_ffn_post"], F32, res=h_mid, name=f"norm_ffn_post{tag}")
    saved = dict(h=h, hn=hn, proj=proj, fl=fl, cqn=cqn, ckvn=ckvn, kv=kv, qf=qf, kf=kf, lse_a=lse_a,
                 b_pad=b_pad, decay=decay, lse_c=lse_c, o=o, y=y, merged=merged, mix=mix, h_mid=h_mid,
                 hn2=hn2, gu=gu, act=act, f=f)
    return h_next, saved


def layer_bwd(cfg, dh, w, s, r, tabs, tag):
    m, d, hd = cfg.m, cfg.d, cfg.heads
    fox_blk = cfg.off_fox // LANES
    tk_m = m
    df, dg4 = rmsnorm_bwd(r["f"], s["g_ffn_post"], dh, BF16, name=f"norm_ffn_post_bwd{tag}")
    dact = matmul(df, w["w_ffn_out"], "nt", BF16, tm=m, tn=_tile(cfg.d_ff, 512), tk=d, name=f"ffn_out_dx{tag}")
    dw_fo = matmul(r["act"], df, "tn", BF16, tm=_tile(cfg.d_ff, 512), tn=_tile(d, 1024), tk=tk_m, name=f"ffn_out_dw{tag}")
    dgu = swiglu_bwd(cfg, dact, r["gu"], name=f"swiglu_bwd{tag}")
    dhn2 = matmul(dgu, w["w_ffn_in"], "nt", F32, tm=m, tn=_tile(d, 512), tk=_tile(2 * cfg.d_ff, 512), name=f"ffn_in_dx{tag}")
    dw_fi = matmul(r["hn2"], dgu, "tn", BF16, tm=_tile(d, 1024), tn=_tile(2 * cfg.d_ff, 512), tk=tk_m, name=f"ffn_in_dw{tag}")
    dh_mid, dg3 = rmsnorm_bwd(r["h_mid"], s["g_ffn_pre"], dhn2, F32, dres=dh, name=f"norm_ffn_pre_bwd{tag}")
    dmix, dg2 = rmsnorm_bwd(r["mix"], s["g_mix_post"], dh_mid, BF16, name=f"norm_mix_post_bwd{tag}")
    dmerged = matmul(dmix, w["w_out"], "nt", BF16, tm=m, tn=_tile(d, 512), tk=d, name=f"out_proj_dx{tag}")
    dw_out = matmul(r["merged"], dmix, "tn", BF16, tm=_tile(d, 1024), tn=_tile(d, 512), tk=tk_m, name=f"out_proj_dw{tag}")
    dy, dgl = gate_merge_bwd(cfg, dmerged, r["y"], r["proj"], name=f"merge_bwd{tag}")
    do = matmul(dy, w["w_branch"], "nt", BF16, tm=m, tn=_tile(cfg.width, 512), tk=d, name=f"branch_dx{tag}")
    dw_br = matmul(r["o"], dy, "tn", BF16, tm=_tile(cfg.width, 1024), tn=_tile(d, 512), tk=tk_m, name=f"branch_dw{tag}")
    dqf, dkf, dv_a = attn_bwd(r["qf"], r["kf"], r["kv"], do, 0, r["lse_a"], heads=hd, dk=2 * LANES, dv=LANES,
                              qblk0=0, kblk0=0, vblk0=hd, scale=(LANES + ROPE) ** -0.5, pad=cfg.pad, name=f"mla_attn_bwd{tag}")
    dq, dkn, dkpe = mla_prep_bwd(cfg, dqf, dkf, tabs[1], name=f"mla_prep_bwd{tag}")
    dkv = jnp.concatenate([dkn, dv_a], axis=1)
    dcqn = matmul(dq, w["w_uq"], "nt", F32, tm=m, tn=cfg.q_rank, tk=_tile(2 * cfg.width, 512), name=f"up_q_dx{tag}")
    dw_uq = matmul(r["cqn"], dq, "tn", BF16, tm=cfg.q_rank, tn=_tile(2 * cfg.width, 512), tk=tk_m, name=f"up_q_dw{tag}")
    dckvn = matmul(dkv, w["w_ukv"], "nt", F32, tm=m, tn=cfg.kv_rank, tk=_tile(2 * cfg.width, 512), name=f"up_kv_dx{tag}")
    dw_ukv = matmul(r["ckvn"], dkv, "tn", BF16, tm=cfg.kv_rank, tn=_tile(2 * cfg.width, 512), tk=tk_m, name=f"up_kv_dw{tag}")
    dcq, dgq = rmsnorm_bwd(r["proj"], s["g_q_lat"], dcqn, BF16, width=cfg.q_rank, col_blk=cfg.off_cq // cfg.q_rank,
                           name=f"norm_q_bwd{tag}")
    dckv, dgkv = rmsnorm_bwd(r["proj"], s["g_kv_lat"], dckvn, BF16, width=cfg.kv_rank, col_blk=cfg.off_ckv // cfg.kv_rank,
                             name=f"norm_kv_bwd{tag}")
    dcb, dcc, dcx, dconv_w = conv_bwd(cfg, r["proj"], s["conv_w"], do, 1, name=f"conv_bwd{tag}")
    dfq, dfk, dfv, dck = attn_bwd(r["proj"], r["proj"], r["proj"], do, 2, r["lse_c"], heads=hd, dk=LANES, dv=LANES,
                                  qblk0=fox_blk, kblk0=fox_blk + hd, vblk0=fox_blk + 2 * hd, scale=LANES ** -0.5,
                                  pad=cfg.pad, decay=r["decay"], name=f"fox_attn_bwd{tag}")
    dc = jnp.pad(dck[:, 0, :].T, ((0, 0), (0, LANES - hd)))
    dfl, dbf = fox_gate_bwd(cfg, r["fl"], r["b_pad"], dc, name=f"fox_gate_bwd{tag}")
    dproj = jnp.concatenate([dgl, dcq, dckv, dcb, dcc, dcx, dfq, dfk, dfv, dkpe.astype(BF16), dfl.astype(BF16)], axis=1)
    dhn = matmul(dproj, w["w_in"], "nt", F32, tm=m, tn=_tile(d, 512), tk=_tile(cfg.d_inp, 256), name=f"proj_dx{tag}")
    dw_in = matmul(r["hn"], dproj, "tn", BF16, tm=_tile(d, 1024), tn=_tile(cfg.d_inp, 256), tk=tk_m, name=f"proj_dw{tag}")
    dh_in, dg1 = rmsnorm_bwd(r["h"], s["g_mix_pre"], dhn, F32, dres=dh_mid, name=f"norm_mix_pre_bwd{tag}")
    dws = dict(w_in=dw_in, w_uq=dw_uq, w_ukv=dw_ukv, w_branch=dw_br, w_out=dw_out, w_ffn_in=dw_fi, w_ffn_out=dw_fo)
    dsmall = dict(g_mix_pre=dg1[0], g_mix_post=dg2[0], g_ffn_pre=dg3[0], g_ffn_post=dg4[0], g_q_lat=dgq[0], g_kv_lat=dgkv[0],
                  b_forget=dbf[0, :hd], conv_w=dconv_w)
    return dh_in, dws, dsmall


def local_step(cfg, x, target, meta, weights, small):
    h = jnp.concatenate([jnp.zeros((cfg.pad, cfg.d), F32), meta, x], axis=0)
    cos, s1, s2 = rope_tables(cfg)
    tabs = ((cos, s1, s2), (cos, -s1, -s2))
    saved = []
    for l in range(cfg.depth):
        h, r = layer_fwd(cfg, h, weights[l], small[l], tabs, f"_{l}")
        saved.append(r)
    dh, loss = loss_head(cfg, h, target, name="loss_head")
    dws, dsmalls = [None] * cfg.depth, [None] * cfg.depth
    for l in reversed(range(cfg.depth)):
        dh, dws[l], dsmalls[l] = layer_bwd(cfg, dh, weights[l], small[l], saved[l], tabs, f"_{l}")
    first = cfg.pad + cfg.n_meta
    return loss, dh[first:], dh[cfg.pad:first], dws, dsmalls


def _cols_from_chips(g):
    return jnp.transpose(g, (1, 0, 2)).reshape(g.shape[1], N_CHIPS * g.shape[2])


def _cols_to_chips(w):
    r, c = w.shape
    return jnp.transpose(w.reshape(r, N_CHIPS, c // N_CHIPS), (1, 0, 2))


def full_weights(cfg, g):
    br = _cols_from_chips(g["w_branch"]).reshape(3, cfg.width, cfg.d)
    return dict(
        w_in=pack_w_in(cfg, _cols_from_chips(g["w_in"])),
        w_uq=pack_w_uq(cfg, _cols_from_chips(g["w_uq"])),
        w_ukv=pack_w_ukv(cfg, _cols_from_chips(g["w_ukv"])),
        w_branch=br,
        w_out=g["w_out"].reshape(cfg.d, cfg.d),
        w_ffn_in=_cols_from_chips(g["w_ffn_in"]),
        w_ffn_out=g["w_ffn_out"].reshape(cfg.d_ff, cfg.d),
    )


def chip_grads(cfg, dw):
    return dict(
        w_in=_cols_to_chips(unpack_w_in(cfg, dw["w_in"])),
        w_uq=_cols_to_chips(unpack_w_uq(cfg, dw["w_uq"])),
        w_ukv=_cols_to_chips(unpack_w_ukv(cfg, dw["w_ukv"])),
        w_branch=_cols_to_chips(dw["w_branch"].reshape(3 * cfg.width, cfg.d)),
        w_out=dw["w_out"].reshape(N_CHIPS, cfg.d // N_CHIPS, cfg.d),
        w_ffn_in=_cols_to_chips(dw["w_ffn_in"]),
        w_ffn_out=dw["w_ffn_out"].reshape(N_CHIPS, cfg.d_ff // N_CHIPS, cfg.d),
    )


def _small_rows(cfg):
    return dict(g_mix_pre=cfg.d // LANES, g_mix_post=cfg.d // LANES, g_ffn_pre=cfg.d // LANES, g_ffn_post=cfg.d // LANES,
                g_q_lat=cfg.q_rank // LANES, g_kv_lat=cfg.kv_rank // LANES, b_forget=1, conv_w=3 * cfg.width // LANES)


def pack_small(cfg, loss, dmeta, dsmalls):
    parts = [loss[0:1, :], dmeta.reshape(-1, LANES)]
    for ds in dsmalls:
        for k in _small_rows(cfg):
            v = ds[k]
            if k == "b_forget":
                v = jnp.pad(v, (0, LANES - cfg.heads))
            parts.append(v.reshape(-1, LANES))
    rows = sum(p.shape[0] for p in parts)
    parts.append(jnp.zeros((-rows % 8, LANES), F32))
    return jnp.concatenate(parts, axis=0)


def unpack_small(cfg, block):
    loss = block[0, 0]
    n = cfg.n_meta * cfg.d // LANES
    dmeta = block[1:1 + n].reshape(cfg.n_meta, cfg.d)
    at = 1 + n
    out = []
    for _ in range(cfg.depth):
        ds = {}
        for k, rows in _small_rows(cfg).items():
            v = block[at:at + rows]
            at += rows
            if k == "b_forget":
                v = v[0, :cfg.heads]
            elif k == "conv_w":
                v = v.reshape(3, cfg.width)
            else:
                v = v.reshape(-1)
            ds[k] = v
        out.append(ds)
    return loss, dmeta, out


def kernel(x, meta, w_in, b_forget, g_q_lat, g_kv_lat, w_uq, w_ukv, conv_w, w_branch, w_out, w_ffn_in, w_ffn_out, g_mix_pre, g_mix_post, g_ffn_pre, g_ffn_post, loss_target, m_meta, m_w_in, m_b_forget, m_g_q_lat, m_g_kv_lat, m_w_uq, m_w_ukv, m_conv_w, m_w_branch, m_w_out, m_w_ffn_in, m_w_ffn_out, m_g_mix_pre, m_g_mix_post, m_g_ffn_pre, m_g_ffn_post, v_meta, v_w_in, v_b_forget, v_g_q_lat, v_g_kv_lat, v_w_uq, v_w_ukv, v_conv_w, v_w_branch, v_w_out, v_w_ffn_in, v_w_ffn_out, v_g_mix_pre, v_g_mix_post, v_g_ffn_pre, v_g_ffn_post):
    cfg = CFG
    names = ("meta", "w_in", "b_forget", "g_q_lat", "g_kv_lat", "w_uq", "w_ukv", "conv_w", "w_branch", "w_out", "w_ffn_in",
             "w_ffn_out", "g_mix_pre", "g_mix_post", "g_ffn_pre", "g_ffn_post")
    params = dict(zip(names, (meta, w_in, b_forget, g_q_lat, g_kv_lat, w_uq, w_ukv, conv_w, w_branch, w_out, w_ffn_in, w_ffn_out,
                              g_mix_pre, g_mix_post, g_ffn_pre, g_ffn_post)))
    mom1 = dict(zip(names, (m_meta, m_w_in, m_b_forget, m_g_q_lat, m_g_kv_lat, m_w_uq, m_w_ukv, m_conv_w, m_w_branch, m_w_out,
                            m_w_ffn_in, m_w_ffn_out, m_g_mix_pre, m_g_mix_post, m_g_ffn_pre, m_g_ffn_post)))
    mom2 = dict(zip(names, (v_meta, v_w_in, v_b_forget, v_g_q_lat, v_g_kv_lat, v_w_uq, v_w_ukv, v_conv_w, v_w_branch, v_w_out,
                            v_w_ffn_in, v_w_ffn_out, v_g_mix_pre, v_g_mix_post, v_g_ffn_pre, v_g_ffn_post)))
    xi, yi, ci = lax.axis_index("x"), lax.axis_index("y"), lax.axis_index("c")
    chip = 2 * xi + yi
    chip_arr = jnp.reshape(chip, (1,)).astype(jnp.int32)
    core_arr = jnp.reshape(ci, (1,)).astype(jnp.int32)

    meta_all = gather_blocks(meta, reduce=False, name="gather_meta")[0::2]
    meta_full = jnp.transpose(meta_all, (1, 0, 2)).reshape(cfg.n_meta, cfg.d)
    conv_rows = conv_w.reshape(cfg.depth * 3, cfg.width // N_CHIPS)
    conv_all = gather_blocks(conv_rows, reduce=False, name="gather_conv_w")[0::2]
    conv_full = jnp.transpose(conv_all, (1, 0, 2)).reshape(cfg.depth, 3, cfg.width)

    def shard2d(name, l):
        w = params[name][l]
        return w.reshape(-1, w.shape[-1]).astype(BF16)

    weights, small = [], []
    for l in range(cfg.depth):
        got = gather_weights([shard2d(n, l) for n in WEIGHTS], name=f"gather_weights_{l}")
        weights.append(full_weights(cfg, dict(zip(WEIGHTS, got))))
        small.append(dict(g_mix_pre=g_mix_pre[l], g_mix_post=g_mix_post[l], g_ffn_pre=g_ffn_pre[l], g_ffn_post=g_ffn_post[l],
                          g_q_lat=g_q_lat[l], g_kv_lat=g_kv_lat[l], b_forget=b_forget[l], conv_w=conv_full[l]))

    loss, grad_x, dmeta, dws, dsmalls = local_step(cfg, x[0], loss_target[0], meta_full, weights, small)

    grads = {n: [] for n in WEIGHTS}
    for l in range(cfg.depth):
        send = chip_grads(cfg, dws[l])
        mine = [send[n] for n in WEIGHTS]
        theirs = swap_halves(mine, name=f"swap_halves_{l}")
        parts = [add_own_half(g, t, core_arr, name=f"add_own_half_{n}_{l}") for n, g, t in zip(WEIGHTS, mine, theirs)]
        others = exchange_chips(parts, name=f"exchange_chips_{l}")
        sums = [sum_chips(p, o, chip_arr, name=f"sum_chips_{n}_{l}") for n, p, o in zip(WEIGHTS, parts, others)]
        done = share_halves(sums, name=f"share_halves_{l}")
        for n, g in zip(WEIGHTS, done):
            grads[n].append(g)
    grad = {n: jnp.stack(grads[n]).reshape(params[n].shape) for n in WEIGHTS}

    total = gather_blocks(pack_small(cfg, loss, dmeta, dsmalls), reduce=True, name="reduce_small")
    loss_sum, dmeta_sum, dsmall_sum = unpack_small(cfg, total)
    for k in _small_rows(cfg):
        grad[k] = jnp.stack([ds[k] for ds in dsmall_sum])
    grad["conv_w"] = lax.dynamic_slice_in_dim(grad["conv_w"], chip * (cfg.width // N_CHIPS), cfg.width // N_CHIPS, axis=2)
    grad["meta"] = lax.dynamic_slice_in_dim(dmeta_sum, chip * (cfg.d // N_CHIPS), cfg.d // N_CHIPS, axis=1)

    delta, new_m, new_v = {}, {}, {}
    for n in names:
        shp = params[n].shape
        two_d = lambda a: a.reshape(-1, shp[-1])
        dl, nm, nv = adamw(two_d(params[n]), two_d(grad[n]), two_d(mom1[n]), two_d(mom2[n]), name=f"adamw_{n}")
        delta[n], new_m[n], new_v[n] = dl.reshape(shp), nm.reshape(shp), nv.reshape(shp)

    return (loss_sum, grad_x[None], *[grad[n] for n in names], *[delta[n] for n in names], *[new_m[n] for n in names],
            *[new_v[n] for n in names])
```

```python
import functools
from typing import NamedTuple

import jax
import jax.numpy as jnp
from jax import lax
from jax.experimental import pallas as pl
from jax.experimental.pallas import tpu as pltpu

F32 = jnp.float32
BF16 = jnp.bfloat16
MESH = pl.DeviceIdType.MESH

EPS = 1e-6
NEG_INF = -1e30
ROPE_THETA = 10000.0
LANES = 128
ROPE = 64
N_CHIPS = 4
N_DEV = 8

ADAM_LR = 0.001
ADAM_B1 = 0.9
ADAM_B2 = 0.999
ADAM_EPS = 1e-08
ADAM_WD = 0.01
ADAM_STEP = 10

VMEM_LIMIT_BYTES = 48 * 1024 * 1024


class Cfg(NamedTuple):
    d: int = 2048
    seq: int = 2048
    depth: int = 4
    n_meta: int = 16
    heads: int = 8
    q_rank: int = 512
    kv_rank: int = 512
    d_ff: int = 5632

    @property
    def width(self):
        return self.heads * LANES

    @property
    def pad(self):
        return (-(self.n_meta + self.seq)) % LANES

    @property
    def m(self):
        return self.pad + self.n_meta + self.seq

    @property
    def nat_splits(self):
        w = self.width
        return (self.q_rank, self.kv_rank, ROPE, w, w, w, w, w, w, self.heads, 3 * self.d)

    @property
    def d_in(self):
        return sum(self.nat_splits)

    @property
    def off_cq(self):
        return 3 * self.d

    @property
    def off_ckv(self):
        return self.off_cq + self.q_rank

    @property
    def off_conv(self):
        return self.off_ckv + self.kv_rank

    @property
    def off_fox(self):
        return self.off_conv + 3 * self.width

    @property
    def off_kpe(self):
        return self.off_fox + 3 * self.width

    @property
    def off_fl(self):
        return self.off_kpe + LANES

    @property
    def d_inp(self):
        return -(-(self.off_fl + LANES) // 512) * 512


CFG = Cfg()


def _tile(n, target, mult=LANES):
    best = None
    t = mult
    while t <= min(n, target):
        if n % t == 0:
            best = t
        t += mult
    return best or n


def _cparams(*sem):
    return pltpu.CompilerParams(dimension_semantics=sem, vmem_limit_bytes=VMEM_LIMIT_BYTES)


def pack_w_in(cfg, w):
    cq, ckv, kpe, cb, cc, cx, fq, fk, fv, fl, gate = jnp.split(w, list(_cumsum(cfg.nat_splits))[:-1], axis=1)
    z = lambda n: jnp.zeros((w.shape[0], n), w.dtype)
    tail = cfg.d_inp - cfg.off_fl - cfg.heads
    return jnp.concatenate([gate, cq, ckv, cb, cc, cx, fq, fk, fv, kpe, z(LANES - ROPE), fl, z(tail)], axis=1)


def unpack_w_in(cfg, wp):
    w = cfg.width
    sizes = (3 * cfg.d, cfg.q_rank, cfg.kv_rank, w, w, w, w, w, w, ROPE, LANES - ROPE, cfg.heads, cfg.d_inp - cfg.off_fl - cfg.heads)
    gate, cq, ckv, cb, cc, cx, fq, fk, fv, kpe, _, fl, _ = jnp.split(wp, list(_cumsum(sizes))[:-1], axis=1)
    return jnp.concatenate([cq, ckv, kpe, cb, cc, cx, fq, fk, fv, fl, gate], axis=1)


def _cumsum(xs):
    out, s = [], 0
    for v in xs:
        s += v
        out.append(s)
    return out


def pack_w_uq(cfg, w):
    r = w.shape[0]
    w3 = w.reshape(r, cfg.heads, LANES + ROPE)
    w3 = jnp.pad(w3, ((0, 0), (0, 0), (0, LANES - ROPE)))
    return w3.reshape(r, cfg.heads * 2 * LANES)


def unpack_w_uq(cfg, wp):
    r = wp.shape[0]
    return wp.reshape(r, cfg.heads, 2 * LANES)[:, :, : LANES + ROPE].reshape(r, cfg.heads * (LANES + ROPE))


def pack_w_ukv(cfg, w):
    r = w.shape[0]
    w4 = w.reshape(r, cfg.heads, 2, LANES)
    return jnp.transpose(w4, (0, 2, 1, 3)).reshape(r, 2 * cfg.heads * LANES)


def unpack_w_ukv(cfg, wp):
    r = wp.shape[0]
    w4 = wp.reshape(r, 2, cfg.heads, LANES)
    return jnp.transpose(w4, (0, 2, 1, 3)).reshape(r, 2 * cfg.heads * LANES)


_DIMS = {
    "nn": (((1,), (0,)), ((), ())),
    "nt": (((1,), (1,)), ((), ())),
    "tn": (((0,), (0,)), ((), ())),
}


def matmul(a, b, mode, out_dtype, *, tm, tn, tk, name):
    batched = a.ndim == 3
    if mode == "nn":
        (m, kc), n = a.shape[-2:], b.shape[-1]
        a_blk, a_idx = (tm, tk), lambda i, j, k: (i, k)
        b_blk, b_idx = (tk, tn), lambda i, j, k: (k, j)
    elif mode == "nt":
        (m, kc), n = a.shape[-2:], b.shape[-2]
        a_blk, a_idx = (tm, tk), lambda i, j, k: (i, k)
        b_blk, b_idx = (tn, tk), lambda i, j, k: (j, k)
    else:
        (kc, m), n = a.shape[-2:], b.shape[-1]
        a_blk, a_idx = (tk, tm), lambda i, j, k: (k, i)
        b_blk, b_idx = (tk, tn), lambda i, j, k: (k, j)
    assert m % tm == 0 and n % tn == 0 and kc % tk == 0, (name, m, n, kc, tm, tn, tk)
    nk = kc // tk
    dims = _DIMS[mode]
    o_blk, o_idx = (tm, tn), lambda i, j, k: (i, j)
    grid = (m // tm, n // tn, nk)
    if batched:
        nb = a.shape[0]
        grid = (nb,) + grid
        wrap = lambda f: (lambda bb, i, j, k: (bb,) + f(i, j, k))
        a_blk, b_blk, o_blk = (None,) + a_blk, (None,) + b_blk, (None,) + o_blk
        a_idx, b_idx, o_idx = wrap(a_idx), wrap(b_idx), wrap(o_idx)
        out_shape = (nb, m, n)
        sem = ("parallel", "parallel", "parallel", "arbitrary")
    else:
        out_shape = (m, n)
        sem = ("parallel", "parallel", "arbitrary")
    k_axis = len(grid) - 1

    def body(a_ref, b_ref, o_ref, *scratch):
        prod = lax.dot_general(a_ref[...], b_ref[...], dims, preferred_element_type=F32)
        if nk == 1:
            o_ref[...] = prod.astype(o_ref.dtype)
        else:
            acc_ref = scratch[0]
            k = pl.program_id(k_axis)

            @pl.when(k == 0)
            def _():
                acc_ref[...] = prod

            @pl.when(k > 0)
            def _():
                acc_ref[...] += prod

            @pl.when(k == nk - 1)
            def _():
                o_ref[...] = acc_ref[...].astype(o_ref.dtype)

    return pl.pallas_call(
        body,
        name=name,
        out_shape=jax.ShapeDtypeStruct(out_shape, out_dtype),
        grid=grid,
        in_specs=[pl.BlockSpec(a_blk, a_idx), pl.BlockSpec(b_blk, b_idx)],
        out_specs=pl.BlockSpec(o_blk, o_idx),
        scratch_shapes=[] if nk == 1 else [pltpu.VMEM((tm, tn), F32)],
        compiler_params=_cparams(*sem),
    )(a, b)


def _row_tile(m):
    return _tile(m, 272, 16)


def rmsnorm_fwd(x, g, out_dtype, *, name, width=None, col_blk=0, res=None):
    m = x.shape[0]
    n = width or x.shape[1]
    tm = _row_tile(m)
    has_res = res is not None

    def body(x_ref, g_ref, *rest):
        o_ref = rest[-1]
        xf = x_ref[...].astype(F32)
        r = lax.rsqrt(jnp.mean(xf * xf, axis=-1, keepdims=True) + EPS)
        y = xf * r * g_ref[...]
        if has_res:
            y = rest[0][...] + y
        o_ref[...] = y.astype(o_ref.dtype)

    in_specs = [pl.BlockSpec((tm, n), lambda i: (i, col_blk)), pl.BlockSpec((1, n), lambda i: (0, 0))]
    args = [x, g.reshape(1, n)]
    if has_res:
        in_specs.append(pl.BlockSpec((tm, n), lambda i: (i, 0)))
        args.append(res)
    return pl.pallas_call(
        body,
        name=name,
        out_shape=jax.ShapeDtypeStruct((m, n), out_dtype),
        grid=(m // tm,),
        in_specs=in_specs,
        out_specs=pl.BlockSpec((tm, n), lambda i: (i, 0)),
        compiler_params=_cparams("parallel"),
    )(*args)


def rmsnorm_bwd(x, g, dy, out_dtype, *, name, width=None, col_blk=0, dres=None):
    m = x.shape[0]
    n = width or x.shape[1]
    tm = _row_tile(m)
    has_res = dres is not None

    def body(x_ref, g_ref, dy_ref, *rest):
        dx_ref, dg_ref = rest[-2:]
        i = pl.program_id(0)
        xf = x_ref[...].astype(F32)
        r = lax.rsqrt(jnp.mean(xf * xf, axis=-1, keepdims=True) + EPS)
        xhat = xf * r
        dyf = dy_ref[...].astype(F32)
        dxh = dyf * g_ref[...]
        dx = r * (dxh - xhat * jnp.mean(dxh * xhat, axis=-1, keepdims=True))
        if has_res:
            dx = dx + rest[0][...]
        dx_ref[...] = dx.astype(dx_ref.dtype)
        part = jnp.sum(dyf * xhat, axis=0, keepdims=True)

        @pl.when(i == 0)
        def _():
            dg_ref[...] = part

        @pl.when(i > 0)
        def _():
            dg_ref[...] += part

    in_specs = [
        pl.BlockSpec((tm, n), lambda i: (i, col_blk)),
        pl.BlockSpec((1, n), lambda i: (0, 0)),
        pl.BlockSpec((tm, n), lambda i: (i, 0)),
    ]
    args = [x, g.reshape(1, n), dy]
    if has_res:
        in_specs.append(pl.BlockSpec((tm, n), lambda i: (i, 0)))
        args.append(dres)
    return pl.pallas_call(
        body,
        name=name,
        out_shape=(jax.ShapeDtypeStruct((m, n), out_dtype), jax.ShapeDtypeStruct((1, n), F32)),
        grid=(m // tm,),
        in_specs=in_specs,
        out_specs=(pl.BlockSpec((tm, n), lambda i: (i, 0)), pl.BlockSpec((1, n), lambda i: (0, 0))),
        compiler_params=_cparams("arbitrary"),
    )(*args)


_NT = (((1,), (1,)), ((), ()))
_NN = (((1,), (0,)), ((), ()))
_TN = (((0,), (0,)), ((), ()))


def _attn_scores(q, k, scale, decay_refs, i, tq, kn, pad):
    s = lax.dot_general(q, k, _NT, preferred_element_type=F32) * scale
    if decay_refs is not None:
        cq_ref, ck_ref = decay_refs
        s = s + (cq_ref[0] - ck_ref[0][:, :kn])
    t_idx = i * tq + lax.broadcasted_iota(jnp.int32, (tq, 1), 0)
    s_idx = lax.broadcasted_iota(jnp.int32, (1, kn), 1)
    mask = (s_idx <= t_idx) & (s_idx >= pad)
    return s, mask, t_idx


def _keys_needed(i, tq, m):
    return min(m, -(-((i + 1) * tq) // LANES) * LANES)


def attn_fwd(q, k, v, *, heads, dk, dv, qblk0, kblk0, vblk0, scale, pad, decay=None, name):
    m = q.shape[0]
    tq = _row_tile(m)
    has_decay = decay is not None

    def body(q_ref, k_ref, v_ref, *rest):
        o_ref, lse_ref = rest[-2:]
        decay_refs = rest[:2] if has_decay else None

        def block(i):
            kn = _keys_needed(i, tq, m)
            s, mask, t_idx = _attn_scores(q_ref[...], k_ref[0:kn, :], scale, decay_refs, i, tq, kn, pad)
            s = jnp.where(mask, s, NEG_INF)
            mx = jnp.max(s, axis=1, keepdims=True)
            p = jnp.exp(s - mx)
            l = jnp.sum(p, axis=1, keepdims=True)
            o = lax.dot_general(p.astype(BF16), v_ref[0:kn, :], _NN, preferred_element_type=F32) / l
            o_ref[...] = jnp.where(t_idx >= pad, o, 0.0).astype(o_ref.dtype)
            lse_ref[0] = mx + jnp.log(l)

        for i in range(m // tq):
            pl.when(pl.program_id(1) == i)(functools.partial(block, i))

    in_specs = [
        pl.BlockSpec((tq, dk), lambda h, i: (i, qblk0 + h)),
        pl.BlockSpec((m, dk), lambda h, i: (0, kblk0 + h)),
        pl.BlockSpec((m, dv), lambda h, i: (0, vblk0 + h)),
    ]
    args = [q, k, v]
    if has_decay:
        in_specs += [pl.BlockSpec((1, tq, 1), lambda h, i: (h, i, 0)), pl.BlockSpec((1, 1, m), lambda h, i: (h, 0, 0))]
        args += list(decay)
    return pl.pallas_call(
        body,
        name=name,
        out_shape=(jax.ShapeDtypeStruct((m, heads * dv), BF16), jax.ShapeDtypeStruct((heads, m, 1), F32)),
        grid=(heads, m // tq),
        in_specs=in_specs,
        out_specs=(pl.BlockSpec((tq, dv), lambda h, i: (i, h)), pl.BlockSpec((1, tq, 1), lambda h, i: (h, i, 0))),
        compiler_params=_cparams("parallel", "parallel"),
    )(*args)


def attn_bwd(q, k, v, do, do_sel, lse, *, heads, dk, dv, qblk0, kblk0, vblk0, scale, pad, decay=None, name):
    m = q.shape[0]
    tq = _row_tile(m)
    nq = m // tq
    has_decay = decay is not None

    def body(q_ref, k_ref, v_ref, do_ref, lse_ref, *rest):
        if has_decay:
            cq_ref, ck_ref, dq_ref, dk_ref, dv_ref, dck_ref, dk_acc, dv_acc = rest
            decay_refs = (cq_ref, ck_ref)
        else:
            dq_ref, dk_ref, dv_ref, dk_acc, dv_acc = rest
            decay_refs = None
        @pl.when(pl.program_id(1) == 0)
        def _():
            dk_acc[...] = jnp.zeros_like(dk_acc)
            dv_acc[...] = jnp.zeros_like(dv_acc)
            if has_decay:
                dck_ref[...] = jnp.zeros_like(dck_ref)

        def block(i):
            kn = _keys_needed(i, tq, m)
            qb, kb, dob = q_ref[...], k_ref[0:kn, :], do_ref[...]
            s, mask, _ = _attn_scores(qb, kb, scale, decay_refs, i, tq, kn, pad)
            p = jnp.where(mask, jnp.exp(s - lse_ref[0]), 0.0)
            dp = lax.dot_general(dob, v_ref[0:kn, :], _NT, preferred_element_type=F32)
            ds = p * (dp - jnp.sum(p * dp, axis=1, keepdims=True))
            dsb = ds.astype(BF16)
            dq_ref[...] = (lax.dot_general(dsb, kb, _NN, preferred_element_type=F32) * scale).astype(dq_ref.dtype)
            dk_acc[0:kn, :] += lax.dot_general(dsb, qb, _TN, preferred_element_type=F32) * scale
            dv_acc[0:kn, :] += lax.dot_general(p.astype(BF16), dob, _TN, preferred_element_type=F32)
            if has_decay:
                dck_ref[0, :, 0:kn] -= jnp.sum(ds, axis=0, keepdims=True)

        for i in range(nq):
            pl.when(pl.program_id(1) == i)(functools.partial(block, i))

        @pl.when(pl.program_id(1) == nq - 1)
        def _():
            dk_ref[...] = dk_acc[...].astype(dk_ref.dtype)
            dv_ref[...] = dv_acc[...].astype(dv_ref.dtype)

    in_specs = [
        pl.BlockSpec((tq, dk), lambda h, i: (i, qblk0 + h)),
        pl.BlockSpec((m, dk), lambda h, i: (0, kblk0 + h)),
        pl.BlockSpec((m, dv), lambda h, i: (0, vblk0 + h)),
        pl.BlockSpec((None, tq, dv), lambda h, i: (do_sel, i, h)),
        pl.BlockSpec((1, tq, 1), lambda h, i: (h, i, 0)),
    ]
    args = [q, k, v, do, lse]
    out_shape = [
        jax.ShapeDtypeStruct((m, heads * dk), BF16),
        jax.ShapeDtypeStruct((m, heads * dk), BF16),
        jax.ShapeDtypeStruct((m, heads * dv), BF16),
    ]
    out_specs = [
        pl.BlockSpec((tq, dk), lambda h, i: (i, h)),
        pl.BlockSpec((m, dk), lambda h, i: (0, h)),
        pl.BlockSpec((m, dv), lambda h, i: (0, h)),
    ]
    if has_decay:
        in_specs += [pl.BlockSpec((1, tq, 1), lambda h, i: (h, i, 0)), pl.BlockSpec((1, 1, m), lambda h, i: (h, 0, 0))]
        args += list(decay)
        out_shape.append(jax.ShapeDtypeStruct((heads, 1, m), F32))
        out_specs.append(pl.BlockSpec((1, 1, m), lambda h, i: (h, 0, 0)))
    return pl.pallas_call(
        body,
        name=name,
        out_shape=tuple(out_shape),
        grid=(heads, nq),
        in_specs=in_specs,
        out_specs=tuple(out_specs),
        scratch_shapes=[pltpu.VMEM((m, dk), F32), pltpu.VMEM((m, dv), F32)],
        compiler_params=_cparams("parallel", "arbitrary"),
    )(*args)


def rope_tables(cfg):
    half = ROPE // 2
    inv_freq = 1.0 / (ROPE_THETA ** (jnp.arange(0, ROPE, 2, dtype=F32) / ROPE))
    pos = (jnp.arange(cfg.m, dtype=jnp.int32) - cfg.pad).astype(F32)
    ang = pos[:, None] * inv_freq[None, :]
    cos, sin = jnp.cos(ang), jnp.sin(ang)
    z = jnp.zeros((cfg.m, half), F32)
    zz = jnp.zeros((cfg.m, LANES - ROPE), F32)
    return (
        jnp.concatenate([cos, cos, zz], axis=1),
        jnp.concatenate([-sin, z, zz], axis=1),
        jnp.concatenate([z, sin, zz], axis=1),
    )


def _rope(x, cos, s1, s2):
    return x * cos + pltpu.roll(x, LANES - ROPE // 2, 1) * s1 + pltpu.roll(x, ROPE // 2, 1) * s2


def mla_prep_fwd(cfg, q, kv, proj, tabs, *, name):
    m, h2 = cfg.m, 2 * LANES
    tm = _tile(m, 544, 16)
    kpe_blk = cfg.off_kpe // LANES

    def body(q_ref, kn_ref, kpe_ref, cos_ref, s1_ref, s2_ref, qf_ref, kf_ref):
        cos, s1, s2 = cos_ref[...], s1_ref[...], s2_ref[...]
        qv = q_ref[...]
        qf_ref[:, :LANES] = qv[:, :LANES]
        qf_ref[:, LANES:] = _rope(qv[:, LANES:].astype(F32), cos, s1, s2).astype(qf_ref.dtype)
        kf_ref[:, :LANES] = kn_ref[...]
        kf_ref[:, LANES:] = _rope(kpe_ref[...].astype(F32), cos, s1, s2).astype(kf_ref.dtype)

    tab = pl.BlockSpec((tm, LANES), lambda i, h: (i, 0))
    return pl.pallas_call(
        body,
        name=name,
        out_shape=(jax.ShapeDtypeStruct((m, cfg.heads * h2), BF16), jax.ShapeDtypeStruct((m, cfg.heads * h2), BF16)),
        grid=(m // tm, cfg.heads),
        in_specs=[
            pl.BlockSpec((tm, h2), lambda i, h: (i, h)),
            pl.BlockSpec((tm, LANES), lambda i, h: (i, h)),
            pl.BlockSpec((tm, LANES), lambda i, h: (i, kpe_blk)),
            tab, tab, tab,
        ],
        out_specs=(pl.BlockSpec((tm, h2), lambda i, h: (i, h)), pl.BlockSpec((tm, h2), lambda i, h: (i, h))),
        compiler_params=_cparams("parallel", "parallel"),
    )(q, kv, proj, *tabs)


def mla_prep_bwd(cfg, dqf, dkf, tabs_t, *, name):
    m, h2 = cfg.m, 2 * LANES
    tm = _tile(m, 544, 16)

    def body(dqf_ref, dkf_ref, cos_ref, s1_ref, s2_ref, dq_ref, dkn_ref, dkpe_ref):
        h = pl.program_id(1)
        cos, s1, s2 = cos_ref[...], s1_ref[...], s2_ref[...]
        dqv, dkv = dqf_ref[...], dkf_ref[...]
        dq_ref[:, :LANES] = dqv[:, :LANES]
        dq_ref[:, LANES:] = _rope(dqv[:, LANES:].astype(F32), cos, s1, s2).astype(dq_ref.dtype)
        dkn_ref[...] = dkv[:, :LANES]
        part = _rope(dkv[:, LANES:].astype(F32), cos, s1, s2)

        @pl.when(h == 0)
        def _():
            dkpe_ref[...] = part

        @pl.when(h > 0)
        def _():
            dkpe_ref[...] += part

    tab = pl.BlockSpec((tm, LANES), lambda i, h: (i, 0))
    return pl.pallas_call(
        body,
        name=name,
        out_shape=(
            jax.ShapeDtypeStruct((m, cfg.heads * h2), BF16),
            jax.ShapeDtypeStruct((m, cfg.heads * LANES), BF16),
            jax.ShapeDtypeStruct((m, LANES), F32),
        ),
        grid=(m // tm, cfg.heads),
        in_specs=[pl.BlockSpec((tm, h2), lambda i, h: (i, h)), pl.BlockSpec((tm, h2), lambda i, h: (i, h)), tab, tab, tab],
        out_specs=(
            pl.BlockSpec((tm, h2), lambda i, h: (i, h)),
            pl.BlockSpec((tm, LANES), lambda i, h: (i, h)),
            pl.BlockSpec((tm, LANES), lambda i, h: (i, 0)),
        ),
        compiler_params=_cparams("parallel", "arbitrary"),
    )(dqf, dkf, *tabs_t)


def _conv_parts(b_ref, c_ref, x_ref, w_ref, m):
    b, c, x = b_ref[...].astype(F32), c_ref[...].astype(F32), x_ref[...].astype(F32)
    u = c * x
    row = lax.broadcasted_iota(jnp.int32, (m, 1), 0)
    u1 = jnp.where(row >= 1, pltpu.roll(u, 1, 0), 0.0)
    u2 = jnp.where(row >= 2, pltpu.roll(u, 2, 0), 0.0)
    w0, w1, w2 = w_ref[0:1, :], w_ref[1:2, :], w_ref[2:3, :]
    uc = w0 * u2 + w1 * u1 + w2 * u
    return b, c, x, u, u1, u2, uc, (w0, w1, w2), row


def _conv_specs(cfg, tn):
    m, nb, blk0 = cfg.m, cfg.width // tn, cfg.off_conv // tn
    return [
        pl.BlockSpec((m, tn), lambda j: (0, blk0 + j)),
        pl.BlockSpec((m, tn), lambda j: (0, blk0 + nb + j)),
        pl.BlockSpec((m, tn), lambda j: (0, blk0 + 2 * nb + j)),
        pl.BlockSpec((3, tn), lambda j: (0, j)),
    ]


def conv_fwd(cfg, proj, conv_w, *, name):
    m, tn = cfg.m, LANES

    def body(b_ref, c_ref, x_ref, w_ref, o_ref):
        b, _, _, _, _, _, uc, _, _ = _conv_parts(b_ref, c_ref, x_ref, w_ref, m)
        o_ref[...] = (b * uc).astype(o_ref.dtype)

    return pl.pallas_call(
        body,
        name=name,
        out_shape=jax.ShapeDtypeStruct((m, cfg.width), BF16),
        grid=(cfg.width // tn,),
        in_specs=_conv_specs(cfg, tn),
        out_specs=pl.BlockSpec((m, tn), lambda j: (0, j)),
        compiler_params=_cparams("parallel"),
    )(proj, proj, proj, conv_w)


def conv_bwd(cfg, proj, conv_w, do, do_sel, *, name):
    m, tn = cfg.m, LANES

    def body(b_ref, c_ref, x_ref, w_ref, do_ref, db_ref, dc_ref, dx_ref, dw_ref):
        b, c, x, u, u1, u2, uc, (w0, w1, w2), row = _conv_parts(b_ref, c_ref, x_ref, w_ref, m)
        dob = do_ref[...].astype(F32)
        db_ref[...] = (dob * uc).astype(db_ref.dtype)
        duc = dob * b
        up1 = jnp.where(row <= m - 2, pltpu.roll(duc, m - 1, 0), 0.0)
        up2 = jnp.where(row <= m - 3, pltpu.roll(duc, m - 2, 0), 0.0)
        du = w2 * duc + w1 * up1 + w0 * up2
        dc_ref[...] = (du * x).astype(dc_ref.dtype)
        dx_ref[...] = (du * c).astype(dx_ref.dtype)
        dw_ref[0:1, :] = jnp.sum(duc * u2, axis=0, keepdims=True)
        dw_ref[1:2, :] = jnp.sum(duc * u1, axis=0, keepdims=True)
        dw_ref[2:3, :] = jnp.sum(duc * u, axis=0, keepdims=True)

    act = jax.ShapeDtypeStruct((m, cfg.width), BF16)
    blk = pl.BlockSpec((m, tn), lambda j: (0, j))
    return pl.pallas_call(
        body,
        name=name,
        out_shape=(act, act, act, jax.ShapeDtypeStruct((3, cfg.width), F32)),
        grid=(cfg.width // tn,),
        in_specs=_conv_specs(cfg, tn) + [pl.BlockSpec((None, m, tn), lambda j: (do_sel, 0, j))],
        out_specs=(blk, blk, blk, pl.BlockSpec((3, tn), lambda j: (0, j))),
        compiler_params=_cparams("parallel"),
    )(proj, proj, proj, conv_w, do)


def _tri(lower):
    r = lax.broadcasted_iota(jnp.int32, (LANES, LANES), 0)
    c = lax.broadcasted_iota(jnp.int32, (LANES, LANES), 1)
    return jnp.where((r >= c) if lower else (r <= c), 1.0, 0.0).astype(F32)


def fox_gate_fwd(cfg, fl, b_pad, *, name):
    m = cfg.m
    nblk = m // LANES

    def body(fl_ref, b_ref, c_ref):
        z = fl_ref[...] + b_ref[...]
        logf = jnp.minimum(z, 0.0) - jnp.log(1.0 + jnp.exp(-jnp.abs(z)))
        row = lax.broadcasted_iota(jnp.int32, (m, 1), 0)
        logf = jnp.where(row >= cfg.pad, logf, 0.0)
        tri = _tri(True)
        carry = jnp.zeros((1, LANES), F32)
        for blk in range(nblk):
            cb = jnp.dot(tri, logf[blk * LANES:(blk + 1) * LANES, :], precision=lax.Precision.HIGHEST,
                         preferred_element_type=F32) + carry
            c_ref[blk * LANES:(blk + 1) * LANES, :] = cb
            carry = cb[LANES - 1:LANES, :]

    full = pl.BlockSpec((m, LANES), lambda: (0, 0))
    return pl.pallas_call(
        body,
        name=name,
        out_shape=jax.ShapeDtypeStruct((m, LANES), F32),
        in_specs=[full, pl.BlockSpec((1, LANES), lambda: (0, 0))],
        out_specs=full,
        compiler_params=pltpu.CompilerParams(vmem_limit_bytes=VMEM_LIMIT_BYTES),
    )(fl, b_pad)


def fox_gate_bwd(cfg, fl, b_pad, dc, *, name):
    m = cfg.m
    nblk = m // LANES

    def body(fl_ref, b_ref, dc_ref, dfl_ref, db_ref):
        z = fl_ref[...] + b_ref[...]
        dlogsig = 1.0 / (1.0 + jnp.exp(z))
        row = lax.broadcasted_iota(jnp.int32, (m, 1), 0)
        gate = jnp.where(row >= cfg.pad, dlogsig, 0.0)
        dcv = dc_ref[...]
        tri = _tri(False)
        carry = jnp.zeros((1, LANES), F32)
        db = jnp.zeros((1, LANES), F32)
        for blk in reversed(range(nblk)):
            sl = slice(blk * LANES, (blk + 1) * LANES)
            rb = jnp.dot(tri, dcv[sl, :], precision=lax.Precision.HIGHEST, preferred_element_type=F32) + carry
            carry = rb[0:1, :]
            dfl = rb * gate[sl, :]
            dfl_ref[sl, :] = dfl
            db = db + jnp.sum(dfl, axis=0, keepdims=True)
        db_ref[...] = db

    full = pl.BlockSpec((m, LANES), lambda: (0, 0))
    one = pl.BlockSpec((1, LANES), lambda: (0, 0))
    return pl.pallas_call(
        body,
        name=name,
        out_shape=(jax.ShapeDtypeStruct((m, LANES), F32), jax.ShapeDtypeStruct((1, LANES), F32)),
        in_specs=[full, one, full],
        out_specs=(full, one),
        compiler_params=pltpu.CompilerParams(vmem_limit_bytes=VMEM_LIMIT_BYTES),
    )(fl, b_pad, dc)


def _sigmoid(x):
    return 1.0 / (1.0 + jnp.exp(-x))


def gate_merge_fwd(cfg, y, proj, *, name):
    m, d = cfg.m, cfg.d
    tm, tn = _tile(m, 1088, 16), _tile(d, 512)
    nd = d // tn

    def body(y_ref, g0_ref, g1_ref, g2_ref, o_ref):
        acc = None
        for n, g_ref in enumerate((g0_ref, g1_ref, g2_ref)):
            t = _sigmoid(g_ref[...].astype(F32)) * y_ref[n].astype(F32)
            acc = t if acc is None else acc + t
        o_ref[...] = acc.astype(o_ref.dtype)

    gate = lambda n: pl.BlockSpec((tm, tn), lambda i, j: (i, n * nd + j))
    return pl.pallas_call(
        body,
        name=name,
        out_shape=jax.ShapeDtypeStruct((m, d), BF16),
        grid=(m // tm, nd),
        in_specs=[pl.BlockSpec((3, tm, tn), lambda i, j: (0, i, j)), gate(0), gate(1), gate(2)],
        out_specs=pl.BlockSpec((tm, tn), lambda i, j: (i, j)),
        compiler_params=_cparams("parallel", "parallel"),
    )(y, proj, proj, proj)


def gate_merge_bwd(cfg, dm, y, proj, *, name):
    m, d = cfg.m, cfg.d
    tm, tn = _tile(m, 1088, 16), _tile(d, 512)
    nd = d // tn

    def body(dm_ref, y_ref, g_ref, dy_ref, dg_ref):
        sg = _sigmoid(g_ref[...].astype(F32))
        dmv = dm_ref[...].astype(F32)
        dy_ref[...] = (sg * dmv).astype(dy_ref.dtype)
        dg_ref[...] = (dmv * y_ref[...].astype(F32) * sg * (1.0 - sg)).astype(dg_ref.dtype)

    return pl.pallas_call(
        body,
        name=name,
        out_shape=(jax.ShapeDtypeStruct((3, m, d), BF16), jax.ShapeDtypeStruct((m, 3 * d), BF16)),
        grid=(m // tm, nd, 3),
        in_specs=[
            pl.BlockSpec((tm, tn), lambda i, j, n: (i, j)),
            pl.BlockSpec((None, tm, tn), lambda i, j, n: (n, i, j)),
            pl.BlockSpec((tm, tn), lambda i, j, n: (i, n * nd + j)),
        ],
        out_specs=(
            pl.BlockSpec((None, tm, tn), lambda i, j, n: (n, i, j)),
            pl.BlockSpec((tm, tn), lambda i, j, n: (i, n * nd + j)),
        ),
        compiler_params=_cparams("parallel", "parallel", "parallel"),
    )(dm, y, proj)


def swiglu_fwd(cfg, gu, *, name):
    m, f = cfg.m, cfg.d_ff
    tm, tn = _tile(m, 1088, 16), _tile(f, 512)
    nf = f // tn

    def body(g_ref, u_ref, o_ref):
        g = g_ref[...].astype(F32)
        o_ref[...] = (g * _sigmoid(g) * u_ref[...].astype(F32)).astype(o_ref.dtype)

    return pl.pallas_call(
        body,
        name=name,
        out_shape=jax.ShapeDtypeStruct((m, f), BF16),
        grid=(m // tm, nf),
        in_specs=[pl.BlockSpec((tm, tn), lambda i, j: (i, j)), pl.BlockSpec((tm, tn), lambda i, j: (i, nf + j))],
        out_specs=pl.BlockSpec((tm, tn), lambda i, j: (i, j)),
        compiler_params=_cparams("parallel", "parallel"),
    )(gu, gu)


def swiglu_bwd(cfg, dact, gu, *, name):
    m, f = cfg.m, cfg.d_ff
    tm, tn = _tile(m, 1088, 16), _tile(f, 512)
    nf = f // tn

    def body(da_ref, g_ref, u_ref, o_ref):
        j = pl.program_id(1)
        g, u, da = g_ref[...].astype(F32), u_ref[...].astype(F32), da_ref[...].astype(F32)
        sg = _sigmoid(g)
        dg = da * u * sg * (1.0 + g * (1.0 - sg))
        du = da * g * sg
        o_ref[...] = jnp.where(j < nf, dg, du).astype(o_ref.dtype)

    return pl.pallas_call(
        body,
        name=name,
        out_shape=jax.ShapeDtypeStruct((m, 2 * f), BF16),
        grid=(m // tm, 2 * nf),
        in_specs=[
            pl.BlockSpec((tm, tn), lambda i, j: (i, j % nf)),
            pl.BlockSpec((tm, tn), lambda i, j: (i, j % nf)),
            pl.BlockSpec((tm, tn), lambda i, j: (i, nf + j % nf)),
        ],
        out_specs=pl.BlockSpec((tm, tn), lambda i, j: (i, j)),
        compiler_params=_cparams("parallel", "parallel"),
    )(dact, gu, gu)


def loss_head(cfg, h, target, *, name):
    m, d = cfg.m, cfg.d
    assert cfg.pad + cfg.n_meta == LANES
    tm = LANES
    inv_d = 1.0 / d

    def body(h_ref, t_ref, dh_ref, loss_ref):
        i = pl.program_id(0)

        @pl.when(i == 0)
        def _():
            dh_ref[...] = jnp.zeros_like(dh_ref)
            loss_ref[...] = jnp.zeros_like(loss_ref)

        @pl.when(i > 0)
        def _():
            err = h_ref[...] - t_ref[...]
            dh_ref[...] = err * inv_d
            loss_ref[...] += 0.5 * inv_d * jnp.sum(err * err)

    return pl.pallas_call(
        body,
        name=name,
        out_shape=(jax.ShapeDtypeStruct((m, d), F32), jax.ShapeDtypeStruct((8, LANES), F32)),
        grid=(m // tm,),
        in_specs=[pl.BlockSpec((tm, d), lambda i: (i, 0)), pl.BlockSpec((tm, d), lambda i: (jnp.maximum(i - 1, 0), 0))],
        out_specs=(pl.BlockSpec((tm, d), lambda i: (i, 0)), pl.BlockSpec((8, LANES), lambda i: (0, 0))),
        compiler_params=_cparams("arbitrary"),
    )(h, target)


def adamw(w, g, m_, v_, *, name):
    r, c = w.shape
    c_pad = -(-c // LANES) * LANES
    tr = r
    if r % 8 == 0:
        tr = _tile(r, max(8, (3 << 19) // (4 * c_pad) // 8 * 8), 8)
    bc1 = 1.0 - ADAM_B1 ** ADAM_STEP
    bc2 = 1.0 - ADAM_B2 ** ADAM_STEP

    def body(w_ref, g_ref, m_ref, v_ref, d_ref, nm_ref, nv_ref):
        gv = g_ref[...]
        nm = ADAM_B1 * m_ref[...] + (1.0 - ADAM_B1) * gv
        nv = ADAM_B2 * v_ref[...] + (1.0 - ADAM_B2) * (gv * gv)
        d_ref[...] = -ADAM_LR * ((nm / bc1) / (jnp.sqrt(nv / bc2) + ADAM_EPS) + ADAM_WD * w_ref[...])
        nm_ref[...] = nm
        nv_ref[...] = nv

    blk = pl.BlockSpec((tr, c), lambda i: (i, 0))
    shp = jax.ShapeDtypeStruct((r, c), F32)
    return pl.pallas_call(
        body,
        name=name,
        out_shape=(shp, shp, shp),
        grid=(r // tr,),
        in_specs=[blk, blk, blk, blk],
        out_specs=(blk, blk, blk),
        compiler_params=_cparams("parallel"),
    )(w, g, m_, v_)


def adamw_halves(w, g_own, g_other, core, m_, v_, *, name):
    nl, r, c = w.shape
    r2 = r // 2
    c_pad = -(-c // LANES) * LANES
    tr = _tile(r2, max(8, (3 << 19) // (4 * c_pad) // 8 * 8), 8)
    nr = r2 // tr
    bc1 = 1.0 - ADAM_B1 ** ADAM_STEP
    bc2 = 1.0 - ADAM_B2 ** ADAM_STEP

    def body(core_ref, w_ref, go_ref, gr_ref, m_ref, v_ref, g_ref, d_ref, nm_ref, nv_ref):
        gv = jnp.where(pl.program_id(2) == core_ref[0], go_ref[...], gr_ref[...])
        nm = ADAM_B1 * m_ref[...] + (1.0 - ADAM_B1) * gv
        nv = ADAM_B2 * v_ref[...] + (1.0 - ADAM_B2) * (gv * gv)
        d_ref[...] = -ADAM_LR * ((nm / bc1) / (jnp.sqrt(nv / bc2) + ADAM_EPS) + ADAM_WD * w_ref[...])
        g_ref[...] = gv
        nm_ref[...] = nm
        nv_ref[...] = nv

    full = pl.BlockSpec((None, tr, c), lambda l, i, hf, core_ref: (l, hf * nr + i, 0))
    half = pl.BlockSpec((None, tr, c), lambda l, i, hf, core_ref: (l, i, 0))
    shp = jax.ShapeDtypeStruct((nl, r, c), F32)
    return pl.pallas_call(
        body,
        name=name,
        out_shape=(shp, shp, shp, shp),
        grid_spec=pltpu.PrefetchScalarGridSpec(
            num_scalar_prefetch=1,
            grid=(nl, nr, 2),
            in_specs=[full, half, half, full, full],
            out_specs=(full, full, full, full),
        ),
        compiler_params=_cparams("parallel", "parallel", "arbitrary"),
    )(core, w, g_own, g_other, m_, v_)


_HBM = pl.BlockSpec(memory_space=pltpu.HBM)


def _place():
    x, y, c = lax.axis_index("x"), lax.axis_index("y"), lax.axis_index("c")
    flips = [(1 - x, y), (x, 1 - y), (1 - x, 1 - y)]
    return x, y, c, flips


def gather_weights(shards, *, name):
    nw = len(shards)
    halves = [s.shape[0] // 2 for s in shards]
    assert all(s.shape[0] % 32 == 0 for s in shards)

    def body(*refs):
        ins, outs = refs[:nw], refs[nw:2 * nw]
        send_sems, recv_sems = refs[2 * nw:]
        x, y, c, flips = _place()
        mine = 2 * x + y
        sibling = (x, y, 1 - c)

        def half(w, chip, core):
            return outs[w].at[chip, pl.ds(core * halves[w], halves[w]), :]

        def copy(w, k, src, chip, core, to):
            return pltpu.make_async_remote_copy(src_ref=src, dst_ref=half(w, chip, core), send_sem=send_sems.at[6 * w + k],
                                                recv_sem=recv_sems.at[6 * w + k], device_id=to, device_id_type=MESH)

        sent = []
        for w in range(nw):
            src = ins[w].at[pl.ds(c * halves[w], halves[w]), :]
            for k, (fx, fy) in enumerate(flips):
                sent.append(copy(w, k, src, mine, c, (fx, fy, c)))
                sent[-1].start()
        for w in range(nw):
            for k, (fx, fy) in enumerate(flips):
                theirs = 2 * fx + fy
                copy(w, k, half(w, theirs, c), theirs, c, sibling).wait_recv()
                sent.append(copy(w, 3 + k, half(w, theirs, c), theirs, c, sibling))
                sent[-1].start()
        for w in range(nw):
            for k, (fx, fy) in enumerate(flips):
                theirs = 2 * fx + fy
                copy(w, 3 + k, half(w, theirs, 1 - c), theirs, 1 - c, sibling).wait_recv()
        for cp in sent:
            cp.wait_send()

    return pl.pallas_call(
        body,
        name=name,
        out_shape=tuple(jax.ShapeDtypeStruct((N_CHIPS,) + s.shape, s.dtype) for s in shards),
        in_specs=[_HBM] * nw,
        out_specs=tuple([_HBM] * nw),
        scratch_shapes=[pltpu.SemaphoreType.DMA((6 * nw,)), pltpu.SemaphoreType.DMA((6 * nw,))],
    )(*shards)


def swap_halves(grads, *, name):
    nw = len(grads)
    halves = [g.shape[1] // 2 for g in grads]

    def body(*refs):
        ins, outs = refs[:nw], refs[nw:2 * nw]
        send_sems, recv_sems = refs[2 * nw:]
        x, y, c, _ = _place()
        copies = [
            pltpu.make_async_remote_copy(src_ref=ins[w].at[:, pl.ds((1 - c) * halves[w], halves[w]), :], dst_ref=outs[w],
                                         send_sem=send_sems.at[w], recv_sem=recv_sems.at[w], device_id=(x, y, 1 - c),
                                         device_id_type=MESH)
            for w in range(nw)
        ]
        for cp in copies:
            cp.start()
        for cp in copies:
            cp.wait()

    return pl.pallas_call(
        body,
        name=name,
        out_shape=tuple(jax.ShapeDtypeStruct((N_CHIPS, h, g.shape[2]), g.dtype) for g, h in zip(grads, halves)),
        in_specs=[_HBM] * nw,
        out_specs=tuple([_HBM] * nw),
        scratch_shapes=[pltpu.SemaphoreType.DMA((nw,)), pltpu.SemaphoreType.DMA((nw,))],
    )(*grads)


def exchange_chips(parts, *, name):
    nw = len(parts)

    def body(*refs):
        ins, outs = refs[:nw], refs[nw:2 * nw]
        send_sems, recv_sems = refs[2 * nw:]
        _, _, c, flips = _place()
        copies = [
            pltpu.make_async_remote_copy(src_ref=ins[w].at[2 * fx + fy], dst_ref=outs[w].at[k], send_sem=send_sems.at[3 * w + k],
                                         recv_sem=recv_sems.at[3 * w + k], device_id=(fx, fy, c), device_id_type=MESH)
            for w in range(nw) for k, (fx, fy) in enumerate(flips)
        ]
        for cp in copies:
            cp.start()
        for cp in copies:
            cp.wait()

    return pl.pallas_call(
        body,
        name=name,
        out_shape=tuple(jax.ShapeDtypeStruct((3,) + p.shape[1:], p.dtype) for p in parts),
        in_specs=[_HBM] * nw,
        out_specs=tuple([_HBM] * nw),
        scratch_shapes=[pltpu.SemaphoreType.DMA((3 * nw,)), pltpu.SemaphoreType.DMA((3 * nw,))],
    )(*parts)


def share_halves(sums, *, name):
    nw = len(sums)

    def body(*refs):
        ins, outs = refs[:nw], refs[nw:2 * nw]
        send_sems, recv_sems = refs[2 * nw:]
        x, y, c, _ = _place()
        copies = [
            pltpu.make_async_remote_copy(src_ref=ins[w], dst_ref=outs[w], send_sem=send_sems.at[w], recv_sem=recv_sems.at[w],
                                         device_id=(x, y, 1 - c), device_id_type=MESH)
            for w in range(nw)
        ]
        for cp in copies:
            cp.start()
        for cp in copies:
            cp.wait()

    return pl.pallas_call(
        body,
        name=name,
        out_shape=tuple(jax.ShapeDtypeStruct(s.shape, s.dtype) for s in sums),
        in_specs=[_HBM] * nw,
        out_specs=tuple([_HBM] * nw),
        scratch_shapes=[pltpu.SemaphoreType.DMA((nw,)), pltpu.SemaphoreType.DMA((nw,))],
    )(*sums)


def gather_blocks(block, *, reduce, name):
    rows, cols = block.shape

    def body(x_ref, out_ref, *rest):
        if reduce:
            buf_ref, send_sems, recv_sems = rest
        else:
            send_sems, recv_sems = rest
            buf_ref = out_ref
        x, y, c, flips = _place()
        me, sibling = (x, y, c), (x, y, 1 - c)

        def slot(px, py, pc):
            return buf_ref.at[4 * px + 2 * py + pc]

        def copy(k, blk, to, src=None):
            return pltpu.make_async_remote_copy(src_ref=slot(*blk) if src is None else src, dst_ref=slot(*blk),
                                                send_sem=send_sems.at[k], recv_sem=recv_sems.at[k], device_id=to,
                                                device_id_type=MESH)

        buf_ref[4 * x + 2 * y + c] = x_ref[...]
        first = [copy(0, me, sibling, src=x_ref)]
        first += [copy(1 + j, me, (*chip, c), src=x_ref) for j, chip in enumerate(flips)]
        for cp in first:
            cp.start()
        passed = [copy(4 + j, (*chip, c), sibling) for j, chip in enumerate(flips)]
        for j, chip in enumerate(flips):
            copy(1 + j, (*chip, c), me).wait_recv()
            passed[j].start()
        copy(0, sibling, me).wait_recv()
        for j, chip in enumerate(flips):
            copy(4 + j, (*chip, 1 - c), me).wait_recv()
        for cp in first + passed:
            cp.wait_send()
        if reduce:
            acc = buf_ref[0]
            for dev in range(1, N_DEV):
                acc = acc + buf_ref[dev]
            out_ref[...] = acc

    vmem = pl.BlockSpec(memory_space=pltpu.VMEM)
    sems = [pltpu.SemaphoreType.DMA((7,)), pltpu.SemaphoreType.DMA((7,))]
    if reduce:
        out_shape = jax.ShapeDtypeStruct((rows, cols), block.dtype)
        scratch = [pltpu.VMEM((N_DEV, rows, cols), block.dtype)] + sems
    else:
        out_shape = jax.ShapeDtypeStruct((N_DEV, rows, cols), block.dtype)
        scratch = sems
    return pl.pallas_call(
        body,
        name=name,
        out_shape=out_shape,
        in_specs=[vmem],
        out_specs=vmem,
        scratch_shapes=scratch,
        compiler_params=pltpu.CompilerParams(vmem_limit_bytes=VMEM_LIMIT_BYTES),
    )(block)


def add_own_half(grad, recv, core, *, name):
    _, r2, cols = recv.shape
    tr = _tile(r2, max(16, (1 << 20) // (2 * cols) // 16 * 16), 16)
    nr = r2 // tr

    def body(core_ref, g_ref, r_ref, o_ref):
        o_ref[...] = (g_ref[...].astype(F32) + r_ref[...].astype(F32)).astype(o_ref.dtype)

    return pl.pallas_call(
        body,
        name=name,
        out_shape=jax.ShapeDtypeStruct(recv.shape, BF16),
        grid_spec=pltpu.PrefetchScalarGridSpec(
            num_scalar_prefetch=1,
            grid=(N_CHIPS, nr),
            in_specs=[
                pl.BlockSpec((None, tr, cols), lambda k, i, core_ref: (k, core_ref[0] * nr + i, 0)),
                pl.BlockSpec((None, tr, cols), lambda k, i, core_ref: (k, i, 0)),
            ],
            out_specs=pl.BlockSpec((None, tr, cols), lambda k, i, core_ref: (k, i, 0)),
        ),
        compiler_params=_cparams("parallel", "parallel"),
    )(core, grad, recv)


def sum_chips(part, recv, chip, *, name):
    _, r2, cols = part.shape
    tr = _tile(r2, max(16, (1 << 20) // (2 * cols) // 16 * 16), 16)

    def body(chip_ref, p_ref, r_ref, o_ref):
        acc = p_ref[...].astype(F32)
        for k in range(3):
            acc = acc + r_ref[k].astype(F32)
        o_ref[...] = acc

    return pl.pallas_call(
        body,
        name=name,
        out_shape=jax.ShapeDtypeStruct((r2, cols), F32),
        grid_spec=pltpu.PrefetchScalarGridSpec(
            num_scalar_prefetch=1,
            grid=(r2 // tr,),
            in_specs=[
                pl.BlockSpec((None, tr, cols), lambda i, chip_ref: (chip_ref[0], i, 0)),
                pl.BlockSpec((3, tr, cols), lambda i, chip_ref: (0, i, 0)),
            ],
            out_specs=pl.BlockSpec((tr, cols), lambda i, chip_ref: (i, 0)),
        ),
        compiler_params=_cparams("parallel"),
    )(chip, part, recv)


WEIGHTS = ("w_in", "w_uq", "w_ukv", "w_branch", "w_out", "w_ffn_in", "w_ffn_out")
GAINS = ("g_mix_pre", "g_mix_post", "g_ffn_pre", "g_ffn_post")


def layer_fwd(cfg, h, w, s, tabs, tag):
    m, d, hd = cfg.m, cfg.d, cfg.heads
    fox_blk = cfg.off_fox // LANES
    hn = rmsnorm_fwd(h, s["g_mix_pre"], BF16, name=f"norm_mix_pre{tag}")
    proj = matmul(hn, w["w_in"], "nn", BF16, tm=m, tn=_tile(cfg.d_inp, 512), tk=d, name=f"proj{tag}")
    fl = matmul(hn, w["w_in"][:, cfg.off_fl:cfg.off_fl + LANES], "nn", F32, tm=m, tn=LANES, tk=d, name=f"proj_forget{tag}")
    cqn = rmsnorm_fwd(proj, s["g_q_lat"], BF16, width=cfg.q_rank, col_blk=cfg.off_cq // cfg.q_rank, name=f"norm_q{tag}")
    ckvn = rmsnorm_fwd(proj, s["g_kv_lat"], BF16, width=cfg.kv_rank, col_blk=cfg.off_ckv // cfg.kv_rank, name=f"norm_kv{tag}")
    q = matmul(cqn, w["w_uq"], "nn", BF16, tm=m, tn=_tile(2 * cfg.width, 512), tk=cfg.q_rank, name=f"up_q{tag}")
    kv = matmul(ckvn, w["w_ukv"], "nn", BF16, tm=m, tn=_tile(2 * cfg.width, 512), tk=cfg.kv_rank, name=f"up_kv{tag}")
    qf, kf = mla_prep_fwd(cfg, q, kv, proj, tabs[0], name=f"mla_prep{tag}")
    o_a, lse_a = attn_fwd(qf, kf, kv, heads=hd, dk=2 * LANES, dv=LANES, qblk0=0, kblk0=0, vblk0=hd,
                          scale=(LANES + ROPE) ** -0.5, pad=cfg.pad, name=f"mla_attn{tag}")
    o_b = conv_fwd(cfg, proj, s["conv_w"], name=f"conv{tag}")
    b_pad = jnp.pad(s["b_forget"], (0, LANES - hd)).reshape(1, LANES)
    cum = fox_gate_fwd(cfg, fl, b_pad, name=f"fox_gate{tag}")
    cum_t = cum[:, :hd].T
    decay = (cum_t[:, :, None], cum_t[:, None, :])
    o_c, lse_c = attn_fwd(proj, proj, proj, heads=hd, dk=LANES, dv=LANES, qblk0=fox_blk, kblk0=fox_blk + hd,
                          vblk0=fox_blk + 2 * hd, scale=LANES ** -0.5, pad=cfg.pad, decay=decay, name=f"fox_attn{tag}")
    o = jnp.stack([o_a, o_b, o_c])
    y = matmul(o, w["w_branch"], "nn", BF16, tm=m, tn=_tile(d, 512), tk=cfg.width, name=f"branch{tag}")
    merged = gate_merge_fwd(cfg, y, proj, name=f"merge{tag}")
    mix = matmul(merged, w["w_out"], "nn", F32, tm=m, tn=_tile(d, 256), tk=d, name=f"out_proj{tag}")
    h_mid = rmsnorm_fwd(mix, s["g_mix_post"], F32, res=h, name=f"norm_mix_post{tag}")
    hn2 = rmsnorm_fwd(h_mid, s["g_ffn_pre"], BF16, name=f"norm_ffn_pre{tag}")
    gu = matmul(hn2, w["w_ffn_in"], "nn", BF16, tm=m, tn=_tile(2 * cfg.d_ff, 512), tk=d, name=f"ffn_in{tag}")
    act = swiglu_fwd(cfg, gu, name=f"swiglu{tag}")
    f = matmul(act, w["w_ffn_out"], "nn", F32, tm=m, tn=_tile(d, 512), tk=_tile(cfg.d_ff, 1408), name=f"ffn_out{tag}")
    h_next = rmsnorm_fwd(f, s["g_ffn_post"], F32, res=h_mid, name=f"norm_ffn_post{tag}")
    saved = dict(h=h, hn=hn, proj=proj, fl=fl, cqn=cqn, ckvn=ckvn, kv=kv, qf=qf, kf=kf, lse_a=lse_a,
                 b_pad=b_pad, decay=decay, lse_c=lse_c, o=o, y=y, merged=merged, mix=mix, h_mid=h_mid,
                 hn2=hn2, gu=gu, act=act, f=f)
    return h_next, saved


def layer_bwd(cfg, dh, w, s, r, tabs, tag):
    m, d, hd = cfg.m, cfg.d, cfg.heads
    fox_blk = cfg.off_fox // LANES
    tk_m = m
    df, dg4 = rmsnorm_bwd(r["f"], s["g_ffn_post"], dh, BF16, name=f"norm_ffn_post_bwd{tag}")
    dact = matmul(df, w["w_ffn_out"], "nt", BF16, tm=m, tn=_tile(cfg.d_ff, 512), tk=d, name=f"ffn_out_dx{tag}")
    dw_fo = matmul(r["act"], df, "tn", BF16, tm=_tile(cfg.d_ff, 512), tn=_tile(d, 1024), tk=tk_m, name=f"ffn_out_dw{tag}")
    dgu = swiglu_bwd(cfg, dact, r["gu"], name=f"swiglu_bwd{tag}")
    dhn2 = matmul(dgu, w["w_ffn_in"], "nt", F32, tm=m, tn=_tile(d, 512), tk=_tile(2 * cfg.d_ff, 1408), name=f"ffn_in_dx{tag}")
    dw_fi = matmul(r["hn2"], dgu, "tn", BF16, tm=_tile(d, 1024), tn=_tile(2 * cfg.d_ff, 512), tk=tk_m, name=f"ffn_in_dw{tag}")
    dh_mid, dg3 = rmsnorm_bwd(r["h_mid"], s["g_ffn_pre"], dhn2, F32, dres=dh, name=f"norm_ffn_pre_bwd{tag}")
    dmix, dg2 = rmsnorm_bwd(r["mix"], s["g_mix_post"], dh_mid, BF16, name=f"norm_mix_post_bwd{tag}")
    dmerged = matmul(dmix, w["w_out"], "nt", BF16, tm=m, tn=_tile(d, 512), tk=d, name=f"out_proj_dx{tag}")
    dw_out = matmul(r["merged"], dmix, "tn", BF16, tm=_tile(d, 1024), tn=_tile(d, 512), tk=tk_m, name=f"out_proj_dw{tag}")
    dy, dgl = gate_merge_bwd(cfg, dmerged, r["y"], r["proj"], name=f"merge_bwd{tag}")
    do = matmul(dy, w["w_branch"], "nt", BF16, tm=m, tn=_tile(cfg.width, 512), tk=d, name=f"branch_dx{tag}")
    dw_br = matmul(r["o"], dy, "tn", BF16, tm=_tile(cfg.width, 1024), tn=_tile(d, 512), tk=tk_m, name=f"branch_dw{tag}")
    dqf, dkf, dv_a = attn_bwd(r["qf"], r["kf"], r["kv"], do, 0, r["lse_a"], heads=hd, dk=2 * LANES, dv=LANES,
                              qblk0=0, kblk0=0, vblk0=hd, scale=(LANES + ROPE) ** -0.5, pad=cfg.pad, name=f"mla_attn_bwd{tag}")
    dq, dkn, dkpe = mla_prep_bwd(cfg, dqf, dkf, tabs[1], name=f"mla_prep_bwd{tag}")
    dkv = jnp.concatenate([dkn, dv_a], axis=1)
    dcqn = matmul(dq, w["w_uq"], "nt", F32, tm=m, tn=cfg.q_rank, tk=2 * cfg.width, name=f"up_q_dx{tag}")
    dw_uq = matmul(r["cqn"], dq, "tn", BF16, tm=cfg.q_rank, tn=_tile(2 * cfg.width, 512), tk=tk_m, name=f"up_q_dw{tag}")
    dckvn = matmul(dkv, w["w_ukv"], "nt", F32, tm=m, tn=cfg.kv_rank, tk=2 * cfg.width, name=f"up_kv_dx{tag}")
    dw_ukv = matmul(r["ckvn"], dkv, "tn", BF16, tm=cfg.kv_rank, tn=_tile(2 * cfg.width, 512), tk=tk_m, name=f"up_kv_dw{tag}")
    dcq, dgq = rmsnorm_bwd(r["proj"], s["g_q_lat"], dcqn, BF16, width=cfg.q_rank, col_blk=cfg.off_cq // cfg.q_rank,
                           name=f"norm_q_bwd{tag}")
    dckv, dgkv = rmsnorm_bwd(r["proj"], s["g_kv_lat"], dckvn, BF16, width=cfg.kv_rank, col_blk=cfg.off_ckv // cfg.kv_rank,
                             name=f"norm_kv_bwd{tag}")
    dcb, dcc, dcx, dconv_w = conv_bwd(cfg, r["proj"], s["conv_w"], do, 1, name=f"conv_bwd{tag}")
    dfq, dfk, dfv, dck = attn_bwd(r["proj"], r["proj"], r["proj"], do, 2, r["lse_c"], heads=hd, dk=LANES, dv=LANES,
                                  qblk0=fox_blk, kblk0=fox_blk + hd, vblk0=fox_blk + 2 * hd, scale=LANES ** -0.5,
                                  pad=cfg.pad, decay=r["decay"], name=f"fox_attn_bwd{tag}")
    dc = jnp.pad(dck[:, 0, :].T, ((0, 0), (0, LANES - hd)))
    dfl, dbf = fox_gate_bwd(cfg, r["fl"], r["b_pad"], dc, name=f"fox_gate_bwd{tag}")
    tail = jnp.zeros((m, cfg.d_inp - cfg.off_fl - LANES), BF16)
    dproj = jnp.concatenate([dgl, dcq, dckv, dcb, dcc, dcx, dfq, dfk, dfv, dkpe.astype(BF16), dfl.astype(BF16), tail], axis=1)
    dhn = matmul(dproj, w["w_in"], "nt", F32, tm=m, tn=_tile(d, 512), tk=_tile(cfg.d_inp, 1536), name=f"proj_dx{tag}")
    dw_in = matmul(r["hn"], dproj, "tn", BF16, tm=_tile(d, 1024), tn=_tile(cfg.d_inp, 512), tk=tk_m, name=f"proj_dw{tag}")
    dh_in, dg1 = rmsnorm_bwd(r["h"], s["g_mix_pre"], dhn, F32, dres=dh_mid, name=f"norm_mix_pre_bwd{tag}")
    dws = dict(w_in=dw_in, w_uq=dw_uq, w_ukv=dw_ukv, w_branch=dw_br, w_out=dw_out, w_ffn_in=dw_fi, w_ffn_out=dw_fo)
    dsmall = dict(g_mix_pre=dg1[0], g_mix_post=dg2[0], g_ffn_pre=dg3[0], g_ffn_post=dg4[0], g_q_lat=dgq[0], g_kv_lat=dgkv[0],
                  b_forget=dbf[0, :hd], conv_w=dconv_w)
    return dh_in, dws, dsmall


def local_step(cfg, x, target, meta, weights, small):
    h = jnp.concatenate([jnp.zeros((cfg.pad, cfg.d), F32), meta, x], axis=0)
    cos, s1, s2 = rope_tables(cfg)
    tabs = ((cos, s1, s2), (cos, -s1, -s2))
    saved = []
    for l in range(cfg.depth):
        h, r = layer_fwd(cfg, h, weights[l], small[l], tabs, f"_{l}")
        saved.append(r)
    dh, loss = loss_head(cfg, h, target, name="loss_head")
    dws, dsmalls = [None] * cfg.depth, [None] * cfg.depth
    for l in reversed(range(cfg.depth)):
        dh, dws[l], dsmalls[l] = layer_bwd(cfg, dh, weights[l], small[l], saved[l], tabs, f"_{l}")
    first = cfg.pad + cfg.n_meta
    return loss, dh[first:], dh[cfg.pad:first], dws, dsmalls


def _cols_from_chips(g):
    return jnp.transpose(g, (1, 0, 2)).reshape(g.shape[1], N_CHIPS * g.shape[2])


def _cols_to_chips(w):
    r, c = w.shape
    return jnp.transpose(w.reshape(r, N_CHIPS, c // N_CHIPS), (1, 0, 2))


def full_weights(cfg, g):
    br = _cols_from_chips(g["w_branch"]).reshape(3, cfg.width, cfg.d)
    return dict(
        w_in=pack_w_in(cfg, _cols_from_chips(g["w_in"])),
        w_uq=pack_w_uq(cfg, _cols_from_chips(g["w_uq"])),
        w_ukv=pack_w_ukv(cfg, _cols_from_chips(g["w_ukv"])),
        w_branch=br,
        w_out=g["w_out"].reshape(cfg.d, cfg.d),
        w_ffn_in=_cols_from_chips(g["w_ffn_in"]),
        w_ffn_out=g["w_ffn_out"].reshape(cfg.d_ff, cfg.d),
    )


def chip_grads(cfg, dw):
    return dict(
        w_in=_cols_to_chips(unpack_w_in(cfg, dw["w_in"])),
        w_uq=_cols_to_chips(unpack_w_uq(cfg, dw["w_uq"])),
        w_ukv=_cols_to_chips(unpack_w_ukv(cfg, dw["w_ukv"])),
        w_branch=_cols_to_chips(dw["w_branch"].reshape(3 * cfg.width, cfg.d)),
        w_out=dw["w_out"].reshape(N_CHIPS, cfg.d // N_CHIPS, cfg.d),
        w_ffn_in=_cols_to_chips(dw["w_ffn_in"]),
        w_ffn_out=dw["w_ffn_out"].reshape(N_CHIPS, cfg.d_ff // N_CHIPS, cfg.d),
    )


def _small_rows(cfg):
    return dict(g_mix_pre=cfg.d // LANES, g_mix_post=cfg.d // LANES, g_ffn_pre=cfg.d // LANES, g_ffn_post=cfg.d // LANES,
                g_q_lat=cfg.q_rank // LANES, g_kv_lat=cfg.kv_rank // LANES, b_forget=1, conv_w=3 * cfg.width // LANES)


def pack_small(cfg, loss, dmeta, dsmalls):
    parts = [loss[0:1, :], dmeta.reshape(-1, LANES)]
    for ds in dsmalls:
        for k in _small_rows(cfg):
            v = ds[k]
            if k == "b_forget":
                v = jnp.pad(v, (0, LANES - cfg.heads))
            parts.append(v.reshape(-1, LANES))
    rows = sum(p.shape[0] for p in parts)
    parts.append(jnp.zeros((-rows % 8, LANES), F32))
    return jnp.concatenate(parts, axis=0)


def unpack_small(cfg, block):
    loss = block[0, 0]
    n = cfg.n_meta * cfg.d // LANES
    dmeta = block[1:1 + n].reshape(cfg.n_meta, cfg.d)
    at = 1 + n
    out = []
    for _ in range(cfg.depth):
        ds = {}
        for k, rows in _small_rows(cfg).items():
            v = block[at:at + rows]
            at += rows
            if k == "b_forget":
                v = v[0, :cfg.heads]
            elif k == "conv_w":
                v = v.reshape(3, cfg.width)
            else:
                v = v.reshape(-1)
            ds[k] = v
        out.append(ds)
    return loss, dmeta, out


def kernel(x, meta, w_in, b_forget, g_q_lat, g_kv_lat, w_uq, w_ukv, conv_w, w_branch, w_out, w_ffn_in, w_ffn_out, g_mix_pre, g_mix_post, g_ffn_pre, g_ffn_post, loss_target, m_meta, m_w_in, m_b_forget, m_g_q_lat, m_g_kv_lat, m_w_uq, m_w_ukv, m_conv_w, m_w_branch, m_w_out, m_w_ffn_in, m_w_ffn_out, m_g_mix_pre, m_g_mix_post, m_g_ffn_pre, m_g_ffn_post, v_meta, v_w_in, v_b_forget, v_g_q_lat, v_g_kv_lat, v_w_uq, v_w_ukv, v_conv_w, v_w_branch, v_w_out, v_w_ffn_in, v_w_ffn_out, v_g_mix_pre, v_g_mix_post, v_g_ffn_pre, v_g_ffn_post):
    cfg = CFG
    names = ("meta", "w_in", "b_forget", "g_q_lat", "g_kv_lat", "w_uq", "w_ukv", "conv_w", "w_branch", "w_out", "w_ffn_in",
             "w_ffn_out", "g_mix_pre", "g_mix_post", "g_ffn_pre", "g_ffn_post")
    params = dict(zip(names, (meta, w_in, b_forget, g_q_lat, g_kv_lat, w_uq, w_ukv, conv_w, w_branch, w_out, w_ffn_in, w_ffn_out,
                              g_mix_pre, g_mix_post, g_ffn_pre, g_ffn_post)))
    mom1 = dict(zip(names, (m_meta, m_w_in, m_b_forget, m_g_q_lat, m_g_kv_lat, m_w_uq, m_w_ukv, m_conv_w, m_w_branch, m_w_out,
                            m_w_ffn_in, m_w_ffn_out, m_g_mix_pre, m_g_mix_post, m_g_ffn_pre, m_g_ffn_post)))
    mom2 = dict(zip(names, (v_meta, v_w_in, v_b_forget, v_g_q_lat, v_g_kv_lat, v_w_uq, v_w_ukv, v_conv_w, v_w_branch, v_w_out,
                            v_w_ffn_in, v_w_ffn_out, v_g_mix_pre, v_g_mix_post, v_g_ffn_pre, v_g_ffn_post)))
    xi, yi, ci = lax.axis_index("x"), lax.axis_index("y"), lax.axis_index("c")
    chip = 2 * xi + yi
    chip_arr = jnp.reshape(chip, (1,)).astype(jnp.int32)
    core_arr = jnp.reshape(ci, (1,)).astype(jnp.int32)

    meta_all = gather_blocks(meta, reduce=False, name="gather_meta")[0::2]
    meta_full = jnp.transpose(meta_all, (1, 0, 2)).reshape(cfg.n_meta, cfg.d)
    conv_rows = conv_w.reshape(cfg.depth * 3, cfg.width // N_CHIPS)
    conv_all = gather_blocks(conv_rows, reduce=False, name="gather_conv_w")[0::2]
    conv_full = jnp.transpose(conv_all, (1, 0, 2)).reshape(cfg.depth, 3, cfg.width)

    def shard2d(name, l):
        w = params[name][l]
        return w.reshape(-1, w.shape[-1]).astype(BF16)

    weights, small = [], []
    is_mine = (jnp.arange(N_CHIPS) == chip)[:, None, None]
    for l in range(cfg.depth):
        own = [shard2d(n, l) for n in WEIGHTS]
        got = gather_weights(own, name=f"gather_weights_{l}")
        got = [jnp.where(is_mine, o[None], g) for o, g in zip(own, got)]
        weights.append(full_weights(cfg, dict(zip(WEIGHTS, got))))
        small.append(dict(g_mix_pre=g_mix_pre[l], g_mix_post=g_mix_post[l], g_ffn_pre=g_ffn_pre[l], g_ffn_post=g_ffn_post[l],
                          g_q_lat=g_q_lat[l], g_kv_lat=g_kv_lat[l], b_forget=b_forget[l], conv_w=conv_full[l]))

    loss, grad_x, dmeta, dws, dsmalls = local_step(cfg, x[0], loss_target[0], meta_full, weights, small)

    grads = {n: [] for n in WEIGHTS}
    for l in range(cfg.depth):
        send = chip_grads(cfg, dws[l])
        mine = [send[n] for n in WEIGHTS]
        theirs = swap_halves(mine, name=f"swap_halves_{l}")
        parts = [add_own_half(g, t, core_arr, name=f"add_own_half_{n}_{l}") for n, g, t in zip(WEIGHTS, mine, theirs)]
        others = exchange_chips(parts, name=f"exchange_chips_{l}")
        sums = [sum_chips(p, o, chip_arr, name=f"sum_chips_{n}_{l}") for n, p, o in zip(WEIGHTS, parts, others)]
        for n, g in zip(WEIGHTS, sums):
            grads[n].append(g)
    own_half = [jnp.stack(grads[n]) for n in WEIGHTS]
    other_half = share_halves(own_half, name="share_halves")
    grad, delta, new_m, new_v = {}, {}, {}, {}
    for n, mine_, theirs_ in zip(WEIGHTS, own_half, other_half):
        shp = params[n].shape
        three_d = lambda a: a.reshape(cfg.depth, -1, shp[-1])
        out = adamw_halves(three_d(params[n]), mine_, theirs_, core_arr, three_d(mom1[n]), three_d(mom2[n]), name=f"adamw_{n}")
        grad[n], delta[n], new_m[n], new_v[n] = (a.reshape(shp) for a in out)

    total = gather_blocks(pack_small(cfg, loss, dmeta, dsmalls), reduce=True, name="reduce_small")
    loss_sum, dmeta_sum, dsmall_sum = unpack_small(cfg, total)
    for k in _small_rows(cfg):
        grad[k] = jnp.stack([ds[k] for ds in dsmall_sum])
    grad["conv_w"] = lax.dynamic_slice_in_dim(grad["conv_w"], chip * (cfg.width // N_CHIPS), cfg.width // N_CHIPS, axis=2)
    grad["meta"] = lax.dynamic_slice_in_dim(dmeta_sum, chip * (cfg.d // N_CHIPS), cfg.d // N_CHIPS, axis=1)

    for n in names:
        if n in WEIGHTS:
            continue
        shp = params[n].shape
        two_d = lambda a: a.reshape(-1, shp[-1])
        dl, nm, nv = adamw(two_d(params[n]), two_d(grad[n]), two_d(mom1[n]), two_d(mom2[n]), name=f"adamw_{n}")
        delta[n], new_m[n], new_v[n] = dl.reshape(shp), nm.reshape(shp), nv.reshape(shp)

    return (loss_sum, grad_x[None], *[grad[n] for n in names], *[delta[n] for n in names], *[new_m[n] for n in names],
            *[new_v[n] for n in names])
```

```python
import functools
from typing import NamedTuple

import jax
import jax.numpy as jnp
from jax import lax
from jax.experimental import pallas as pl
from jax.experimental.pallas import tpu as pltpu

F32 = jnp.float32
BF16 = jnp.bfloat16
MESH = pl.DeviceIdType.MESH

EPS = 1e-6
NEG_INF = -1e30
ROPE_THETA = 10000.0
LANES = 128
ROPE = 64
N_CHIPS = 4
N_DEV = 8

ADAM_LR = 0.001
ADAM_B1 = 0.9
ADAM_B2 = 0.999
ADAM_EPS = 1e-08
ADAM_WD = 0.01
ADAM_STEP = 10

VMEM_LIMIT_BYTES = 48 * 1024 * 1024


class Cfg(NamedTuple):
    d: int = 2048
    seq: int = 2048
    depth: int = 4
    n_meta: int = 16
    heads: int = 8
    q_rank: int = 512
    kv_rank: int = 512
    d_ff: int = 5632

    @property
    def width(self):
        return self.heads * LANES

    @property
    def pad(self):
        return (-(self.n_meta + self.seq)) % LANES

    @property
    def m(self):
        return self.pad + self.n_meta + self.seq

    @property
    def nat_splits(self):
        w = self.width
        return (self.q_rank, self.kv_rank, ROPE, w, w, w, w, w, w, self.heads, 3 * self.d)

    @property
    def d_in(self):
        return sum(self.nat_splits)

    @property
    def off_cq(self):
        return 3 * self.d

    @property
    def off_ckv(self):
        return self.off_cq + self.q_rank

    @property
    def off_conv(self):
        return self.off_ckv + self.kv_rank

    @property
    def off_fox(self):
        return self.off_conv + 3 * self.width

    @property
    def off_kpe(self):
        return self.off_fox + 3 * self.width

    @property
    def off_fl(self):
        return self.off_kpe + LANES

    @property
    def d_inp(self):
        return -(-(self.off_fl + LANES) // 512) * 512


CFG = Cfg()


def _tile(n, target, mult=LANES):
    best = None
    t = mult
    while t <= min(n, target):
        if n % t == 0:
            best = t
        t += mult
    return best or n


def _cparams(*sem):
    return pltpu.CompilerParams(dimension_semantics=sem, vmem_limit_bytes=VMEM_LIMIT_BYTES)


def pack_w_in(cfg, w):
    cq, ckv, kpe, cb, cc, cx, fq, fk, fv, fl, gate = jnp.split(w, list(_cumsum(cfg.nat_splits))[:-1], axis=1)
    z = lambda n: jnp.zeros((w.shape[0], n), w.dtype)
    tail = cfg.d_inp - cfg.off_fl - cfg.heads
    return jnp.concatenate([gate, cq, ckv, cb, cc, cx, fq, fk, fv, kpe, z(LANES - ROPE), fl, z(tail)], axis=1)


def unpack_w_in(cfg, wp):
    w = cfg.width
    sizes = (3 * cfg.d, cfg.q_rank, cfg.kv_rank, w, w, w, w, w, w, ROPE, LANES - ROPE, cfg.heads, cfg.d_inp - cfg.off_fl - cfg.heads)
    gate, cq, ckv, cb, cc, cx, fq, fk, fv, kpe, _, fl, _ = jnp.split(wp, list(_cumsum(sizes))[:-1], axis=1)
    return jnp.concatenate([cq, ckv, kpe, cb, cc, cx, fq, fk, fv, fl, gate], axis=1)


def _cumsum(xs):
    out, s = [], 0
    for v in xs:
        s += v
        out.append(s)
    return out


def pack_w_uq(cfg, w):
    r = w.shape[0]
    w3 = w.reshape(r, cfg.heads, LANES + ROPE)
    w3 = jnp.pad(w3, ((0, 0), (0, 0), (0, LANES - ROPE)))
    return w3.reshape(r, cfg.heads * 2 * LANES)


def unpack_w_uq(cfg, wp):
    r = wp.shape[0]
    return wp.reshape(r, cfg.heads, 2 * LANES)[:, :, : LANES + ROPE].reshape(r, cfg.heads * (LANES + ROPE))


def pack_w_ukv(cfg, w):
    r = w.shape[0]
    w4 = w.reshape(r, cfg.heads, 2, LANES)
    return jnp.transpose(w4, (0, 2, 1, 3)).reshape(r, 2 * cfg.heads * LANES)


def unpack_w_ukv(cfg, wp):
    r = wp.shape[0]
    w4 = wp.reshape(r, 2, cfg.heads, LANES)
    return jnp.transpose(w4, (0, 2, 1, 3)).reshape(r, 2 * cfg.heads * LANES)


_DIMS = {
    "nn": (((1,), (0,)), ((), ())),
    "nt": (((1,), (1,)), ((), ())),
    "tn": (((0,), (0,)), ((), ())),
}


def matmul(a, b, mode, out_dtype, *, tm, tn, tk, name):
    batched = a.ndim == 3
    if mode == "nn":
        (m, kc), n = a.shape[-2:], b.shape[-1]
        a_blk, a_idx = (tm, tk), lambda i, j, k: (i, k)
        b_blk, b_idx = (tk, tn), lambda i, j, k: (k, j)
    elif mode == "nt":
        (m, kc), n = a.shape[-2:], b.shape[-2]
        a_blk, a_idx = (tm, tk), lambda i, j, k: (i, k)
        b_blk, b_idx = (tn, tk), lambda i, j, k: (j, k)
    else:
        (kc, m), n = a.shape[-2:], b.shape[-1]
        a_blk, a_idx = (tk, tm), lambda i, j, k: (k, i)
        b_blk, b_idx = (tk, tn), lambda i, j, k: (k, j)
    assert m % tm == 0 and n % tn == 0 and kc % tk == 0, (name, m, n, kc, tm, tn, tk)
    nk = kc // tk
    dims = _DIMS[mode]
    o_blk, o_idx = (tm, tn), lambda i, j, k: (i, j)
    grid = (m // tm, n // tn, nk)
    if batched:
        nb = a.shape[0]
        grid = (nb,) + grid
        wrap = lambda f: (lambda bb, i, j, k: (bb,) + f(i, j, k))
        a_blk, b_blk, o_blk = (None,) + a_blk, (None,) + b_blk, (None,) + o_blk
        a_idx, b_idx, o_idx = wrap(a_idx), wrap(b_idx), wrap(o_idx)
        out_shape = (nb, m, n)
        sem = ("parallel", "parallel", "parallel", "arbitrary")
    else:
        out_shape = (m, n)
        sem = ("parallel", "parallel", "arbitrary")
    k_axis = len(grid) - 1

    def body(a_ref, b_ref, o_ref, *scratch):
        prod = lax.dot_general(a_ref[...], b_ref[...], dims, preferred_element_type=F32)
        if nk == 1:
            o_ref[...] = prod.astype(o_ref.dtype)
        else:
            acc_ref = scratch[0]
            k = pl.program_id(k_axis)

            @pl.when(k == 0)
            def _():
                acc_ref[...] = prod

            @pl.when(k > 0)
            def _():
                acc_ref[...] += prod

            @pl.when(k == nk - 1)
            def _():
                o_ref[...] = acc_ref[...].astype(o_ref.dtype)

    return pl.pallas_call(
        body,
        name=name,
        out_shape=jax.ShapeDtypeStruct(out_shape, out_dtype),
        grid=grid,
        in_specs=[pl.BlockSpec(a_blk, a_idx), pl.BlockSpec(b_blk, b_idx)],
        out_specs=pl.BlockSpec(o_blk, o_idx),
        scratch_shapes=[] if nk == 1 else [pltpu.VMEM((tm, tn), F32)],
        compiler_params=_cparams(*sem),
    )(a, b)


def _row_tile(m):
    return _tile(m, 272, 16)


def rmsnorm_fwd(x, g, out_dtype, *, name, width=None, col_blk=0, res=None):
    m = x.shape[0]
    n = width or x.shape[1]
    tm = _row_tile(m)
    has_res = res is not None

    def body(x_ref, g_ref, *rest):
        o_ref = rest[-1]
        xf = x_ref[...].astype(F32)
        r = lax.rsqrt(jnp.mean(xf * xf, axis=-1, keepdims=True) + EPS)
        y = xf * r * g_ref[...]
        if has_res:
            y = rest[0][...] + y
        o_ref[...] = y.astype(o_ref.dtype)

    in_specs = [pl.BlockSpec((tm, n), lambda i: (i, col_blk)), pl.BlockSpec((1, n), lambda i: (0, 0))]
    args = [x, g.reshape(1, n)]
    if has_res:
        in_specs.append(pl.BlockSpec((tm, n), lambda i: (i, 0)))
        args.append(res)
    return pl.pallas_call(
        body,
        name=name,
        out_shape=jax.ShapeDtypeStruct((m, n), out_dtype),
        grid=(m // tm,),
        in_specs=in_specs,
        out_specs=pl.BlockSpec((tm, n), lambda i: (i, 0)),
        compiler_params=_cparams("parallel"),
    )(*args)


def rmsnorm_bwd(x, g, dy, out_dtype, *, name, width=None, col_blk=0, dres=None):
    m = x.shape[0]
    n = width or x.shape[1]
    tm = _row_tile(m)
    has_res = dres is not None

    def body(x_ref, g_ref, dy_ref, *rest):
        dx_ref, dg_ref = rest[-2:]
        i = pl.program_id(0)
        xf = x_ref[...].astype(F32)
        r = lax.rsqrt(jnp.mean(xf * xf, axis=-1, keepdims=True) + EPS)
        xhat = xf * r
        dyf = dy_ref[...].astype(F32)
        dxh = dyf * g_ref[...]
        dx = r * (dxh - xhat * jnp.mean(dxh * xhat, axis=-1, keepdims=True))
        if has_res:
            dx = dx + rest[0][...]
        dx_ref[...] = dx.astype(dx_ref.dtype)
        part = jnp.sum(dyf * xhat, axis=0, keepdims=True)

        @pl.when(i == 0)
        def _():
            dg_ref[...] = part

        @pl.when(i > 0)
        def _():
            dg_ref[...] += part

    in_specs = [
        pl.BlockSpec((tm, n), lambda i: (i, col_blk)),
        pl.BlockSpec((1, n), lambda i: (0, 0)),
        pl.BlockSpec((tm, n), lambda i: (i, 0)),
    ]
    args = [x, g.reshape(1, n), dy]
    if has_res:
        in_specs.append(pl.BlockSpec((tm, n), lambda i: (i, 0)))
        args.append(dres)
    return pl.pallas_call(
        body,
        name=name,
        out_shape=(jax.ShapeDtypeStruct((m, n), out_dtype), jax.ShapeDtypeStruct((1, n), F32)),
        grid=(m // tm,),
        in_specs=in_specs,
        out_specs=(pl.BlockSpec((tm, n), lambda i: (i, 0)), pl.BlockSpec((1, n), lambda i: (0, 0))),
        compiler_params=_cparams("arbitrary"),
    )(*args)


_NT = (((1,), (1,)), ((), ()))
_NN = (((1,), (0,)), ((), ()))
_TN = (((0,), (0,)), ((), ()))


def _attn_scores(q, k, scale, decay_refs, i, tq, kn, pad):
    s = lax.dot_general(q, k, _NT, preferred_element_type=F32) * scale
    if decay_refs is not None:
        cq_ref, ck_ref = decay_refs
        s = s + (cq_ref[0] - ck_ref[0][:, :kn])
    t_idx = i * tq + lax.broadcasted_iota(jnp.int32, (tq, 1), 0)
    s_idx = lax.broadcasted_iota(jnp.int32, (1, kn), 1)
    mask = (s_idx <= t_idx) & (s_idx >= pad)
    return s, mask, t_idx


def _keys_needed(i, tq, m):
    return min(m, -(-((i + 1) * tq) // LANES) * LANES)


def attn_fwd(q, k, v, *, heads, dk, dv, qblk0, kblk0, vblk0, scale, pad, decay=None, name):
    m = q.shape[0]
    tq = _row_tile(m)
    has_decay = decay is not None

    def body(q_ref, k_ref, v_ref, *rest):
        o_ref, lse_ref = rest[-2:]
        decay_refs = rest[:2] if has_decay else None

        def block(i):
            kn = _keys_needed(i, tq, m)
            s, mask, t_idx = _attn_scores(q_ref[...], k_ref[0:kn, :], scale, decay_refs, i, tq, kn, pad)
            s = jnp.where(mask, s, NEG_INF)
            mx = jnp.max(s, axis=1, keepdims=True)
            p = jnp.exp(s - mx)
            l = jnp.sum(p, axis=1, keepdims=True)
            o = lax.dot_general(p.astype(BF16), v_ref[0:kn, :], _NN, preferred_element_type=F32) / l
            o_ref[...] = jnp.where(t_idx >= pad, o, 0.0).astype(o_ref.dtype)
            lse_ref[0] = mx + jnp.log(l)

        for i in range(m // tq):
            pl.when(pl.program_id(1) == i)(functools.partial(block, i))

    in_specs = [
        pl.BlockSpec((tq, dk), lambda h, i: (i, qblk0 + h)),
        pl.BlockSpec((m, dk), lambda h, i: (0, kblk0 + h)),
        pl.BlockSpec((m, dv), lambda h, i: (0, vblk0 + h)),
    ]
    args = [q, k, v]
    if has_decay:
        in_specs += [pl.BlockSpec((1, tq, 1), lambda h, i: (h, i, 0)), pl.BlockSpec((1, 1, m), lambda h, i: (h, 0, 0))]
        args += list(decay)
    return pl.pallas_call(
        body,
        name=name,
        out_shape=(jax.ShapeDtypeStruct((m, heads * dv), BF16), jax.ShapeDtypeStruct((heads, m, 1), F32)),
        grid=(heads, m // tq),
        in_specs=in_specs,
        out_specs=(pl.BlockSpec((tq, dv), lambda h, i: (i, h)), pl.BlockSpec((1, tq, 1), lambda h, i: (h, i, 0))),
        compiler_params=_cparams("parallel", "parallel"),
    )(*args)


def attn_bwd(q, k, v, do, do_sel, lse, *, heads, dk, dv, qblk0, kblk0, vblk0, scale, pad, decay=None, name):
    m = q.shape[0]
    tq = _row_tile(m)
    nq = m // tq
    has_decay = decay is not None

    def body(q_ref, k_ref, v_ref, do_ref, lse_ref, *rest):
        if has_decay:
            cq_ref, ck_ref, dq_ref, dk_ref, dv_ref, dck_ref, dk_acc, dv_acc = rest
            decay_refs = (cq_ref, ck_ref)
        else:
            dq_ref, dk_ref, dv_ref, dk_acc, dv_acc = rest
            decay_refs = None
        @pl.when(pl.program_id(1) == 0)
        def _():
            dk_acc[...] = jnp.zeros_like(dk_acc)
            dv_acc[...] = jnp.zeros_like(dv_acc)
            if has_decay:
                dck_ref[...] = jnp.zeros_like(dck_ref)

        def block(i):
            kn = _keys_needed(i, tq, m)
            qb, kb, dob = q_ref[...], k_ref[0:kn, :], do_ref[...]
            s, mask, _ = _attn_scores(qb, kb, scale, decay_refs, i, tq, kn, pad)
            p = jnp.where(mask, jnp.exp(s - lse_ref[0]), 0.0)
            dp = lax.dot_general(dob, v_ref[0:kn, :], _NT, preferred_element_type=F32)
            ds = p * (dp - jnp.sum(p * dp, axis=1, keepdims=True))
            dsb = ds.astype(BF16)
            dq_ref[...] = (lax.dot_general(dsb, kb, _NN, preferred_element_type=F32) * scale).astype(dq_ref.dtype)
            dk_acc[0:kn, :] += lax.dot_general(dsb, qb, _TN, preferred_element_type=F32) * scale
            dv_acc[0:kn, :] += lax.dot_general(p.astype(BF16), dob, _TN, preferred_element_type=F32)
            if has_decay:
                dck_ref[0, :, 0:kn] -= jnp.sum(ds, axis=0, keepdims=True)

        for i in range(nq):
            pl.when(pl.program_id(1) == i)(functools.partial(block, i))

        @pl.when(pl.program_id(1) == nq - 1)
        def _():
            dk_ref[...] = dk_acc[...].astype(dk_ref.dtype)
            dv_ref[...] = dv_acc[...].astype(dv_ref.dtype)

    in_specs = [
        pl.BlockSpec((tq, dk), lambda h, i: (i, qblk0 + h)),
        pl.BlockSpec((m, dk), lambda h, i: (0, kblk0 + h)),
        pl.BlockSpec((m, dv), lambda h, i: (0, vblk0 + h)),
        pl.BlockSpec((None, tq, dv), lambda h, i: (do_sel, i, h)),
        pl.BlockSpec((1, tq, 1), lambda h, i: (h, i, 0)),
    ]
    args = [q, k, v, do, lse]
    out_shape = [
        jax.ShapeDtypeStruct((m, heads * dk), BF16),
        jax.ShapeDtypeStruct((m, heads * dk), BF16),
        jax.ShapeDtypeStruct((m, heads * dv), BF16),
    ]
    out_specs = [
        pl.BlockSpec((tq, dk), lambda h, i: (i, h)),
        pl.BlockSpec((m, dk), lambda h, i: (0, h)),
        pl.BlockSpec((m, dv), lambda h, i: (0, h)),
    ]
    if has_decay:
        in_specs += [pl.BlockSpec((1, tq, 1), lambda h, i: (h, i, 0)), pl.BlockSpec((1, 1, m), lambda h, i: (h, 0, 0))]
        args += list(decay)
        out_shape.append(jax.ShapeDtypeStruct((heads, 1, m), F32))
        out_specs.append(pl.BlockSpec((1, 1, m), lambda h, i: (h, 0, 0)))
    return pl.pallas_call(
        body,
        name=name,
        out_shape=tuple(out_shape),
        grid=(heads, nq),
        in_specs=in_specs,
        out_specs=tuple(out_specs),
        scratch_shapes=[pltpu.VMEM((m, dk), F32), pltpu.VMEM((m, dv), F32)],
        compiler_params=_cparams("parallel", "arbitrary"),
    )(*args)


def rope_tables(cfg):
    half = ROPE // 2
    inv_freq = 1.0 / (ROPE_THETA ** (jnp.arange(0, ROPE, 2, dtype=F32) / ROPE))
    pos = (jnp.arange(cfg.m, dtype=jnp.int32) - cfg.pad).astype(F32)
    ang = pos[:, None] * inv_freq[None, :]
    cos, sin = jnp.cos(ang), jnp.sin(ang)
    z = jnp.zeros((cfg.m, half), F32)
    zz = jnp.zeros((cfg.m, LANES - ROPE), F32)
    return (
        jnp.concatenate([cos, cos, zz], axis=1),
        jnp.concatenate([-sin, z, zz], axis=1),
        jnp.concatenate([z, sin, zz], axis=1),
    )


def _rope(x, cos, s1, s2):
    return x * cos + pltpu.roll(x, LANES - ROPE // 2, 1) * s1 + pltpu.roll(x, ROPE // 2, 1) * s2


def mla_prep_fwd(cfg, q, kv, proj, tabs, *, name):
    m, h2 = cfg.m, 2 * LANES
    tm = _tile(m, 544, 16)
    kpe_blk = cfg.off_kpe // LANES

    def body(q_ref, kn_ref, kpe_ref, cos_ref, s1_ref, s2_ref, qf_ref, kf_ref):
        cos, s1, s2 = cos_ref[...], s1_ref[...], s2_ref[...]
        qv = q_ref[...]
        qf_ref[:, :LANES] = qv[:, :LANES]
        qf_ref[:, LANES:] = _rope(qv[:, LANES:].astype(F32), cos, s1, s2).astype(qf_ref.dtype)
        kf_ref[:, :LANES] = kn_ref[...]
        kf_ref[:, LANES:] = _rope(kpe_ref[...].astype(F32), cos, s1, s2).astype(kf_ref.dtype)

    tab = pl.BlockSpec((tm, LANES), lambda i, h: (i, 0))
    return pl.pallas_call(
        body,
        name=name,
        out_shape=(jax.ShapeDtypeStruct((m, cfg.heads * h2), BF16), jax.ShapeDtypeStruct((m, cfg.heads * h2), BF16)),
        grid=(m // tm, cfg.heads),
        in_specs=[
            pl.BlockSpec((tm, h2), lambda i, h: (i, h)),
            pl.BlockSpec((tm, LANES), lambda i, h: (i, h)),
            pl.BlockSpec((tm, LANES), lambda i, h: (i, kpe_blk)),
            tab, tab, tab,
        ],
        out_specs=(pl.BlockSpec((tm, h2), lambda i, h: (i, h)), pl.BlockSpec((tm, h2), lambda i, h: (i, h))),
        compiler_params=_cparams("parallel", "parallel"),
    )(q, kv, proj, *tabs)


def mla_prep_bwd(cfg, dqf, dkf, tabs_t, *, name):
    m, h2 = cfg.m, 2 * LANES
    tm = _tile(m, 544, 16)

    def body(dqf_ref, dkf_ref, cos_ref, s1_ref, s2_ref, dq_ref, dkn_ref, dkpe_ref):
        h = pl.program_id(1)
        cos, s1, s2 = cos_ref[...], s1_ref[...], s2_ref[...]
        dqv, dkv = dqf_ref[...], dkf_ref[...]
        dq_ref[:, :LANES] = dqv[:, :LANES]
        dq_ref[:, LANES:] = _rope(dqv[:, LANES:].astype(F32), cos, s1, s2).astype(dq_ref.dtype)
        dkn_ref[...] = dkv[:, :LANES]
        part = _rope(dkv[:, LANES:].astype(F32), cos, s1, s2)

        @pl.when(h == 0)
        def _():
            dkpe_ref[...] = part

        @pl.when(h > 0)
        def _():
            dkpe_ref[...] += part

    tab = pl.BlockSpec((tm, LANES), lambda i, h: (i, 0))
    return pl.pallas_call(
        body,
        name=name,
        out_shape=(
            jax.ShapeDtypeStruct((m, cfg.heads * h2), BF16),
            jax.ShapeDtypeStruct((m, cfg.heads * LANES), BF16),
            jax.ShapeDtypeStruct((m, LANES), F32),
        ),
        grid=(m // tm, cfg.heads),
        in_specs=[pl.BlockSpec((tm, h2), lambda i, h: (i, h)), pl.BlockSpec((tm, h2), lambda i, h: (i, h)), tab, tab, tab],
        out_specs=(
            pl.BlockSpec((tm, h2), lambda i, h: (i, h)),
            pl.BlockSpec((tm, LANES), lambda i, h: (i, h)),
            pl.BlockSpec((tm, LANES), lambda i, h: (i, 0)),
        ),
        compiler_params=_cparams("parallel", "arbitrary"),
    )(dqf, dkf, *tabs_t)


def _conv_parts(b_ref, c_ref, x_ref, w_ref, m):
    b, c, x = b_ref[...].astype(F32), c_ref[...].astype(F32), x_ref[...].astype(F32)
    u = c * x
    row = lax.broadcasted_iota(jnp.int32, (m, 1), 0)
    u1 = jnp.where(row >= 1, pltpu.roll(u, 1, 0), 0.0)
    u2 = jnp.where(row >= 2, pltpu.roll(u, 2, 0), 0.0)
    w0, w1, w2 = w_ref[0:1, :], w_ref[1:2, :], w_ref[2:3, :]
    uc = w0 * u2 + w1 * u1 + w2 * u
    return b, c, x, u, u1, u2, uc, (w0, w1, w2), row


def _conv_specs(cfg, tn):
    m, nb, blk0 = cfg.m, cfg.width // tn, cfg.off_conv // tn
    return [
        pl.BlockSpec((m, tn), lambda j: (0, blk0 + j)),
        pl.BlockSpec((m, tn), lambda j: (0, blk0 + nb + j)),
        pl.BlockSpec((m, tn), lambda j: (0, blk0 + 2 * nb + j)),
        pl.BlockSpec((3, tn), lambda j: (0, j)),
    ]


def conv_fwd(cfg, proj, conv_w, *, name):
    m, tn = cfg.m, LANES

    def body(b_ref, c_ref, x_ref, w_ref, o_ref):
        b, _, _, _, _, _, uc, _, _ = _conv_parts(b_ref, c_ref, x_ref, w_ref, m)
        o_ref[...] = (b * uc).astype(o_ref.dtype)

    return pl.pallas_call(
        body,
        name=name,
        out_shape=jax.ShapeDtypeStruct((m, cfg.width), BF16),
        grid=(cfg.width // tn,),
        in_specs=_conv_specs(cfg, tn),
        out_specs=pl.BlockSpec((m, tn), lambda j: (0, j)),
        compiler_params=_cparams("parallel"),
    )(proj, proj, proj, conv_w)


def conv_bwd(cfg, proj, conv_w, do, do_sel, *, name):
    m, tn = cfg.m, LANES

    def body(b_ref, c_ref, x_ref, w_ref, do_ref, db_ref, dc_ref, dx_ref, dw_ref):
        b, c, x, u, u1, u2, uc, (w0, w1, w2), row = _conv_parts(b_ref, c_ref, x_ref, w_ref, m)
        dob = do_ref[...].astype(F32)
        db_ref[...] = (dob * uc).astype(db_ref.dtype)
        duc = dob * b
        up1 = jnp.where(row <= m - 2, pltpu.roll(duc, m - 1, 0), 0.0)
        up2 = jnp.where(row <= m - 3, pltpu.roll(duc, m - 2, 0), 0.0)
        du = w2 * duc + w1 * up1 + w0 * up2
        dc_ref[...] = (du * x).astype(dc_ref.dtype)
        dx_ref[...] = (du * c).astype(dx_ref.dtype)
        dw_ref[0:1, :] = jnp.sum(duc * u2, axis=0, keepdims=True)
        dw_ref[1:2, :] = jnp.sum(duc * u1, axis=0, keepdims=True)
        dw_ref[2:3, :] = jnp.sum(duc * u, axis=0, keepdims=True)

    act = jax.ShapeDtypeStruct((m, cfg.width), BF16)
    blk = pl.BlockSpec((m, tn), lambda j: (0, j))
    return pl.pallas_call(
        body,
        name=name,
        out_shape=(act, act, act, jax.ShapeDtypeStruct((3, cfg.width), F32)),
        grid=(cfg.width // tn,),
        in_specs=_conv_specs(cfg, tn) + [pl.BlockSpec((None, m, tn), lambda j: (do_sel, 0, j))],
        out_specs=(blk, blk, blk, pl.BlockSpec((3, tn), lambda j: (0, j))),
        compiler_params=_cparams("parallel"),
    )(proj, proj, proj, conv_w, do)


def _tri(lower):
    r = lax.broadcasted_iota(jnp.int32, (LANES, LANES), 0)
    c = lax.broadcasted_iota(jnp.int32, (LANES, LANES), 1)
    return jnp.where((r >= c) if lower else (r <= c), 1.0, 0.0).astype(F32)


def fox_gate_fwd(cfg, fl, b_pad, *, name):
    m = cfg.m
    nblk = m // LANES

    def body(fl_ref, b_ref, c_ref):
        z = fl_ref[...] + b_ref[...]
        logf = jnp.minimum(z, 0.0) - jnp.log(1.0 + jnp.exp(-jnp.abs(z)))
        row = lax.broadcasted_iota(jnp.int32, (m, 1), 0)
        logf = jnp.where(row >= cfg.pad, logf, 0.0)
        tri = _tri(True)
        carry = jnp.zeros((1, LANES), F32)
        for blk in range(nblk):
            cb = jnp.dot(tri, logf[blk * LANES:(blk + 1) * LANES, :], precision=lax.Precision.HIGHEST,
                         preferred_element_type=F32) + carry
            c_ref[blk * LANES:(blk + 1) * LANES, :] = cb
            carry = cb[LANES - 1:LANES, :]

    full = pl.BlockSpec((m, LANES), lambda: (0, 0))
    return pl.pallas_call(
        body,
        name=name,
        out_shape=jax.ShapeDtypeStruct((m, LANES), F32),
        in_specs=[full, pl.BlockSpec((1, LANES), lambda: (0, 0))],
        out_specs=full,
        compiler_params=pltpu.CompilerParams(vmem_limit_bytes=VMEM_LIMIT_BYTES),
    )(fl, b_pad)


def fox_gate_bwd(cfg, fl, b_pad, dc, *, name):
    m = cfg.m
    nblk = m // LANES

    def body(fl_ref, b_ref, dc_ref, dfl_ref, db_ref):
        z = fl_ref[...] + b_ref[...]
        dlogsig = 1.0 / (1.0 + jnp.exp(z))
        row = lax.broadcasted_iota(jnp.int32, (m, 1), 0)
        gate = jnp.where(row >= cfg.pad, dlogsig, 0.0)
        dcv = dc_ref[...]
        tri = _tri(False)
        carry = jnp.zeros((1, LANES), F32)
        db = jnp.zeros((1, LANES), F32)
        for blk in reversed(range(nblk)):
            sl = slice(blk * LANES, (blk + 1) * LANES)
            rb = jnp.dot(tri, dcv[sl, :], precision=lax.Precision.HIGHEST, preferred_element_type=F32) + carry
            carry = rb[0:1, :]
            dfl = rb * gate[sl, :]
            dfl_ref[sl, :] = dfl
            db = db + jnp.sum(dfl, axis=0, keepdims=True)
        db_ref[...] = db

    full = pl.BlockSpec((m, LANES), lambda: (0, 0))
    one = pl.BlockSpec((1, LANES), lambda: (0, 0))
    return pl.pallas_call(
        body,
        name=name,
        out_shape=(jax.ShapeDtypeStruct((m, LANES), F32), jax.ShapeDtypeStruct((1, LANES), F32)),
        in_specs=[full, one, full],
        out_specs=(full, one),
        compiler_params=pltpu.CompilerParams(vmem_limit_bytes=VMEM_LIMIT_BYTES),
    )(fl, b_pad, dc)


def _sigmoid(x):
    return 1.0 / (1.0 + jnp.exp(-x))


def gate_merge_fwd(cfg, y, proj, *, name):
    m, d = cfg.m, cfg.d
    tm, tn = _tile(m, 1088, 16), _tile(d, 512)
    nd = d // tn

    def body(y_ref, g0_ref, g1_ref, g2_ref, o_ref):
        acc = None
        for n, g_ref in enumerate((g0_ref, g1_ref, g2_ref)):
            t = _sigmoid(g_ref[...].astype(F32)) * y_ref[n].astype(F32)
            acc = t if acc is None else acc + t
        o_ref[...] = acc.astype(o_ref.dtype)

    gate = lambda n: pl.BlockSpec((tm, tn), lambda i, j: (i, n * nd + j))
    return pl.pallas_call(
        body,
        name=name,
        out_shape=jax.ShapeDtypeStruct((m, d), BF16),
        grid=(m // tm, nd),
        in_specs=[pl.BlockSpec((3, tm, tn), lambda i, j: (0, i, j)), gate(0), gate(1), gate(2)],
        out_specs=pl.BlockSpec((tm, tn), lambda i, j: (i, j)),
        compiler_params=_cparams("parallel", "parallel"),
    )(y, proj, proj, proj)


def gate_merge_bwd(cfg, dm, y, proj, *, name):
    m, d = cfg.m, cfg.d
    tm, tn = _tile(m, 1088, 16), _tile(d, 512)
    nd = d // tn

    def body(dm_ref, y_ref, g_ref, dy_ref, dg_ref):
        sg = _sigmoid(g_ref[...].astype(F32))
        dmv = dm_ref[...].astype(F32)
        dy_ref[...] = (sg * dmv).astype(dy_ref.dtype)
        dg_ref[...] = (dmv * y_ref[...].astype(F32) * sg * (1.0 - sg)).astype(dg_ref.dtype)

    return pl.pallas_call(
        body,
        name=name,
        out_shape=(jax.ShapeDtypeStruct((3, m, d), BF16), jax.ShapeDtypeStruct((m, 3 * d), BF16)),
        grid=(m // tm, nd, 3),
        in_specs=[
            pl.BlockSpec((tm, tn), lambda i, j, n: (i, j)),
            pl.BlockSpec((None, tm, tn), lambda i, j, n: (n, i, j)),
            pl.BlockSpec((tm, tn), lambda i, j, n: (i, n * nd + j)),
        ],
        out_specs=(
            pl.BlockSpec((None, tm, tn), lambda i, j, n: (n, i, j)),
            pl.BlockSpec((tm, tn), lambda i, j, n: (i, n * nd + j)),
        ),
        compiler_params=_cparams("parallel", "parallel", "parallel"),
    )(dm, y, proj)


def swiglu_fwd(cfg, gu, *, name):
    m, f = cfg.m, cfg.d_ff
    tm, tn = _tile(m, 1088, 16), _tile(f, 512)
    nf = f // tn

    def body(g_ref, u_ref, o_ref):
        g = g_ref[...].astype(F32)
        o_ref[...] = (g * _sigmoid(g) * u_ref[...].astype(F32)).astype(o_ref.dtype)

    return pl.pallas_call(
        body,
        name=name,
        out_shape=jax.ShapeDtypeStruct((m, f), BF16),
        grid=(m // tm, nf),
        in_specs=[pl.BlockSpec((tm, tn), lambda i, j: (i, j)), pl.BlockSpec((tm, tn), lambda i, j: (i, nf + j))],
        out_specs=pl.BlockSpec((tm, tn), lambda i, j: (i, j)),
        compiler_params=_cparams("parallel", "parallel"),
    )(gu, gu)


def swiglu_bwd(cfg, dact, gu, *, name):
    m, f = cfg.m, cfg.d_ff
    tm, tn = _tile(m, 1088, 16), _tile(f, 512)
    nf = f // tn

    def body(da_ref, g_ref, u_ref, o_ref):
        j = pl.program_id(1)
        g, u, da = g_ref[...].astype(F32), u_ref[...].astype(F32), da_ref[...].astype(F32)
        sg = _sigmoid(g)
        dg = da * u * sg * (1.0 + g * (1.0 - sg))
        du = da * g * sg
        o_ref[...] = jnp.where(j < nf, dg, du).astype(o_ref.dtype)

    return pl.pallas_call(
        body,
        name=name,
        out_shape=jax.ShapeDtypeStruct((m, 2 * f), BF16),
        grid=(m // tm, 2 * nf),
        in_specs=[
            pl.BlockSpec((tm, tn), lambda i, j: (i, j % nf)),
            pl.BlockSpec((tm, tn), lambda i, j: (i, j % nf)),
            pl.BlockSpec((tm, tn), lambda i, j: (i, nf + j % nf)),
        ],
        out_specs=pl.BlockSpec((tm, tn), lambda i, j: (i, j)),
        compiler_params=_cparams("parallel", "parallel"),
    )(dact, gu, gu)


def loss_head(cfg, h, target, *, name):
    m, d = cfg.m, cfg.d
    assert cfg.pad + cfg.n_meta == LANES
    tm = LANES
    inv_d = 1.0 / d

    def body(h_ref, t_ref, dh_ref, loss_ref):
        i = pl.program_id(0)

        @pl.when(i == 0)
        def _():
            dh_ref[...] = jnp.zeros_like(dh_ref)
            loss_ref[...] = jnp.zeros_like(loss_ref)

        @pl.when(i > 0)
        def _():
            err = h_ref[...] - t_ref[...]
            dh_ref[...] = err * inv_d
            loss_ref[...] += 0.5 * inv_d * jnp.sum(err * err)

    return pl.pallas_call(
        body,
        name=name,
        out_shape=(jax.ShapeDtypeStruct((m, d), F32), jax.ShapeDtypeStruct((8, LANES), F32)),
        grid=(m // tm,),
        in_specs=[pl.BlockSpec((tm, d), lambda i: (i, 0)), pl.BlockSpec((tm, d), lambda i: (jnp.maximum(i - 1, 0), 0))],
        out_specs=(pl.BlockSpec((tm, d), lambda i: (i, 0)), pl.BlockSpec((8, LANES), lambda i: (0, 0))),
        compiler_params=_cparams("arbitrary"),
    )(h, target)


def adamw(w, g, m_, v_, *, name):
    r, c = w.shape
    c_pad = -(-c // LANES) * LANES
    tr = r
    if r % 8 == 0:
        tr = _tile(r, max(8, (3 << 19) // (4 * c_pad) // 8 * 8), 8)
    bc1 = 1.0 - ADAM_B1 ** ADAM_STEP
    bc2 = 1.0 - ADAM_B2 ** ADAM_STEP

    def body(w_ref, g_ref, m_ref, v_ref, d_ref, nm_ref, nv_ref):
        gv = g_ref[...]
        nm = ADAM_B1 * m_ref[...] + (1.0 - ADAM_B1) * gv
        nv = ADAM_B2 * v_ref[...] + (1.0 - ADAM_B2) * (gv * gv)
        d_ref[...] = -ADAM_LR * ((nm / bc1) / (jnp.sqrt(nv / bc2) + ADAM_EPS) + ADAM_WD * w_ref[...])
        nm_ref[...] = nm
        nv_ref[...] = nv

    blk = pl.BlockSpec((tr, c), lambda i: (i, 0))
    shp = jax.ShapeDtypeStruct((r, c), F32)
    return pl.pallas_call(
        body,
        name=name,
        out_shape=(shp, shp, shp),
        grid=(r // tr,),
        in_specs=[blk, blk, blk, blk],
        out_specs=(blk, blk, blk),
        compiler_params=_cparams("parallel"),
    )(w, g, m_, v_)


def adamw_halves(w, g_own, g_other, core, m_, v_, *, name):
    nl, r, c = w.shape
    r2 = r // 2
    c_pad = -(-c // LANES) * LANES
    tr = _tile(r2, max(8, (3 << 19) // (4 * c_pad) // 8 * 8), 8)
    nr = r2 // tr
    bc1 = 1.0 - ADAM_B1 ** ADAM_STEP
    bc2 = 1.0 - ADAM_B2 ** ADAM_STEP

    def body(core_ref, w_ref, go_ref, gr_ref, m_ref, v_ref, g_ref, d_ref, nm_ref, nv_ref):
        gv = jnp.where(pl.program_id(2) == core_ref[0], go_ref[...], gr_ref[...])
        nm = ADAM_B1 * m_ref[...] + (1.0 - ADAM_B1) * gv
        nv = ADAM_B2 * v_ref[...] + (1.0 - ADAM_B2) * (gv * gv)
        d_ref[...] = -ADAM_LR * ((nm / bc1) / (jnp.sqrt(nv / bc2) + ADAM_EPS) + ADAM_WD * w_ref[...])
        g_ref[...] = gv
        nm_ref[...] = nm
        nv_ref[...] = nv

    full = pl.BlockSpec((None, tr, c), lambda l, i, hf, core_ref: (l, hf * nr + i, 0))
    half = pl.BlockSpec((None, tr, c), lambda l, i, hf, core_ref: (l, i, 0))
    shp = jax.ShapeDtypeStruct((nl, r, c), F32)
    return pl.pallas_call(
        body,
        name=name,
        out_shape=(shp, shp, shp, shp),
        grid_spec=pltpu.PrefetchScalarGridSpec(
            num_scalar_prefetch=1,
            grid=(nl, nr, 2),
            in_specs=[full, half, half, full, full],
            out_specs=(full, full, full, full),
        ),
        compiler_params=_cparams("parallel", "parallel", "arbitrary"),
    )(core, w, g_own, g_other, m_, v_)


_HBM = pl.BlockSpec(memory_space=pltpu.HBM)


def _place():
    x, y, c = lax.axis_index("x"), lax.axis_index("y"), lax.axis_index("c")
    flips = [(1 - x, y), (x, 1 - y), (1 - x, 1 - y)]
    return x, y, c, flips


def gather_weights(shards, *, name):
    nw = len(shards)
    halves = [s.shape[0] // 2 for s in shards]
    assert all(s.shape[0] % 32 == 0 for s in shards)

    def body(*refs):
        ins, outs = refs[:nw], refs[nw:2 * nw]
        send_sems, recv_sems = refs[2 * nw:]
        x, y, c, flips = _place()
        mine = 2 * x + y
        sibling = (x, y, 1 - c)

        def half(w, chip, core):
            return outs[w].at[chip, pl.ds(core * halves[w], halves[w]), :]

        def copy(w, k, src, chip, core, to):
            return pltpu.make_async_remote_copy(src_ref=src, dst_ref=half(w, chip, core), send_sem=send_sems.at[6 * w + k],
                                                recv_sem=recv_sems.at[6 * w + k], device_id=to, device_id_type=MESH)

        sent = []
        for w in range(nw):
            src = ins[w].at[pl.ds(c * halves[w], halves[w]), :]
            for k, (fx, fy) in enumerate(flips):
                sent.append(copy(w, k, src, mine, c, (fx, fy, c)))
                sent[-1].start()
        for w in range(nw):
            for k, (fx, fy) in enumerate(flips):
                theirs = 2 * fx + fy
                copy(w, k, half(w, theirs, c), theirs, c, sibling).wait_recv()
                sent.append(copy(w, 3 + k, half(w, theirs, c), theirs, c, sibling))
                sent[-1].start()
        for w in range(nw):
            for k, (fx, fy) in enumerate(flips):
                theirs = 2 * fx + fy
                copy(w, 3 + k, half(w, theirs, 1 - c), theirs, 1 - c, sibling).wait_recv()
        for cp in sent:
            cp.wait_send()

    return pl.pallas_call(
        body,
        name=name,
        out_shape=tuple(jax.ShapeDtypeStruct((N_CHIPS,) + s.shape, s.dtype) for s in shards),
        in_specs=[_HBM] * nw,
        out_specs=tuple([_HBM] * nw),
        scratch_shapes=[pltpu.SemaphoreType.DMA((6 * nw,)), pltpu.SemaphoreType.DMA((6 * nw,))],
    )(*shards)


def swap_halves(grads, *, name):
    nw = len(grads)
    halves = [g.shape[1] // 2 for g in grads]

    def body(*refs):
        ins, outs = refs[:nw], refs[nw:2 * nw]
        send_sems, recv_sems = refs[2 * nw:]
        x, y, c, _ = _place()
        copies = [
            pltpu.make_async_remote_copy(src_ref=ins[w].at[:, pl.ds((1 - c) * halves[w], halves[w]), :], dst_ref=outs[w],
                                         send_sem=send_sems.at[w], recv_sem=recv_sems.at[w], device_id=(x, y, 1 - c),
                                         device_id_type=MESH)
            for w in range(nw)
        ]
        for cp in copies:
            cp.start()
        for cp in copies:
            cp.wait()

    return pl.pallas_call(
        body,
        name=name,
        out_shape=tuple(jax.ShapeDtypeStruct((N_CHIPS, h, g.shape[2]), g.dtype) for g, h in zip(grads, halves)),
        in_specs=[_HBM] * nw,
        out_specs=tuple([_HBM] * nw),
        scratch_shapes=[pltpu.SemaphoreType.DMA((nw,)), pltpu.SemaphoreType.DMA((nw,))],
    )(*grads)


def exchange_chips(parts, *, name):
    nw = len(parts)

    def body(*refs):
        ins, outs = refs[:nw], refs[nw:2 * nw]
        send_sems, recv_sems = refs[2 * nw:]
        _, _, c, flips = _place()
        copies = [
            pltpu.make_async_remote_copy(src_ref=ins[w].at[2 * fx + fy], dst_ref=outs[w].at[k], send_sem=send_sems.at[3 * w + k],
                                         recv_sem=recv_sems.at[3 * w + k], device_id=(fx, fy, c), device_id_type=MESH)
            for w in range(nw) for k, (fx, fy) in enumerate(flips)
        ]
        for cp in copies:
            cp.start()
        for cp in copies:
            cp.wait()

    return pl.pallas_call(
        body,
        name=name,
        out_shape=tuple(jax.ShapeDtypeStruct((3,) + p.shape[1:], p.dtype) for p in parts),
        in_specs=[_HBM] * nw,
        out_specs=tuple([_HBM] * nw),
        scratch_shapes=[pltpu.SemaphoreType.DMA((3 * nw,)), pltpu.SemaphoreType.DMA((3 * nw,))],
    )(*parts)


_SEM = pl.BlockSpec(memory_space=pltpu.SEMAPHORE)
_EFFECT = pltpu.SideEffectType.DATAFLOW_SIDE_EFFECTING


def _gather_plan(halves):
    def plan(src_refs, land_refs, arrival):
        x, y, c, flips = _place()
        mine = 2 * x + y
        out = []
        for w, h in enumerate(halves):
            for fx, fy in flips:
                slot = (2 * fx + fy) if arrival else mine
                out.append((src_refs[w].at[pl.ds(c * h, h), :], land_refs[w].at[slot, pl.ds(c * h, h), :], (fx, fy, c)))
        return out
    return plan


def _exchange_plan(nw):
    def plan(src_refs, land_refs, arrival):
        _, _, c, flips = _place()
        return [(src_refs[w].at[2 * fx + fy], land_refs[w].at[k], (fx, fy, c)) for w in range(nw) for k, (fx, fy) in enumerate(flips)]
    return plan


def copies_start(srcs, land_shapes, plan, *, name):
    lands = [lax.empty(s, a.dtype) for s, a in zip(land_shapes, srcs)]
    n_in = len(srcs) + len(lands)
    n_copies = 3 * len(srcs)

    def body(*refs):
        src_refs, land_refs = refs[:len(srcs)], refs[len(srcs):n_in]
        send_sems, recv_sems, token = refs[n_in], refs[n_in + 1], refs[-1]
        for i, (src, dst, to) in enumerate(plan(src_refs, land_refs, False)):
            pltpu.make_async_remote_copy(src_ref=src, dst_ref=dst, send_sem=send_sems.at[i], recv_sem=recv_sems.at[i],
                                         device_id=to, device_id_type=MESH).start()
        token[...] = jnp.zeros_like(token)

    operands = list(srcs) + lands
    out = pl.pallas_call(
        body,
        name=name,
        out_shape=(pltpu.SemaphoreType.DMA((n_copies,)), pltpu.SemaphoreType.DMA((n_copies,)),
                   *[pltpu.HBM(a.shape, a.dtype) for a in operands], jax.ShapeDtypeStruct((8, LANES), F32)),
        in_specs=[_HBM] * n_in,
        out_specs=(_SEM, _SEM, *[_HBM] * n_in, pl.BlockSpec(memory_space=pltpu.VMEM)),
        input_output_aliases={i: 2 + i for i in range(n_in)},
        compiler_params=pltpu.CompilerParams(has_side_effects=_EFFECT),
    )(*[pltpu.with_memory_space_constraint(a, pltpu.HBM) for a in operands])
    return out[0], out[1], list(out[2:2 + len(srcs)]), list(out[2 + len(srcs):2 + n_in]), out[-1]


def copies_wait(send_sems, recv_sems, srcs, lands, plan, after, *, name):
    n_in = len(srcs) + len(lands)

    def body(*refs):
        src_refs, land_refs = refs[:len(srcs)], refs[len(srcs):n_in]
        send_ref, recv_ref, token = refs[n_in], refs[n_in + 1], refs[-1]
        token[...] = jnp.zeros_like(token)
        for i, (src, dst, to) in enumerate(plan(src_refs, land_refs, True)):
            copy = pltpu.make_async_remote_copy(src_ref=src, dst_ref=dst, send_sem=send_ref.at[i], recv_sem=recv_ref.at[i],
                                                device_id=to, device_id_type=MESH)
            copy.wait_send()
            copy.wait_recv()

    operands = list(srcs) + list(lands)
    out = pl.pallas_call(
        body,
        name=name,
        out_shape=(*[pltpu.HBM(a.shape, a.dtype) for a in operands], jax.ShapeDtypeStruct((8, LANES), F32)),
        in_specs=[_HBM] * n_in + [_SEM, _SEM, pl.BlockSpec(memory_space=pl.ANY)],
        out_specs=(*[_HBM] * n_in, pl.BlockSpec(memory_space=pltpu.VMEM)),
        input_output_aliases={i: i for i in range(n_in)},
        compiler_params=pltpu.CompilerParams(has_side_effects=_EFFECT),
    )(*operands, send_sems, recv_sems, after)
    return list(out[:len(srcs)]), list(out[len(srcs):n_in]), out[-1]


def forward_halves(lands, *, name):
    nw = len(lands)
    halves = [a.shape[1] // 2 for a in lands]

    def body(*refs):
        ins, outs = refs[:nw], refs[nw:2 * nw]
        send_sems, recv_sems = refs[2 * nw:]
        x, y, c, flips = _place()
        copies = []
        for w, h in enumerate(halves):
            for k, (fx, fy) in enumerate(flips):
                rows = (2 * fx + fy, pl.ds(c * h, h), slice(None))
                copies.append(pltpu.make_async_remote_copy(src_ref=ins[w].at[rows], dst_ref=outs[w].at[rows], send_sem=send_sems.at[3 * w + k],
                                                           recv_sem=recv_sems.at[3 * w + k], device_id=(x, y, 1 - c), device_id_type=MESH))
        for cp in copies:
            cp.start()
        for cp in copies:
            cp.wait()

    return pl.pallas_call(
        body,
        name=name,
        out_shape=tuple(jax.ShapeDtypeStruct(a.shape, a.dtype) for a in lands),
        in_specs=[_HBM] * nw,
        out_specs=tuple([_HBM] * nw),
        input_output_aliases={w: w for w in range(nw)},
        scratch_shapes=[pltpu.SemaphoreType.DMA((3 * nw,)), pltpu.SemaphoreType.DMA((3 * nw,))],
    )(*lands)


def share_halves(sums, *, name):
    nw = len(sums)

    def body(*refs):
        ins, outs = refs[:nw], refs[nw:2 * nw]
        send_sems, recv_sems = refs[2 * nw:]
        x, y, c, _ = _place()
        copies = [
            pltpu.make_async_remote_copy(src_ref=ins[w], dst_ref=outs[w], send_sem=send_sems.at[w], recv_sem=recv_sems.at[w],
                                         device_id=(x, y, 1 - c), device_id_type=MESH)
            for w in range(nw)
        ]
        for cp in copies:
            cp.start()
        for cp in copies:
            cp.wait()

    return pl.pallas_call(
        body,
        name=name,
        out_shape=tuple(jax.ShapeDtypeStruct(s.shape, s.dtype) for s in sums),
        in_specs=[_HBM] * nw,
        out_specs=tuple([_HBM] * nw),
        scratch_shapes=[pltpu.SemaphoreType.DMA((nw,)), pltpu.SemaphoreType.DMA((nw,))],
    )(*sums)


def gather_blocks(block, *, reduce, name):
    rows, cols = block.shape

    def body(x_ref, out_ref, *rest):
        if reduce:
            buf_ref, send_sems, recv_sems = rest
        else:
            send_sems, recv_sems = rest
            buf_ref = out_ref
        x, y, c, flips = _place()
        me, sibling = (x, y, c), (x, y, 1 - c)

        def slot(px, py, pc):
            return buf_ref.at[4 * px + 2 * py + pc]

        def copy(k, blk, to, src=None):
            return pltpu.make_async_remote_copy(src_ref=slot(*blk) if src is None else src, dst_ref=slot(*blk),
                                                send_sem=send_sems.at[k], recv_sem=recv_sems.at[k], device_id=to,
                                                device_id_type=MESH)

        buf_ref[4 * x + 2 * y + c] = x_ref[...]
        first = [copy(0, me, sibling, src=x_ref)]
        first += [copy(1 + j, me, (*chip, c), src=x_ref) for j, chip in enumerate(flips)]
        for cp in first:
            cp.start()
        passed = [copy(4 + j, (*chip, c), sibling) for j, chip in enumerate(flips)]
        for j, chip in enumerate(flips):
            copy(1 + j, (*chip, c), me).wait_recv()
            passed[j].start()
        copy(0, sibling, me).wait_recv()
        for j, chip in enumerate(flips):
            copy(4 + j, (*chip, 1 - c), me).wait_recv()
        for cp in first + passed:
            cp.wait_send()
        if reduce:
            acc = buf_ref[0]
            for dev in range(1, N_DEV):
                acc = acc + buf_ref[dev]
            out_ref[...] = acc

    vmem = pl.BlockSpec(memory_space=pltpu.VMEM)
    sems = [pltpu.SemaphoreType.DMA((7,)), pltpu.SemaphoreType.DMA((7,))]
    if reduce:
        out_shape = jax.ShapeDtypeStruct((rows, cols), block.dtype)
        scratch = [pltpu.VMEM((N_DEV, rows, cols), block.dtype)] + sems
    else:
        out_shape = jax.ShapeDtypeStruct((N_DEV, rows, cols), block.dtype)
        scratch = sems
    return pl.pallas_call(
        body,
        name=name,
        out_shape=out_shape,
        in_specs=[vmem],
        out_specs=vmem,
        scratch_shapes=scratch,
        compiler_params=pltpu.CompilerParams(vmem_limit_bytes=VMEM_LIMIT_BYTES),
    )(block)


def add_own_half(grad, recv, core, *, name):
    _, r2, cols = recv.shape
    tr = _tile(r2, max(16, (1 << 20) // (2 * cols) // 16 * 16), 16)
    nr = r2 // tr

    def body(core_ref, g_ref, r_ref, o_ref):
        o_ref[...] = (g_ref[...].astype(F32) + r_ref[...].astype(F32)).astype(o_ref.dtype)

    return pl.pallas_call(
        body,
        name=name,
        out_shape=jax.ShapeDtypeStruct(recv.shape, BF16),
        grid_spec=pltpu.PrefetchScalarGridSpec(
            num_scalar_prefetch=1,
            grid=(N_CHIPS, nr),
            in_specs=[
                pl.BlockSpec((None, tr, cols), lambda k, i, core_ref: (k, core_ref[0] * nr + i, 0)),
                pl.BlockSpec((None, tr, cols), lambda k, i, core_ref: (k, i, 0)),
            ],
            out_specs=pl.BlockSpec((None, tr, cols), lambda k, i, core_ref: (k, i, 0)),
        ),
        compiler_params=_cparams("parallel", "parallel"),
    )(core, grad, recv)


def sum_chips(part, recv, chip, *, name):
    _, r2, cols = part.shape
    tr = _tile(r2, max(16, (1 << 20) // (2 * cols) // 16 * 16), 16)

    def body(chip_ref, p_ref, r_ref, o_ref):
        acc = p_ref[...].astype(F32)
        for k in range(3):
            acc = acc + r_ref[k].astype(F32)
        o_ref[...] = acc

    return pl.pallas_call(
        body,
        name=name,
        out_shape=jax.ShapeDtypeStruct((r2, cols), F32),
        grid_spec=pltpu.PrefetchScalarGridSpec(
            num_scalar_prefetch=1,
            grid=(r2 // tr,),
            in_specs=[
                pl.BlockSpec((None, tr, cols), lambda i, chip_ref: (chip_ref[0], i, 0)),
                pl.BlockSpec((3, tr, cols), lambda i, chip_ref: (0, i, 0)),
            ],
            out_specs=pl.BlockSpec((tr, cols), lambda i, chip_ref: (i, 0)),
        ),
        compiler_params=_cparams("parallel"),
    )(chip, part, recv)


WEIGHTS = ("w_in", "w_uq", "w_ukv", "w_branch", "w_out", "w_ffn_in", "w_ffn_out")
GAINS = ("g_mix_pre", "g_mix_post", "g_ffn_pre", "g_ffn_post")


def layer_fwd(cfg, h, w, s, tabs, tag):
    m, d, hd = cfg.m, cfg.d, cfg.heads
    fox_blk = cfg.off_fox // LANES
    hn = rmsnorm_fwd(h, s["g_mix_pre"], BF16, name=f"norm_mix_pre{tag}")
    proj = matmul(hn, w["w_in"], "nn", BF16, tm=m, tn=_tile(cfg.d_inp, 512), tk=d, name=f"proj{tag}")
    fl = matmul(hn, w["w_in"][:, cfg.off_fl:cfg.off_fl + LANES], "nn", F32, tm=m, tn=LANES, tk=d, name=f"proj_forget{tag}")
    cqn = rmsnorm_fwd(proj, s["g_q_lat"], BF16, width=cfg.q_rank, col_blk=cfg.off_cq // cfg.q_rank, name=f"norm_q{tag}")
    ckvn = rmsnorm_fwd(proj, s["g_kv_lat"], BF16, width=cfg.kv_rank, col_blk=cfg.off_ckv // cfg.kv_rank, name=f"norm_kv{tag}")
    q = matmul(cqn, w["w_uq"], "nn", BF16, tm=m, tn=_tile(2 * cfg.width, 512), tk=cfg.q_rank, name=f"up_q{tag}")
    kv = matmul(ckvn, w["w_ukv"], "nn", BF16, tm=m, tn=_tile(2 * cfg.width, 512), tk=cfg.kv_rank, name=f"up_kv{tag}")
    qf, kf = mla_prep_fwd(cfg, q, kv, proj, tabs[0], name=f"mla_prep{tag}")
    o_a, lse_a = attn_fwd(qf, kf, kv, heads=hd, dk=2 * LANES, dv=LANES, qblk0=0, kblk0=0, vblk0=hd,
                          scale=(LANES + ROPE) ** -0.5, pad=cfg.pad, name=f"mla_attn{tag}")
    o_b = conv_fwd(cfg, proj, s["conv_w"], name=f"conv{tag}")
    b_pad = jnp.pad(s["b_forget"], (0, LANES - hd)).reshape(1, LANES)
    cum = fox_gate_fwd(cfg, fl, b_pad, name=f"fox_gate{tag}")
    cum_t = cum[:, :hd].T
    decay = (cum_t[:, :, None], cum_t[:, None, :])
    o_c, lse_c = attn_fwd(proj, proj, proj, heads=hd, dk=LANES, dv=LANES, qblk0=fox_blk, kblk0=fox_blk + hd,
                          vblk0=fox_blk + 2 * hd, scale=LANES ** -0.5, pad=cfg.pad, decay=decay, name=f"fox_attn{tag}")
    o = jnp.stack([o_a, o_b, o_c])
    y = matmul(o, w["w_branch"], "nn", BF16, tm=m, tn=_tile(d, 512), tk=cfg.width, name=f"branch{tag}")
    merged = gate_merge_fwd(cfg, y, proj, name=f"merge{tag}")
    mix = matmul(merged, w["w_out"], "nn", F32, tm=m, tn=_tile(d, 256), tk=d, name=f"out_proj{tag}")
    h_mid = rmsnorm_fwd(mix, s["g_mix_post"], F32, res=h, name=f"norm_mix_post{tag}")
    hn2 = rmsnorm_fwd(h_mid, s["g_ffn_pre"], BF16, name=f"norm_ffn_pre{tag}")
    gu = matmul(hn2, w["w_ffn_in"], "nn", BF16, tm=m, tn=_tile(2 * cfg.d_ff, 512), tk=d, name=f"ffn_in{tag}")
    act = swiglu_fwd(cfg, gu, name=f"swiglu{tag}")
    f = matmul(act, w["w_ffn_out"], "nn", F32, tm=m, tn=_tile(d, 512), tk=_tile(cfg.d_ff, 1408), name=f"ffn_out{tag}")
    h_next = rmsnorm_fwd(f, s["g_ffn_post"], F32, res=h_mid, name=f"norm_ffn_post{tag}")
    saved = dict(h=h, hn=hn, proj=proj, fl=fl, cqn=cqn, ckvn=ckvn, kv=kv, qf=qf, kf=kf, lse_a=lse_a,
                 b_pad=b_pad, decay=decay, lse_c=lse_c, o=o, y=y, merged=merged, mix=mix, h_mid=h_mid,
                 hn2=hn2, gu=gu, act=act, f=f)
    return h_next, saved


def layer_bwd(cfg, dh, w, s, r, tabs, tag):
    m, d, hd = cfg.m, cfg.d, cfg.heads
    fox_blk = cfg.off_fox // LANES
    tk_m = m
    df, dg4 = rmsnorm_bwd(r["f"], s["g_ffn_post"], dh, BF16, name=f"norm_ffn_post_bwd{tag}")
    dact = matmul(df, w["w_ffn_out"], "nt", BF16, tm=m, tn=_tile(cfg.d_ff, 512), tk=d, name=f"ffn_out_dx{tag}")
    dw_fo = matmul(r["act"], df, "tn", BF16, tm=_tile(cfg.d_ff, 512), tn=_tile(d, 1024), tk=tk_m, name=f"ffn_out_dw{tag}")
    dgu = swiglu_bwd(cfg, dact, r["gu"], name=f"swiglu_bwd{tag}")
    dhn2 = matmul(dgu, w["w_ffn_in"], "nt", F32, tm=m, tn=_tile(d, 512), tk=_tile(2 * cfg.d_ff, 1408), name=f"ffn_in_dx{tag}")
    dw_fi = matmul(r["hn2"], dgu, "tn", BF16, tm=_tile(d, 1024), tn=_tile(2 * cfg.d_ff, 512), tk=tk_m, name=f"ffn_in_dw{tag}")
    dh_mid, dg3 = rmsnorm_bwd(r["h_mid"], s["g_ffn_pre"], dhn2, F32, dres=dh, name=f"norm_ffn_pre_bwd{tag}")
    dmix, dg2 = rmsnorm_bwd(r["mix"], s["g_mix_post"], dh_mid, BF16, name=f"norm_mix_post_bwd{tag}")
    dmerged = matmul(dmix, w["w_out"], "nt", BF16, tm=m, tn=_tile(d, 512), tk=d, name=f"out_proj_dx{tag}")
    dw_out = matmul(r["merged"], dmix, "tn", BF16, tm=_tile(d, 1024), tn=_tile(d, 512), tk=tk_m, name=f"out_proj_dw{tag}")
    dy, dgl = gate_merge_bwd(cfg, dmerged, r["y"], r["proj"], name=f"merge_bwd{tag}")
    do = matmul(dy, w["w_branch"], "nt", BF16, tm=m, tn=_tile(cfg.width, 512), tk=d, name=f"branch_dx{tag}")
    dw_br = matmul(r["o"], dy, "tn", BF16, tm=_tile(cfg.width, 1024), tn=_tile(d, 512), tk=tk_m, name=f"branch_dw{tag}")
    dqf, dkf, dv_a = attn_bwd(r["qf"], r["kf"], r["kv"], do, 0, r["lse_a"], heads=hd, dk=2 * LANES, dv=LANES,
                              qblk0=0, kblk0=0, vblk0=hd, scale=(LANES + ROPE) ** -0.5, pad=cfg.pad, name=f"mla_attn_bwd{tag}")
    dq, dkn, dkpe = mla_prep_bwd(cfg, dqf, dkf, tabs[1], name=f"mla_prep_bwd{tag}")
    dkv = jnp.concatenate([dkn, dv_a], axis=1)
    dcqn = matmul(dq, w["w_uq"], "nt", F32, tm=m, tn=cfg.q_rank, tk=2 * cfg.width, name=f"up_q_dx{tag}")
    dw_uq = matmul(r["cqn"], dq, "tn", BF16, tm=cfg.q_rank, tn=_tile(2 * cfg.width, 512), tk=tk_m, name=f"up_q_dw{tag}")
    dckvn = matmul(dkv, w["w_ukv"], "nt", F32, tm=m, tn=cfg.kv_rank, tk=2 * cfg.width, name=f"up_kv_dx{tag}")
    dw_ukv = matmul(r["ckvn"], dkv, "tn", BF16, tm=cfg.kv_rank, tn=_tile(2 * cfg.width, 512), tk=tk_m, name=f"up_kv_dw{tag}")
    dcq, dgq = rmsnorm_bwd(r["proj"], s["g_q_lat"], dcqn, BF16, width=cfg.q_rank, col_blk=cfg.off_cq // cfg.q_rank,
                           name=f"norm_q_bwd{tag}")
    dckv, dgkv = rmsnorm_bwd(r["proj"], s["g_kv_lat"], dckvn, BF16, width=cfg.kv_rank, col_blk=cfg.off_ckv // cfg.kv_rank,
                             name=f"norm_kv_bwd{tag}")
    dcb, dcc, dcx, dconv_w = conv_bwd(cfg, r["proj"], s["conv_w"], do, 1, name=f"conv_bwd{tag}")
    dfq, dfk, dfv, dck = attn_bwd(r["proj"], r["proj"], r["proj"], do, 2, r["lse_c"], heads=hd, dk=LANES, dv=LANES,
                                  qblk0=fox_blk, kblk0=fox_blk + hd, vblk0=fox_blk + 2 * hd, scale=LANES ** -0.5,
                                  pad=cfg.pad, decay=r["decay"], name=f"fox_attn_bwd{tag}")
    dc = jnp.pad(dck[:, 0, :].T, ((0, 0), (0, LANES - hd)))
    dfl, dbf = fox_gate_bwd(cfg, r["fl"], r["b_pad"], dc, name=f"fox_gate_bwd{tag}")
    tail = jnp.zeros((m, cfg.d_inp - cfg.off_fl - LANES), BF16)
    dproj = jnp.concatenate([dgl, dcq, dckv, dcb, dcc, dcx, dfq, dfk, dfv, dkpe.astype(BF16), dfl.astype(BF16), tail], axis=1)
    dhn = matmul(dproj, w["w_in"], "nt", F32, tm=m, tn=_tile(d, 512), tk=_tile(cfg.d_inp, 1536), name=f"proj_dx{tag}")
    dw_in = matmul(r["hn"], dproj, "tn", BF16, tm=_tile(d, 1024), tn=_tile(cfg.d_inp, 512), tk=tk_m, name=f"proj_dw{tag}")
    dh_in, dg1 = rmsnorm_bwd(r["h"], s["g_mix_pre"], dhn, F32, dres=dh_mid, name=f"norm_mix_pre_bwd{tag}")
    dws = dict(w_in=dw_in, w_uq=dw_uq, w_ukv=dw_ukv, w_branch=dw_br, w_out=dw_out, w_ffn_in=dw_fi, w_ffn_out=dw_fo)
    dsmall = dict(g_mix_pre=dg1[0], g_mix_post=dg2[0], g_ffn_pre=dg3[0], g_ffn_post=dg4[0], g_q_lat=dgq[0], g_kv_lat=dgkv[0],
                  b_forget=dbf[0, :hd], conv_w=dconv_w)
    return dh_in, dws, dsmall


def local_step(cfg, x, target, meta, layer_params, grads_done):
    h = jnp.concatenate([jnp.zeros((cfg.pad, cfg.d), F32), meta, x], axis=0)
    cos, s1, s2 = rope_tables(cfg)
    tabs = ((cos, s1, s2), (cos, -s1, -s2))
    saved = []
    for l in range(cfg.depth):
        w, s = layer_params(l, h)
        h, r = layer_fwd(cfg, h, w, s, tabs, f"_{l}")
        saved.append((w, s, r))
    dh, loss = loss_head(cfg, h, target, name="loss_head")
    dsmalls, token = [None] * cfg.depth, None
    for l in reversed(range(cfg.depth)):
        w, s, r = saved[l]
        if token is not None:
            s = {**s, "g_ffn_post": s["g_ffn_post"] + token[0, 0]}
        dh, dws, dsmalls[l] = layer_bwd(cfg, dh, w, s, r, tabs, f"_{l}")
        token = grads_done(l, dws)
    first = cfg.pad + cfg.n_meta
    return loss, dh[first:], dh[cfg.pad:first], dsmalls


def _cols_from_chips(g):
    return jnp.transpose(g, (1, 0, 2)).reshape(g.shape[1], N_CHIPS * g.shape[2])


def _cols_to_chips(w):
    r, c = w.shape
    return jnp.transpose(w.reshape(r, N_CHIPS, c // N_CHIPS), (1, 0, 2))


def full_weights(cfg, g):
    br = _cols_from_chips(g["w_branch"]).reshape(3, cfg.width, cfg.d)
    return dict(
        w_in=pack_w_in(cfg, _cols_from_chips(g["w_in"])),
        w_uq=pack_w_uq(cfg, _cols_from_chips(g["w_uq"])),
        w_ukv=pack_w_ukv(cfg, _cols_from_chips(g["w_ukv"])),
        w_branch=br,
        w_out=g["w_out"].reshape(cfg.d, cfg.d),
        w_ffn_in=_cols_from_chips(g["w_ffn_in"]),
        w_ffn_out=g["w_ffn_out"].reshape(cfg.d_ff, cfg.d),
    )


def chip_grads(cfg, dw):
    return dict(
        w_in=_cols_to_chips(unpack_w_in(cfg, dw["w_in"])),
        w_uq=_cols_to_chips(unpack_w_uq(cfg, dw["w_uq"])),
        w_ukv=_cols_to_chips(unpack_w_ukv(cfg, dw["w_ukv"])),
        w_branch=_cols_to_chips(dw["w_branch"].reshape(3 * cfg.width, cfg.d)),
        w_out=dw["w_out"].reshape(N_CHIPS, cfg.d // N_CHIPS, cfg.d),
        w_ffn_in=_cols_to_chips(dw["w_ffn_in"]),
        w_ffn_out=dw["w_ffn_out"].reshape(N_CHIPS, cfg.d_ff // N_CHIPS, cfg.d),
    )


def _small_rows(cfg):
    return dict(g_mix_pre=cfg.d // LANES, g_mix_post=cfg.d // LANES, g_ffn_pre=cfg.d // LANES, g_ffn_post=cfg.d // LANES,
                g_q_lat=cfg.q_rank // LANES, g_kv_lat=cfg.kv_rank // LANES, b_forget=1, conv_w=3 * cfg.width // LANES)


def pack_small(cfg, loss, dmeta, dsmalls):
    parts = [loss[0:1, :], dmeta.reshape(-1, LANES)]
    for ds in dsmalls:
        for k in _small_rows(cfg):
            v = ds[k]
            if k == "b_forget":
                v = jnp.pad(v, (0, LANES - cfg.heads))
            parts.append(v.reshape(-1, LANES))
    rows = sum(p.shape[0] for p in parts)
    parts.append(jnp.zeros((-rows % 8, LANES), F32))
    return jnp.concatenate(parts, axis=0)


def unpack_small(cfg, block):
    loss = block[0, 0]
    n = cfg.n_meta * cfg.d // LANES
    dmeta = block[1:1 + n].reshape(cfg.n_meta, cfg.d)
    at = 1 + n
    out = []
    for _ in range(cfg.depth):
        ds = {}
        for k, rows in _small_rows(cfg).items():
            v = block[at:at + rows]
            at += rows
            if k == "b_forget":
                v = v[0, :cfg.heads]
            elif k == "conv_w":
                v = v.reshape(3, cfg.width)
            else:
                v = v.reshape(-1)
            ds[k] = v
        out.append(ds)
    return loss, dmeta, out


def kernel(x, meta, w_in, b_forget, g_q_lat, g_kv_lat, w_uq, w_ukv, conv_w, w_branch, w_out, w_ffn_in, w_ffn_out, g_mix_pre, g_mix_post, g_ffn_pre, g_ffn_post, loss_target, m_meta, m_w_in, m_b_forget, m_g_q_lat, m_g_kv_lat, m_w_uq, m_w_ukv, m_conv_w, m_w_branch, m_w_out, m_w_ffn_in, m_w_ffn_out, m_g_mix_pre, m_g_mix_post, m_g_ffn_pre, m_g_ffn_post, v_meta, v_w_in, v_b_forget, v_g_q_lat, v_g_kv_lat, v_w_uq, v_w_ukv, v_conv_w, v_w_branch, v_w_out, v_w_ffn_in, v_w_ffn_out, v_g_mix_pre, v_g_mix_post, v_g_ffn_pre, v_g_ffn_post):
    cfg = CFG
    names = ("meta", "w_in", "b_forget", "g_q_lat", "g_kv_lat", "w_uq", "w_ukv", "conv_w", "w_branch", "w_out", "w_ffn_in",
             "w_ffn_out", "g_mix_pre", "g_mix_post", "g_ffn_pre", "g_ffn_post")
    params = dict(zip(names, (meta, w_in, b_forget, g_q_lat, g_kv_lat, w_uq, w_ukv, conv_w, w_branch, w_out, w_ffn_in, w_ffn_out,
                              g_mix_pre, g_mix_post, g_ffn_pre, g_ffn_post)))
    mom1 = dict(zip(names, (m_meta, m_w_in, m_b_forget, m_g_q_lat, m_g_kv_lat, m_w_uq, m_w_ukv, m_conv_w, m_w_branch, m_w_out,
                            m_w_ffn_in, m_w_ffn_out, m_g_mix_pre, m_g_mix_post, m_g_ffn_pre, m_g_ffn_post)))
    mom2 = dict(zip(names, (v_meta, v_w_in, v_b_forget, v_g_q_lat, v_g_kv_lat, v_w_uq, v_w_ukv, v_conv_w, v_w_branch, v_w_out,
                            v_w_ffn_in, v_w_ffn_out, v_g_mix_pre, v_g_mix_post, v_g_ffn_pre, v_g_ffn_post)))
    xi, yi, ci = lax.axis_index("x"), lax.axis_index("y"), lax.axis_index("c")
    chip = 2 * xi + yi
    chip_arr = jnp.reshape(chip, (1,)).astype(jnp.int32)
    core_arr = jnp.reshape(ci, (1,)).astype(jnp.int32)

    meta_all = gather_blocks(meta, reduce=False, name="gather_meta")[0::2]
    meta_full = jnp.transpose(meta_all, (1, 0, 2)).reshape(cfg.n_meta, cfg.d)
    conv_rows = conv_w.reshape(cfg.depth * 3, cfg.width // N_CHIPS)
    conv_all = gather_blocks(conv_rows, reduce=False, name="gather_conv_w")[0::2]
    conv_full = jnp.transpose(conv_all, (1, 0, 2)).reshape(cfg.depth, 3, cfg.width)

    def shard2d(name, l, after=None):
        w = params[name][l]
        if after is not None:
            w = w + after
        return w.reshape(-1, w.shape[-1]).astype(BF16)

    is_mine = (jnp.arange(N_CHIPS) == chip)[:, None, None]
    shard_shapes = [shard2d(n, 0).shape for n in WEIGHTS]
    gather_plan = _gather_plan([s[0] // 2 for s in shard_shapes])
    exchange_plan = _exchange_plan(len(WEIGHTS))

    def gather_start(l, after=None):
        return copies_start([shard2d(n, l, after) for n in WEIGHTS], [(N_CHIPS,) + s for s in shard_shapes], gather_plan,
                            name=f"gather_start_{l}")

    in_flight = {0: gather_start(0)}

    def layer_params(l, h):
        send_sems, recv_sems, own, lands, _ = in_flight.pop(l)
        own, lands, landed = copies_wait(send_sems, recv_sems, own, lands, gather_plan, h, name=f"gather_wait_{l}")
        lands = forward_halves(lands, name=f"forward_halves_{l}")
        got = [jnp.where(is_mine, o[None], g) for o, g in zip(own, lands)]
        s = dict(g_mix_pre=g_mix_pre[l], g_mix_post=g_mix_post[l], g_ffn_pre=g_ffn_pre[l], g_ffn_post=g_ffn_post[l],
                 g_q_lat=g_q_lat[l], g_kv_lat=g_kv_lat[l], b_forget=b_forget[l], conv_w=conv_full[l])
        if l + 1 < cfg.depth:
            in_flight[l + 1] = gather_start(l + 1, landed[0, 0])
            s["g_mix_pre"] = s["g_mix_pre"] + in_flight[l + 1][4][0, 0]
        return full_weights(cfg, dict(zip(WEIGHTS, got))), s

    grads = {n: [None] * cfg.depth for n in WEIGHTS}
    exchanging = []

    def finish_exchange(after):
        l, (send_sems, recv_sems, parts, lands, _) = exchanging.pop()
        parts, others, _ = copies_wait(send_sems, recv_sems, parts, lands, exchange_plan, after, name=f"exchange_wait_{l}")
        for n, p, o in zip(WEIGHTS, parts, others):
            grads[n][l] = sum_chips(p, o, chip_arr, name=f"sum_chips_{n}_{l}")

    def grads_done(l, dws):
        send = chip_grads(cfg, dws)
        mine = [send[n] for n in WEIGHTS]
        theirs = swap_halves(mine, name=f"swap_halves_{l}")
        parts = [add_own_half(g, t, core_arr, name=f"add_own_half_{n}_{l}") for n, g, t in zip(WEIGHTS, mine, theirs)]
        started = copies_start(parts, [(3,) + p.shape[1:] for p in parts], exchange_plan, name=f"exchange_start_{l}")
        if exchanging:
            finish_exchange(started[4])
        exchanging.append((l, started))
        return started[4]

    loss, grad_x, dmeta, dsmalls = local_step(cfg, x[0], loss_target[0], meta_full, layer_params, grads_done)
    finish_exchange(exchanging[-1][1][4])
    own_half = [jnp.stack(grads[n]) for n in WEIGHTS]
    other_half = share_halves(own_half, name="share_halves")
    grad, delta, new_m, new_v = {}, {}, {}, {}
    for n, mine_, theirs_ in zip(WEIGHTS, own_half, other_half):
        shp = params[n].shape
        three_d = lambda a: a.reshape(cfg.depth, -1, shp[-1])
        out = adamw_halves(three_d(params[n]), mine_, theirs_, core_arr, three_d(mom1[n]), three_d(mom2[n]), name=f"adamw_{n}")
        grad[n], delta[n], new_m[n], new_v[n] = (a.reshape(shp) for a in out)

    total = gather_blocks(pack_small(cfg, loss, dmeta, dsmalls), reduce=True, name="reduce_small")
    loss_sum, dmeta_sum, dsmall_sum = unpack_small(cfg, total)
    for k in _small_rows(cfg):
        grad[k] = jnp.stack([ds[k] for ds in dsmall_sum])
    grad["conv_w"] = lax.dynamic_slice_in_dim(grad["conv_w"], chip * (cfg.width // N_CHIPS), cfg.width // N_CHIPS, axis=2)
    grad["meta"] = lax.dynamic_slice_in_dim(dmeta_sum, chip * (cfg.d // N_CHIPS), cfg.d // N_CHIPS, axis=1)

    for n in names:
        if n in WEIGHTS:
            continue
        shp = params[n].shape
        two_d = lambda a: a.reshape(-1, shp[-1])
        dl, nm, nv = adamw(two_d(params[n]), two_d(grad[n]), two_d(mom1[n]), two_d(mom2[n]), name=f"adamw_{n}")
        delta[n], new_m[n], new_v[n] = dl.reshape(shp), nm.reshape(shp), nv.reshape(shp)

    return (loss_sum, grad_x[None], *[grad[n] for n in names], *[delta[n] for n in names], *[new_m[n] for n in names],
            *[new_v[n] for n in names])
```

```python
import functools
from typing import NamedTuple

import jax
import jax.numpy as jnp
from jax import lax
from jax.experimental import pallas as pl
from jax.experimental.pallas import tpu as pltpu

F32 = jnp.float32
BF16 = jnp.bfloat16
MESH = pl.DeviceIdType.MESH

EPS = 1e-6
NEG_INF = -1e30
ROPE_THETA = 10000.0
LANES = 128
ROPE = 64
N_CHIPS = 4
N_DEV = 8

ADAM_LR = 0.001
ADAM_B1 = 0.9
ADAM_B2 = 0.999
ADAM_EPS = 1e-08
ADAM_WD = 0.01
ADAM_STEP = 10

VMEM_LIMIT_BYTES = 48 * 1024 * 1024


class Cfg(NamedTuple):
    d: int = 2048
    seq: int = 2048
    depth: int = 4
    n_meta: int = 16
    heads: int = 8
    q_rank: int = 512
    kv_rank: int = 512
    d_ff: int = 5632

    @property
    def width(self):
        return self.heads * LANES

    @property
    def pad(self):
        return (-(self.n_meta + self.seq)) % LANES

    @property
    def m(self):
        return self.pad + self.n_meta + self.seq

    @property
    def nat_splits(self):
        w = self.width
        return (self.q_rank, self.kv_rank, ROPE, w, w, w, w, w, w, self.heads, 3 * self.d)

    @property
    def d_in(self):
        return sum(self.nat_splits)

    @property
    def off_cq(self):
        return 3 * self.d

    @property
    def off_ckv(self):
        return self.off_cq + self.q_rank

    @property
    def off_conv(self):
        return self.off_ckv + self.kv_rank

    @property
    def off_fox(self):
        return self.off_conv + 3 * self.width

    @property
    def off_kpe(self):
        return self.off_fox + 3 * self.width

    @property
    def off_fl(self):
        return self.off_kpe + LANES

    @property
    def d_inp(self):
        return -(-(self.off_fl + LANES) // 512) * 512


CFG = Cfg()


def _tile(n, target, mult=LANES):
    best = None
    t = mult
    while t <= min(n, target):
        if n % t == 0:
            best = t
        t += mult
    return best or n


def _cparams(*sem):
    return pltpu.CompilerParams(dimension_semantics=sem, vmem_limit_bytes=VMEM_LIMIT_BYTES)


def pack_w_in(cfg, w):
    cq, ckv, kpe, cb, cc, cx, fq, fk, fv, fl, gate = jnp.split(w, list(_cumsum(cfg.nat_splits))[:-1], axis=1)
    z = lambda n: jnp.zeros((w.shape[0], n), w.dtype)
    tail = cfg.d_inp - cfg.off_fl - cfg.heads
    return jnp.concatenate([gate, cq, ckv, cb, cc, cx, fq, fk, fv, kpe, z(LANES - ROPE), fl, z(tail)], axis=1)


def unpack_w_in(cfg, wp):
    w = cfg.width
    sizes = (3 * cfg.d, cfg.q_rank, cfg.kv_rank, w, w, w, w, w, w, ROPE, LANES - ROPE, cfg.heads, cfg.d_inp - cfg.off_fl - cfg.heads)
    gate, cq, ckv, cb, cc, cx, fq, fk, fv, kpe, _, fl, _ = jnp.split(wp, list(_cumsum(sizes))[:-1], axis=1)
    return jnp.concatenate([cq, ckv, kpe, cb, cc, cx, fq, fk, fv, fl, gate], axis=1)


def _cumsum(xs):
    out, s = [], 0
    for v in xs:
        s += v
        out.append(s)
    return out


def pack_w_uq(cfg, w):
    r = w.shape[0]
    w3 = w.reshape(r, cfg.heads, LANES + ROPE)
    w3 = jnp.pad(w3, ((0, 0), (0, 0), (0, LANES - ROPE)))
    return w3.reshape(r, cfg.heads * 2 * LANES)


def unpack_w_uq(cfg, wp):
    r = wp.shape[0]
    return wp.reshape(r, cfg.heads, 2 * LANES)[:, :, : LANES + ROPE].reshape(r, cfg.heads * (LANES + ROPE))


def pack_w_ukv(cfg, w):
    r = w.shape[0]
    w4 = w.reshape(r, cfg.heads, 2, LANES)
    return jnp.transpose(w4, (0, 2, 1, 3)).reshape(r, 2 * cfg.heads * LANES)


def unpack_w_ukv(cfg, wp):
    r = wp.shape[0]
    w4 = wp.reshape(r, 2, cfg.heads, LANES)
    return jnp.transpose(w4, (0, 2, 1, 3)).reshape(r, 2 * cfg.heads * LANES)


_DIMS = {
    "nn": (((1,), (0,)), ((), ())),
    "nt": (((1,), (1,)), ((), ())),
    "tn": (((0,), (0,)), ((), ())),
}


def matmul(a, b, mode, out_dtype, *, tm, tn, tk, name):
    batched = a.ndim == 3
    if mode == "nn":
        (m, kc), n = a.shape[-2:], b.shape[-1]
        a_blk, a_idx = (tm, tk), lambda i, j, k: (i, k)
        b_blk, b_idx = (tk, tn), lambda i, j, k: (k, j)
    elif mode == "nt":
        (m, kc), n = a.shape[-2:], b.shape[-2]
        a_blk, a_idx = (tm, tk), lambda i, j, k: (i, k)
        b_blk, b_idx = (tn, tk), lambda i, j, k: (j, k)
    else:
        (kc, m), n = a.shape[-2:], b.shape[-1]
        a_blk, a_idx = (tk, tm), lambda i, j, k: (k, i)
        b_blk, b_idx = (tk, tn), lambda i, j, k: (k, j)
    assert m % tm == 0 and n % tn == 0 and kc % tk == 0, (name, m, n, kc, tm, tn, tk)
    nk = kc // tk
    dims = _DIMS[mode]
    o_blk, o_idx = (tm, tn), lambda i, j, k: (i, j)
    grid = (m // tm, n // tn, nk)
    if batched:
        nb = a.shape[0]
        grid = (nb,) + grid
        wrap = lambda f: (lambda bb, i, j, k: (bb,) + f(i, j, k))
        a_blk, b_blk, o_blk = (None,) + a_blk, (None,) + b_blk, (None,) + o_blk
        a_idx, b_idx, o_idx = wrap(a_idx), wrap(b_idx), wrap(o_idx)
        out_shape = (nb, m, n)
        sem = ("parallel", "parallel", "parallel", "arbitrary")
    else:
        out_shape = (m, n)
        sem = ("parallel", "parallel", "arbitrary")
    k_axis = len(grid) - 1

    def body(a_ref, b_ref, o_ref, *scratch):
        prod = lax.dot_general(a_ref[...], b_ref[...], dims, preferred_element_type=F32)
        if nk == 1:
            o_ref[...] = prod.astype(o_ref.dtype)
        else:
            acc_ref = scratch[0]
            k = pl.program_id(k_axis)

            @pl.when(k == 0)
            def _():
                acc_ref[...] = prod

            @pl.when(k > 0)
            def _():
                acc_ref[...] += prod

            @pl.when(k == nk - 1)
            def _():
                o_ref[...] = acc_ref[...].astype(o_ref.dtype)

    return pl.pallas_call(
        body,
        name=name,
        out_shape=jax.ShapeDtypeStruct(out_shape, out_dtype),
        grid=grid,
        in_specs=[pl.BlockSpec(a_blk, a_idx), pl.BlockSpec(b_blk, b_idx)],
        out_specs=pl.BlockSpec(o_blk, o_idx),
        scratch_shapes=[] if nk == 1 else [pltpu.VMEM((tm, tn), F32)],
        compiler_params=_cparams(*sem),
    )(a, b)


def _row_tile(m):
    return _tile(m, 272, 16)


def rmsnorm_fwd(x, g, out_dtype, *, name, width=None, col_blk=0, res=None):
    m = x.shape[0]
    n = width or x.shape[1]
    tm = _row_tile(m)
    has_res = res is not None

    def body(x_ref, g_ref, *rest):
        o_ref = rest[-1]
        xf = x_ref[...].astype(F32)
        r = lax.rsqrt(jnp.mean(xf * xf, axis=-1, keepdims=True) + EPS)
        y = xf * r * g_ref[...]
        if has_res:
            y = rest[0][...] + y
        o_ref[...] = y.astype(o_ref.dtype)

    in_specs = [pl.BlockSpec((tm, n), lambda i: (i, col_blk)), pl.BlockSpec((1, n), lambda i: (0, 0))]
    args = [x, g.reshape(1, n)]
    if has_res:
        in_specs.append(pl.BlockSpec((tm, n), lambda i: (i, 0)))
        args.append(res)
    return pl.pallas_call(
        body,
        name=name,
        out_shape=jax.ShapeDtypeStruct((m, n), out_dtype),
        grid=(m // tm,),
        in_specs=in_specs,
        out_specs=pl.BlockSpec((tm, n), lambda i: (i, 0)),
        compiler_params=_cparams("parallel"),
    )(*args)


def rmsnorm_bwd(x, g, dy, out_dtype, *, name, width=None, col_blk=0, dres=None):
    m = x.shape[0]
    n = width or x.shape[1]
    tm = _row_tile(m)
    has_res = dres is not None

    def body(x_ref, g_ref, dy_ref, *rest):
        dx_ref, dg_ref = rest[-2:]
        i = pl.program_id(0)
        xf = x_ref[...].astype(F32)
        r = lax.rsqrt(jnp.mean(xf * xf, axis=-1, keepdims=True) + EPS)
        xhat = xf * r
        dyf = dy_ref[...].astype(F32)
        dxh = dyf * g_ref[...]
        dx = r * (dxh - xhat * jnp.mean(dxh * xhat, axis=-1, keepdims=True))
        if has_res:
            dx = dx + rest[0][...]
        dx_ref[...] = dx.astype(dx_ref.dtype)
        part = jnp.sum(dyf * xhat, axis=0, keepdims=True)

        @pl.when(i == 0)
        def _():
            dg_ref[...] = part

        @pl.when(i > 0)
        def _():
            dg_ref[...] += part

    in_specs = [
        pl.BlockSpec((tm, n), lambda i: (i, col_blk)),
        pl.BlockSpec((1, n), lambda i: (0, 0)),
        pl.BlockSpec((tm, n), lambda i: (i, 0)),
    ]
    args = [x, g.reshape(1, n), dy]
    if has_res:
        in_specs.append(pl.BlockSpec((tm, n), lambda i: (i, 0)))
        args.append(dres)
    return pl.pallas_call(
        body,
        name=name,
        out_shape=(jax.ShapeDtypeStruct((m, n), out_dtype), jax.ShapeDtypeStruct((1, n), F32)),
        grid=(m // tm,),
        in_specs=in_specs,
        out_specs=(pl.BlockSpec((tm, n), lambda i: (i, 0)), pl.BlockSpec((1, n), lambda i: (0, 0))),
        compiler_params=_cparams("arbitrary"),
    )(*args)


_NT = (((1,), (1,)), ((), ()))
_NN = (((1,), (0,)), ((), ()))
_TN = (((0,), (0,)), ((), ()))


def _attn_scores(q, k, scale, decay_refs, i, tq, kn, pad):
    s = lax.dot_general(q, k, _NT, preferred_element_type=F32) * scale
    if decay_refs is not None:
        cq_ref, ck_ref = decay_refs
        s = s + (cq_ref[0] - ck_ref[0][:, :kn])
    t_idx = i * tq + lax.broadcasted_iota(jnp.int32, (tq, 1), 0)
    s_idx = lax.broadcasted_iota(jnp.int32, (1, kn), 1)
    mask = (s_idx <= t_idx) & (s_idx >= pad)
    return s, mask, t_idx


def _keys_needed(i, tq, m):
    return min(m, -(-((i + 1) * tq) // LANES) * LANES)


def attn_fwd(q, k, v, *, heads, dk, dv, qblk0, kblk0, vblk0, scale, pad, decay=None, name):
    m = q.shape[0]
    tq = _row_tile(m)
    has_decay = decay is not None

    def body(q_ref, k_ref, v_ref, *rest):
        o_ref, lse_ref = rest[-2:]
        decay_refs = rest[:2] if has_decay else None

        def block(i):
            kn = _keys_needed(i, tq, m)
            s, mask, t_idx = _attn_scores(q_ref[...], k_ref[0:kn, :], scale, decay_refs, i, tq, kn, pad)
            s = jnp.where(mask, s, NEG_INF)
            mx = jnp.max(s, axis=1, keepdims=True)
            p = jnp.exp(s - mx)
            l = jnp.sum(p, axis=1, keepdims=True)
            o = lax.dot_general(p.astype(BF16), v_ref[0:kn, :], _NN, preferred_element_type=F32) / l
            o_ref[...] = jnp.where(t_idx >= pad, o, 0.0).astype(o_ref.dtype)
            lse_ref[0] = mx + jnp.log(l)

        for i in range(m // tq):
            pl.when(pl.program_id(1) == i)(functools.partial(block, i))

    in_specs = [
        pl.BlockSpec((tq, dk), lambda h, i: (i, qblk0 + h)),
        pl.BlockSpec((m, dk), lambda h, i: (0, kblk0 + h)),
        pl.BlockSpec((m, dv), lambda h, i: (0, vblk0 + h)),
    ]
    args = [q, k, v]
    if has_decay:
        in_specs += [pl.BlockSpec((1, tq, 1), lambda h, i: (h, i, 0)), pl.BlockSpec((1, 1, m), lambda h, i: (h, 0, 0))]
        args += list(decay)
    return pl.pallas_call(
        body,
        name=name,
        out_shape=(jax.ShapeDtypeStruct((m, heads * dv), BF16), jax.ShapeDtypeStruct((heads, m, 1), F32)),
        grid=(heads, m // tq),
        in_specs=in_specs,
        out_specs=(pl.BlockSpec((tq, dv), lambda h, i: (i, h)), pl.BlockSpec((1, tq, 1), lambda h, i: (h, i, 0))),
        compiler_params=_cparams("parallel", "parallel"),
    )(*args)


def attn_bwd(q, k, v, do, do_sel, lse, *, heads, dk, dv, qblk0, kblk0, vblk0, scale, pad, decay=None, name):
    m = q.shape[0]
    tq = _row_tile(m)
    nq = m // tq
    has_decay = decay is not None

    def body(q_ref, k_ref, v_ref, do_ref, lse_ref, *rest):
        if has_decay:
            cq_ref, ck_ref, dq_ref, dk_ref, dv_ref, dck_ref, dk_acc, dv_acc = rest
            decay_refs = (cq_ref, ck_ref)
        else:
            dq_ref, dk_ref, dv_ref, dk_acc, dv_acc = rest
            decay_refs = None
        @pl.when(pl.program_id(1) == 0)
        def _():
            dk_acc[...] = jnp.zeros_like(dk_acc)
            dv_acc[...] = jnp.zeros_like(dv_acc)
            if has_decay:
                dck_ref[...] = jnp.zeros_like(dck_ref)

        def block(i):
            kn = _keys_needed(i, tq, m)
            qb, kb, dob = q_ref[...], k_ref[0:kn, :], do_ref[...]
            s, mask, _ = _attn_scores(qb, kb, scale, decay_refs, i, tq, kn, pad)
            p = jnp.where(mask, jnp.exp(s - lse_ref[0]), 0.0)
            dp = lax.dot_general(dob, v_ref[0:kn, :], _NT, preferred_element_type=F32)
            ds = p * (dp - jnp.sum(p * dp, axis=1, keepdims=True))
            dsb = ds.astype(BF16)
            dq_ref[...] = (lax.dot_general(dsb, kb, _NN, preferred_element_type=F32) * scale).astype(dq_ref.dtype)
            dk_acc[0:kn, :] += lax.dot_general(dsb, qb, _TN, preferred_element_type=F32) * scale
            dv_acc[0:kn, :] += lax.dot_general(p.astype(BF16), dob, _TN, preferred_element_type=F32)
            if has_decay:
                dck_ref[0, :, 0:kn] -= jnp.sum(ds, axis=0, keepdims=True)

        for i in range(nq):
            pl.when(pl.program_id(1) == i)(functools.partial(block, i))

        @pl.when(pl.program_id(1) == nq - 1)
        def _():
            dk_ref[...] = dk_acc[...].astype(dk_ref.dtype)
            dv_ref[...] = dv_acc[...].astype(dv_ref.dtype)

    in_specs = [
        pl.BlockSpec((tq, dk), lambda h, i: (i, qblk0 + h)),
        pl.BlockSpec((m, dk), lambda h, i: (0, kblk0 + h)),
        pl.BlockSpec((m, dv), lambda h, i: (0, vblk0 + h)),
        pl.BlockSpec((None, tq, dv), lambda h, i: (do_sel, i, h)),
        pl.BlockSpec((1, tq, 1), lambda h, i: (h, i, 0)),
    ]
    args = [q, k, v, do, lse]
    out_shape = [
        jax.ShapeDtypeStruct((m, heads * dk), BF16),
        jax.ShapeDtypeStruct((m, heads * dk), BF16),
        jax.ShapeDtypeStruct((m, heads * dv), BF16),
    ]
    out_specs = [
        pl.BlockSpec((tq, dk), lambda h, i: (i, h)),
        pl.BlockSpec((m, dk), lambda h, i: (0, h)),
        pl.BlockSpec((m, dv), lambda h, i: (0, h)),
    ]
    if has_decay:
        in_specs += [pl.BlockSpec((1, tq, 1), lambda h, i: (h, i, 0)), pl.BlockSpec((1, 1, m), lambda h, i: (h, 0, 0))]
        args += list(decay)
        out_shape.append(jax.ShapeDtypeStruct((heads, 1, m), F32))
        out_specs.append(pl.BlockSpec((1, 1, m), lambda h, i: (h, 0, 0)))
    return pl.pallas_call(
        body,
        name=name,
        out_shape=tuple(out_shape),
        grid=(heads, nq),
        in_specs=in_specs,
        out_specs=tuple(out_specs),
        scratch_shapes=[pltpu.VMEM((m, dk), F32), pltpu.VMEM((m, dv), F32)],
        compiler_params=_cparams("parallel", "arbitrary"),
    )(*args)


def rope_tables(cfg):
    half = ROPE // 2
    inv_freq = 1.0 / (ROPE_THETA ** (jnp.arange(0, ROPE, 2, dtype=F32) / ROPE))
    pos = (jnp.arange(cfg.m, dtype=jnp.int32) - cfg.pad).astype(F32)
    ang = pos[:, None] * inv_freq[None, :]
    cos, sin = jnp.cos(ang), jnp.sin(ang)
    z = jnp.zeros((cfg.m, half), F32)
    zz = jnp.zeros((cfg.m, LANES - ROPE), F32)
    return (
        jnp.concatenate([cos, cos, zz], axis=1),
        jnp.concatenate([-sin, z, zz], axis=1),
        jnp.concatenate([z, sin, zz], axis=1),
    )


def _rope(x, cos, s1, s2):
    return x * cos + pltpu.roll(x, LANES - ROPE // 2, 1) * s1 + pltpu.roll(x, ROPE // 2, 1) * s2


def mla_prep_fwd(cfg, q, kv, proj, tabs, *, name):
    m, h2 = cfg.m, 2 * LANES
    tm = _tile(m, 544, 16)
    kpe_blk = cfg.off_kpe // LANES

    def body(q_ref, kn_ref, kpe_ref, cos_ref, s1_ref, s2_ref, qf_ref, kf_ref):
        cos, s1, s2 = cos_ref[...], s1_ref[...], s2_ref[...]
        qv = q_ref[...]
        qf_ref[:, :LANES] = qv[:, :LANES]
        qf_ref[:, LANES:] = _rope(qv[:, LANES:].astype(F32), cos, s1, s2).astype(qf_ref.dtype)
        kf_ref[:, :LANES] = kn_ref[...]
        kf_ref[:, LANES:] = _rope(kpe_ref[...].astype(F32), cos, s1, s2).astype(kf_ref.dtype)

    tab = pl.BlockSpec((tm, LANES), lambda i, h: (i, 0))
    return pl.pallas_call(
        body,
        name=name,
        out_shape=(jax.ShapeDtypeStruct((m, cfg.heads * h2), BF16), jax.ShapeDtypeStruct((m, cfg.heads * h2), BF16)),
        grid=(m // tm, cfg.heads),
        in_specs=[
            pl.BlockSpec((tm, h2), lambda i, h: (i, h)),
            pl.BlockSpec((tm, LANES), lambda i, h: (i, h)),
            pl.BlockSpec((tm, LANES), lambda i, h: (i, kpe_blk)),
            tab, tab, tab,
        ],
        out_specs=(pl.BlockSpec((tm, h2), lambda i, h: (i, h)), pl.BlockSpec((tm, h2), lambda i, h: (i, h))),
        compiler_params=_cparams("parallel", "parallel"),
    )(q, kv, proj, *tabs)


def mla_prep_bwd(cfg, dqf, dkf, tabs_t, *, name):
    m, h2 = cfg.m, 2 * LANES
    tm = _tile(m, 544, 16)

    def body(dqf_ref, dkf_ref, cos_ref, s1_ref, s2_ref, dq_ref, dkn_ref, dkpe_ref):
        h = pl.program_id(1)
        cos, s1, s2 = cos_ref[...], s1_ref[...], s2_ref[...]
        dqv, dkv = dqf_ref[...], dkf_ref[...]
        dq_ref[:, :LANES] = dqv[:, :LANES]
        dq_ref[:, LANES:] = _rope(dqv[:, LANES:].astype(F32), cos, s1, s2).astype(dq_ref.dtype)
        dkn_ref[...] = dkv[:, :LANES]
        part = _rope(dkv[:, LANES:].astype(F32), cos, s1, s2)

        @pl.when(h == 0)
        def _():
            dkpe_ref[...] = part

        @pl.when(h > 0)
        def _():
            dkpe_ref[...] += part

    tab = pl.BlockSpec((tm, LANES), lambda i, h: (i, 0))
    return pl.pallas_call(
        body,
        name=name,
        out_shape=(
            jax.ShapeDtypeStruct((m, cfg.heads * h2), BF16),
            jax.ShapeDtypeStruct((m, cfg.heads * LANES), BF16),
            jax.ShapeDtypeStruct((m, LANES), F32),
        ),
        grid=(m // tm, cfg.heads),
        in_specs=[pl.BlockSpec((tm, h2), lambda i, h: (i, h)), pl.BlockSpec((tm, h2), lambda i, h: (i, h)), tab, tab, tab],
        out_specs=(
            pl.BlockSpec((tm, h2), lambda i, h: (i, h)),
            pl.BlockSpec((tm, LANES), lambda i, h: (i, h)),
            pl.BlockSpec((tm, LANES), lambda i, h: (i, 0)),
        ),
        compiler_params=_cparams("parallel", "arbitrary"),
    )(dqf, dkf, *tabs_t)


def _conv_parts(b_ref, c_ref, x_ref, w_ref, m):
    b, c, x = b_ref[...].astype(F32), c_ref[...].astype(F32), x_ref[...].astype(F32)
    u = c * x
    row = lax.broadcasted_iota(jnp.int32, (m, 1), 0)
    u1 = jnp.where(row >= 1, pltpu.roll(u, 1, 0), 0.0)
    u2 = jnp.where(row >= 2, pltpu.roll(u, 2, 0), 0.0)
    w0, w1, w2 = w_ref[0:1, :], w_ref[1:2, :], w_ref[2:3, :]
    uc = w0 * u2 + w1 * u1 + w2 * u
    return b, c, x, u, u1, u2, uc, (w0, w1, w2), row


def _conv_specs(cfg, tn):
    m, nb, blk0 = cfg.m, cfg.width // tn, cfg.off_conv // tn
    return [
        pl.BlockSpec((m, tn), lambda j: (0, blk0 + j)),
        pl.BlockSpec((m, tn), lambda j: (0, blk0 + nb + j)),
        pl.BlockSpec((m, tn), lambda j: (0, blk0 + 2 * nb + j)),
        pl.BlockSpec((3, tn), lambda j: (0, j)),
    ]


def conv_fwd(cfg, proj, conv_w, *, name):
    m, tn = cfg.m, LANES

    def body(b_ref, c_ref, x_ref, w_ref, o_ref):
        b, _, _, _, _, _, uc, _, _ = _conv_parts(b_ref, c_ref, x_ref, w_ref, m)
        o_ref[...] = (b * uc).astype(o_ref.dtype)

    return pl.pallas_call(
        body,
        name=name,
        out_shape=jax.ShapeDtypeStruct((m, cfg.width), BF16),
        grid=(cfg.width // tn,),
        in_specs=_conv_specs(cfg, tn),
        out_specs=pl.BlockSpec((m, tn), lambda j: (0, j)),
        compiler_params=_cparams("parallel"),
    )(proj, proj, proj, conv_w)


def conv_bwd(cfg, proj, conv_w, do, do_sel, *, name):
    m, tn = cfg.m, LANES

    def body(b_ref, c_ref, x_ref, w_ref, do_ref, db_ref, dc_ref, dx_ref, dw_ref):
        b, c, x, u, u1, u2, uc, (w0, w1, w2), row = _conv_parts(b_ref, c_ref, x_ref, w_ref, m)
        dob = do_ref[...].astype(F32)
        db_ref[...] = (dob * uc).astype(db_ref.dtype)
        duc = dob * b
        up1 = jnp.where(row <= m - 2, pltpu.roll(duc, m - 1, 0), 0.0)
        up2 = jnp.where(row <= m - 3, pltpu.roll(duc, m - 2, 0), 0.0)
        du = w2 * duc + w1 * up1 + w0 * up2
        dc_ref[...] = (du * x).astype(dc_ref.dtype)
        dx_ref[...] = (du * c).astype(dx_ref.dtype)
        dw_ref[0:1, :] = jnp.sum(duc * u2, axis=0, keepdims=True)
        dw_ref[1:2, :] = jnp.sum(duc * u1, axis=0, keepdims=True)
        dw_ref[2:3, :] = jnp.sum(duc * u, axis=0, keepdims=True)

    act = jax.ShapeDtypeStruct((m, cfg.width), BF16)
    blk = pl.BlockSpec((m, tn), lambda j: (0, j))
    return pl.pallas_call(
        body,
        name=name,
        out_shape=(act, act, act, jax.ShapeDtypeStruct((3, cfg.width), F32)),
        grid=(cfg.width // tn,),
        in_specs=_conv_specs(cfg, tn) + [pl.BlockSpec((None, m, tn), lambda j: (do_sel, 0, j))],
        out_specs=(blk, blk, blk, pl.BlockSpec((3, tn), lambda j: (0, j))),
        compiler_params=_cparams("parallel"),
    )(proj, proj, proj, conv_w, do)


def _tri(lower):
    r = lax.broadcasted_iota(jnp.int32, (LANES, LANES), 0)
    c = lax.broadcasted_iota(jnp.int32, (LANES, LANES), 1)
    return jnp.where((r >= c) if lower else (r <= c), 1.0, 0.0).astype(F32)


def fox_gate_fwd(cfg, fl, b_pad, *, name):
    m = cfg.m
    nblk = m // LANES

    def body(fl_ref, b_ref, c_ref):
        z = fl_ref[...] + b_ref[...]
        logf = jnp.minimum(z, 0.0) - jnp.log(1.0 + jnp.exp(-jnp.abs(z)))
        row = lax.broadcasted_iota(jnp.int32, (m, 1), 0)
        logf = jnp.where(row >= cfg.pad, logf, 0.0)
        tri = _tri(True)
        carry = jnp.zeros((1, LANES), F32)
        for blk in range(nblk):
            cb = jnp.dot(tri, logf[blk * LANES:(blk + 1) * LANES, :], precision=lax.Precision.HIGHEST,
                         preferred_element_type=F32) + carry
            c_ref[blk * LANES:(blk + 1) * LANES, :] = cb
            carry = cb[LANES - 1:LANES, :]

    full = pl.BlockSpec((m, LANES), lambda: (0, 0))
    return pl.pallas_call(
        body,
        name=name,
        out_shape=jax.ShapeDtypeStruct((m, LANES), F32),
        in_specs=[full, pl.BlockSpec((1, LANES), lambda: (0, 0))],
        out_specs=full,
        compiler_params=pltpu.CompilerParams(vmem_limit_bytes=VMEM_LIMIT_BYTES),
    )(fl, b_pad)


def fox_gate_bwd(cfg, fl, b_pad, dc, *, name):
    m = cfg.m
    nblk = m // LANES

    def body(fl_ref, b_ref, dc_ref, dfl_ref, db_ref):
        z = fl_ref[...] + b_ref[...]
        dlogsig = 1.0 / (1.0 + jnp.exp(z))
        row = lax.broadcasted_iota(jnp.int32, (m, 1), 0)
        gate = jnp.where(row >= cfg.pad, dlogsig, 0.0)
        dcv = dc_ref[...]
        tri = _tri(False)
        carry = jnp.zeros((1, LANES), F32)
        db = jnp.zeros((1, LANES), F32)
        for blk in reversed(range(nblk)):
            sl = slice(blk * LANES, (blk + 1) * LANES)
            rb = jnp.dot(tri, dcv[sl, :], precision=lax.Precision.HIGHEST, preferred_element_type=F32) + carry
            carry = rb[0:1, :]
            dfl = rb * gate[sl, :]
            dfl_ref[sl, :] = dfl
            db = db + jnp.sum(dfl, axis=0, keepdims=True)
        db_ref[...] = db

    full = pl.BlockSpec((m, LANES), lambda: (0, 0))
    one = pl.BlockSpec((1, LANES), lambda: (0, 0))
    return pl.pallas_call(
        body,
        name=name,
        out_shape=(jax.ShapeDtypeStruct((m, LANES), F32), jax.ShapeDtypeStruct((1, LANES), F32)),
        in_specs=[full, one, full],
        out_specs=(full, one),
        compiler_params=pltpu.CompilerParams(vmem_limit_bytes=VMEM_LIMIT_BYTES),
    )(fl, b_pad, dc)


def _sigmoid(x):
    return 1.0 / (1.0 + jnp.exp(-x))


def gate_merge_fwd(cfg, y, proj, *, name):
    m, d = cfg.m, cfg.d
    tm, tn = _tile(m, 1088, 16), _tile(d, 512)
    nd = d // tn

    def body(y_ref, g0_ref, g1_ref, g2_ref, o_ref):
        acc = None
        for n, g_ref in enumerate((g0_ref, g1_ref, g2_ref)):
            t = _sigmoid(g_ref[...].astype(F32)) * y_ref[n].astype(F32)
            acc = t if acc is None else acc + t
        o_ref[...] = acc.astype(o_ref.dtype)

    gate = lambda n: pl.BlockSpec((tm, tn), lambda i, j: (i, n * nd + j))
    return pl.pallas_call(
        body,
        name=name,
        out_shape=jax.ShapeDtypeStruct((m, d), BF16),
        grid=(m // tm, nd),
        in_specs=[pl.BlockSpec((3, tm, tn), lambda i, j: (0, i, j)), gate(0), gate(1), gate(2)],
        out_specs=pl.BlockSpec((tm, tn), lambda i, j: (i, j)),
        compiler_params=_cparams("parallel", "parallel"),
    )(y, proj, proj, proj)


def gate_merge_bwd(cfg, dm, y, proj, *, name):
    m, d = cfg.m, cfg.d
    tm, tn = _tile(m, 1088, 16), _tile(d, 512)
    nd = d // tn

    def body(dm_ref, y_ref, g_ref, dy_ref, dg_ref):
        sg = _sigmoid(g_ref[...].astype(F32))
        dmv = dm_ref[...].astype(F32)
        dy_ref[...] = (sg * dmv).astype(dy_ref.dtype)
        dg_ref[...] = (dmv * y_ref[...].astype(F32) * sg * (1.0 - sg)).astype(dg_ref.dtype)

    return pl.pallas_call(
        body,
        name=name,
        out_shape=(jax.ShapeDtypeStruct((3, m, d), BF16), jax.ShapeDtypeStruct((m, 3 * d), BF16)),
        grid=(m // tm, nd, 3),
        in_specs=[
            pl.BlockSpec((tm, tn), lambda i, j, n: (i, j)),
            pl.BlockSpec((None, tm, tn), lambda i, j, n: (n, i, j)),
            pl.BlockSpec((tm, tn), lambda i, j, n: (i, n * nd + j)),
        ],
        out_specs=(
            pl.BlockSpec((None, tm, tn), lambda i, j, n: (n, i, j)),
            pl.BlockSpec((tm, tn), lambda i, j, n: (i, n * nd + j)),
        ),
        compiler_params=_cparams("parallel", "parallel", "parallel"),
    )(dm, y, proj)


def swiglu_fwd(cfg, gu, *, name):
    m, f = cfg.m, cfg.d_ff
    tm, tn = _tile(m, 1088, 16), _tile(f, 512)
    nf = f // tn

    def body(g_ref, u_ref, o_ref):
        g = g_ref[...].astype(F32)
        o_ref[...] = (g * _sigmoid(g) * u_ref[...].astype(F32)).astype(o_ref.dtype)

    return pl.pallas_call(
        body,
        name=name,
        out_shape=jax.ShapeDtypeStruct((m, f), BF16),
        grid=(m // tm, nf),
        in_specs=[pl.BlockSpec((tm, tn), lambda i, j: (i, j)), pl.BlockSpec((tm, tn), lambda i, j: (i, nf + j))],
        out_specs=pl.BlockSpec((tm, tn), lambda i, j: (i, j)),
        compiler_params=_cparams("parallel", "parallel"),
    )(gu, gu)


def swiglu_bwd(cfg, dact, gu, *, name):
    m, f = cfg.m, cfg.d_ff
    tm, tn = _tile(m, 1088, 16), _tile(f, 512)
    nf = f // tn

    def body(da_ref, g_ref, u_ref, o_ref):
        j = pl.program_id(1)
        g, u, da = g_ref[...].astype(F32), u_ref[...].astype(F32), da_ref[...].astype(F32)
        sg = _sigmoid(g)
        dg = da * u * sg * (1.0 + g * (1.0 - sg))
        du = da * g * sg
        o_ref[...] = jnp.where(j < nf, dg, du).astype(o_ref.dtype)

    return pl.pallas_call(
        body,
        name=name,
        out_shape=jax.ShapeDtypeStruct((m, 2 * f), BF16),
        grid=(m // tm, 2 * nf),
        in_specs=[
            pl.BlockSpec((tm, tn), lambda i, j: (i, j % nf)),
            pl.BlockSpec((tm, tn), lambda i, j: (i, j % nf)),
            pl.BlockSpec((tm, tn), lambda i, j: (i, nf + j % nf)),
        ],
        out_specs=pl.BlockSpec((tm, tn), lambda i, j: (i, j)),
        compiler_params=_cparams("parallel", "parallel"),
    )(dact, gu, gu)


def loss_head(cfg, h, target, *, name):
    m, d = cfg.m, cfg.d
    assert cfg.pad + cfg.n_meta == LANES
    tm = LANES
    inv_d = 1.0 / d

    def body(h_ref, t_ref, dh_ref, loss_ref):
        i = pl.program_id(0)

        @pl.when(i == 0)
        def _():
            dh_ref[...] = jnp.zeros_like(dh_ref)
            loss_ref[...] = jnp.zeros_like(loss_ref)

        @pl.when(i > 0)
        def _():
            err = h_ref[...] - t_ref[...]
            dh_ref[...] = err * inv_d
            loss_ref[...] += 0.5 * inv_d * jnp.sum(err * err)

    return pl.pallas_call(
        body,
        name=name,
        out_shape=(jax.ShapeDtypeStruct((m, d), F32), jax.ShapeDtypeStruct((8, LANES), F32)),
        grid=(m // tm,),
        in_specs=[pl.BlockSpec((tm, d), lambda i: (i, 0)), pl.BlockSpec((tm, d), lambda i: (jnp.maximum(i - 1, 0), 0))],
        out_specs=(pl.BlockSpec((tm, d), lambda i: (i, 0)), pl.BlockSpec((8, LANES), lambda i: (0, 0))),
        compiler_params=_cparams("arbitrary"),
    )(h, target)


def adamw(w, g, m_, v_, *, name):
    r, c = w.shape
    c_pad = -(-c // LANES) * LANES
    tr = r
    if r % 8 == 0:
        tr = _tile(r, max(8, (3 << 19) // (4 * c_pad) // 8 * 8), 8)
    bc1 = 1.0 - ADAM_B1 ** ADAM_STEP
    bc2 = 1.0 - ADAM_B2 ** ADAM_STEP

    def body(w_ref, g_ref, m_ref, v_ref, d_ref, nm_ref, nv_ref):
        gv = g_ref[...]
        nm = ADAM_B1 * m_ref[...] + (1.0 - ADAM_B1) * gv
        nv = ADAM_B2 * v_ref[...] + (1.0 - ADAM_B2) * (gv * gv)
        d_ref[...] = -ADAM_LR * ((nm / bc1) / (jnp.sqrt(nv / bc2) + ADAM_EPS) + ADAM_WD * w_ref[...])
        nm_ref[...] = nm
        nv_ref[...] = nv

    blk = pl.BlockSpec((tr, c), lambda i: (i, 0))
    shp = jax.ShapeDtypeStruct((r, c), F32)
    return pl.pallas_call(
        body,
        name=name,
        out_shape=(shp, shp, shp),
        grid=(r // tr,),
        in_specs=[blk, blk, blk, blk],
        out_specs=(blk, blk, blk),
        compiler_params=_cparams("parallel"),
    )(w, g, m_, v_)


def adamw_halves(w, g_own, g_other, core, m_, v_, *, name):
    nl, r, c = w.shape
    r2 = r // 2
    c_pad = -(-c // LANES) * LANES
    tr = _tile(r2, max(8, (3 << 19) // (4 * c_pad) // 8 * 8), 8)
    nr = r2 // tr
    bc1 = 1.0 - ADAM_B1 ** ADAM_STEP
    bc2 = 1.0 - ADAM_B2 ** ADAM_STEP

    def body(core_ref, w_ref, go_ref, gr_ref, m_ref, v_ref, g_ref, d_ref, nm_ref, nv_ref):
        gv = jnp.where(pl.program_id(2) == core_ref[0], go_ref[...], gr_ref[...])[:, :c]
        nm = ADAM_B1 * m_ref[...] + (1.0 - ADAM_B1) * gv
        nv = ADAM_B2 * v_ref[...] + (1.0 - ADAM_B2) * (gv * gv)
        d_ref[...] = -ADAM_LR * ((nm / bc1) / (jnp.sqrt(nv / bc2) + ADAM_EPS) + ADAM_WD * w_ref[...])
        g_ref[...] = gv
        nm_ref[...] = nm
        nv_ref[...] = nv

    full = pl.BlockSpec((None, tr, c), lambda l, i, hf, core_ref: (l, hf * nr + i, 0))
    half = pl.BlockSpec((None, tr, g_own.shape[2]), lambda l, i, hf, core_ref: (l, i, 0))
    shp = jax.ShapeDtypeStruct((nl, r, c), F32)
    return pl.pallas_call(
        body,
        name=name,
        out_shape=(shp, shp, shp, shp),
        grid_spec=pltpu.PrefetchScalarGridSpec(
            num_scalar_prefetch=1,
            grid=(nl, nr, 2),
            in_specs=[full, half, half, full, full],
            out_specs=(full, full, full, full),
        ),
        compiler_params=_cparams("parallel", "parallel", "arbitrary"),
    )(core, w, g_own, g_other, m_, v_)


_HBM = pl.BlockSpec(memory_space=pltpu.HBM)


def _place():
    x, y, c = lax.axis_index("x"), lax.axis_index("y"), lax.axis_index("c")
    flips = [(1 - x, y), (x, 1 - y), (1 - x, 1 - y)]
    return x, y, c, flips


def gather_weights(shards, *, name):
    nw = len(shards)
    halves = [s.shape[0] // 2 for s in shards]
    assert all(s.shape[0] % 32 == 0 for s in shards)

    def body(*refs):
        ins, outs = refs[:nw], refs[nw:2 * nw]
        send_sems, recv_sems = refs[2 * nw:]
        x, y, c, flips = _place()
        mine = 2 * x + y
        sibling = (x, y, 1 - c)

        def half(w, chip, core):
            return outs[w].at[chip, pl.ds(core * halves[w], halves[w]), :]

        def copy(w, k, src, chip, core, to):
            return pltpu.make_async_remote_copy(src_ref=src, dst_ref=half(w, chip, core), send_sem=send_sems.at[6 * w + k],
                                                recv_sem=recv_sems.at[6 * w + k], device_id=to, device_id_type=MESH)

        sent = []
        for w in range(nw):
            src = ins[w].at[pl.ds(c * halves[w], halves[w]), :]
            for k, (fx, fy) in enumerate(flips):
                sent.append(copy(w, k, src, mine, c, (fx, fy, c)))
                sent[-1].start()
        for w in range(nw):
            for k, (fx, fy) in enumerate(flips):
                theirs = 2 * fx + fy
                copy(w, k, half(w, theirs, c), theirs, c, sibling).wait_recv()
                sent.append(copy(w, 3 + k, half(w, theirs, c), theirs, c, sibling))
                sent[-1].start()
        for w in range(nw):
            for k, (fx, fy) in enumerate(flips):
                theirs = 2 * fx + fy
                copy(w, 3 + k, half(w, theirs, 1 - c), theirs, 1 - c, sibling).wait_recv()
        for cp in sent:
            cp.wait_send()

    return pl.pallas_call(
        body,
        name=name,
        out_shape=tuple(jax.ShapeDtypeStruct((N_CHIPS,) + s.shape, s.dtype) for s in shards),
        in_specs=[_HBM] * nw,
        out_specs=tuple([_HBM] * nw),
        scratch_shapes=[pltpu.SemaphoreType.DMA((6 * nw,)), pltpu.SemaphoreType.DMA((6 * nw,))],
    )(*shards)


def swap_halves(grads, *, name):
    nw = len(grads)
    halves = [g.shape[1] // 2 for g in grads]

    def body(*refs):
        ins, outs = refs[:nw], refs[nw:2 * nw]
        send_sems, recv_sems = refs[2 * nw:]
        x, y, c, _ = _place()
        copies = [
            pltpu.make_async_remote_copy(src_ref=ins[w].at[:, pl.ds((1 - c) * halves[w], halves[w]), :], dst_ref=outs[w],
                                         send_sem=send_sems.at[w], recv_sem=recv_sems.at[w], device_id=(x, y, 1 - c),
                                         device_id_type=MESH)
            for w in range(nw)
        ]
        for cp in copies:
            cp.start()
        for cp in copies:
            cp.wait()

    return pl.pallas_call(
        body,
        name=name,
        out_shape=tuple(jax.ShapeDtypeStruct((N_CHIPS, h, g.shape[2]), g.dtype) for g, h in zip(grads, halves)),
        in_specs=[_HBM] * nw,
        out_specs=tuple([_HBM] * nw),
        scratch_shapes=[pltpu.SemaphoreType.DMA((nw,)), pltpu.SemaphoreType.DMA((nw,))],
    )(*grads)


def exchange_chips(parts, *, name):
    nw = len(parts)

    def body(*refs):
        ins, outs = refs[:nw], refs[nw:2 * nw]
        send_sems, recv_sems = refs[2 * nw:]
        _, _, c, flips = _place()
        copies = [
            pltpu.make_async_remote_copy(src_ref=ins[w].at[2 * fx + fy], dst_ref=outs[w].at[k], send_sem=send_sems.at[3 * w + k],
                                         recv_sem=recv_sems.at[3 * w + k], device_id=(fx, fy, c), device_id_type=MESH)
            for w in range(nw) for k, (fx, fy) in enumerate(flips)
        ]
        for cp in copies:
            cp.start()
        for cp in copies:
            cp.wait()

    return pl.pallas_call(
        body,
        name=name,
        out_shape=tuple(jax.ShapeDtypeStruct((3,) + p.shape[1:], p.dtype) for p in parts),
        in_specs=[_HBM] * nw,
        out_specs=tuple([_HBM] * nw),
        scratch_shapes=[pltpu.SemaphoreType.DMA((3 * nw,)), pltpu.SemaphoreType.DMA((3 * nw,))],
    )(*parts)


_SEM = pl.BlockSpec(memory_space=pltpu.SEMAPHORE)
_EFFECT = pltpu.SideEffectType.DATAFLOW_SIDE_EFFECTING


def _gather_plan(halves):
    def plan(src_refs, land_refs, arrival):
        x, y, c, flips = _place()
        mine = 2 * x + y
        out = []
        for w, h in enumerate(halves):
            for fx, fy in flips:
                slot = (2 * fx + fy) if arrival else mine
                out.append((src_refs[w].at[pl.ds(c * h, h), :], land_refs[w].at[slot, pl.ds(c * h, h), :], (fx, fy, c)))
        return out
    return plan


def _exchange_plan(nw):
    def plan(src_refs, land_refs, arrival):
        _, _, c, flips = _place()
        return [(src_refs[w].at[2 * fx + fy], land_refs[w].at[k], (fx, fy, c)) for w in range(nw) for k, (fx, fy) in enumerate(flips)]
    return plan


def copies_start(srcs, land_shapes, plan, *, name):
    lands = [lax.empty(s, a.dtype) for s, a in zip(land_shapes, srcs)]
    n_in = len(srcs) + len(lands)
    n_copies = 3 * len(srcs)

    def body(*refs):
        src_refs, land_refs = refs[:len(srcs)], refs[len(srcs):n_in]
        send_sems, recv_sems, token = refs[n_in], refs[n_in + 1], refs[-1]
        for i, (src, dst, to) in enumerate(plan(src_refs, land_refs, False)):
            pltpu.make_async_remote_copy(src_ref=src, dst_ref=dst, send_sem=send_sems.at[i], recv_sem=recv_sems.at[i],
                                         device_id=to, device_id_type=MESH).start()
        token[...] = jnp.zeros_like(token)

    operands = list(srcs) + lands
    out = pl.pallas_call(
        body,
        name=name,
        out_shape=(pltpu.SemaphoreType.DMA((n_copies,)), pltpu.SemaphoreType.DMA((n_copies,)),
                   *[pltpu.HBM(a.shape, a.dtype) for a in operands], jax.ShapeDtypeStruct((8, LANES), F32)),
        in_specs=[_HBM] * n_in,
        out_specs=(_SEM, _SEM, *[_HBM] * n_in, pl.BlockSpec(memory_space=pltpu.VMEM)),
        input_output_aliases={i: 2 + i for i in range(n_in)},
        compiler_params=pltpu.CompilerParams(has_side_effects=_EFFECT),
    )(*[pltpu.with_memory_space_constraint(a, pltpu.HBM) for a in operands])
    return out[0], out[1], list(out[2:2 + len(srcs)]), list(out[2 + len(srcs):2 + n_in]), out[-1]


def copies_wait(send_sems, recv_sems, srcs, lands, plan, after, *, name):
    n_in = len(srcs) + len(lands)

    def body(*refs):
        src_refs, land_refs = refs[:len(srcs)], refs[len(srcs):n_in]
        send_ref, recv_ref, token = refs[n_in], refs[n_in + 1], refs[-1]
        token[...] = jnp.zeros_like(token)
        for i, (src, dst, to) in enumerate(plan(src_refs, land_refs, True)):
            copy = pltpu.make_async_remote_copy(src_ref=src, dst_ref=dst, send_sem=send_ref.at[i], recv_sem=recv_ref.at[i],
                                                device_id=to, device_id_type=MESH)
            copy.wait_send()
            copy.wait_recv()

    operands = list(srcs) + list(lands)
    out = pl.pallas_call(
        body,
        name=name,
        out_shape=(*[pltpu.HBM(a.shape, a.dtype) for a in operands], jax.ShapeDtypeStruct((8, LANES), F32)),
        in_specs=[_HBM] * n_in + [_SEM, _SEM, pl.BlockSpec(memory_space=pl.ANY)],
        out_specs=(*[_HBM] * n_in, pl.BlockSpec(memory_space=pltpu.VMEM)),
        input_output_aliases={i: i for i in range(n_in)},
        compiler_params=pltpu.CompilerParams(has_side_effects=_EFFECT),
    )(*operands, send_sems, recv_sems, after)
    return list(out[:len(srcs)]), list(out[len(srcs):n_in]), out[-1]


def forward_halves(lands, *, name):
    nw = len(lands)
    halves = [a.shape[1] // 2 for a in lands]

    def body(*refs):
        ins, outs = refs[:nw], refs[nw:2 * nw]
        send_sems, recv_sems = refs[2 * nw:]
        x, y, c, flips = _place()
        copies = []
        for w, h in enumerate(halves):
            for k, (fx, fy) in enumerate(flips):
                rows = (2 * fx + fy, pl.ds(c * h, h), slice(None))
                copies.append(pltpu.make_async_remote_copy(src_ref=ins[w].at[rows], dst_ref=outs[w].at[rows], send_sem=send_sems.at[3 * w + k],
                                                           recv_sem=recv_sems.at[3 * w + k], device_id=(x, y, 1 - c), device_id_type=MESH))
        for cp in copies:
            cp.start()
        for cp in copies:
            cp.wait()

    return pl.pallas_call(
        body,
        name=name,
        out_shape=tuple(jax.ShapeDtypeStruct(a.shape, a.dtype) for a in lands),
        in_specs=[_HBM] * nw,
        out_specs=tuple([_HBM] * nw),
        input_output_aliases={w: w for w in range(nw)},
        scratch_shapes=[pltpu.SemaphoreType.DMA((3 * nw,)), pltpu.SemaphoreType.DMA((3 * nw,))],
    )(*lands)


def share_halves(sums, *, name):
    nw = len(sums)

    def body(*refs):
        ins, outs = refs[:nw], refs[nw:2 * nw]
        send_sems, recv_sems = refs[2 * nw:]
        x, y, c, _ = _place()
        copies = [
            pltpu.make_async_remote_copy(src_ref=ins[w], dst_ref=outs[w], send_sem=send_sems.at[w], recv_sem=recv_sems.at[w],
                                         device_id=(x, y, 1 - c), device_id_type=MESH)
            for w in range(nw)
        ]
        for cp in copies:
            cp.start()
        for cp in copies:
            cp.wait()

    return pl.pallas_call(
        body,
        name=name,
        out_shape=tuple(jax.ShapeDtypeStruct(s.shape, s.dtype) for s in sums),
        in_specs=[_HBM] * nw,
        out_specs=tuple([_HBM] * nw),
        scratch_shapes=[pltpu.SemaphoreType.DMA((nw,)), pltpu.SemaphoreType.DMA((nw,))],
    )(*sums)


def gather_blocks(block, *, reduce, name):
    rows, cols = block.shape

    def body(x_ref, out_ref, *rest):
        if reduce:
            buf_ref, send_sems, recv_sems = rest
        else:
            send_sems, recv_sems = rest
            buf_ref = out_ref
        x, y, c, flips = _place()
        me, sibling = (x, y, c), (x, y, 1 - c)

        def slot(px, py, pc):
            return buf_ref.at[4 * px + 2 * py + pc]

        def copy(k, blk, to, src=None):
            return pltpu.make_async_remote_copy(src_ref=slot(*blk) if src is None else src, dst_ref=slot(*blk),
                                                send_sem=send_sems.at[k], recv_sem=recv_sems.at[k], device_id=to,
                                                device_id_type=MESH)

        buf_ref[4 * x + 2 * y + c] = x_ref[...]
        first = [copy(0, me, sibling, src=x_ref)]
        first += [copy(1 + j, me, (*chip, c), src=x_ref) for j, chip in enumerate(flips)]
        for cp in first:
            cp.start()
        passed = [copy(4 + j, (*chip, c), sibling) for j, chip in enumerate(flips)]
        for j, chip in enumerate(flips):
            copy(1 + j, (*chip, c), me).wait_recv()
            passed[j].start()
        copy(0, sibling, me).wait_recv()
        for j, chip in enumerate(flips):
            copy(4 + j, (*chip, 1 - c), me).wait_recv()
        for cp in first + passed:
            cp.wait_send()
        if reduce:
            acc = buf_ref[0]
            for dev in range(1, N_DEV):
                acc = acc + buf_ref[dev]
            out_ref[...] = acc

    vmem = pl.BlockSpec(memory_space=pltpu.VMEM)
    sems = [pltpu.SemaphoreType.DMA((7,)), pltpu.SemaphoreType.DMA((7,))]
    if reduce:
        out_shape = jax.ShapeDtypeStruct((rows, cols), block.dtype)
        scratch = [pltpu.VMEM((N_DEV, rows, cols), block.dtype)] + sems
    else:
        out_shape = jax.ShapeDtypeStruct((N_DEV, rows, cols), block.dtype)
        scratch = sems
    return pl.pallas_call(
        body,
        name=name,
        out_shape=out_shape,
        in_specs=[vmem],
        out_specs=vmem,
        scratch_shapes=scratch,
        compiler_params=pltpu.CompilerParams(vmem_limit_bytes=VMEM_LIMIT_BYTES),
    )(block)


def add_own_half(grad, recv, core, *, name):
    _, r2, cols = recv.shape
    tr = _tile(r2, max(16, (1 << 20) // (2 * cols) // 16 * 16), 16)
    nr = r2 // tr

    def body(core_ref, g_ref, r_ref, o_ref):
        o_ref[...] = (g_ref[...].astype(F32) + r_ref[...].astype(F32)).astype(o_ref.dtype)

    return pl.pallas_call(
        body,
        name=name,
        out_shape=jax.ShapeDtypeStruct(recv.shape, BF16),
        grid_spec=pltpu.PrefetchScalarGridSpec(
            num_scalar_prefetch=1,
            grid=(N_CHIPS, nr),
            in_specs=[
                pl.BlockSpec((None, tr, cols), lambda k, i, core_ref: (k, core_ref[0] * nr + i, 0)),
                pl.BlockSpec((None, tr, cols), lambda k, i, core_ref: (k, i, 0)),
            ],
            out_specs=pl.BlockSpec((None, tr, cols), lambda k, i, core_ref: (k, i, 0)),
        ),
        compiler_params=_cparams("parallel", "parallel"),
    )(core, grad, recv)


def sum_chips(part, recv, chip, sums, layer, *, name):
    _, r2, cols = part.shape
    tr = _tile(r2, max(16, (1 << 20) // (2 * cols) // 16 * 16), 16)

    def body(chip_ref, p_ref, r_ref, sums_ref, o_ref):
        acc = p_ref[...].astype(F32)
        for k in range(3):
            acc = acc + r_ref[k].astype(F32)
        o_ref[...] = acc

    return pl.pallas_call(
        body,
        name=name,
        out_shape=jax.ShapeDtypeStruct(sums.shape, F32),
        grid_spec=pltpu.PrefetchScalarGridSpec(
            num_scalar_prefetch=1,
            grid=(r2 // tr,),
            in_specs=[
                pl.BlockSpec((None, tr, cols), lambda i, chip_ref: (chip_ref[0], i, 0)),
                pl.BlockSpec((3, tr, cols), lambda i, chip_ref: (0, i, 0)),
                pl.BlockSpec(memory_space=pl.ANY),
            ],
            out_specs=pl.BlockSpec((None, tr, cols), lambda i, chip_ref: (layer, i, 0)),
        ),
        input_output_aliases={3: 0},
        compiler_params=_cparams("parallel"),
    )(chip, part, recv, sums)


WEIGHTS = ("w_in", "w_uq", "w_ukv", "w_branch", "w_out", "w_ffn_in", "w_ffn_out")
GAINS = ("g_mix_pre", "g_mix_post", "g_ffn_pre", "g_ffn_post")


def layer_fwd(cfg, h, w, s, tabs, tag):
    m, d, hd = cfg.m, cfg.d, cfg.heads
    fox_blk = cfg.off_fox // LANES
    hn = rmsnorm_fwd(h, s["g_mix_pre"], BF16, name=f"norm_mix_pre{tag}")
    proj = matmul(hn, w["w_in"], "nn", BF16, tm=m, tn=_tile(cfg.d_inp, 512), tk=d, name=f"proj{tag}")
    fl = matmul(hn, w["w_in"][:, cfg.off_fl:cfg.off_fl + LANES], "nn", F32, tm=m, tn=LANES, tk=d, name=f"proj_forget{tag}")
    cqn = rmsnorm_fwd(proj, s["g_q_lat"], BF16, width=cfg.q_rank, col_blk=cfg.off_cq // cfg.q_rank, name=f"norm_q{tag}")
    ckvn = rmsnorm_fwd(proj, s["g_kv_lat"], BF16, width=cfg.kv_rank, col_blk=cfg.off_ckv // cfg.kv_rank, name=f"norm_kv{tag}")
    q = matmul(cqn, w["w_uq"], "nn", BF16, tm=m, tn=_tile(2 * cfg.width, 512), tk=cfg.q_rank, name=f"up_q{tag}")
    kv = matmul(ckvn, w["w_ukv"], "nn", BF16, tm=m, tn=_tile(2 * cfg.width, 512), tk=cfg.kv_rank, name=f"up_kv{tag}")
    qf, kf = mla_prep_fwd(cfg, q, kv, proj, tabs[0], name=f"mla_prep{tag}")
    o_a, lse_a = attn_fwd(qf, kf, kv, heads=hd, dk=2 * LANES, dv=LANES, qblk0=0, kblk0=0, vblk0=hd,
                          scale=(LANES + ROPE) ** -0.5, pad=cfg.pad, name=f"mla_attn{tag}")
    o_b = conv_fwd(cfg, proj, s["conv_w"], name=f"conv{tag}")
    b_pad = jnp.pad(s["b_forget"], (0, LANES - hd)).reshape(1, LANES)
    cum = fox_gate_fwd(cfg, fl, b_pad, name=f"fox_gate{tag}")
    cum_t = cum[:, :hd].T
    decay = (cum_t[:, :, None], cum_t[:, None, :])
    o_c, lse_c = attn_fwd(proj, proj, proj, heads=hd, dk=LANES, dv=LANES, qblk0=fox_blk, kblk0=fox_blk + hd,
                          vblk0=fox_blk + 2 * hd, scale=LANES ** -0.5, pad=cfg.pad, decay=decay, name=f"fox_attn{tag}")
    o = jnp.stack([o_a, o_b, o_c])
    y = matmul(o, w["w_branch"], "nn", BF16, tm=m, tn=_tile(d, 512), tk=cfg.width, name=f"branch{tag}")
    merged = gate_merge_fwd(cfg, y, proj, name=f"merge{tag}")
    mix = matmul(merged, w["w_out"], "nn", F32, tm=m, tn=_tile(d, 256), tk=d, name=f"out_proj{tag}")
    h_mid = rmsnorm_fwd(mix, s["g_mix_post"], F32, res=h, name=f"norm_mix_post{tag}")
    if hasattr(w, "land_ffn"):
        s = w.land_ffn(h_mid, s)
    hn2 = rmsnorm_fwd(h_mid, s["g_ffn_pre"], BF16, name=f"norm_ffn_pre{tag}")
    gu = matmul(hn2, w["w_ffn_in"], "nn", BF16, tm=m, tn=_tile(2 * cfg.d_ff, 512), tk=d, name=f"ffn_in{tag}")
    act = swiglu_fwd(cfg, gu, name=f"swiglu{tag}")
    f = matmul(act, w["w_ffn_out"], "nn", F32, tm=m, tn=_tile(d, 512), tk=_tile(cfg.d_ff, 1408), name=f"ffn_out{tag}")
    h_next = rmsnorm_fwd(f, s["g_ffn_post"], F32, res=h_mid, name=f"norm_ffn_post{tag}")
    saved = dict(h=h, hn=hn, proj=proj, fl=fl, cqn=cqn, ckvn=ckvn, kv=kv, qf=qf, kf=kf, lse_a=lse_a,
                 b_pad=b_pad, decay=decay, lse_c=lse_c, o=o, y=y, merged=merged, mix=mix, h_mid=h_mid,
                 hn2=hn2, gu=gu, act=act, f=f)
    return h_next, s, saved


def layer_bwd(cfg, dh, w, s, r, tabs, tag, grads_done):
    m, d, hd = cfg.m, cfg.d, cfg.heads
    fox_blk = cfg.off_fox // LANES
    tk_m = m
    df, dg4 = rmsnorm_bwd(r["f"], s["g_ffn_post"], dh, BF16, name=f"norm_ffn_post_bwd{tag}")
    dact = matmul(df, w["w_ffn_out"], "nt", BF16, tm=m, tn=_tile(cfg.d_ff, 512), tk=d, name=f"ffn_out_dx{tag}")
    dw_fo = matmul(r["act"], df, "tn", BF16, tm=_tile(cfg.d_ff, 512), tn=_tile(d, 1024), tk=tk_m, name=f"ffn_out_dw{tag}")
    dgu = swiglu_bwd(cfg, dact, r["gu"], name=f"swiglu_bwd{tag}")
    dhn2 = matmul(dgu, w["w_ffn_in"], "nt", F32, tm=m, tn=_tile(d, 512), tk=_tile(2 * cfg.d_ff, 1408), name=f"ffn_in_dx{tag}")
    dw_fi = matmul(r["hn2"], dgu, "tn", BF16, tm=_tile(d, 1024), tn=_tile(2 * cfg.d_ff, 512), tk=tk_m, name=f"ffn_in_dw{tag}")
    token = grads_done(dict(w_ffn_in=dw_fi, w_ffn_out=dw_fo))
    if token is not None:
        s = {**s, "g_ffn_pre": s["g_ffn_pre"] + token[0, 0]}
    dh_mid, dg3 = rmsnorm_bwd(r["h_mid"], s["g_ffn_pre"], dhn2, F32, dres=dh, name=f"norm_ffn_pre_bwd{tag}")
    dmix, dg2 = rmsnorm_bwd(r["mix"], s["g_mix_post"], dh_mid, BF16, name=f"norm_mix_post_bwd{tag}")
    dmerged = matmul(dmix, w["w_out"], "nt", BF16, tm=m, tn=_tile(d, 512), tk=d, name=f"out_proj_dx{tag}")
    dw_out = matmul(r["merged"], dmix, "tn", BF16, tm=_tile(d, 1024), tn=_tile(d, 512), tk=tk_m, name=f"out_proj_dw{tag}")
    dy, dgl = gate_merge_bwd(cfg, dmerged, r["y"], r["proj"], name=f"merge_bwd{tag}")
    do = matmul(dy, w["w_branch"], "nt", BF16, tm=m, tn=_tile(cfg.width, 512), tk=d, name=f"branch_dx{tag}")
    dw_br = matmul(r["o"], dy, "tn", BF16, tm=_tile(cfg.width, 1024), tn=_tile(d, 512), tk=tk_m, name=f"branch_dw{tag}")
    dqf, dkf, dv_a = attn_bwd(r["qf"], r["kf"], r["kv"], do, 0, r["lse_a"], heads=hd, dk=2 * LANES, dv=LANES,
                              qblk0=0, kblk0=0, vblk0=hd, scale=(LANES + ROPE) ** -0.5, pad=cfg.pad, name=f"mla_attn_bwd{tag}")
    dq, dkn, dkpe = mla_prep_bwd(cfg, dqf, dkf, tabs[1], name=f"mla_prep_bwd{tag}")
    dkv = jnp.concatenate([dkn, dv_a], axis=1)
    dcqn = matmul(dq, w["w_uq"], "nt", F32, tm=m, tn=cfg.q_rank, tk=2 * cfg.width, name=f"up_q_dx{tag}")
    dw_uq = matmul(r["cqn"], dq, "tn", BF16, tm=cfg.q_rank, tn=_tile(2 * cfg.width, 512), tk=tk_m, name=f"up_q_dw{tag}")
    dckvn = matmul(dkv, w["w_ukv"], "nt", F32, tm=m, tn=cfg.kv_rank, tk=2 * cfg.width, name=f"up_kv_dx{tag}")
    dw_ukv = matmul(r["ckvn"], dkv, "tn", BF16, tm=cfg.kv_rank, tn=_tile(2 * cfg.width, 512), tk=tk_m, name=f"up_kv_dw{tag}")
    dcq, dgq = rmsnorm_bwd(r["proj"], s["g_q_lat"], dcqn, BF16, width=cfg.q_rank, col_blk=cfg.off_cq // cfg.q_rank,
                           name=f"norm_q_bwd{tag}")
    dckv, dgkv = rmsnorm_bwd(r["proj"], s["g_kv_lat"], dckvn, BF16, width=cfg.kv_rank, col_blk=cfg.off_ckv // cfg.kv_rank,
                             name=f"norm_kv_bwd{tag}")
    dcb, dcc, dcx, dconv_w = conv_bwd(cfg, r["proj"], s["conv_w"], do, 1, name=f"conv_bwd{tag}")
    dfq, dfk, dfv, dck = attn_bwd(r["proj"], r["proj"], r["proj"], do, 2, r["lse_c"], heads=hd, dk=LANES, dv=LANES,
                                  qblk0=fox_blk, kblk0=fox_blk + hd, vblk0=fox_blk + 2 * hd, scale=LANES ** -0.5,
                                  pad=cfg.pad, decay=r["decay"], name=f"fox_attn_bwd{tag}")
    dc = jnp.pad(dck[:, 0, :].T, ((0, 0), (0, LANES - hd)))
    dfl, dbf = fox_gate_bwd(cfg, r["fl"], r["b_pad"], dc, name=f"fox_gate_bwd{tag}")
    tail = jnp.zeros((m, cfg.d_inp - cfg.off_fl - LANES), BF16)
    dproj = jnp.concatenate([dgl, dcq, dckv, dcb, dcc, dcx, dfq, dfk, dfv, dkpe.astype(BF16), dfl.astype(BF16), tail], axis=1)
    dhn = matmul(dproj, w["w_in"], "nt", F32, tm=m, tn=_tile(d, 512), tk=_tile(cfg.d_inp, 1536), name=f"proj_dx{tag}")
    dw_in = matmul(r["hn"], dproj, "tn", BF16, tm=_tile(d, 1024), tn=_tile(cfg.d_inp, 512), tk=tk_m, name=f"proj_dw{tag}")
    dh_in, dg1 = rmsnorm_bwd(r["h"], s["g_mix_pre"], dhn, F32, dres=dh_mid, name=f"norm_mix_pre_bwd{tag}")
    token = grads_done(dict(w_in=dw_in, w_uq=dw_uq, w_ukv=dw_ukv, w_branch=dw_br, w_out=dw_out))
    dsmall = dict(g_mix_pre=dg1[0], g_mix_post=dg2[0], g_ffn_pre=dg3[0], g_ffn_post=dg4[0], g_q_lat=dgq[0], g_kv_lat=dgkv[0],
                  b_forget=dbf[0, :hd], conv_w=dconv_w)
    return dh_in, dsmall, token


def local_step(cfg, x, target, meta, layer_params, grads_done):
    h = jnp.concatenate([jnp.zeros((cfg.pad, cfg.d), F32), meta, x], axis=0)
    cos, s1, s2 = rope_tables(cfg)
    tabs = ((cos, s1, s2), (cos, -s1, -s2))
    saved = []
    for l in range(cfg.depth):
        w, s = layer_params(l, h)
        h, s, r = layer_fwd(cfg, h, w, s, tabs, f"_{l}")
        saved.append((w, s, r))
    dh, loss = loss_head(cfg, h, target, name="loss_head")
    dsmalls, token = [None] * cfg.depth, None
    for l in reversed(range(cfg.depth)):
        w, s, r = saved[l]
        if token is not None:
            s = {**s, "g_ffn_post": s["g_ffn_post"] + token[0, 0]}
        dh, dsmalls[l], token = layer_bwd(cfg, dh, w, s, r, tabs, f"_{l}", functools.partial(grads_done, l))
    first = cfg.pad + cfg.n_meta
    return loss, dh[first:], dh[cfg.pad:first], dsmalls


def _cols_from_chips(g):
    return jnp.transpose(g, (1, 0, 2)).reshape(g.shape[1], N_CHIPS * g.shape[2])


def _cols_to_chips(w):
    r, c = w.shape
    return jnp.transpose(w.reshape(r, N_CHIPS, c // N_CHIPS), (1, 0, 2))


def _packed_segments(cfg):
    nat = [0] + _cumsum(cfg.nat_splits)
    w = cfg.width
    order = [(10, 0), (0, cfg.off_cq), (1, cfg.off_ckv), (3, cfg.off_conv), (4, cfg.off_conv + w), (5, cfg.off_conv + 2 * w),
             (6, cfg.off_fox), (7, cfg.off_fox + w), (8, cfg.off_fox + 2 * w), (2, cfg.off_kpe), (9, cfg.off_fl)]
    return [(pk, nat[i], cfg.nat_splits[i]) for i, pk in order]


def _chip_cols(cfg):
    n = cfg.d_in // N_CHIPS
    return n, -(-n // LANES) * LANES


def _lane_pieces(cfg, to_packed):
    n, n_pad = _chip_cols(cfg)
    tiles = [[] for _ in range(cfg.d_inp // LANES if to_packed else N_CHIPS * n_pad // LANES)]
    for pk, nat, width in _packed_segments(cfg):
        g = nat
        while g < nat + width:
            k, a = divmod(g, n)
            dst = (pk + g - nat) if to_packed else (k * n_pad + a)
            run = min(nat + width - g, n - a, LANES - dst % LANES)
            src = (k, a) if to_packed else (0, pk + g - nat)
            tiles[dst // LANES].append((dst % LANES, run, *src))
            g += run
    return tiles


def _fill_tiles(pieces, read, write, rows):
    lane = lax.broadcasted_iota(jnp.int32, (rows, LANES), 1)
    for t, parts in enumerate(pieces):
        tile = jnp.zeros((rows, LANES), F32)
        for dl, run, blk, col in parts:
            w0 = col // LANES * LANES
            off = col - w0
            span = LANES if off + run <= LANES else 2 * LANES
            win = read(blk, w0, span)
            shift = (dl - off) % span
            if shift:
                win = pltpu.roll(win, shift, 1)
            win = win[:, :LANES]
            tile = win if (dl == 0 and run == LANES) else jnp.where((lane >= dl) & (lane < dl + run), win, tile)
        write(t, tile)


def pack_w_in_blocks(cfg, lands, own, chip, *, name):
    d = cfg.d
    n, n_pad = _chip_cols(cfg)
    tr = _tile(d, 256, 16)
    pieces = _lane_pieces(cfg, True)

    def body(chip_ref, land_ref, own_ref, o_ref):
        def read(k, w0, span):
            theirs = land_ref[k, :, w0:w0 + span]
            return jnp.where(chip_ref[0] == k, own_ref[:, w0:w0 + span], theirs).astype(F32)

        def write(t, tile):
            o_ref[:, t * LANES:(t + 1) * LANES] = tile.astype(o_ref.dtype)

        _fill_tiles(pieces, read, write, tr)

    return pl.pallas_call(
        body,
        name=name,
        out_shape=jax.ShapeDtypeStruct((d, cfg.d_inp), BF16),
        grid_spec=pltpu.PrefetchScalarGridSpec(
            num_scalar_prefetch=1,
            grid=(d // tr,),
            in_specs=[pl.BlockSpec((N_CHIPS, tr, n_pad), lambda i, chip_ref: (0, i, 0)),
                      pl.BlockSpec((tr, n_pad), lambda i, chip_ref: (i, 0))],
            out_specs=pl.BlockSpec((tr, cfg.d_inp), lambda i, chip_ref: (i, 0)),
        ),
        compiler_params=_cparams("parallel"),
    )(chip, lands, own)


def unpack_w_in_blocks(cfg, dw, *, name):
    d = cfg.d
    n, n_pad = _chip_cols(cfg)
    tr = _tile(d, 256, 16)
    per_blk = n_pad // LANES
    pieces = _lane_pieces(cfg, False)

    def body(dw_ref, o_ref):
        def read(_, w0, span):
            return dw_ref[:, w0:w0 + span].astype(F32)

        def write(t, tile):
            k, i = divmod(t, per_blk)
            o_ref[k, :, i * LANES:(i + 1) * LANES] = tile.astype(o_ref.dtype)

        _fill_tiles(pieces, read, write, tr)

    return pl.pallas_call(
        body,
        name=name,
        out_shape=jax.ShapeDtypeStruct((N_CHIPS, d, n_pad), BF16),
        grid=(d // tr,),
        in_specs=[pl.BlockSpec((tr, cfg.d_inp), lambda i: (i, 0))],
        out_specs=pl.BlockSpec((N_CHIPS, tr, n_pad), lambda i: (0, i, 0)),
        compiler_params=_cparams("parallel"),
    )(dw)


def full_weights(cfg, g, w_in=None):
    make = dict(
        w_uq=lambda a: pack_w_uq(cfg, _cols_from_chips(a)),
        w_ukv=lambda a: pack_w_ukv(cfg, _cols_from_chips(a)),
        w_branch=lambda a: _cols_from_chips(a).reshape(3, cfg.width, cfg.d),
        w_out=lambda a: a.reshape(cfg.d, cfg.d),
        w_ffn_in=_cols_from_chips,
        w_ffn_out=lambda a: a.reshape(cfg.d_ff, cfg.d),
    )
    out = {n: make[n](a) for n, a in g.items()}
    if w_in is not None:
        out["w_in"] = w_in
    return out


def chip_grads(cfg, dw, tag):
    make = dict(
        w_in=lambda a: unpack_w_in_blocks(cfg, a, name=f"unpack_w_in{tag}"),
        w_uq=lambda a: _cols_to_chips(unpack_w_uq(cfg, a)),
        w_ukv=lambda a: _cols_to_chips(unpack_w_ukv(cfg, a)),
        w_branch=lambda a: _cols_to_chips(a.reshape(3 * cfg.width, cfg.d)),
        w_out=lambda a: a.reshape(N_CHIPS, cfg.d // N_CHIPS, cfg.d),
        w_ffn_in=_cols_to_chips,
        w_ffn_out=lambda a: a.reshape(N_CHIPS, cfg.d_ff // N_CHIPS, cfg.d),
    )
    return {n: make[n](a) for n, a in dw.items()}


def _small_rows(cfg):
    return dict(g_mix_pre=cfg.d // LANES, g_mix_post=cfg.d // LANES, g_ffn_pre=cfg.d // LANES, g_ffn_post=cfg.d // LANES,
                g_q_lat=cfg.q_rank // LANES, g_kv_lat=cfg.kv_rank // LANES, b_forget=1, conv_w=3 * cfg.width // LANES)


def pack_small(cfg, loss, dmeta, dsmalls):
    parts = [loss[0:1, :], dmeta.reshape(-1, LANES)]
    for ds in dsmalls:
        for k in _small_rows(cfg):
            v = ds[k]
            if k == "b_forget":
                v = jnp.pad(v, (0, LANES - cfg.heads))
            parts.append(v.reshape(-1, LANES))
    rows = sum(p.shape[0] for p in parts)
    parts.append(jnp.zeros((-rows % 8, LANES), F32))
    return jnp.concatenate(parts, axis=0)


def unpack_small(cfg, block):
    loss = block[0, 0]
    n = cfg.n_meta * cfg.d // LANES
    dmeta = block[1:1 + n].reshape(cfg.n_meta, cfg.d)
    at = 1 + n
    out = []
    for _ in range(cfg.depth):
        ds = {}
        for k, rows in _small_rows(cfg).items():
            v = block[at:at + rows]
            at += rows
            if k == "b_forget":
                v = v[0, :cfg.heads]
            elif k == "conv_w":
                v = v.reshape(3, cfg.width)
            else:
                v = v.reshape(-1)
            ds[k] = v
        out.append(ds)
    return loss, dmeta, out


def kernel(x, meta, w_in, b_forget, g_q_lat, g_kv_lat, w_uq, w_ukv, conv_w, w_branch, w_out, w_ffn_in, w_ffn_out, g_mix_pre, g_mix_post, g_ffn_pre, g_ffn_post, loss_target, m_meta, m_w_in, m_b_forget, m_g_q_lat, m_g_kv_lat, m_w_uq, m_w_ukv, m_conv_w, m_w_branch, m_w_out, m_w_ffn_in, m_w_ffn_out, m_g_mix_pre, m_g_mix_post, m_g_ffn_pre, m_g_ffn_post, v_meta, v_w_in, v_b_forget, v_g_q_lat, v_g_kv_lat, v_w_uq, v_w_ukv, v_conv_w, v_w_branch, v_w_out, v_w_ffn_in, v_w_ffn_out, v_g_mix_pre, v_g_mix_post, v_g_ffn_pre, v_g_ffn_post):
    cfg = CFG
    names = ("meta", "w_in", "b_forget", "g_q_lat", "g_kv_lat", "w_uq", "w_ukv", "conv_w", "w_branch", "w_out", "w_ffn_in",
             "w_ffn_out", "g_mix_pre", "g_mix_post", "g_ffn_pre", "g_ffn_post")
    params = dict(zip(names, (meta, w_in, b_forget, g_q_lat, g_kv_lat, w_uq, w_ukv, conv_w, w_branch, w_out, w_ffn_in, w_ffn_out,
                              g_mix_pre, g_mix_post, g_ffn_pre, g_ffn_post)))
    mom1 = dict(zip(names, (m_meta, m_w_in, m_b_forget, m_g_q_lat, m_g_kv_lat, m_w_uq, m_w_ukv, m_conv_w, m_w_branch, m_w_out,
                            m_w_ffn_in, m_w_ffn_out, m_g_mix_pre, m_g_mix_post, m_g_ffn_pre, m_g_ffn_post)))
    mom2 = dict(zip(names, (v_meta, v_w_in, v_b_forget, v_g_q_lat, v_g_kv_lat, v_w_uq, v_w_ukv, v_conv_w, v_w_branch, v_w_out,
                            v_w_ffn_in, v_w_ffn_out, v_g_mix_pre, v_g_mix_post, v_g_ffn_pre, v_g_ffn_post)))
    xi, yi, ci = lax.axis_index("x"), lax.axis_index("y"), lax.axis_index("c")
    chip = 2 * xi + yi
    chip_arr = jnp.reshape(chip, (1,)).astype(jnp.int32)
    core_arr = jnp.reshape(ci, (1,)).astype(jnp.int32)

    meta_all = gather_blocks(meta, reduce=False, name="gather_meta")[0::2]
    meta_full = jnp.transpose(meta_all, (1, 0, 2)).reshape(cfg.n_meta, cfg.d)
    conv_rows = conv_w.reshape(cfg.depth * 3, cfg.width // N_CHIPS)
    conv_all = gather_blocks(conv_rows, reduce=False, name="gather_conv_w")[0::2]
    conv_full = jnp.transpose(conv_all, (1, 0, 2)).reshape(cfg.depth, 3, cfg.width)

    def shard2d(name, l, after=None):
        w = params[name][l]
        if after is not None:
            w = w + after
        w = w.reshape(-1, w.shape[-1]).astype(BF16)
        if name == "w_in":
            w = jnp.pad(w, ((0, 0), (0, _chip_cols(cfg)[1] - w.shape[1])))
        return w

    is_mine = (jnp.arange(N_CHIPS) == chip)[:, None, None]
    shard_shape = {n: shard2d(n, 0).shape for n in WEIGHTS}
    groups = (("w_in", "w_uq", "w_ukv", "w_branch", "w_out"), ("w_ffn_in", "w_ffn_out"))
    gather_plans = [_gather_plan([shard_shape[n][0] // 2 for n in g]) for g in groups]

    def gather_start(l, after):
        started = []
        for gi, g in enumerate(groups):
            started.append(copies_start([shard2d(n, l, after) for n in g], [(N_CHIPS,) + shard_shape[n] for n in g],
                                        gather_plans[gi], name=f"gather_start_{gi}_{l}"))
            after = started[-1][4][0, 0]
        return started

    def land(l, gi, started, after):
        send_sems, recv_sems, own, lands, _ = started
        own, lands, landed = copies_wait(send_sems, recv_sems, own, lands, gather_plans[gi], after, name=f"gather_wait_{gi}_{l}")
        return own, forward_halves(lands, name=f"forward_halves_{gi}_{l}"), landed

    in_flight = {0: gather_start(0, None)}

    class LayerWeights(dict):
        def __init__(self, l, h):
            own, lands, _ = land(l, 0, in_flight[l][0], h)
            got = {n: jnp.where(is_mine, o[None], g) for n, o, g in zip(groups[0][1:], own[1:], lands[1:])}
            super().__init__(full_weights(cfg, got, pack_w_in_blocks(cfg, lands[0], own[0], chip_arr, name=f"pack_w_in_{l}")))
            self.layer = l

        def land_ffn(self, after, s):
            l = self.layer
            own, lands, landed = land(l, 1, in_flight.pop(l)[1], after)
            self.update(full_weights(cfg, {n: jnp.where(is_mine, o[None], g) for n, o, g in zip(groups[1], own, lands)}))
            if l + 1 == cfg.depth:
                return s
            in_flight[l + 1] = gather_start(l + 1, landed[0, 0])
            return {**s, "g_ffn_pre": s["g_ffn_pre"] + in_flight[l + 1][1][4][0, 0]}

    def layer_params(l, h):
        s = dict(g_mix_pre=g_mix_pre[l], g_mix_post=g_mix_post[l], g_ffn_pre=g_ffn_pre[l], g_ffn_post=g_ffn_post[l],
                 g_q_lat=g_q_lat[l], g_kv_lat=g_kv_lat[l], b_forget=b_forget[l], conv_w=conv_full[l])
        return LayerWeights(l, h), s

    sums = {n: jnp.zeros((cfg.depth, shard_shape[n][0] // 2, shard_shape[n][1]), F32) for n in WEIGHTS}
    exchanging = []

    def finish_exchange(after):
        l, names, (send_sems, recv_sems, parts, lands, _) = exchanging.pop()
        parts, others, _ = copies_wait(send_sems, recv_sems, parts, lands, _exchange_plan(len(names)), after,
                                       name=f"exchange_wait_{names[0]}_{l}")
        for n, p, o in zip(names, parts, others):
            sums[n] = sum_chips(p, o, chip_arr, sums[n], l, name=f"sum_chips_{n}_{l}")

    def grads_done(l, dws):
        names = [n for n in WEIGHTS if n in dws]
        send = chip_grads(cfg, dws, f"_{l}")
        mine = [send[n] for n in names]
        theirs = swap_halves(mine, name=f"swap_halves_{names[0]}_{l}")
        parts = [add_own_half(g, t, core_arr, name=f"add_own_half_{n}_{l}") for n, g, t in zip(names, mine, theirs)]
        started = copies_start(parts, [(3,) + p.shape[1:] for p in parts], _exchange_plan(len(names)),
                               name=f"exchange_start_{names[0]}_{l}")
        if exchanging:
            finish_exchange(started[4])
        exchanging.append((l, names, started))
        return started[4]

    loss, grad_x, dmeta, dsmalls = local_step(cfg, x[0], loss_target[0], meta_full, layer_params, grads_done)
    finish_exchange(exchanging[-1][2][4])
    own_half = [sums[n] for n in WEIGHTS]
    other_half = share_halves(own_half, name="share_halves")
    grad, delta, new_m, new_v = {}, {}, {}, {}
    for n, mine_, theirs_ in zip(WEIGHTS, own_half, other_half):
        shp = params[n].shape
        three_d = lambda a: a.reshape(cfg.depth, -1, shp[-1])
        out = adamw_halves(three_d(params[n]), mine_, theirs_, core_arr, three_d(mom1[n]), three_d(mom2[n]), name=f"adamw_{n}")
        grad[n], delta[n], new_m[n], new_v[n] = (a.reshape(shp) for a in out)

    total = gather_blocks(pack_small(cfg, loss, dmeta, dsmalls), reduce=True, name="reduce_small")
    loss_sum, dmeta_sum, dsmall_sum = unpack_small(cfg, total)
    for k in _small_rows(cfg):
        grad[k] = jnp.stack([ds[k] for ds in dsmall_sum])
    grad["conv_w"] = lax.dynamic_slice_in_dim(grad["conv_w"], chip * (cfg.width // N_CHIPS), cfg.width // N_CHIPS, axis=2)
    grad["meta"] = lax.dynamic_slice_in_dim(dmeta_sum, chip * (cfg.d // N_CHIPS), cfg.d // N_CHIPS, axis=1)

    for n in names:
        if n in WEIGHTS:
            continue
        shp = params[n].shape
        two_d = lambda a: a.reshape(-1, shp[-1])
        dl, nm, nv = adamw(two_d(params[n]), two_d(grad[n]), two_d(mom1[n]), two_d(mom2[n]), name=f"adamw_{n}")
        delta[n], new_m[n], new_v[n] = dl.reshape(shp), nm.reshape(shp), nv.reshape(shp)

    return (loss_sum, grad_x[None], *[grad[n] for n in names], *[delta[n] for n in names], *[new_m[n] for n in names],
            *[new_v[n] for n in names])
```

```python
import functools
from typing import NamedTuple

import jax
import jax.numpy as jnp
from jax import lax
from jax.experimental import pallas as pl
from jax.experimental.pallas import tpu as pltpu

F32 = jnp.float32
BF16 = jnp.bfloat16
MESH = pl.DeviceIdType.MESH

EPS = 1e-6
NEG_INF = -1e30
ROPE_THETA = 10000.0
LANES = 128
ROPE = 64
N_CHIPS = 4
N_DEV = 8

ADAM_LR = 0.001
ADAM_B1 = 0.9
ADAM_B2 = 0.999
ADAM_EPS = 1e-08
ADAM_WD = 0.01
ADAM_STEP = 10

VMEM_LIMIT_BYTES = 48 * 1024 * 1024


class Cfg(NamedTuple):
    d: int = 2048
    seq: int = 2048
    depth: int = 4
    n_meta: int = 16
    heads: int = 8
    q_rank: int = 512
    kv_rank: int = 512
    d_ff: int = 5632

    @property
    def width(self):
        return self.heads * LANES

    @property
    def pad(self):
        return (-(self.n_meta + self.seq)) % LANES

    @property
    def m(self):
        return self.pad + self.n_meta + self.seq

    @property
    def nat_splits(self):
        w = self.width
        return (self.q_rank, self.kv_rank, ROPE, w, w, w, w, w, w, self.heads, 3 * self.d)

    @property
    def d_in(self):
        return sum(self.nat_splits)

    @property
    def off_cq(self):
        return 3 * self.d

    @property
    def off_ckv(self):
        return self.off_cq + self.q_rank

    @property
    def off_conv(self):
        return self.off_ckv + self.kv_rank

    @property
    def off_fox(self):
        return self.off_conv + 3 * self.width

    @property
    def off_kpe(self):
        return self.off_fox + 3 * self.width

    @property
    def off_fl(self):
        return self.off_kpe + LANES

    @property
    def d_inp(self):
        return -(-(self.off_fl + LANES) // 512) * 512


CFG = Cfg()


def _tile(n, target, mult=LANES):
    best = None
    t = mult
    while t <= min(n, target):
        if n % t == 0:
            best = t
        t += mult
    return best or n


def _cparams(*sem):
    return pltpu.CompilerParams(dimension_semantics=sem, vmem_limit_bytes=VMEM_LIMIT_BYTES)


def pack_w_in(cfg, w):
    cq, ckv, kpe, cb, cc, cx, fq, fk, fv, fl, gate = jnp.split(w, list(_cumsum(cfg.nat_splits))[:-1], axis=1)
    z = lambda n: jnp.zeros((w.shape[0], n), w.dtype)
    tail = cfg.d_inp - cfg.off_fl - cfg.heads
    return jnp.concatenate([gate, cq, ckv, cb, cc, cx, fq, fk, fv, kpe, z(LANES - ROPE), fl, z(tail)], axis=1)


def unpack_w_in(cfg, wp):
    w = cfg.width
    sizes = (3 * cfg.d, cfg.q_rank, cfg.kv_rank, w, w, w, w, w, w, ROPE, LANES - ROPE, cfg.heads, cfg.d_inp - cfg.off_fl - cfg.heads)
    gate, cq, ckv, cb, cc, cx, fq, fk, fv, kpe, _, fl, _ = jnp.split(wp, list(_cumsum(sizes))[:-1], axis=1)
    return jnp.concatenate([cq, ckv, kpe, cb, cc, cx, fq, fk, fv, fl, gate], axis=1)


def _cumsum(xs):
    out, s = [], 0
    for v in xs:
        s += v
        out.append(s)
    return out


def pack_w_uq(cfg, w):
    r = w.shape[0]
    w3 = w.reshape(r, cfg.heads, LANES + ROPE)
    w3 = jnp.pad(w3, ((0, 0), (0, 0), (0, LANES - ROPE)))
    return w3.reshape(r, cfg.heads * 2 * LANES)


def unpack_w_uq(cfg, wp):
    r = wp.shape[0]
    return wp.reshape(r, cfg.heads, 2 * LANES)[:, :, : LANES + ROPE].reshape(r, cfg.heads * (LANES + ROPE))


def pack_w_ukv(cfg, w):
    r = w.shape[0]
    w4 = w.reshape(r, cfg.heads, 2, LANES)
    return jnp.transpose(w4, (0, 2, 1, 3)).reshape(r, 2 * cfg.heads * LANES)


def unpack_w_ukv(cfg, wp):
    r = wp.shape[0]
    w4 = wp.reshape(r, 2, cfg.heads, LANES)
    return jnp.transpose(w4, (0, 2, 1, 3)).reshape(r, 2 * cfg.heads * LANES)


_DIMS = {
    "nn": (((1,), (0,)), ((), ())),
    "nt": (((1,), (1,)), ((), ())),
    "tn": (((0,), (0,)), ((), ())),
}


def matmul(a, b, mode, out_dtype, *, tm, tn, tk, name, chip_cols=False):
    batched = a.ndim == 3
    if mode == "nn":
        (m, kc), n = a.shape[-2:], b.shape[-1]
        a_blk, a_idx = (tm, tk), lambda i, j, k: (i, k)
        b_blk, b_idx = (tk, tn), lambda i, j, k: (k, j)
    elif mode == "nt":
        (m, kc), n = a.shape[-2:], b.shape[-2]
        a_blk, a_idx = (tm, tk), lambda i, j, k: (i, k)
        b_blk, b_idx = (tn, tk), lambda i, j, k: (j, k)
    else:
        (kc, m), n = a.shape[-2:], b.shape[-1]
        a_blk, a_idx = (tk, tm), lambda i, j, k: (k, i)
        b_blk, b_idx = (tk, tn), lambda i, j, k: (k, j)
    assert m % tm == 0 and n % tn == 0 and kc % tk == 0, (name, m, n, kc, tm, tn, tk)
    nk = kc // tk
    dims = _DIMS[mode]
    o_blk, o_idx = (tm, tn), lambda i, j, k: (i, j)
    grid = (m // tm, n // tn, nk)
    per = n // N_CHIPS // tn
    assert not chip_cols or n // N_CHIPS % tn == 0
    if batched:
        nb = a.shape[0]
        grid = (nb,) + grid
        wrap = lambda f: (lambda bb, i, j, k: (bb,) + f(i, j, k))
        a_blk, b_blk, o_blk = (None,) + a_blk, (None,) + b_blk, (None,) + o_blk
        a_idx, b_idx, o_idx = wrap(a_idx), wrap(b_idx), wrap(o_idx)
        out_shape = (nb, m, n)
        if chip_cols:
            o_blk, o_idx = (None,) + o_blk, lambda bb, i, j, k: (j // per, bb, i, j % per)
            out_shape = (N_CHIPS, nb, m, n // N_CHIPS)
        sem = ("parallel", "parallel", "parallel", "arbitrary")
    else:
        out_shape = (m, n)
        if chip_cols:
            o_blk, o_idx = (None,) + o_blk, lambda i, j, k: (j // per, i, j % per)
            out_shape = (N_CHIPS, m, n // N_CHIPS)
        sem = ("parallel", "parallel", "arbitrary")
    k_axis = len(grid) - 1

    def body(a_ref, b_ref, o_ref, *scratch):
        prod = lax.dot_general(a_ref[...], b_ref[...], dims, preferred_element_type=F32)
        if nk == 1:
            o_ref[...] = prod.astype(o_ref.dtype)
        else:
            acc_ref = scratch[0]
            k = pl.program_id(k_axis)

            @pl.when(k == 0)
            def _():
                acc_ref[...] = prod

            @pl.when(k > 0)
            def _():
                acc_ref[...] += prod

            @pl.when(k == nk - 1)
            def _():
                o_ref[...] = acc_ref[...].astype(o_ref.dtype)

    return pl.pallas_call(
        body,
        name=name,
        out_shape=jax.ShapeDtypeStruct(out_shape, out_dtype),
        grid=grid,
        in_specs=[pl.BlockSpec(a_blk, a_idx), pl.BlockSpec(b_blk, b_idx)],
        out_specs=pl.BlockSpec(o_blk, o_idx),
        scratch_shapes=[] if nk == 1 else [pltpu.VMEM((tm, tn), F32)],
        compiler_params=_cparams(*sem),
    )(a, b)


def _row_tile(m):
    return _tile(m, 272, 16)


def rmsnorm_fwd(x, g, out_dtype, *, name, width=None, col_blk=0, res=None):
    m = x.shape[0]
    n = width or x.shape[1]
    tm = _row_tile(m)
    has_res = res is not None

    def body(x_ref, g_ref, *rest):
        o_ref = rest[-1]
        xf = x_ref[...].astype(F32)
        r = lax.rsqrt(jnp.mean(xf * xf, axis=-1, keepdims=True) + EPS)
        y = xf * r * g_ref[...]
        if has_res:
            y = rest[0][...] + y
        o_ref[...] = y.astype(o_ref.dtype)

    in_specs = [pl.BlockSpec((tm, n), lambda i: (i, col_blk)), pl.BlockSpec((1, n), lambda i: (0, 0))]
    args = [x, g.reshape(1, n)]
    if has_res:
        in_specs.append(pl.BlockSpec((tm, n), lambda i: (i, 0)))
        args.append(res)
    return pl.pallas_call(
        body,
        name=name,
        out_shape=jax.ShapeDtypeStruct((m, n), out_dtype),
        grid=(m // tm,),
        in_specs=in_specs,
        out_specs=pl.BlockSpec((tm, n), lambda i: (i, 0)),
        compiler_params=_cparams("parallel"),
    )(*args)


def rmsnorm_bwd(x, g, dy, out_dtype, *, name, width=None, col_blk=0, dres=None):
    m = x.shape[0]
    n = width or x.shape[1]
    tm = _row_tile(m)
    has_res = dres is not None

    def body(x_ref, g_ref, dy_ref, *rest):
        dx_ref, dg_ref = rest[-2:]
        i = pl.program_id(0)
        xf = x_ref[...].astype(F32)
        r = lax.rsqrt(jnp.mean(xf * xf, axis=-1, keepdims=True) + EPS)
        xhat = xf * r
        dyf = dy_ref[...].astype(F32)
        dxh = dyf * g_ref[...]
        dx = r * (dxh - xhat * jnp.mean(dxh * xhat, axis=-1, keepdims=True))
        if has_res:
            dx = dx + rest[0][...]
        dx_ref[...] = dx.astype(dx_ref.dtype)
        part = jnp.sum(dyf * xhat, axis=0, keepdims=True)

        @pl.when(i == 0)
        def _():
            dg_ref[...] = part

        @pl.when(i > 0)
        def _():
            dg_ref[...] += part

    in_specs = [
        pl.BlockSpec((tm, n), lambda i: (i, col_blk)),
        pl.BlockSpec((1, n), lambda i: (0, 0)),
        pl.BlockSpec((tm, n), lambda i: (i, 0)),
    ]
    args = [x, g.reshape(1, n), dy]
    if has_res:
        in_specs.append(pl.BlockSpec((tm, n), lambda i: (i, 0)))
        args.append(dres)
    return pl.pallas_call(
        body,
        name=name,
        out_shape=(jax.ShapeDtypeStruct((m, n), out_dtype), jax.ShapeDtypeStruct((1, n), F32)),
        grid=(m // tm,),
        in_specs=in_specs,
        out_specs=(pl.BlockSpec((tm, n), lambda i: (i, 0)), pl.BlockSpec((1, n), lambda i: (0, 0))),
        compiler_params=_cparams("arbitrary"),
    )(*args)


_NT = (((1,), (1,)), ((), ()))
_NN = (((1,), (0,)), ((), ()))
_TN = (((0,), (0,)), ((), ()))


def _attn_scores(q, k, scale, decay_refs, i, tq, kn, pad):
    s = lax.dot_general(q, k, _NT, preferred_element_type=F32) * scale
    if decay_refs is not None:
        cq_ref, ck_ref = decay_refs
        s = s + (cq_ref[0] - ck_ref[0][:, :kn])
    t_idx = i * tq + lax.broadcasted_iota(jnp.int32, (tq, 1), 0)
    s_idx = lax.broadcasted_iota(jnp.int32, (1, kn), 1)
    mask = (s_idx <= t_idx) & (s_idx >= pad)
    return s, mask, t_idx


def _keys_needed(i, tq, m):
    return min(m, -(-((i + 1) * tq) // LANES) * LANES)


def attn_fwd(q, k, v, *, heads, dk, dv, qblk0, kblk0, vblk0, scale, pad, decay=None, name):
    m = q.shape[0]
    tq = _row_tile(m)
    has_decay = decay is not None

    def body(q_ref, k_ref, v_ref, *rest):
        o_ref, lse_ref = rest[-2:]
        decay_refs = rest[:2] if has_decay else None

        def block(i):
            kn = _keys_needed(i, tq, m)
            s, mask, t_idx = _attn_scores(q_ref[...], k_ref[0:kn, :], scale, decay_refs, i, tq, kn, pad)
            s = jnp.where(mask, s, NEG_INF)
            mx = jnp.max(s, axis=1, keepdims=True)
            p = jnp.exp(s - mx)
            l = jnp.sum(p, axis=1, keepdims=True)
            o = lax.dot_general(p.astype(BF16), v_ref[0:kn, :], _NN, preferred_element_type=F32) / l
            o_ref[...] = jnp.where(t_idx >= pad, o, 0.0).astype(o_ref.dtype)
            lse_ref[0] = mx + jnp.log(l)

        for i in range(m // tq):
            pl.when(pl.program_id(1) == i)(functools.partial(block, i))

    in_specs = [
        pl.BlockSpec((tq, dk), lambda h, i: (i, qblk0 + h)),
        pl.BlockSpec((m, dk), lambda h, i: (0, kblk0 + h)),
        pl.BlockSpec((m, dv), lambda h, i: (0, vblk0 + h)),
    ]
    args = [q, k, v]
    if has_decay:
        in_specs += [pl.BlockSpec((1, tq, 1), lambda h, i: (h, i, 0)), pl.BlockSpec((1, 1, m), lambda h, i: (h, 0, 0))]
        args += list(decay)
    return pl.pallas_call(
        body,
        name=name,
        out_shape=(jax.ShapeDtypeStruct((m, heads * dv), BF16), jax.ShapeDtypeStruct((heads, m, 1), F32)),
        grid=(heads, m // tq),
        in_specs=in_specs,
        out_specs=(pl.BlockSpec((tq, dv), lambda h, i: (i, h)), pl.BlockSpec((1, tq, 1), lambda h, i: (h, i, 0))),
        compiler_params=_cparams("parallel", "parallel"),
    )(*args)


def attn_bwd(q, k, v, do, do_sel, lse, *, heads, dk, dv, qblk0, kblk0, vblk0, scale, pad, decay=None, name):
    m = q.shape[0]
    tq = _row_tile(m)
    nq = m // tq
    has_decay = decay is not None

    def body(q_ref, k_ref, v_ref, do_ref, lse_ref, *rest):
        if has_decay:
            cq_ref, ck_ref, dq_ref, dk_ref, dv_ref, dck_ref, dk_acc, dv_acc = rest
            decay_refs = (cq_ref, ck_ref)
        else:
            dq_ref, dk_ref, dv_ref, dk_acc, dv_acc = rest
            decay_refs = None
        @pl.when(pl.program_id(1) == 0)
        def _():
            dk_acc[...] = jnp.zeros_like(dk_acc)
            dv_acc[...] = jnp.zeros_like(dv_acc)
            if has_decay:
                dck_ref[...] = jnp.zeros_like(dck_ref)

        def block(i):
            kn = _keys_needed(i, tq, m)
            qb, kb, dob = q_ref[...], k_ref[0:kn, :], do_ref[...]
            s, mask, _ = _attn_scores(qb, kb, scale, decay_refs, i, tq, kn, pad)
            p = jnp.where(mask, jnp.exp(s - lse_ref[0]), 0.0)
            dp = lax.dot_general(dob, v_ref[0:kn, :], _NT, preferred_element_type=F32)
            ds = p * (dp - jnp.sum(p * dp, axis=1, keepdims=True))
            dsb = ds.astype(BF16)
            dq_ref[...] = (lax.dot_general(dsb, kb, _NN, preferred_element_type=F32) * scale).astype(dq_ref.dtype)
            dk_acc[0:kn, :] += lax.dot_general(dsb, qb, _TN, preferred_element_type=F32) * scale
            dv_acc[0:kn, :] += lax.dot_general(p.astype(BF16), dob, _TN, preferred_element_type=F32)
            if has_decay:
                dck_ref[0, :, 0:kn] -= jnp.sum(ds, axis=0, keepdims=True)

        for i in range(nq):
            pl.when(pl.program_id(1) == i)(functools.partial(block, i))

        @pl.when(pl.program_id(1) == nq - 1)
        def _():
            dk_ref[...] = dk_acc[...].astype(dk_ref.dtype)
            dv_ref[...] = dv_acc[...].astype(dv_ref.dtype)

    in_specs = [
        pl.BlockSpec((tq, dk), lambda h, i: (i, qblk0 + h)),
        pl.BlockSpec((m, dk), lambda h, i: (0, kblk0 + h)),
        pl.BlockSpec((m, dv), lambda h, i: (0, vblk0 + h)),
        pl.BlockSpec((None, tq, dv), lambda h, i: (do_sel, i, h)),
        pl.BlockSpec((1, tq, 1), lambda h, i: (h, i, 0)),
    ]
    args = [q, k, v, do, lse]
    out_shape = [
        jax.ShapeDtypeStruct((m, heads * dk), BF16),
        jax.ShapeDtypeStruct((m, heads * dk), BF16),
        jax.ShapeDtypeStruct((m, heads * dv), BF16),
    ]
    out_specs = [
        pl.BlockSpec((tq, dk), lambda h, i: (i, h)),
        pl.BlockSpec((m, dk), lambda h, i: (0, h)),
        pl.BlockSpec((m, dv), lambda h, i: (0, h)),
    ]
    if has_decay:
        in_specs += [pl.BlockSpec((1, tq, 1), lambda h, i: (h, i, 0)), pl.BlockSpec((1, 1, m), lambda h, i: (h, 0, 0))]
        args += list(decay)
        out_shape.append(jax.ShapeDtypeStruct((heads, 1, m), F32))
        out_specs.append(pl.BlockSpec((1, 1, m), lambda h, i: (h, 0, 0)))
    return pl.pallas_call(
        body,
        name=name,
        out_shape=tuple(out_shape),
        grid=(heads, nq),
        in_specs=in_specs,
        out_specs=tuple(out_specs),
        scratch_shapes=[pltpu.VMEM((m, dk), F32), pltpu.VMEM((m, dv), F32)],
        compiler_params=_cparams("parallel", "arbitrary"),
    )(*args)


def rope_tables(cfg):
    half = ROPE // 2
    inv_freq = 1.0 / (ROPE_THETA ** (jnp.arange(0, ROPE, 2, dtype=F32) / ROPE))
    pos = (jnp.arange(cfg.m, dtype=jnp.int32) - cfg.pad).astype(F32)
    ang = pos[:, None] * inv_freq[None, :]
    cos, sin = jnp.cos(ang), jnp.sin(ang)
    z = jnp.zeros((cfg.m, half), F32)
    zz = jnp.zeros((cfg.m, LANES - ROPE), F32)
    return (
        jnp.concatenate([cos, cos, zz], axis=1),
        jnp.concatenate([-sin, z, zz], axis=1),
        jnp.concatenate([z, sin, zz], axis=1),
    )


def _rope(x, cos, s1, s2):
    return x * cos + pltpu.roll(x, LANES - ROPE // 2, 1) * s1 + pltpu.roll(x, ROPE // 2, 1) * s2


def mla_prep_fwd(cfg, q, kv, proj, tabs, *, name):
    m, h2 = cfg.m, 2 * LANES
    tm = _tile(m, 544, 16)
    kpe_blk = cfg.off_kpe // LANES

    def body(q_ref, kn_ref, kpe_ref, cos_ref, s1_ref, s2_ref, qf_ref, kf_ref):
        cos, s1, s2 = cos_ref[...], s1_ref[...], s2_ref[...]
        qv = q_ref[...]
        qf_ref[:, :LANES] = qv[:, :LANES]
        qf_ref[:, LANES:] = _rope(qv[:, LANES:].astype(F32), cos, s1, s2).astype(qf_ref.dtype)
        kf_ref[:, :LANES] = kn_ref[...]
        kf_ref[:, LANES:] = _rope(kpe_ref[...].astype(F32), cos, s1, s2).astype(kf_ref.dtype)

    tab = pl.BlockSpec((tm, LANES), lambda i, h: (i, 0))
    return pl.pallas_call(
        body,
        name=name,
        out_shape=(jax.ShapeDtypeStruct((m, cfg.heads * h2), BF16), jax.ShapeDtypeStruct((m, cfg.heads * h2), BF16)),
        grid=(m // tm, cfg.heads),
        in_specs=[
            pl.BlockSpec((tm, h2), lambda i, h: (i, h)),
            pl.BlockSpec((tm, LANES), lambda i, h: (i, h)),
            pl.BlockSpec((tm, LANES), lambda i, h: (i, kpe_blk)),
            tab, tab, tab,
        ],
        out_specs=(pl.BlockSpec((tm, h2), lambda i, h: (i, h)), pl.BlockSpec((tm, h2), lambda i, h: (i, h))),
        compiler_params=_cparams("parallel", "parallel"),
    )(q, kv, proj, *tabs)


def mla_prep_bwd(cfg, dqf, dkf, tabs_t, *, name):
    m, h2 = cfg.m, 2 * LANES
    tm = _tile(m, 544, 16)

    def body(dqf_ref, dkf_ref, cos_ref, s1_ref, s2_ref, dq_ref, dkn_ref, dkpe_ref):
        h = pl.program_id(1)
        cos, s1, s2 = cos_ref[...], s1_ref[...], s2_ref[...]
        dqv, dkv = dqf_ref[...], dkf_ref[...]
        dq_ref[:, :LANES] = dqv[:, :LANES]
        dq_ref[:, LANES:] = _rope(dqv[:, LANES:].astype(F32), cos, s1, s2).astype(dq_ref.dtype)
        dkn_ref[...] = dkv[:, :LANES]
        part = _rope(dkv[:, LANES:].astype(F32), cos, s1, s2)

        @pl.when(h == 0)
        def _():
            dkpe_ref[...] = part

        @pl.when(h > 0)
        def _():
            dkpe_ref[...] += part

    tab = pl.BlockSpec((tm, LANES), lambda i, h: (i, 0))
    return pl.pallas_call(
        body,
        name=name,
        out_shape=(
            jax.ShapeDtypeStruct((m, cfg.heads * h2), BF16),
            jax.ShapeDtypeStruct((m, cfg.heads * LANES), BF16),
            jax.ShapeDtypeStruct((m, LANES), F32),
        ),
        grid=(m // tm, cfg.heads),
        in_specs=[pl.BlockSpec((tm, h2), lambda i, h: (i, h)), pl.BlockSpec((tm, h2), lambda i, h: (i, h)), tab, tab, tab],
        out_specs=(
            pl.BlockSpec((tm, h2), lambda i, h: (i, h)),
            pl.BlockSpec((tm, LANES), lambda i, h: (i, h)),
            pl.BlockSpec((tm, LANES), lambda i, h: (i, 0)),
        ),
        compiler_params=_cparams("parallel", "arbitrary"),
    )(dqf, dkf, *tabs_t)


def _conv_parts(b_ref, c_ref, x_ref, w_ref, m):
    b, c, x = b_ref[...].astype(F32), c_ref[...].astype(F32), x_ref[...].astype(F32)
    u = c * x
    row = lax.broadcasted_iota(jnp.int32, (m, 1), 0)
    u1 = jnp.where(row >= 1, pltpu.roll(u, 1, 0), 0.0)
    u2 = jnp.where(row >= 2, pltpu.roll(u, 2, 0), 0.0)
    w0, w1, w2 = w_ref[0:1, :], w_ref[1:2, :], w_ref[2:3, :]
    uc = w0 * u2 + w1 * u1 + w2 * u
    return b, c, x, u, u1, u2, uc, (w0, w1, w2), row


def _conv_specs(cfg, tn):
    m, nb, blk0 = cfg.m, cfg.width // tn, cfg.off_conv // tn
    return [
        pl.BlockSpec((m, tn), lambda j: (0, blk0 + j)),
        pl.BlockSpec((m, tn), lambda j: (0, blk0 + nb + j)),
        pl.BlockSpec((m, tn), lambda j: (0, blk0 + 2 * nb + j)),
        pl.BlockSpec((3, tn), lambda j: (0, j)),
    ]


def conv_fwd(cfg, proj, conv_w, *, name):
    m, tn = cfg.m, LANES

    def body(b_ref, c_ref, x_ref, w_ref, o_ref):
        b, _, _, _, _, _, uc, _, _ = _conv_parts(b_ref, c_ref, x_ref, w_ref, m)
        o_ref[...] = (b * uc).astype(o_ref.dtype)

    return pl.pallas_call(
        body,
        name=name,
        out_shape=jax.ShapeDtypeStruct((m, cfg.width), BF16),
        grid=(cfg.width // tn,),
        in_specs=_conv_specs(cfg, tn),
        out_specs=pl.BlockSpec((m, tn), lambda j: (0, j)),
        compiler_params=_cparams("parallel"),
    )(proj, proj, proj, conv_w)


def conv_bwd(cfg, proj, conv_w, do, do_sel, *, name):
    m, tn = cfg.m, LANES

    def body(b_ref, c_ref, x_ref, w_ref, do_ref, db_ref, dc_ref, dx_ref, dw_ref):
        b, c, x, u, u1, u2, uc, (w0, w1, w2), row = _conv_parts(b_ref, c_ref, x_ref, w_ref, m)
        dob = do_ref[...].astype(F32)
        db_ref[...] = (dob * uc).astype(db_ref.dtype)
        duc = dob * b
        up1 = jnp.where(row <= m - 2, pltpu.roll(duc, m - 1, 0), 0.0)
        up2 = jnp.where(row <= m - 3, pltpu.roll(duc, m - 2, 0), 0.0)
        du = w2 * duc + w1 * up1 + w0 * up2
        dc_ref[...] = (du * x).astype(dc_ref.dtype)
        dx_ref[...] = (du * c).astype(dx_ref.dtype)
        dw_ref[0:1, :] = jnp.sum(duc * u2, axis=0, keepdims=True)
        dw_ref[1:2, :] = jnp.sum(duc * u1, axis=0, keepdims=True)
        dw_ref[2:3, :] = jnp.sum(duc * u, axis=0, keepdims=True)

    act = jax.ShapeDtypeStruct((m, cfg.width), BF16)
    blk = pl.BlockSpec((m, tn), lambda j: (0, j))
    return pl.pallas_call(
        body,
        name=name,
        out_shape=(act, act, act, jax.ShapeDtypeStruct((3, cfg.width), F32)),
        grid=(cfg.width // tn,),
        in_specs=_conv_specs(cfg, tn) + [pl.BlockSpec((None, m, tn), lambda j: (do_sel, 0, j))],
        out_specs=(blk, blk, blk, pl.BlockSpec((3, tn), lambda j: (0, j))),
        compiler_params=_cparams("parallel"),
    )(proj, proj, proj, conv_w, do)


def _tri(lower):
    r = lax.broadcasted_iota(jnp.int32, (LANES, LANES), 0)
    c = lax.broadcasted_iota(jnp.int32, (LANES, LANES), 1)
    return jnp.where((r >= c) if lower else (r <= c), 1.0, 0.0).astype(F32)


def fox_gate_fwd(cfg, fl, b_pad, *, name):
    m = cfg.m
    nblk = m // LANES

    def body(fl_ref, b_ref, c_ref):
        z = fl_ref[...] + b_ref[...]
        logf = jnp.minimum(z, 0.0) - jnp.log(1.0 + jnp.exp(-jnp.abs(z)))
        row = lax.broadcasted_iota(jnp.int32, (m, 1), 0)
        logf = jnp.where(row >= cfg.pad, logf, 0.0)
        tri = _tri(True)
        carry = jnp.zeros((1, LANES), F32)
        for blk in range(nblk):
            cb = jnp.dot(tri, logf[blk * LANES:(blk + 1) * LANES, :], precision=lax.Precision.HIGHEST,
                         preferred_element_type=F32) + carry
            c_ref[blk * LANES:(blk + 1) * LANES, :] = cb
            carry = cb[LANES - 1:LANES, :]

    full = pl.BlockSpec((m, LANES), lambda: (0, 0))
    return pl.pallas_call(
        body,
        name=name,
        out_shape=jax.ShapeDtypeStruct((m, LANES), F32),
        in_specs=[full, pl.BlockSpec((1, LANES), lambda: (0, 0))],
        out_specs=full,
        compiler_params=pltpu.CompilerParams(vmem_limit_bytes=VMEM_LIMIT_BYTES),
    )(fl, b_pad)


def fox_gate_bwd(cfg, fl, b_pad, dc, *, name):
    m = cfg.m
    nblk = m // LANES

    def body(fl_ref, b_ref, dc_ref, dfl_ref, db_ref):
        z = fl_ref[...] + b_ref[...]
        dlogsig = 1.0 / (1.0 + jnp.exp(z))
        row = lax.broadcasted_iota(jnp.int32, (m, 1), 0)
        gate = jnp.where(row >= cfg.pad, dlogsig, 0.0)
        dcv = dc_ref[...]
        tri = _tri(False)
        carry = jnp.zeros((1, LANES), F32)
        db = jnp.zeros((1, LANES), F32)
        for blk in reversed(range(nblk)):
            sl = slice(blk * LANES, (blk + 1) * LANES)
            rb = jnp.dot(tri, dcv[sl, :], precision=lax.Precision.HIGHEST, preferred_element_type=F32) + carry
            carry = rb[0:1, :]
            dfl = rb * gate[sl, :]
            dfl_ref[sl, :] = dfl
            db = db + jnp.sum(dfl, axis=0, keepdims=True)
        db_ref[...] = db

    full = pl.BlockSpec((m, LANES), lambda: (0, 0))
    one = pl.BlockSpec((1, LANES), lambda: (0, 0))
    return pl.pallas_call(
        body,
        name=name,
        out_shape=(jax.ShapeDtypeStruct((m, LANES), F32), jax.ShapeDtypeStruct((1, LANES), F32)),
        in_specs=[full, one, full],
        out_specs=(full, one),
        compiler_params=pltpu.CompilerParams(vmem_limit_bytes=VMEM_LIMIT_BYTES),
    )(fl, b_pad, dc)


def _sigmoid(x):
    return 1.0 / (1.0 + jnp.exp(-x))


def gate_merge_fwd(cfg, y, proj, *, name):
    m, d = cfg.m, cfg.d
    tm, tn = _tile(m, 1088, 16), _tile(d, 512)
    nd = d // tn

    def body(y_ref, g0_ref, g1_ref, g2_ref, o_ref):
        acc = None
        for n, g_ref in enumerate((g0_ref, g1_ref, g2_ref)):
            t = _sigmoid(g_ref[...].astype(F32)) * y_ref[n].astype(F32)
            acc = t if acc is None else acc + t
        o_ref[...] = acc.astype(o_ref.dtype)

    gate = lambda n: pl.BlockSpec((tm, tn), lambda i, j: (i, n * nd + j))
    return pl.pallas_call(
        body,
        name=name,
        out_shape=jax.ShapeDtypeStruct((m, d), BF16),
        grid=(m // tm, nd),
        in_specs=[pl.BlockSpec((3, tm, tn), lambda i, j: (0, i, j)), gate(0), gate(1), gate(2)],
        out_specs=pl.BlockSpec((tm, tn), lambda i, j: (i, j)),
        compiler_params=_cparams("parallel", "parallel"),
    )(y, proj, proj, proj)


def gate_merge_bwd(cfg, dm, y, proj, *, name):
    m, d = cfg.m, cfg.d
    tm, tn = _tile(m, 1088, 16), _tile(d, 512)
    nd = d // tn

    def body(dm_ref, y_ref, g_ref, dy_ref, dg_ref):
        sg = _sigmoid(g_ref[...].astype(F32))
        dmv = dm_ref[...].astype(F32)
        dy_ref[...] = (sg * dmv).astype(dy_ref.dtype)
        dg_ref[...] = (dmv * y_ref[...].astype(F32) * sg * (1.0 - sg)).astype(dg_ref.dtype)

    return pl.pallas_call(
        body,
        name=name,
        out_shape=(jax.ShapeDtypeStruct((3, m, d), BF16), jax.ShapeDtypeStruct((m, 3 * d), BF16)),
        grid=(m // tm, nd, 3),
        in_specs=[
            pl.BlockSpec((tm, tn), lambda i, j, n: (i, j)),
            pl.BlockSpec((None, tm, tn), lambda i, j, n: (n, i, j)),
            pl.BlockSpec((tm, tn), lambda i, j, n: (i, n * nd + j)),
        ],
        out_specs=(
            pl.BlockSpec((None, tm, tn), lambda i, j, n: (n, i, j)),
            pl.BlockSpec((tm, tn), lambda i, j, n: (i, n * nd + j)),
        ),
        compiler_params=_cparams("parallel", "parallel", "parallel"),
    )(dm, y, proj)


def swiglu_fwd(cfg, gu, *, name):
    m, f = cfg.m, cfg.d_ff
    tm, tn = _tile(m, 1088, 16), _tile(f, 512)
    nf = f // tn

    def body(g_ref, u_ref, o_ref):
        g = g_ref[...].astype(F32)
        o_ref[...] = (g * _sigmoid(g) * u_ref[...].astype(F32)).astype(o_ref.dtype)

    return pl.pallas_call(
        body,
        name=name,
        out_shape=jax.ShapeDtypeStruct((m, f), BF16),
        grid=(m // tm, nf),
        in_specs=[pl.BlockSpec((tm, tn), lambda i, j: (i, j)), pl.BlockSpec((tm, tn), lambda i, j: (i, nf + j))],
        out_specs=pl.BlockSpec((tm, tn), lambda i, j: (i, j)),
        compiler_params=_cparams("parallel", "parallel"),
    )(gu, gu)


def swiglu_bwd(cfg, dact, gu, *, name):
    m, f = cfg.m, cfg.d_ff
    tm, tn = _tile(m, 1088, 16), _tile(f, 512)
    nf = f // tn

    def body(da_ref, g_ref, u_ref, o_ref):
        j = pl.program_id(1)
        g, u, da = g_ref[...].astype(F32), u_ref[...].astype(F32), da_ref[...].astype(F32)
        sg = _sigmoid(g)
        dg = da * u * sg * (1.0 + g * (1.0 - sg))
        du = da * g * sg
        o_ref[...] = jnp.where(j < nf, dg, du).astype(o_ref.dtype)

    return pl.pallas_call(
        body,
        name=name,
        out_shape=jax.ShapeDtypeStruct((m, 2 * f), BF16),
        grid=(m // tm, 2 * nf),
        in_specs=[
            pl.BlockSpec((tm, tn), lambda i, j: (i, j % nf)),
            pl.BlockSpec((tm, tn), lambda i, j: (i, j % nf)),
            pl.BlockSpec((tm, tn), lambda i, j: (i, nf + j % nf)),
        ],
        out_specs=pl.BlockSpec((tm, tn), lambda i, j: (i, j)),
        compiler_params=_cparams("parallel", "parallel"),
    )(dact, gu, gu)


def loss_head(cfg, h, target, *, name):
    m, d = cfg.m, cfg.d
    assert cfg.pad + cfg.n_meta == LANES
    tm = LANES
    inv_d = 1.0 / d

    def body(h_ref, t_ref, dh_ref, loss_ref):
        i = pl.program_id(0)

        @pl.when(i == 0)
        def _():
            dh_ref[...] = jnp.zeros_like(dh_ref)
            loss_ref[...] = jnp.zeros_like(loss_ref)

        @pl.when(i > 0)
        def _():
            err = h_ref[...] - t_ref[...]
            dh_ref[...] = err * inv_d
            loss_ref[...] += 0.5 * inv_d * jnp.sum(err * err)

    return pl.pallas_call(
        body,
        name=name,
        out_shape=(jax.ShapeDtypeStruct((m, d), F32), jax.ShapeDtypeStruct((8, LANES), F32)),
        grid=(m // tm,),
        in_specs=[pl.BlockSpec((tm, d), lambda i: (i, 0)), pl.BlockSpec((tm, d), lambda i: (jnp.maximum(i - 1, 0), 0))],
        out_specs=(pl.BlockSpec((tm, d), lambda i: (i, 0)), pl.BlockSpec((8, LANES), lambda i: (0, 0))),
        compiler_params=_cparams("arbitrary"),
    )(h, target)


def adamw(w, g, m_, v_, *, name):
    r, c = w.shape
    c_pad = -(-c // LANES) * LANES
    tr = r
    if r % 8 == 0:
        tr = _tile(r, max(8, (3 << 19) // (4 * c_pad) // 8 * 8), 8)
    bc1 = 1.0 - ADAM_B1 ** ADAM_STEP
    bc2 = 1.0 - ADAM_B2 ** ADAM_STEP

    def body(w_ref, g_ref, m_ref, v_ref, d_ref, nm_ref, nv_ref):
        gv = g_ref[...]
        nm = ADAM_B1 * m_ref[...] + (1.0 - ADAM_B1) * gv
        nv = ADAM_B2 * v_ref[...] + (1.0 - ADAM_B2) * (gv * gv)
        d_ref[...] = -ADAM_LR * ((nm / bc1) / (jnp.sqrt(nv / bc2) + ADAM_EPS) + ADAM_WD * w_ref[...])
        nm_ref[...] = nm
        nv_ref[...] = nv

    blk = pl.BlockSpec((tr, c), lambda i: (i, 0))
    shp = jax.ShapeDtypeStruct((r, c), F32)
    return pl.pallas_call(
        body,
        name=name,
        out_shape=(shp, shp, shp),
        grid=(r // tr,),
        in_specs=[blk, blk, blk, blk],
        out_specs=(blk, blk, blk),
        compiler_params=_cparams("parallel"),
    )(w, g, m_, v_)


def adamw_halves(w, g_own, g_other, core, m_, v_, *, first_layer=0, prev=None, name):
    _, r, c = w.shape
    nl = g_own.shape[0]
    r2 = r // 2
    c_pad = -(-c // LANES) * LANES
    tr = _tile(r2, max(8, (3 << 19) // (4 * c_pad) // 8 * 8), 8)
    nr = r2 // tr
    bc1 = 1.0 - ADAM_B1 ** ADAM_STEP
    bc2 = 1.0 - ADAM_B2 ** ADAM_STEP

    def body(core_ref, w_ref, go_ref, gr_ref, m_ref, v_ref, *rest):
        g_ref, d_ref, nm_ref, nv_ref = rest[-4:]
        gv = jnp.where(pl.program_id(2) == core_ref[0], go_ref[...], gr_ref[...])[:, :c]
        nm = ADAM_B1 * m_ref[...] + (1.0 - ADAM_B1) * gv
        nv = ADAM_B2 * v_ref[...] + (1.0 - ADAM_B2) * (gv * gv)
        d_ref[...] = -ADAM_LR * ((nm / bc1) / (jnp.sqrt(nv / bc2) + ADAM_EPS) + ADAM_WD * w_ref[...])
        g_ref[...] = gv
        nm_ref[...] = nm
        nv_ref[...] = nv

    full = pl.BlockSpec((None, tr, c), lambda l, i, hf, core_ref: (first_layer + l, hf * nr + i, 0))
    half = pl.BlockSpec((None, tr, g_own.shape[2]), lambda l, i, hf, core_ref: (l, i, 0))
    shp = jax.ShapeDtypeStruct(w.shape, F32)
    kept = list(prev or ())
    return pl.pallas_call(
        body,
        name=name,
        out_shape=(shp, shp, shp, shp),
        grid_spec=pltpu.PrefetchScalarGridSpec(
            num_scalar_prefetch=1,
            grid=(nl, nr, 2),
            in_specs=[full, half, half, full, full] + [pl.BlockSpec(memory_space=pl.ANY)] * len(kept),
            out_specs=(full, full, full, full),
        ),
        input_output_aliases={6 + k: k for k in range(len(kept))},
        compiler_params=_cparams("parallel", "parallel", "arbitrary"),
    )(core, w, g_own, g_other, m_, v_, *kept)


_HBM = pl.BlockSpec(memory_space=pltpu.HBM)


def _place():
    x, y, c = lax.axis_index("x"), lax.axis_index("y"), lax.axis_index("c")
    flips = [(1 - x, y), (x, 1 - y), (1 - x, 1 - y)]
    return x, y, c, flips


_SEM = pl.BlockSpec(memory_space=pltpu.SEMAPHORE)
_EFFECT = pltpu.SideEffectType.DATAFLOW_SIDE_EFFECTING


def _gather_plan(halves):
    def plan(src_refs, land_refs, arrival):
        x, y, c, flips = _place()
        mine = 2 * x + y
        out = []
        for w, h in enumerate(halves):
            for fx, fy in flips:
                slot = (2 * fx + fy) if arrival else mine
                out.append((src_refs[w].at[pl.ds(c * h, h), :], land_refs[w].at[slot, pl.ds(c * h, h), :], (fx, fy, c)))
        return out
    return plan


def _exchange_plan(nw):
    def plan(src_refs, land_refs, arrival):
        _, _, c, flips = _place()
        return [(src_refs[w].at[2 * fx + fy], land_refs[w].at[k], (fx, fy, c)) for w in range(nw) for k, (fx, fy) in enumerate(flips)]
    return plan


def _swap_plan(halves):
    def plan(src_refs, land_refs, arrival):
        x, y, c, _ = _place()
        return [(src_refs[w].at[:, pl.ds((1 - c) * h, h), :], land_refs[w], (x, y, 1 - c)) for w, h in enumerate(halves)]
    return plan


def copies_start(srcs, land_shapes, plan, n_copies, *, name):
    lands = [lax.empty(s, a.dtype) for s, a in zip(land_shapes, srcs)]
    n_in = len(srcs) + len(lands)

    def body(*refs):
        src_refs, land_refs = refs[:len(srcs)], refs[len(srcs):n_in]
        send_sems, recv_sems, token = refs[n_in], refs[n_in + 1], refs[-1]
        for i, (src, dst, to) in enumerate(plan(src_refs, land_refs, False)):
            pltpu.make_async_remote_copy(src_ref=src, dst_ref=dst, send_sem=send_sems.at[i], recv_sem=recv_sems.at[i],
                                         device_id=to, device_id_type=MESH).start()
        token[...] = jnp.zeros_like(token)

    operands = list(srcs) + lands
    out = pl.pallas_call(
        body,
        name=name,
        out_shape=(pltpu.SemaphoreType.DMA((n_copies,)), pltpu.SemaphoreType.DMA((n_copies,)),
                   *[pltpu.HBM(a.shape, a.dtype) for a in operands], jax.ShapeDtypeStruct((8, LANES), F32)),
        in_specs=[_HBM] * n_in,
        out_specs=(_SEM, _SEM, *[_HBM] * n_in, pl.BlockSpec(memory_space=pltpu.VMEM)),
        input_output_aliases={i: 2 + i for i in range(n_in)},
        compiler_params=pltpu.CompilerParams(has_side_effects=_EFFECT),
    )(*[pltpu.with_memory_space_constraint(a, pltpu.HBM) for a in operands])
    return out[0], out[1], list(out[2:2 + len(srcs)]), list(out[2 + len(srcs):2 + n_in]), out[-1]


def copies_wait(send_sems, recv_sems, srcs, lands, plan, after, *, name):
    n_in = len(srcs) + len(lands)

    def body(*refs):
        src_refs, land_refs = refs[:len(srcs)], refs[len(srcs):n_in]
        send_ref, recv_ref, token = refs[n_in], refs[n_in + 1], refs[-1]
        token[...] = jnp.zeros_like(token)
        for i, (src, dst, to) in enumerate(plan(src_refs, land_refs, True)):
            copy = pltpu.make_async_remote_copy(src_ref=src, dst_ref=dst, send_sem=send_ref.at[i], recv_sem=recv_ref.at[i],
                                                device_id=to, device_id_type=MESH)
            copy.wait_send()
            copy.wait_recv()

    operands = list(srcs) + list(lands)
    out = pl.pallas_call(
        body,
        name=name,
        out_shape=(*[pltpu.HBM(a.shape, a.dtype) for a in operands], jax.ShapeDtypeStruct((8, LANES), F32)),
        in_specs=[_HBM] * n_in + [_SEM, _SEM, pl.BlockSpec(memory_space=pl.ANY)],
        out_specs=(*[_HBM] * n_in, pl.BlockSpec(memory_space=pltpu.VMEM)),
        input_output_aliases={i: i for i in range(n_in)},
        compiler_params=pltpu.CompilerParams(has_side_effects=_EFFECT),
    )(*operands, send_sems, recv_sems, after)
    return list(out[:len(srcs)]), list(out[len(srcs):n_in]), out[-1]


def forward_halves(lands, *, name):
    nw = len(lands)
    halves = [a.shape[1] // 2 for a in lands]

    def body(*refs):
        ins, outs = refs[:nw], refs[nw:2 * nw]
        send_sems, recv_sems = refs[2 * nw:]
        x, y, c, flips = _place()
        copies = []
        for w, h in enumerate(halves):
            for k, (fx, fy) in enumerate(flips):
                rows = (2 * fx + fy, pl.ds(c * h, h), slice(None))
                copies.append(pltpu.make_async_remote_copy(src_ref=ins[w].at[rows], dst_ref=outs[w].at[rows], send_sem=send_sems.at[3 * w + k],
                                                           recv_sem=recv_sems.at[3 * w + k], device_id=(x, y, 1 - c), device_id_type=MESH))
        for cp in copies:
            cp.start()
        for cp in copies:
            cp.wait()

    return pl.pallas_call(
        body,
        name=name,
        out_shape=tuple(jax.ShapeDtypeStruct(a.shape, a.dtype) for a in lands),
        in_specs=[_HBM] * nw,
        out_specs=tuple([_HBM] * nw),
        input_output_aliases={w: w for w in range(nw)},
        scratch_shapes=[pltpu.SemaphoreType.DMA((3 * nw,)), pltpu.SemaphoreType.DMA((3 * nw,))],
    )(*lands)


def share_halves(sums, *, name):
    nw = len(sums)

    def body(*refs):
        ins, outs = refs[:nw], refs[nw:2 * nw]
        send_sems, recv_sems = refs[2 * nw:]
        x, y, c, _ = _place()
        copies = [
            pltpu.make_async_remote_copy(src_ref=ins[w], dst_ref=outs[w], send_sem=send_sems.at[w], recv_sem=recv_sems.at[w],
                                         device_id=(x, y, 1 - c), device_id_type=MESH)
            for w in range(nw)
        ]
        for cp in copies:
            cp.start()
        for cp in copies:
            cp.wait()

    return pl.pallas_call(
        body,
        name=name,
        out_shape=tuple(jax.ShapeDtypeStruct(s.shape, s.dtype) for s in sums),
        in_specs=[_HBM] * nw,
        out_specs=tuple([_HBM] * nw),
        scratch_shapes=[pltpu.SemaphoreType.DMA((nw,)), pltpu.SemaphoreType.DMA((nw,))],
    )(*sums)


def gather_blocks(block, *, reduce, name):
    rows, cols = block.shape

    def body(x_ref, out_ref, *rest):
        if reduce:
            buf_ref, send_sems, recv_sems = rest
        else:
            send_sems, recv_sems = rest
            buf_ref = out_ref
        x, y, c, flips = _place()
        me, sibling = (x, y, c), (x, y, 1 - c)

        def slot(px, py, pc):
            return buf_ref.at[4 * px + 2 * py + pc]

        def copy(k, blk, to, src=None):
            return pltpu.make_async_remote_copy(src_ref=slot(*blk) if src is None else src, dst_ref=slot(*blk),
                                                send_sem=send_sems.at[k], recv_sem=recv_sems.at[k], device_id=to,
                                                device_id_type=MESH)

        buf_ref[4 * x + 2 * y + c] = x_ref[...]
        first = [copy(0, me, sibling, src=x_ref)]
        first += [copy(1 + j, me, (*chip, c), src=x_ref) for j, chip in enumerate(flips)]
        for cp in first:
            cp.start()
        passed = [copy(4 + j, (*chip, c), sibling) for j, chip in enumerate(flips)]
        for j, chip in enumerate(flips):
            copy(1 + j, (*chip, c), me).wait_recv()
            passed[j].start()
        copy(0, sibling, me).wait_recv()
        for j, chip in enumerate(flips):
            copy(4 + j, (*chip, 1 - c), me).wait_recv()
        for cp in first + passed:
            cp.wait_send()
        if reduce:
            acc = buf_ref[0]
            for dev in range(1, N_DEV):
                acc = acc + buf_ref[dev]
            out_ref[...] = acc

    vmem = pl.BlockSpec(memory_space=pltpu.VMEM)
    sems = [pltpu.SemaphoreType.DMA((7,)), pltpu.SemaphoreType.DMA((7,))]
    if reduce:
        out_shape = jax.ShapeDtypeStruct((rows, cols), block.dtype)
        scratch = [pltpu.VMEM((N_DEV, rows, cols), block.dtype)] + sems
    else:
        out_shape = jax.ShapeDtypeStruct((N_DEV, rows, cols), block.dtype)
        scratch = sems
    return pl.pallas_call(
        body,
        name=name,
        out_shape=out_shape,
        in_specs=[vmem],
        out_specs=vmem,
        scratch_shapes=scratch,
        compiler_params=pltpu.CompilerParams(vmem_limit_bytes=VMEM_LIMIT_BYTES),
    )(block)


def add_own_half(grad, recv, core, *, name):
    _, r2, cols = recv.shape
    tr = _tile(r2, max(16, (1 << 20) // (2 * cols) // 16 * 16), 16)
    nr = r2 // tr

    def body(core_ref, g_ref, r_ref, o_ref):
        o_ref[...] = (g_ref[...].astype(F32) + r_ref[...].astype(F32)).astype(o_ref.dtype)

    return pl.pallas_call(
        body,
        name=name,
        out_shape=jax.ShapeDtypeStruct(recv.shape, BF16),
        grid_spec=pltpu.PrefetchScalarGridSpec(
            num_scalar_prefetch=1,
            grid=(N_CHIPS, nr),
            in_specs=[
                pl.BlockSpec((None, tr, cols), lambda k, i, core_ref: (k, core_ref[0] * nr + i, 0)),
                pl.BlockSpec((None, tr, cols), lambda k, i, core_ref: (k, i, 0)),
            ],
            out_specs=pl.BlockSpec((None, tr, cols), lambda k, i, core_ref: (k, i, 0)),
        ),
        compiler_params=_cparams("parallel", "parallel"),
    )(core, grad, recv)


def sum_chips(part, recv, chip, sums, layer, *, name):
    _, r2, cols = part.shape
    tr = _tile(r2, max(16, (1 << 20) // (2 * cols) // 16 * 16), 16)

    def body(chip_ref, p_ref, r_ref, sums_ref, o_ref):
        acc = p_ref[...].astype(F32)
        for k in range(3):
            acc = acc + r_ref[k].astype(F32)
        o_ref[...] = acc

    return pl.pallas_call(
        body,
        name=name,
        out_shape=jax.ShapeDtypeStruct(sums.shape, F32),
        grid_spec=pltpu.PrefetchScalarGridSpec(
            num_scalar_prefetch=1,
            grid=(r2 // tr,),
            in_specs=[
                pl.BlockSpec((None, tr, cols), lambda i, chip_ref: (chip_ref[0], i, 0)),
                pl.BlockSpec((3, tr, cols), lambda i, chip_ref: (0, i, 0)),
                pl.BlockSpec(memory_space=pl.ANY),
            ],
            out_specs=pl.BlockSpec((None, tr, cols), lambda i, chip_ref: (layer, i, 0)),
        ),
        input_output_aliases={3: 0},
        compiler_params=_cparams("parallel"),
    )(chip, part, recv, sums)


WEIGHTS = ("w_in", "w_uq", "w_ukv", "w_branch", "w_out", "w_ffn_in", "w_ffn_out")
GAINS = ("g_mix_pre", "g_mix_post", "g_ffn_pre", "g_ffn_post")


def layer_fwd(cfg, h, w, s, tabs, tag):
    m, d, hd = cfg.m, cfg.d, cfg.heads
    fox_blk = cfg.off_fox // LANES
    hn = rmsnorm_fwd(h, s["g_mix_pre"], BF16, name=f"norm_mix_pre{tag}")
    proj = matmul(hn, w["w_in"], "nn", BF16, tm=m, tn=_tile(cfg.d_inp, 512), tk=d, name=f"proj{tag}")
    fl = matmul(hn, w["w_in"][:, cfg.off_fl:cfg.off_fl + LANES], "nn", F32, tm=m, tn=LANES, tk=d, name=f"proj_forget{tag}")
    cqn = rmsnorm_fwd(proj, s["g_q_lat"], BF16, width=cfg.q_rank, col_blk=cfg.off_cq // cfg.q_rank, name=f"norm_q{tag}")
    ckvn = rmsnorm_fwd(proj, s["g_kv_lat"], BF16, width=cfg.kv_rank, col_blk=cfg.off_ckv // cfg.kv_rank, name=f"norm_kv{tag}")
    q = matmul(cqn, w["w_uq"], "nn", BF16, tm=m, tn=_tile(2 * cfg.width, 512), tk=cfg.q_rank, name=f"up_q{tag}")
    kv = matmul(ckvn, w["w_ukv"], "nn", BF16, tm=m, tn=_tile(2 * cfg.width, 512), tk=cfg.kv_rank, name=f"up_kv{tag}")
    qf, kf = mla_prep_fwd(cfg, q, kv, proj, tabs[0], name=f"mla_prep{tag}")
    o_a, lse_a = attn_fwd(qf, kf, kv, heads=hd, dk=2 * LANES, dv=LANES, qblk0=0, kblk0=0, vblk0=hd,
                          scale=(LANES + ROPE) ** -0.5, pad=cfg.pad, name=f"mla_attn{tag}")
    o_b = conv_fwd(cfg, proj, s["conv_w"], name=f"conv{tag}")
    b_pad = jnp.pad(s["b_forget"], (0, LANES - hd)).reshape(1, LANES)
    cum = fox_gate_fwd(cfg, fl, b_pad, name=f"fox_gate{tag}")
    cum_t = cum[:, :hd].T
    decay = (cum_t[:, :, None], cum_t[:, None, :])
    o_c, lse_c = attn_fwd(proj, proj, proj, heads=hd, dk=LANES, dv=LANES, qblk0=fox_blk, kblk0=fox_blk + hd,
                          vblk0=fox_blk + 2 * hd, scale=LANES ** -0.5, pad=cfg.pad, decay=decay, name=f"fox_attn{tag}")
    o = jnp.stack([o_a, o_b, o_c])
    y = matmul(o, w["w_branch"], "nn", BF16, tm=m, tn=_tile(d, 512), tk=cfg.width, name=f"branch{tag}")
    merged = gate_merge_fwd(cfg, y, proj, name=f"merge{tag}")
    mix = matmul(merged, w["w_out"], "nn", F32, tm=m, tn=_tile(d, 256), tk=d, name=f"out_proj{tag}")
    h_mid = rmsnorm_fwd(mix, s["g_mix_post"], F32, res=h, name=f"norm_mix_post{tag}")
    if hasattr(w, "land_ffn"):
        s = w.land_ffn(h_mid, s)
    hn2 = rmsnorm_fwd(h_mid, s["g_ffn_pre"], BF16, name=f"norm_ffn_pre{tag}")
    gu = matmul(hn2, w["w_ffn_in"], "nn", BF16, tm=m, tn=_tile(2 * cfg.d_ff, 512), tk=d, name=f"ffn_in{tag}")
    act = swiglu_fwd(cfg, gu, name=f"swiglu{tag}")
    f = matmul(act, w["w_ffn_out"], "nn", F32, tm=m, tn=_tile(d, 512), tk=_tile(cfg.d_ff, 1408), name=f"ffn_out{tag}")
    h_next = rmsnorm_fwd(f, s["g_ffn_post"], F32, res=h_mid, name=f"norm_ffn_post{tag}")
    saved = dict(h=h, hn=hn, proj=proj, fl=fl, cqn=cqn, ckvn=ckvn, kv=kv, qf=qf, kf=kf, lse_a=lse_a,
                 b_pad=b_pad, decay=decay, lse_c=lse_c, o=o, y=y, merged=merged, mix=mix, h_mid=h_mid,
                 hn2=hn2, gu=gu, act=act, f=f)
    return h_next, s, saved


def layer_bwd(cfg, dh, w, s, r, tabs, tag, grads_done):
    m, d, hd = cfg.m, cfg.d, cfg.heads
    fox_blk = cfg.off_fox // LANES
    tk_m = m
    df, dg4 = rmsnorm_bwd(r["f"], s["g_ffn_post"], dh, BF16, name=f"norm_ffn_post_bwd{tag}")
    dact = matmul(df, w["w_ffn_out"], "nt", BF16, tm=m, tn=_tile(cfg.d_ff, 512), tk=d, name=f"ffn_out_dx{tag}")
    dw_fo = matmul(r["act"], df, "tn", BF16, tm=_tile(cfg.d_ff, 512), tn=_tile(d, 1024), tk=tk_m, name=f"ffn_out_dw{tag}")
    dgu = swiglu_bwd(cfg, dact, r["gu"], name=f"swiglu_bwd{tag}")
    dhn2 = matmul(dgu, w["w_ffn_in"], "nt", F32, tm=m, tn=_tile(d, 512), tk=_tile(2 * cfg.d_ff, 1408), name=f"ffn_in_dx{tag}")
    dw_fi = matmul(r["hn2"], dgu, "tn", BF16, tm=_tile(d, 1024), tn=_tile(2 * cfg.d_ff // N_CHIPS, 1408), tk=tk_m, chip_cols=True,
                   name=f"ffn_in_dw{tag}")
    token = grads_done(dict(w_ffn_in=dw_fi, w_ffn_out=dw_fo))
    if token is not None:
        s = {**s, "g_ffn_pre": s["g_ffn_pre"] + token[0, 0]}
    dh_mid, dg3 = rmsnorm_bwd(r["h_mid"], s["g_ffn_pre"], dhn2, F32, dres=dh, name=f"norm_ffn_pre_bwd{tag}")
    dmix, dg2 = rmsnorm_bwd(r["mix"], s["g_mix_post"], dh_mid, BF16, name=f"norm_mix_post_bwd{tag}")
    dmerged = matmul(dmix, w["w_out"], "nt", BF16, tm=m, tn=_tile(d, 512), tk=d, name=f"out_proj_dx{tag}")
    dw_out = matmul(r["merged"], dmix, "tn", BF16, tm=_tile(d, 1024), tn=_tile(d, 512), tk=tk_m, name=f"out_proj_dw{tag}")
    dy, dgl = gate_merge_bwd(cfg, dmerged, r["y"], r["proj"], name=f"merge_bwd{tag}")
    do = matmul(dy, w["w_branch"], "nt", BF16, tm=m, tn=_tile(cfg.width, 512), tk=d, name=f"branch_dx{tag}")
    dw_br = matmul(r["o"], dy, "tn", BF16, tm=_tile(cfg.width, 1024), tn=_tile(d // N_CHIPS, 512), tk=tk_m, chip_cols=True,
                   name=f"branch_dw{tag}")
    dqf, dkf, dv_a = attn_bwd(r["qf"], r["kf"], r["kv"], do, 0, r["lse_a"], heads=hd, dk=2 * LANES, dv=LANES,
                              qblk0=0, kblk0=0, vblk0=hd, scale=(LANES + ROPE) ** -0.5, pad=cfg.pad, name=f"mla_attn_bwd{tag}")
    dq, dkn, dkpe = mla_prep_bwd(cfg, dqf, dkf, tabs[1], name=f"mla_prep_bwd{tag}")
    dkv = jnp.concatenate([dkn, dv_a], axis=1)
    dcqn = matmul(dq, w["w_uq"], "nt", F32, tm=m, tn=cfg.q_rank, tk=2 * cfg.width, name=f"up_q_dx{tag}")
    dw_uq = matmul(r["cqn"], dq, "tn", BF16, tm=cfg.q_rank, tn=_tile(2 * cfg.width, 512), tk=tk_m, name=f"up_q_dw{tag}")
    dckvn = matmul(dkv, w["w_ukv"], "nt", F32, tm=m, tn=cfg.kv_rank, tk=2 * cfg.width, name=f"up_kv_dx{tag}")
    dw_ukv = matmul(r["ckvn"], dkv, "tn", BF16, tm=cfg.kv_rank, tn=_tile(2 * cfg.width, 512), tk=tk_m, name=f"up_kv_dw{tag}")
    dcq, dgq = rmsnorm_bwd(r["proj"], s["g_q_lat"], dcqn, BF16, width=cfg.q_rank, col_blk=cfg.off_cq // cfg.q_rank,
                           name=f"norm_q_bwd{tag}")
    dckv, dgkv = rmsnorm_bwd(r["proj"], s["g_kv_lat"], dckvn, BF16, width=cfg.kv_rank, col_blk=cfg.off_ckv // cfg.kv_rank,
                             name=f"norm_kv_bwd{tag}")
    dcb, dcc, dcx, dconv_w = conv_bwd(cfg, r["proj"], s["conv_w"], do, 1, name=f"conv_bwd{tag}")
    dfq, dfk, dfv, dck = attn_bwd(r["proj"], r["proj"], r["proj"], do, 2, r["lse_c"], heads=hd, dk=LANES, dv=LANES,
                                  qblk0=fox_blk, kblk0=fox_blk + hd, vblk0=fox_blk + 2 * hd, scale=LANES ** -0.5,
                                  pad=cfg.pad, decay=r["decay"], name=f"fox_attn_bwd{tag}")
    dc = jnp.pad(dck[:, 0, :].T, ((0, 0), (0, LANES - hd)))
    dfl, dbf = fox_gate_bwd(cfg, r["fl"], r["b_pad"], dc, name=f"fox_gate_bwd{tag}")
    tail = jnp.zeros((m, cfg.d_inp - cfg.off_fl - LANES), BF16)
    dproj = jnp.concatenate([dgl, dcq, dckv, dcb, dcc, dcx, dfq, dfk, dfv, dkpe.astype(BF16), dfl.astype(BF16), tail], axis=1)
    dhn = matmul(dproj, w["w_in"], "nt", F32, tm=m, tn=_tile(d, 512), tk=_tile(cfg.d_inp, 1536), name=f"proj_dx{tag}")
    dw_in = matmul(r["hn"], dproj, "tn", BF16, tm=_tile(d, 1024), tn=_tile(cfg.d_inp, 512), tk=tk_m, name=f"proj_dw{tag}")
    dh_in, dg1 = rmsnorm_bwd(r["h"], s["g_mix_pre"], dhn, F32, dres=dh_mid, name=f"norm_mix_pre_bwd{tag}")
    token = grads_done(dict(w_in=dw_in, w_uq=dw_uq, w_ukv=dw_ukv, w_branch=dw_br, w_out=dw_out))
    dsmall = dict(g_mix_pre=dg1[0], g_mix_post=dg2[0], g_ffn_pre=dg3[0], g_ffn_post=dg4[0], g_q_lat=dgq[0], g_kv_lat=dgkv[0],
                  b_forget=dbf[0, :hd], conv_w=dconv_w)
    return dh_in, dsmall, token


def local_step(cfg, x, target, meta, layer_params, grads_done):
    h = jnp.concatenate([jnp.zeros((cfg.pad, cfg.d), F32), meta, x], axis=0)
    cos, s1, s2 = rope_tables(cfg)
    tabs = ((cos, s1, s2), (cos, -s1, -s2))
    saved = []
    for l in range(cfg.depth):
        w, s = layer_params(l, h)
        h, s, r = layer_fwd(cfg, h, w, s, tabs, f"_{l}")
        saved.append((w, s, r))
    dh, loss = loss_head(cfg, h, target, name="loss_head")
    dsmalls, token = [None] * cfg.depth, None
    for l in reversed(range(cfg.depth)):
        w, s, r = saved[l]
        if token is not None:
            s = {**s, "g_ffn_post": s["g_ffn_post"] + token[0, 0]}
        dh, dsmalls[l], token = layer_bwd(cfg, dh, w, s, r, tabs, f"_{l}", functools.partial(grads_done, l))
    first = cfg.pad + cfg.n_meta
    return loss, dh[first:], dh[cfg.pad:first], dsmalls


def _cols_from_chips(g):
    return jnp.transpose(g, (1, 0, 2)).reshape(g.shape[1], N_CHIPS * g.shape[2])


def _cols_to_chips(w):
    r, c = w.shape
    return jnp.transpose(w.reshape(r, N_CHIPS, c // N_CHIPS), (1, 0, 2))


def _packed_segments(cfg):
    nat = [0] + _cumsum(cfg.nat_splits)
    w = cfg.width
    order = [(10, 0), (0, cfg.off_cq), (1, cfg.off_ckv), (3, cfg.off_conv), (4, cfg.off_conv + w), (5, cfg.off_conv + 2 * w),
             (6, cfg.off_fox), (7, cfg.off_fox + w), (8, cfg.off_fox + 2 * w), (2, cfg.off_kpe), (9, cfg.off_fl)]
    return [(pk, nat[i], cfg.nat_splits[i]) for i, pk in order]


def _chip_cols(cfg):
    n = cfg.d_in // N_CHIPS
    return n, -(-n // LANES) * LANES


def _lane_pieces(cfg, to_packed):
    n, n_pad = _chip_cols(cfg)
    tiles = [[] for _ in range(cfg.d_inp // LANES if to_packed else N_CHIPS * n_pad // LANES)]
    for pk, nat, width in _packed_segments(cfg):
        g = nat
        while g < nat + width:
            k, a = divmod(g, n)
            dst = (pk + g - nat) if to_packed else (k * n_pad + a)
            run = min(nat + width - g, n - a, LANES - dst % LANES)
            src = (k, a) if to_packed else (0, pk + g - nat)
            tiles[dst // LANES].append((dst % LANES, run, *src))
            g += run
    return tiles


def _fill_tiles(pieces, read, write, rows):
    lane = lax.broadcasted_iota(jnp.int32, (rows, LANES), 1)
    for t, parts in enumerate(pieces):
        tile = jnp.zeros((rows, LANES), F32)
        for dl, run, blk, col in parts:
            w0 = col // LANES * LANES
            off = col - w0
            span = LANES if off + run <= LANES else 2 * LANES
            win = read(blk, w0, span)
            shift = (dl - off) % span
            if shift:
                win = pltpu.roll(win, shift, 1)
            win = win[:, :LANES]
            tile = win if (dl == 0 and run == LANES) else jnp.where((lane >= dl) & (lane < dl + run), win, tile)
        write(t, tile)


def pack_w_in_blocks(cfg, lands, own, chip, *, name):
    d = cfg.d
    n, n_pad = _chip_cols(cfg)
    tr = _tile(d, 256, 16)
    pieces = _lane_pieces(cfg, True)

    def body(chip_ref, land_ref, own_ref, o_ref):
        def read(k, w0, span):
            theirs = land_ref[k, :, w0:w0 + span]
            return jnp.where(chip_ref[0] == k, own_ref[:, w0:w0 + span], theirs).astype(F32)

        def write(t, tile):
            o_ref[:, t * LANES:(t + 1) * LANES] = tile.astype(o_ref.dtype)

        _fill_tiles(pieces, read, write, tr)

    return pl.pallas_call(
        body,
        name=name,
        out_shape=jax.ShapeDtypeStruct((d, cfg.d_inp), BF16),
        grid_spec=pltpu.PrefetchScalarGridSpec(
            num_scalar_prefetch=1,
            grid=(d // tr,),
            in_specs=[pl.BlockSpec((N_CHIPS, tr, n_pad), lambda i, chip_ref: (0, i, 0)),
                      pl.BlockSpec((tr, n_pad), lambda i, chip_ref: (i, 0))],
            out_specs=pl.BlockSpec((tr, cfg.d_inp), lambda i, chip_ref: (i, 0)),
        ),
        compiler_params=_cparams("parallel"),
    )(chip, lands, own)


def unpack_w_in_blocks(cfg, dw, *, name):
    d = cfg.d
    n, n_pad = _chip_cols(cfg)
    tr = _tile(d, 256, 16)
    per_blk = n_pad // LANES
    pieces = _lane_pieces(cfg, False)

    def body(dw_ref, o_ref):
        def read(_, w0, span):
            return dw_ref[:, w0:w0 + span].astype(F32)

        def write(t, tile):
            k, i = divmod(t, per_blk)
            o_ref[k, :, i * LANES:(i + 1) * LANES] = tile.astype(o_ref.dtype)

        _fill_tiles(pieces, read, write, tr)

    return pl.pallas_call(
        body,
        name=name,
        out_shape=jax.ShapeDtypeStruct((N_CHIPS, d, n_pad), BF16),
        grid=(d // tr,),
        in_specs=[pl.BlockSpec((tr, cfg.d_inp), lambda i: (i, 0))],
        out_specs=pl.BlockSpec((N_CHIPS, tr, n_pad), lambda i: (0, i, 0)),
        compiler_params=_cparams("parallel"),
    )(dw)


def full_weights(cfg, g, w_in=None):
    make = dict(
        w_uq=lambda a: pack_w_uq(cfg, _cols_from_chips(a)),
        w_ukv=lambda a: pack_w_ukv(cfg, _cols_from_chips(a)),
        w_branch=lambda a: _cols_from_chips(a).reshape(3, cfg.width, cfg.d),
        w_out=lambda a: a.reshape(cfg.d, cfg.d),
        w_ffn_in=_cols_from_chips,
        w_ffn_out=lambda a: a.reshape(cfg.d_ff, cfg.d),
    )
    out = {n: make[n](a) for n, a in g.items()}
    if w_in is not None:
        out["w_in"] = w_in
    return out


def chip_grads(cfg, dw, tag):
    make = dict(
        w_in=lambda a: unpack_w_in_blocks(cfg, a, name=f"unpack_w_in{tag}"),
        w_uq=lambda a: _cols_to_chips(unpack_w_uq(cfg, a)),
        w_ukv=lambda a: _cols_to_chips(unpack_w_ukv(cfg, a)),
        w_branch=lambda a: a.reshape(N_CHIPS, 3 * cfg.width, cfg.d // N_CHIPS),
        w_out=lambda a: a.reshape(N_CHIPS, cfg.d // N_CHIPS, cfg.d),
        w_ffn_in=lambda a: a,
        w_ffn_out=lambda a: a.reshape(N_CHIPS, cfg.d_ff // N_CHIPS, cfg.d),
    )
    return {n: make[n](a) for n, a in dw.items()}


def _small_rows(cfg):
    return dict(g_mix_pre=cfg.d // LANES, g_mix_post=cfg.d // LANES, g_ffn_pre=cfg.d // LANES, g_ffn_post=cfg.d // LANES,
                g_q_lat=cfg.q_rank // LANES, g_kv_lat=cfg.kv_rank // LANES, b_forget=1, conv_w=3 * cfg.width // LANES)


def pack_small(cfg, loss, dmeta, dsmalls):
    parts = [loss[0:1, :], dmeta.reshape(-1, LANES)]
    for ds in dsmalls:
        for k in _small_rows(cfg):
            v = ds[k]
            if k == "b_forget":
                v = jnp.pad(v, (0, LANES - cfg.heads))
            parts.append(v.reshape(-1, LANES))
    rows = sum(p.shape[0] for p in parts)
    parts.append(jnp.zeros((-rows % 8, LANES), F32))
    return jnp.concatenate(parts, axis=0)


def unpack_small(cfg, block):
    loss = block[0, 0]
    n = cfg.n_meta * cfg.d // LANES
    dmeta = block[1:1 + n].reshape(cfg.n_meta, cfg.d)
    at = 1 + n
    out = []
    for _ in range(cfg.depth):
        ds = {}
        for k, rows in _small_rows(cfg).items():
            v = block[at:at + rows]
            at += rows
            if k == "b_forget":
                v = v[0, :cfg.heads]
            elif k == "conv_w":
                v = v.reshape(3, cfg.width)
            else:
                v = v.reshape(-1)
            ds[k] = v
        out.append(ds)
    return loss, dmeta, out


def kernel(x, meta, w_in, b_forget, g_q_lat, g_kv_lat, w_uq, w_ukv, conv_w, w_branch, w_out, w_ffn_in, w_ffn_out, g_mix_pre, g_mix_post, g_ffn_pre, g_ffn_post, loss_target, m_meta, m_w_in, m_b_forget, m_g_q_lat, m_g_kv_lat, m_w_uq, m_w_ukv, m_conv_w, m_w_branch, m_w_out, m_w_ffn_in, m_w_ffn_out, m_g_mix_pre, m_g_mix_post, m_g_ffn_pre, m_g_ffn_post, v_meta, v_w_in, v_b_forget, v_g_q_lat, v_g_kv_lat, v_w_uq, v_w_ukv, v_conv_w, v_w_branch, v_w_out, v_w_ffn_in, v_w_ffn_out, v_g_mix_pre, v_g_mix_post, v_g_ffn_pre, v_g_ffn_post):
    cfg = CFG
    names = ("meta", "w_in", "b_forget", "g_q_lat", "g_kv_lat", "w_uq", "w_ukv", "conv_w", "w_branch", "w_out", "w_ffn_in",
             "w_ffn_out", "g_mix_pre", "g_mix_post", "g_ffn_pre", "g_ffn_post")
    params = dict(zip(names, (meta, w_in, b_forget, g_q_lat, g_kv_lat, w_uq, w_ukv, conv_w, w_branch, w_out, w_ffn_in, w_ffn_out,
                              g_mix_pre, g_mix_post, g_ffn_pre, g_ffn_post)))
    mom1 = dict(zip(names, (m_meta, m_w_in, m_b_forget, m_g_q_lat, m_g_kv_lat, m_w_uq, m_w_ukv, m_conv_w, m_w_branch, m_w_out,
                            m_w_ffn_in, m_w_ffn_out, m_g_mix_pre, m_g_mix_post, m_g_ffn_pre, m_g_ffn_post)))
    mom2 = dict(zip(names, (v_meta, v_w_in, v_b_forget, v_g_q_lat, v_g_kv_lat, v_w_uq, v_w_ukv, v_conv_w, v_w_branch, v_w_out,
                            v_w_ffn_in, v_w_ffn_out, v_g_mix_pre, v_g_mix_post, v_g_ffn_pre, v_g_ffn_post)))
    xi, yi, ci = lax.axis_index("x"), lax.axis_index("y"), lax.axis_index("c")
    chip = 2 * xi + yi
    chip_arr = jnp.reshape(chip, (1,)).astype(jnp.int32)
    core_arr = jnp.reshape(ci, (1,)).astype(jnp.int32)

    meta_all = gather_blocks(meta, reduce=False, name="gather_meta")[0::2]
    meta_full = jnp.transpose(meta_all, (1, 0, 2)).reshape(cfg.n_meta, cfg.d)
    conv_rows = conv_w.reshape(cfg.depth * 3, cfg.width // N_CHIPS)
    conv_all = gather_blocks(conv_rows, reduce=False, name="gather_conv_w")[0::2]
    conv_full = jnp.transpose(conv_all, (1, 0, 2)).reshape(cfg.depth, 3, cfg.width)

    def shard2d(name, l, after=None):
        w = params[name][l]
        if after is not None:
            w = w + after
        w = w.reshape(-1, w.shape[-1]).astype(BF16)
        if name == "w_in":
            w = jnp.pad(w, ((0, 0), (0, _chip_cols(cfg)[1] - w.shape[1])))
        return w

    is_mine = (jnp.arange(N_CHIPS) == chip)[:, None, None]
    shard_shape = {n: shard2d(n, 0).shape for n in WEIGHTS}
    groups = (("w_in", "w_uq", "w_ukv", "w_branch", "w_out"), ("w_ffn_in", "w_ffn_out"))
    gather_plans = [_gather_plan([shard_shape[n][0] // 2 for n in g]) for g in groups]

    def gather_start(l, after):
        started = []
        for gi, g in enumerate(groups):
            started.append(copies_start([shard2d(n, l, after) for n in g], [(N_CHIPS,) + shard_shape[n] for n in g],
                                        gather_plans[gi], 3 * len(g), name=f"gather_start_{gi}_{l}"))
            after = started[-1][4][0, 0]
        return started

    def land(l, gi, started, after):
        send_sems, recv_sems, own, lands, _ = started
        own, lands, landed = copies_wait(send_sems, recv_sems, own, lands, gather_plans[gi], after, name=f"gather_wait_{gi}_{l}")
        return own, forward_halves(lands, name=f"forward_halves_{gi}_{l}"), landed

    in_flight = {0: gather_start(0, None)}

    class LayerWeights(dict):
        def __init__(self, l, h):
            own, lands, _ = land(l, 0, in_flight[l][0], h)
            got = {n: jnp.where(is_mine, o[None], g) for n, o, g in zip(groups[0][1:], own[1:], lands[1:])}
            super().__init__(full_weights(cfg, got, pack_w_in_blocks(cfg, lands[0], own[0], chip_arr, name=f"pack_w_in_{l}")))
            self.layer = l

        def land_ffn(self, after, s):
            l = self.layer
            own, lands, landed = land(l, 1, in_flight.pop(l)[1], after)
            self.update(full_weights(cfg, {n: jnp.where(is_mine, o[None], g) for n, o, g in zip(groups[1], own, lands)}))
            if l + 1 == cfg.depth:
                return s
            in_flight[l + 1] = gather_start(l + 1, landed[0, 0])
            return {**s, "g_ffn_pre": s["g_ffn_pre"] + in_flight[l + 1][1][4][0, 0]}

    def layer_params(l, h):
        s = dict(g_mix_pre=g_mix_pre[l], g_mix_post=g_mix_post[l], g_ffn_pre=g_ffn_pre[l], g_ffn_post=g_ffn_post[l],
                 g_q_lat=g_q_lat[l], g_kv_lat=g_kv_lat[l], b_forget=b_forget[l], conv_w=conv_full[l])
        s["g_mix_pre"] = s["g_mix_pre"] + in_flight[l][1][4][0, 0]
        return LayerWeights(l, h), s

    half_shape = {n: (shard_shape[n][0] // 2, shard_shape[n][1]) for n in WEIGHTS}
    sums_upper = {n: jnp.zeros((cfg.depth - 1,) + half_shape[n], F32) for n in WEIGHTS}
    sums_first = {n: jnp.zeros((1,) + half_shape[n], F32) for n in WEIGHTS}
    swapping, exchanging = [], []

    def finish_exchange(after):
        l, names_, (send_sems, recv_sems, parts, lands, _) = exchanging.pop(0)
        parts, others, _ = copies_wait(send_sems, recv_sems, parts, lands, _exchange_plan(len(names_)), after,
                                       name=f"exchange_wait_{names_[0]}_{l}")
        for n, p, o in zip(names_, parts, others):
            if l == 0:
                sums_first[n] = sum_chips(p, o, chip_arr, sums_first[n], 0, name=f"sum_chips_{n}_{l}")
            else:
                sums_upper[n] = sum_chips(p, o, chip_arr, sums_upper[n], l - 1, name=f"sum_chips_{n}_{l}")

    def finish_swap(after):
        l, names_, (send_sems, recv_sems, mine, lands, _) = swapping.pop(0)
        plan = _swap_plan([g.shape[1] // 2 for g in mine])
        mine, theirs, _ = copies_wait(send_sems, recv_sems, mine, lands, plan, after, name=f"swap_wait_{names_[0]}_{l}")
        parts = [add_own_half(g, t, core_arr, name=f"add_own_half_{n}_{l}") for n, g, t in zip(names_, mine, theirs)]
        started = copies_start(parts, [(3,) + p.shape[1:] for p in parts], _exchange_plan(len(names_)), 3 * len(names_),
                               name=f"exchange_start_{names_[0]}_{l}")
        exchanging.append((l, names_, started))
        return started[4]

    def grads_done(l, dws):
        names_ = [n for n in WEIGHTS if n in dws]
        send = chip_grads(cfg, dws, f"_{l}")
        mine = [send[n] for n in names_]
        halves = [g.shape[1] // 2 for g in mine]
        started = copies_start(mine, [(N_CHIPS, h, g.shape[2]) for g, h in zip(mine, halves)], _swap_plan(halves), len(mine),
                               name=f"swap_start_{names_[0]}_{l}")
        token = started[4]
        if swapping:
            token = finish_swap(token)
            if len(exchanging) > 1:
                finish_exchange(token)
        swapping.append((l, names_, started))
        return token

    loss, grad_x, dmeta, dsmalls = local_step(cfg, x[0], loss_target[0], meta_full, layer_params, grads_done)
    last = finish_swap(swapping[0][2][4])
    while exchanging[0][0] > 0:
        finish_exchange(last)

    def update(own, first_layer, prev, tag):
        other = dict(zip(WEIGHTS, share_halves([own[n] for n in WEIGHTS], name=f"share_halves_{tag}")))
        out = {}
        for n in WEIGHTS:
            three_d = lambda a: a.reshape(cfg.depth, -1, params[n].shape[-1])
            out[n] = adamw_halves(three_d(params[n]), own[n], other[n], core_arr, three_d(mom1[n]), three_d(mom2[n]),
                                  first_layer=first_layer, prev=prev and prev[n], name=f"adamw_{tag}_{n}")
        return out

    sums_upper["w_uq"] = sums_upper["w_uq"] + last[0, 0]
    upper = update(sums_upper, 1, None, "upper")
    busy = sum(upper[n][1][1, 0, :LANES] for n in WEIGHTS)
    while exchanging:
        finish_exchange(busy)
    done = update(sums_first, 0, upper, "first")
    grad, delta, new_m, new_v = ({n: done[n][k].reshape(params[n].shape) for n in WEIGHTS} for k in range(4))

    total = gather_blocks(pack_small(cfg, loss, dmeta, dsmalls), reduce=True, name="reduce_small")
    loss_sum, dmeta_sum, dsmall_sum = unpack_small(cfg, total)
    for k in _small_rows(cfg):
        grad[k] = jnp.stack([ds[k] for ds in dsmall_sum])
    grad["conv_w"] = lax.dynamic_slice_in_dim(grad["conv_w"], chip * (cfg.width // N_CHIPS), cfg.width // N_CHIPS, axis=2)
    grad["meta"] = lax.dynamic_slice_in_dim(dmeta_sum, chip * (cfg.d // N_CHIPS), cfg.d // N_CHIPS, axis=1)

    for n in names:
        if n in WEIGHTS:
            continue
        shp = params[n].shape
        two_d = lambda a: a.reshape(-1, shp[-1])
        dl, nm, nv = adamw(two_d(params[n]), two_d(grad[n]), two_d(mom1[n]), two_d(mom2[n]), name=f"adamw_{n}")
        delta[n], new_m[n], new_v[n] = dl.reshape(shp), nm.reshape(shp), nv.reshape(shp)

    return (loss_sum, grad_x[None], *[grad[n] for n in names], *[delta[n] for n in names], *[new_m[n] for n in names],
            *[new_v[n] for n in names])
```

```python
import functools
from typing import NamedTuple

import jax
import jax.numpy as jnp
from jax import lax
from jax.experimental import pallas as pl
from jax.experimental.pallas import tpu as pltpu

F32 = jnp.float32
BF16 = jnp.bfloat16
MESH = pl.DeviceIdType.MESH

EPS = 1e-6
NEG_INF = -1e30
ROPE_THETA = 10000.0
LANES = 128
ROPE = 64
N_CHIPS = 4
N_DEV = 8

ADAM_LR = 0.001
ADAM_B1 = 0.9
ADAM_B2 = 0.999
ADAM_EPS = 1e-08
ADAM_WD = 0.01
ADAM_STEP = 10

VMEM_LIMIT_BYTES = 48 * 1024 * 1024


class Cfg(NamedTuple):
    d: int = 2048
    seq: int = 2048
    depth: int = 4
    n_meta: int = 16
    heads: int = 8
    q_rank: int = 512
    kv_rank: int = 512
    d_ff: int = 5632

    @property
    def width(self):
        return self.heads * LANES

    @property
    def pad(self):
        return (-(self.n_meta + self.seq)) % LANES

    @property
    def m(self):
        return self.pad + self.n_meta + self.seq

    @property
    def nat_splits(self):
        w = self.width
        return (self.q_rank, self.kv_rank, ROPE, w, w, w, w, w, w, self.heads, 3 * self.d)

    @property
    def d_in(self):
        return sum(self.nat_splits)

    @property
    def off_cq(self):
        return 3 * self.d

    @property
    def off_ckv(self):
        return self.off_cq + self.q_rank

    @property
    def off_conv(self):
        return self.off_ckv + self.kv_rank

    @property
    def off_fox(self):
        return self.off_conv + 3 * self.width

    @property
    def off_kpe(self):
        return self.off_fox + 3 * self.width

    @property
    def off_fl(self):
        return self.off_kpe + LANES

    @property
    def d_inp(self):
        return -(-(self.off_fl + LANES) // 512) * 512


CFG = Cfg()


def _tile(n, target, mult=LANES):
    best = None
    t = mult
    while t <= min(n, target):
        if n % t == 0:
            best = t
        t += mult
    return best or n


def _cparams(*sem):
    return pltpu.CompilerParams(dimension_semantics=sem, vmem_limit_bytes=VMEM_LIMIT_BYTES)


def _cumsum(xs):
    out, s = [], 0
    for v in xs:
        s += v
        out.append(s)
    return out


def pack_w_uq(cfg, w):
    r = w.shape[0]
    w3 = w.reshape(r, cfg.heads, LANES + ROPE)
    w3 = jnp.pad(w3, ((0, 0), (0, 0), (0, LANES - ROPE)))
    return w3.reshape(r, cfg.heads * 2 * LANES)


def unpack_w_uq(cfg, wp):
    r = wp.shape[0]
    return wp.reshape(r, cfg.heads, 2 * LANES)[:, :, : LANES + ROPE].reshape(r, cfg.heads * (LANES + ROPE))


def pack_w_ukv(cfg, w):
    r = w.shape[0]
    w4 = w.reshape(r, cfg.heads, 2, LANES)
    return jnp.transpose(w4, (0, 2, 1, 3)).reshape(r, 2 * cfg.heads * LANES)


def unpack_w_ukv(cfg, wp):
    r = wp.shape[0]
    w4 = wp.reshape(r, 2, cfg.heads, LANES)
    return jnp.transpose(w4, (0, 2, 1, 3)).reshape(r, 2 * cfg.heads * LANES)


_DIMS = {
    "nn": (((1,), (0,)), ((), ())),
    "nt": (((1,), (1,)), ((), ())),
    "tn": (((0,), (0,)), ((), ())),
}


def matmul(a, b, mode, out_dtype, *, tm, tn, tk, name, chip_cols=False):
    batched = a.ndim == 3
    if mode == "nn":
        (m, kc), n = a.shape[-2:], b.shape[-1]
        a_blk, a_idx = (tm, tk), lambda i, j, k: (i, k)
        b_blk, b_idx = (tk, tn), lambda i, j, k: (k, j)
    elif mode == "nt":
        (m, kc), n = a.shape[-2:], b.shape[-2]
        a_blk, a_idx = (tm, tk), lambda i, j, k: (i, k)
        b_blk, b_idx = (tn, tk), lambda i, j, k: (j, k)
    else:
        (kc, m), n = a.shape[-2:], b.shape[-1]
        a_blk, a_idx = (tk, tm), lambda i, j, k: (k, i)
        b_blk, b_idx = (tk, tn), lambda i, j, k: (k, j)
    assert m % tm == 0 and n % tn == 0 and kc % tk == 0, (name, m, n, kc, tm, tn, tk)
    nk = kc // tk
    dims = _DIMS[mode]
    o_blk, o_idx = (tm, tn), lambda i, j, k: (i, j)
    grid = (m // tm, n // tn, nk)
    per = n // N_CHIPS // tn
    assert not chip_cols or n // N_CHIPS % tn == 0
    if batched:
        nb = a.shape[0]
        grid = (nb,) + grid
        wrap = lambda f: (lambda bb, i, j, k: (bb,) + f(i, j, k))
        a_blk, b_blk, o_blk = (None,) + a_blk, (None,) + b_blk, (None,) + o_blk
        a_idx, b_idx, o_idx = wrap(a_idx), wrap(b_idx), wrap(o_idx)
        out_shape = (nb, m, n)
        if chip_cols:
            o_blk, o_idx = (None,) + o_blk, lambda bb, i, j, k: (j // per, bb, i, j % per)
            out_shape = (N_CHIPS, nb, m, n // N_CHIPS)
        sem = ("parallel", "parallel", "parallel", "arbitrary")
    else:
        out_shape = (m, n)
        if chip_cols:
            o_blk, o_idx = (None,) + o_blk, lambda i, j, k: (j // per, i, j % per)
            out_shape = (N_CHIPS, m, n // N_CHIPS)
        sem = ("parallel", "parallel", "arbitrary")
    k_axis = len(grid) - 1

    def body(a_ref, b_ref, o_ref, *scratch):
        prod = lax.dot_general(a_ref[...], b_ref[...], dims, preferred_element_type=F32)
        if nk == 1:
            o_ref[...] = prod.astype(o_ref.dtype)
        else:
            acc_ref = scratch[0]
            k = pl.program_id(k_axis)

            @pl.when(k == 0)
            def _():
                acc_ref[...] = prod

            @pl.when(k > 0)
            def _():
                acc_ref[...] += prod

            @pl.when(k == nk - 1)
            def _():
                o_ref[...] = acc_ref[...].astype(o_ref.dtype)

    return pl.pallas_call(
        body,
        name=name,
        out_shape=jax.ShapeDtypeStruct(out_shape, out_dtype),
        grid=grid,
        in_specs=[pl.BlockSpec(a_blk, a_idx), pl.BlockSpec(b_blk, b_idx)],
        out_specs=pl.BlockSpec(o_blk, o_idx),
        scratch_shapes=[] if nk == 1 else [pltpu.VMEM((tm, tn), F32)],
        compiler_params=_cparams(*sem),
    )(a, b)


def _row_tile(m):
    return _tile(m, 272, 16)


def rmsnorm_fwd(x, g, out_dtype, *, name, width=None, col_blk=0, res=None):
    m = x.shape[0]
    n = width or x.shape[1]
    tm = _row_tile(m)
    has_res = res is not None

    def body(x_ref, g_ref, *rest):
        o_ref = rest[-1]
        xf = x_ref[...].astype(F32)
        r = lax.rsqrt(jnp.mean(xf * xf, axis=-1, keepdims=True) + EPS)
        y = xf * r * g_ref[...]
        if has_res:
            y = rest[0][...] + y
        o_ref[...] = y.astype(o_ref.dtype)

    in_specs = [pl.BlockSpec((tm, n), lambda i: (i, col_blk)), pl.BlockSpec((1, n), lambda i: (0, 0))]
    args = [x, g.reshape(1, n)]
    if has_res:
        in_specs.append(pl.BlockSpec((tm, n), lambda i: (i, 0)))
        args.append(res)
    return pl.pallas_call(
        body,
        name=name,
        out_shape=jax.ShapeDtypeStruct((m, n), out_dtype),
        grid=(m // tm,),
        in_specs=in_specs,
        out_specs=pl.BlockSpec((tm, n), lambda i: (i, 0)),
        compiler_params=_cparams("parallel"),
    )(*args)


def rmsnorm_bwd(x, g, dy, out_dtype, *, name, width=None, col_blk=0, dres=None):
    m = x.shape[0]
    n = width or x.shape[1]
    tm = _row_tile(m)
    has_res = dres is not None

    def body(x_ref, g_ref, dy_ref, *rest):
        dx_ref, dg_ref = rest[-2:]
        i = pl.program_id(0)
        xf = x_ref[...].astype(F32)
        r = lax.rsqrt(jnp.mean(xf * xf, axis=-1, keepdims=True) + EPS)
        xhat = xf * r
        dyf = dy_ref[...].astype(F32)
        dxh = dyf * g_ref[...]
        dx = r * (dxh - xhat * jnp.mean(dxh * xhat, axis=-1, keepdims=True))
        if has_res:
            dx = dx + rest[0][...]
        dx_ref[...] = dx.astype(dx_ref.dtype)
        part = jnp.sum(dyf * xhat, axis=0, keepdims=True)

        @pl.when(i == 0)
        def _():
            dg_ref[...] = part

        @pl.when(i > 0)
        def _():
            dg_ref[...] += part

    in_specs = [
        pl.BlockSpec((tm, n), lambda i: (i, col_blk)),
        pl.BlockSpec((1, n), lambda i: (0, 0)),
        pl.BlockSpec((tm, n), lambda i: (i, 0)),
    ]
    args = [x, g.reshape(1, n), dy]
    if has_res:
        in_specs.append(pl.BlockSpec((tm, n), lambda i: (i, 0)))
        args.append(dres)
    return pl.pallas_call(
        body,
        name=name,
        out_shape=(jax.ShapeDtypeStruct((m, n), out_dtype), jax.ShapeDtypeStruct((1, n), F32)),
        grid=(m // tm,),
        in_specs=in_specs,
        out_specs=(pl.BlockSpec((tm, n), lambda i: (i, 0)), pl.BlockSpec((1, n), lambda i: (0, 0))),
        compiler_params=_cparams("arbitrary"),
    )(*args)


_NT = (((1,), (1,)), ((), ()))
_NN = (((1,), (0,)), ((), ()))
_TN = (((0,), (0,)), ((), ()))


def _attn_scores(q, k, scale, decay_refs, i, tq, kn, pad):
    s = lax.dot_general(q, k, _NT, preferred_element_type=F32) * scale
    if decay_refs is not None:
        cq_ref, ck_ref = decay_refs
        s = s + (cq_ref[0] - ck_ref[0][:, :kn])
    t_idx = i * tq + lax.broadcasted_iota(jnp.int32, (tq, 1), 0)
    s_idx = lax.broadcasted_iota(jnp.int32, (1, kn), 1)
    mask = (s_idx <= t_idx) & (s_idx >= pad)
    return s, mask, t_idx


def _keys_needed(i, tq, m):
    return min(m, -(-((i + 1) * tq) // LANES) * LANES)


def attn_fwd(q, k, v, *, heads, dk, dv, qblk0, kblk0, vblk0, scale, pad, slot, branches=None, decay=None, name):
    m = q.shape[0]
    tq = _row_tile(m)
    has_decay = decay is not None

    def body(q_ref, k_ref, v_ref, *rest):
        o_ref, lse_ref = rest[-2:]
        decay_refs = rest[:2] if has_decay else None

        def block(i):
            kn = _keys_needed(i, tq, m)
            s, mask, t_idx = _attn_scores(q_ref[...], k_ref[0:kn, :], scale, decay_refs, i, tq, kn, pad)
            s = jnp.where(mask, s, NEG_INF)
            mx = jnp.max(s, axis=1, keepdims=True)
            p = jnp.exp(s - mx)
            l = jnp.sum(p, axis=1, keepdims=True)
            o = lax.dot_general(p.astype(BF16), v_ref[0:kn, :], _NN, preferred_element_type=F32) / l
            o_ref[...] = jnp.where(t_idx >= pad, o, 0.0).astype(o_ref.dtype)
            lse_ref[0] = mx + jnp.log(l)

        for i in range(m // tq):
            pl.when(pl.program_id(1) == i)(functools.partial(block, i))

    in_specs = [
        pl.BlockSpec((tq, dk), lambda h, i: (i, qblk0 + h)),
        pl.BlockSpec((m, dk), lambda h, i: (0, kblk0 + h)),
        pl.BlockSpec((m, dv), lambda h, i: (0, vblk0 + h)),
    ]
    args = [q, k, v]
    if has_decay:
        in_specs += [pl.BlockSpec((1, tq, 1), lambda h, i: (h, i, 0)), pl.BlockSpec((1, 1, m), lambda h, i: (h, 0, 0))]
        args += list(decay)
    aliases = {}
    if branches is not None:
        aliases = {len(args): 0}
        in_specs.append(pl.BlockSpec(memory_space=pl.ANY))
        args.append(branches)
    return pl.pallas_call(
        body,
        name=name,
        out_shape=(jax.ShapeDtypeStruct((3, m, heads * dv), BF16), jax.ShapeDtypeStruct((heads, m, 1), F32)),
        grid=(heads, m // tq),
        in_specs=in_specs,
        out_specs=(pl.BlockSpec((None, tq, dv), lambda h, i: (slot, i, h)), pl.BlockSpec((1, tq, 1), lambda h, i: (h, i, 0))),
        input_output_aliases=aliases,
        compiler_params=_cparams("parallel", "parallel"),
    )(*args)


def attn_bwd(q, k, v, do, do_sel, lse, *, heads, dk, dv, qblk0, kblk0, vblk0, scale, pad, decay=None, name):
    m = q.shape[0]
    tq = _row_tile(m)
    nq = m // tq
    has_decay = decay is not None

    def body(q_ref, k_ref, v_ref, do_ref, lse_ref, *rest):
        if has_decay:
            cq_ref, ck_ref, dq_ref, dk_ref, dv_ref, dck_ref, dk_acc, dv_acc = rest
            decay_refs = (cq_ref, ck_ref)
        else:
            dq_ref, dk_ref, dv_ref, dk_acc, dv_acc = rest
            decay_refs = None
        @pl.when(pl.program_id(1) == 0)
        def _():
            dk_acc[...] = jnp.zeros_like(dk_acc)
            dv_acc[...] = jnp.zeros_like(dv_acc)
            if has_decay:
                dck_ref[...] = jnp.zeros_like(dck_ref)

        def block(i):
            kn = _keys_needed(i, tq, m)
            qb, kb, dob = q_ref[...], k_ref[0:kn, :], do_ref[...]
            s, mask, _ = _attn_scores(qb, kb, scale, decay_refs, i, tq, kn, pad)
            p = jnp.where(mask, jnp.exp(s - lse_ref[0]), 0.0)
            dp = lax.dot_general(dob, v_ref[0:kn, :], _NT, preferred_element_type=F32)
            ds = p * (dp - jnp.sum(p * dp, axis=1, keepdims=True))
            dsb = ds.astype(BF16)
            dq_ref[...] = (lax.dot_general(dsb, kb, _NN, preferred_element_type=F32) * scale).astype(dq_ref.dtype)
            dk_acc[0:kn, :] += lax.dot_general(dsb, qb, _TN, preferred_element_type=F32) * scale
            dv_acc[0:kn, :] += lax.dot_general(p.astype(BF16), dob, _TN, preferred_element_type=F32)
            if has_decay:
                dck_ref[0, :, 0:kn] -= jnp.sum(ds, axis=0, keepdims=True)

        for i in range(nq):
            pl.when(pl.program_id(1) == i)(functools.partial(block, i))

        @pl.when(pl.program_id(1) == nq - 1)
        def _():
            dk_ref[...] = dk_acc[...].astype(dk_ref.dtype)
            dv_ref[...] = dv_acc[...].astype(dv_ref.dtype)

    in_specs = [
        pl.BlockSpec((tq, dk), lambda h, i: (i, qblk0 + h)),
        pl.BlockSpec((m, dk), lambda h, i: (0, kblk0 + h)),
        pl.BlockSpec((m, dv), lambda h, i: (0, vblk0 + h)),
        pl.BlockSpec((None, tq, dv), lambda h, i: (do_sel, i, h)),
        pl.BlockSpec((1, tq, 1), lambda h, i: (h, i, 0)),
    ]
    args = [q, k, v, do, lse]
    out_shape = [
        jax.ShapeDtypeStruct((m, heads * dk), BF16),
        jax.ShapeDtypeStruct((m, heads * dk), BF16),
        jax.ShapeDtypeStruct((m, heads * dv), BF16),
    ]
    out_specs = [
        pl.BlockSpec((tq, dk), lambda h, i: (i, h)),
        pl.BlockSpec((m, dk), lambda h, i: (0, h)),
        pl.BlockSpec((m, dv), lambda h, i: (0, h)),
    ]
    if has_decay:
        in_specs += [pl.BlockSpec((1, tq, 1), lambda h, i: (h, i, 0)), pl.BlockSpec((1, 1, m), lambda h, i: (h, 0, 0))]
        args += list(decay)
        out_shape.append(jax.ShapeDtypeStruct((heads, 1, m), F32))
        out_specs.append(pl.BlockSpec((1, 1, m), lambda h, i: (h, 0, 0)))
    return pl.pallas_call(
        body,
        name=name,
        out_shape=tuple(out_shape),
        grid=(heads, nq),
        in_specs=in_specs,
        out_specs=tuple(out_specs),
        scratch_shapes=[pltpu.VMEM((m, dk), F32), pltpu.VMEM((m, dv), F32)],
        compiler_params=_cparams("parallel", "arbitrary"),
    )(*args)


def rope_tables(cfg):
    half = ROPE // 2
    inv_freq = 1.0 / (ROPE_THETA ** (jnp.arange(0, ROPE, 2, dtype=F32) / ROPE))
    pos = (jnp.arange(cfg.m, dtype=jnp.int32) - cfg.pad).astype(F32)
    ang = pos[:, None] * inv_freq[None, :]
    cos, sin = jnp.cos(ang), jnp.sin(ang)
    z = jnp.zeros((cfg.m, half), F32)
    zz = jnp.zeros((cfg.m, LANES - ROPE), F32)
    return (
        jnp.concatenate([cos, cos, zz], axis=1),
        jnp.concatenate([-sin, z, zz], axis=1),
        jnp.concatenate([z, sin, zz], axis=1),
    )


def _rope(x, cos, s1, s2):
    return x * cos + pltpu.roll(x, LANES - ROPE // 2, 1) * s1 + pltpu.roll(x, ROPE // 2, 1) * s2


def mla_prep_fwd(cfg, q, kv, proj, tabs, *, name):
    m, h2 = cfg.m, 2 * LANES
    tm = _tile(m, 544, 16)
    kpe_blk = cfg.off_kpe // LANES

    def body(q_ref, kn_ref, kpe_ref, cos_ref, s1_ref, s2_ref, qf_ref, kf_ref):
        cos, s1, s2 = cos_ref[...], s1_ref[...], s2_ref[...]
        qv = q_ref[...]
        qf_ref[:, :LANES] = qv[:, :LANES]
        qf_ref[:, LANES:] = _rope(qv[:, LANES:].astype(F32), cos, s1, s2).astype(qf_ref.dtype)
        kf_ref[:, :LANES] = kn_ref[...]
        kf_ref[:, LANES:] = _rope(kpe_ref[...].astype(F32), cos, s1, s2).astype(kf_ref.dtype)

    tab = pl.BlockSpec((tm, LANES), lambda i, h: (i, 0))
    return pl.pallas_call(
        body,
        name=name,
        out_shape=(jax.ShapeDtypeStruct((m, cfg.heads * h2), BF16), jax.ShapeDtypeStruct((m, cfg.heads * h2), BF16)),
        grid=(m // tm, cfg.heads),
        in_specs=[
            pl.BlockSpec((tm, h2), lambda i, h: (i, h)),
            pl.BlockSpec((tm, LANES), lambda i, h: (i, h)),
            pl.BlockSpec((tm, LANES), lambda i, h: (i, kpe_blk)),
            tab, tab, tab,
        ],
        out_specs=(pl.BlockSpec((tm, h2), lambda i, h: (i, h)), pl.BlockSpec((tm, h2), lambda i, h: (i, h))),
        compiler_params=_cparams("parallel", "parallel"),
    )(q, kv, proj, *tabs)


def mla_prep_bwd(cfg, dqf, dkf, tabs_t, *, name):
    m, h2 = cfg.m, 2 * LANES
    tm = _tile(m, 544, 16)

    def body(dqf_ref, dkf_ref, cos_ref, s1_ref, s2_ref, dq_ref, dkn_ref, dkpe_ref):
        h = pl.program_id(1)
        cos, s1, s2 = cos_ref[...], s1_ref[...], s2_ref[...]
        dqv, dkv = dqf_ref[...], dkf_ref[...]
        dq_ref[:, :LANES] = dqv[:, :LANES]
        dq_ref[:, LANES:] = _rope(dqv[:, LANES:].astype(F32), cos, s1, s2).astype(dq_ref.dtype)
        dkn_ref[...] = dkv[:, :LANES]
        part = _rope(dkv[:, LANES:].astype(F32), cos, s1, s2)

        @pl.when(h == 0)
        def _():
            dkpe_ref[...] = part

        @pl.when(h > 0)
        def _():
            dkpe_ref[...] += part

    tab = pl.BlockSpec((tm, LANES), lambda i, h: (i, 0))
    return pl.pallas_call(
        body,
        name=name,
        out_shape=(
            jax.ShapeDtypeStruct((m, cfg.heads * h2), BF16),
            jax.ShapeDtypeStruct((m, cfg.heads * LANES), BF16),
            jax.ShapeDtypeStruct((m, LANES), F32),
        ),
        grid=(m // tm, cfg.heads),
        in_specs=[pl.BlockSpec((tm, h2), lambda i, h: (i, h)), pl.BlockSpec((tm, h2), lambda i, h: (i, h)), tab, tab, tab],
        out_specs=(
            pl.BlockSpec((tm, h2), lambda i, h: (i, h)),
            pl.BlockSpec((tm, LANES), lambda i, h: (i, h)),
            pl.BlockSpec((tm, LANES), lambda i, h: (i, 0)),
        ),
        compiler_params=_cparams("parallel", "arbitrary"),
    )(dqf, dkf, *tabs_t)


def _conv_parts(b_ref, c_ref, x_ref, w_ref, m):
    b, c, x = b_ref[...].astype(F32), c_ref[...].astype(F32), x_ref[...].astype(F32)
    u = c * x
    row = lax.broadcasted_iota(jnp.int32, (m, 1), 0)
    u1 = jnp.where(row >= 1, pltpu.roll(u, 1, 0), 0.0)
    u2 = jnp.where(row >= 2, pltpu.roll(u, 2, 0), 0.0)
    w0, w1, w2 = w_ref[0:1, :], w_ref[1:2, :], w_ref[2:3, :]
    uc = w0 * u2 + w1 * u1 + w2 * u
    return b, c, x, u, u1, u2, uc, (w0, w1, w2), row


def _conv_specs(cfg, tn):
    m, nb, blk0 = cfg.m, cfg.width // tn, cfg.off_conv // tn
    return [
        pl.BlockSpec((m, tn), lambda j: (0, blk0 + j)),
        pl.BlockSpec((m, tn), lambda j: (0, blk0 + nb + j)),
        pl.BlockSpec((m, tn), lambda j: (0, blk0 + 2 * nb + j)),
        pl.BlockSpec((3, tn), lambda j: (0, j)),
    ]


def conv_fwd(cfg, proj, conv_w, branches, slot, *, name):
    m, tn = cfg.m, LANES

    def body(b_ref, c_ref, x_ref, w_ref, kept_ref, o_ref):
        b, _, _, _, _, _, uc, _, _ = _conv_parts(b_ref, c_ref, x_ref, w_ref, m)
        o_ref[...] = (b * uc).astype(o_ref.dtype)

    return pl.pallas_call(
        body,
        name=name,
        out_shape=jax.ShapeDtypeStruct((3, m, cfg.width), BF16),
        grid=(cfg.width // tn,),
        in_specs=_conv_specs(cfg, tn) + [pl.BlockSpec(memory_space=pl.ANY)],
        out_specs=pl.BlockSpec((None, m, tn), lambda j: (slot, 0, j)),
        input_output_aliases={4: 0},
        compiler_params=_cparams("parallel"),
    )(proj, proj, proj, conv_w, branches)


def conv_bwd(cfg, proj, conv_w, do, do_sel, *, name):
    m, tn = cfg.m, LANES

    def body(b_ref, c_ref, x_ref, w_ref, do_ref, db_ref, dc_ref, dx_ref, dw_ref):
        b, c, x, u, u1, u2, uc, (w0, w1, w2), row = _conv_parts(b_ref, c_ref, x_ref, w_ref, m)
        dob = do_ref[...].astype(F32)
        db_ref[...] = (dob * uc).astype(db_ref.dtype)
        duc = dob * b
        up1 = jnp.where(row <= m - 2, pltpu.roll(duc, m - 1, 0), 0.0)
        up2 = jnp.where(row <= m - 3, pltpu.roll(duc, m - 2, 0), 0.0)
        du = w2 * duc + w1 * up1 + w0 * up2
        dc_ref[...] = (du * x).astype(dc_ref.dtype)
        dx_ref[...] = (du * c).astype(dx_ref.dtype)
        dw_ref[0:1, :] = jnp.sum(duc * u2, axis=0, keepdims=True)
        dw_ref[1:2, :] = jnp.sum(duc * u1, axis=0, keepdims=True)
        dw_ref[2:3, :] = jnp.sum(duc * u, axis=0, keepdims=True)

    act = jax.ShapeDtypeStruct((m, cfg.width), BF16)
    blk = pl.BlockSpec((m, tn), lambda j: (0, j))
    return pl.pallas_call(
        body,
        name=name,
        out_shape=(act, act, act, jax.ShapeDtypeStruct((3, cfg.width), F32)),
        grid=(cfg.width // tn,),
        in_specs=_conv_specs(cfg, tn) + [pl.BlockSpec((None, m, tn), lambda j: (do_sel, 0, j))],
        out_specs=(blk, blk, blk, pl.BlockSpec((3, tn), lambda j: (0, j))),
        compiler_params=_cparams("parallel"),
    )(proj, proj, proj, conv_w, do)


def _tri(lower):
    r = lax.broadcasted_iota(jnp.int32, (LANES, LANES), 0)
    c = lax.broadcasted_iota(jnp.int32, (LANES, LANES), 1)
    return jnp.where((r >= c) if lower else (r <= c), 1.0, 0.0).astype(F32)


def fox_gate_fwd(cfg, fl, b_pad, *, name):
    m = cfg.m
    nblk = m // LANES

    def body(fl_ref, b_ref, c_ref):
        z = fl_ref[...] + b_ref[...]
        logf = jnp.minimum(z, 0.0) - jnp.log(1.0 + jnp.exp(-jnp.abs(z)))
        row = lax.broadcasted_iota(jnp.int32, (m, 1), 0)
        logf = jnp.where(row >= cfg.pad, logf, 0.0)
        tri = _tri(True)
        carry = jnp.zeros((1, LANES), F32)
        for blk in range(nblk):
            cb = jnp.dot(tri, logf[blk * LANES:(blk + 1) * LANES, :], precision=lax.Precision.HIGHEST,
                         preferred_element_type=F32) + carry
            c_ref[blk * LANES:(blk + 1) * LANES, :] = cb
            carry = cb[LANES - 1:LANES, :]

    full = pl.BlockSpec((m, LANES), lambda: (0, 0))
    return pl.pallas_call(
        body,
        name=name,
        out_shape=jax.ShapeDtypeStruct((m, LANES), F32),
        in_specs=[full, pl.BlockSpec((1, LANES), lambda: (0, 0))],
        out_specs=full,
        compiler_params=pltpu.CompilerParams(vmem_limit_bytes=VMEM_LIMIT_BYTES),
    )(fl, b_pad)


def fox_gate_bwd(cfg, fl, b_pad, dc, *, name):
    m = cfg.m
    nblk = m // LANES

    def body(fl_ref, b_ref, dc_ref, dfl_ref, db_ref):
        z = fl_ref[...] + b_ref[...]
        dlogsig = 1.0 / (1.0 + jnp.exp(z))
        row = lax.broadcasted_iota(jnp.int32, (m, 1), 0)
        gate = jnp.where(row >= cfg.pad, dlogsig, 0.0)
        dcv = dc_ref[...]
        tri = _tri(False)
        carry = jnp.zeros((1, LANES), F32)
        db = jnp.zeros((1, LANES), F32)
        for blk in reversed(range(nblk)):
            sl = slice(blk * LANES, (blk + 1) * LANES)
            rb = jnp.dot(tri, dcv[sl, :], precision=lax.Precision.HIGHEST, preferred_element_type=F32) + carry
            carry = rb[0:1, :]
            dfl = rb * gate[sl, :]
            dfl_ref[sl, :] = dfl
            db = db + jnp.sum(dfl, axis=0, keepdims=True)
        db_ref[...] = db

    full = pl.BlockSpec((m, LANES), lambda: (0, 0))
    one = pl.BlockSpec((1, LANES), lambda: (0, 0))
    return pl.pallas_call(
        body,
        name=name,
        out_shape=(jax.ShapeDtypeStruct((m, LANES), F32), jax.ShapeDtypeStruct((1, LANES), F32)),
        in_specs=[full, one, full],
        out_specs=(full, one),
        compiler_params=pltpu.CompilerParams(vmem_limit_bytes=VMEM_LIMIT_BYTES),
    )(fl, b_pad, dc)


def _sigmoid(x):
    return 1.0 / (1.0 + jnp.exp(-x))


def gate_merge_fwd(cfg, y, proj, *, name):
    m, d = cfg.m, cfg.d
    tm, tn = _tile(m, 1088, 16), _tile(d, 512)
    nd = d // tn

    def body(y_ref, g0_ref, g1_ref, g2_ref, o_ref):
        acc = None
        for n, g_ref in enumerate((g0_ref, g1_ref, g2_ref)):
            t = _sigmoid(g_ref[...].astype(F32)) * y_ref[n].astype(F32)
            acc = t if acc is None else acc + t
        o_ref[...] = acc.astype(o_ref.dtype)

    gate = lambda n: pl.BlockSpec((tm, tn), lambda i, j: (i, n * nd + j))
    return pl.pallas_call(
        body,
        name=name,
        out_shape=jax.ShapeDtypeStruct((m, d), BF16),
        grid=(m // tm, nd),
        in_specs=[pl.BlockSpec((3, tm, tn), lambda i, j: (0, i, j)), gate(0), gate(1), gate(2)],
        out_specs=pl.BlockSpec((tm, tn), lambda i, j: (i, j)),
        compiler_params=_cparams("parallel", "parallel"),
    )(y, proj, proj, proj)


def gate_merge_bwd(cfg, dm, y, proj, *, name):
    m, d = cfg.m, cfg.d
    tm, tn = _tile(m, 1088, 16), _tile(d, 512)
    nd = d // tn

    def body(dm_ref, y_ref, g_ref, dy_ref, dg_ref):
        sg = _sigmoid(g_ref[...].astype(F32))
        dmv = dm_ref[...].astype(F32)
        dy_ref[...] = (sg * dmv).astype(dy_ref.dtype)
        dg_ref[...] = (dmv * y_ref[...].astype(F32) * sg * (1.0 - sg)).astype(dg_ref.dtype)

    return pl.pallas_call(
        body,
        name=name,
        out_shape=(jax.ShapeDtypeStruct((3, m, d), BF16), jax.ShapeDtypeStruct((m, 3 * d), BF16)),
        grid=(m // tm, nd, 3),
        in_specs=[
            pl.BlockSpec((tm, tn), lambda i, j, n: (i, j)),
            pl.BlockSpec((None, tm, tn), lambda i, j, n: (n, i, j)),
            pl.BlockSpec((tm, tn), lambda i, j, n: (i, n * nd + j)),
        ],
        out_specs=(
            pl.BlockSpec((None, tm, tn), lambda i, j, n: (n, i, j)),
            pl.BlockSpec((tm, tn), lambda i, j, n: (i, n * nd + j)),
        ),
        compiler_params=_cparams("parallel", "parallel", "parallel"),
    )(dm, y, proj)


def swiglu_fwd(cfg, gu, *, name):
    m, f = cfg.m, cfg.d_ff
    tm, tn = _tile(m, 1088, 16), _tile(f, 512)
    nf = f // tn

    def body(g_ref, u_ref, o_ref):
        g = g_ref[...].astype(F32)
        o_ref[...] = (g * _sigmoid(g) * u_ref[...].astype(F32)).astype(o_ref.dtype)

    return pl.pallas_call(
        body,
        name=name,
        out_shape=jax.ShapeDtypeStruct((m, f), BF16),
        grid=(m // tm, nf),
        in_specs=[pl.BlockSpec((tm, tn), lambda i, j: (i, j)), pl.BlockSpec((tm, tn), lambda i, j: (i, nf + j))],
        out_specs=pl.BlockSpec((tm, tn), lambda i, j: (i, j)),
        compiler_params=_cparams("parallel", "parallel"),
    )(gu, gu)


def swiglu_bwd(cfg, dact, gu, *, name):
    m, f = cfg.m, cfg.d_ff
    tm, tn = _tile(m, 1088, 16), _tile(f, 512)
    nf = f // tn

    def body(da_ref, g_ref, u_ref, o_ref):
        j = pl.program_id(1)
        g, u, da = g_ref[...].astype(F32), u_ref[...].astype(F32), da_ref[...].astype(F32)
        sg = _sigmoid(g)
        dg = da * u * sg * (1.0 + g * (1.0 - sg))
        du = da * g * sg
        o_ref[...] = jnp.where(j < nf, dg, du).astype(o_ref.dtype)

    return pl.pallas_call(
        body,
        name=name,
        out_shape=jax.ShapeDtypeStruct((m, 2 * f), BF16),
        grid=(m // tm, 2 * nf),
        in_specs=[
            pl.BlockSpec((tm, tn), lambda i, j: (i, j % nf)),
            pl.BlockSpec((tm, tn), lambda i, j: (i, j % nf)),
            pl.BlockSpec((tm, tn), lambda i, j: (i, nf + j % nf)),
        ],
        out_specs=pl.BlockSpec((tm, tn), lambda i, j: (i, j)),
        compiler_params=_cparams("parallel", "parallel"),
    )(dact, gu, gu)


def loss_head(cfg, h, target, *, name):
    m, d = cfg.m, cfg.d
    assert cfg.pad + cfg.n_meta == LANES
    tm = LANES
    inv_d = 1.0 / d

    def body(h_ref, t_ref, dh_ref, loss_ref):
        i = pl.program_id(0)

        @pl.when(i == 0)
        def _():
            dh_ref[...] = jnp.zeros_like(dh_ref)
            loss_ref[...] = jnp.zeros_like(loss_ref)

        @pl.when(i > 0)
        def _():
            err = h_ref[...] - t_ref[...]
            dh_ref[...] = err * inv_d
            loss_ref[...] += 0.5 * inv_d * jnp.sum(err * err)

    return pl.pallas_call(
        body,
        name=name,
        out_shape=(jax.ShapeDtypeStruct((m, d), F32), jax.ShapeDtypeStruct((8, LANES), F32)),
        grid=(m // tm,),
        in_specs=[pl.BlockSpec((tm, d), lambda i: (i, 0)), pl.BlockSpec((tm, d), lambda i: (jnp.maximum(i - 1, 0), 0))],
        out_specs=(pl.BlockSpec((tm, d), lambda i: (i, 0)), pl.BlockSpec((8, LANES), lambda i: (0, 0))),
        compiler_params=_cparams("arbitrary"),
    )(h, target)


def adamw(w, g, m_, v_, *, name):
    r, c = w.shape
    c_pad = -(-c // LANES) * LANES
    tr = r
    if r % 8 == 0:
        tr = _tile(r, max(8, (3 << 19) // (4 * c_pad) // 8 * 8), 8)
    bc1 = 1.0 - ADAM_B1 ** ADAM_STEP
    bc2 = 1.0 - ADAM_B2 ** ADAM_STEP

    def body(w_ref, g_ref, m_ref, v_ref, d_ref, nm_ref, nv_ref):
        gv = g_ref[...]
        nm = ADAM_B1 * m_ref[...] + (1.0 - ADAM_B1) * gv
        nv = ADAM_B2 * v_ref[...] + (1.0 - ADAM_B2) * (gv * gv)
        d_ref[...] = -ADAM_LR * ((nm / bc1) / (jnp.sqrt(nv / bc2) + ADAM_EPS) + ADAM_WD * w_ref[...])
        nm_ref[...] = nm
        nv_ref[...] = nv

    blk = pl.BlockSpec((tr, c), lambda i: (i, 0))
    shp = jax.ShapeDtypeStruct((r, c), F32)
    return pl.pallas_call(
        body,
        name=name,
        out_shape=(shp, shp, shp),
        grid=(r // tr,),
        in_specs=[blk, blk, blk, blk],
        out_specs=(blk, blk, blk),
        compiler_params=_cparams("parallel"),
    )(w, g, m_, v_)


def adamw_halves(w, g_own, g_other, core, m_, v_, *, first_layer=0, prev=None, name):
    _, r, c = w.shape
    nl = g_own.shape[0]
    r2 = r // 2
    c_pad = -(-c // LANES) * LANES
    tr = _tile(r2, max(8, (3 << 19) // (4 * c_pad) // 8 * 8), 8)
    nr = r2 // tr
    bc1 = 1.0 - ADAM_B1 ** ADAM_STEP
    bc2 = 1.0 - ADAM_B2 ** ADAM_STEP

    def body(core_ref, w_ref, go_ref, gr_ref, m_ref, v_ref, *rest):
        g_ref, d_ref, nm_ref, nv_ref = rest[-4:]
        gv = jnp.where(pl.program_id(2) == core_ref[0], go_ref[...], gr_ref[...])[:, :c]
        nm = ADAM_B1 * m_ref[...] + (1.0 - ADAM_B1) * gv
        nv = ADAM_B2 * v_ref[...] + (1.0 - ADAM_B2) * (gv * gv)
        d_ref[...] = -ADAM_LR * ((nm / bc1) / (jnp.sqrt(nv / bc2) + ADAM_EPS) + ADAM_WD * w_ref[...])
        g_ref[...] = gv
        nm_ref[...] = nm
        nv_ref[...] = nv

    full = pl.BlockSpec((None, tr, c), lambda l, i, hf, core_ref: (first_layer + l, hf * nr + i, 0))
    half = pl.BlockSpec((None, tr, g_own.shape[2]), lambda l, i, hf, core_ref: (l, i, 0))
    shp = jax.ShapeDtypeStruct(w.shape, F32)
    kept = list(prev or ())
    return pl.pallas_call(
        body,
        name=name,
        out_shape=(shp, shp, shp, shp),
        grid_spec=pltpu.PrefetchScalarGridSpec(
            num_scalar_prefetch=1,
            grid=(nl, nr, 2),
            in_specs=[full, half, half, full, full] + [pl.BlockSpec(memory_space=pl.ANY)] * len(kept),
            out_specs=(full, full, full, full),
        ),
        input_output_aliases={6 + k: k for k in range(len(kept))},
        compiler_params=_cparams("parallel", "parallel", "arbitrary"),
    )(core, w, g_own, g_other, m_, v_, *kept)


_HBM = pl.BlockSpec(memory_space=pltpu.HBM)


def _place():
    x, y, c = lax.axis_index("x"), lax.axis_index("y"), lax.axis_index("c")
    flips = [(1 - x, y), (x, 1 - y), (1 - x, 1 - y)]
    return x, y, c, flips


_SEM = pl.BlockSpec(memory_space=pltpu.SEMAPHORE)
_EFFECT = pltpu.SideEffectType.DATAFLOW_SIDE_EFFECTING


def _gather_plan(halves):
    def plan(src_refs, land_refs, arrival):
        x, y, c, flips = _place()
        mine = 2 * x + y
        out = []
        for w, h in enumerate(halves):
            for fx, fy in flips:
                slot = (2 * fx + fy) if arrival else mine
                out.append((src_refs[w].at[pl.ds(c * h, h), :], land_refs[w].at[slot, pl.ds(c * h, h), :], (fx, fy, c)))
        return out
    return plan


def _exchange_plan(nw):
    def plan(src_refs, land_refs, arrival):
        _, _, c, flips = _place()
        return [(src_refs[w].at[2 * fx + fy], land_refs[w].at[k], (fx, fy, c)) for w in range(nw) for k, (fx, fy) in enumerate(flips)]
    return plan


def _swap_plan(halves):
    def plan(src_refs, land_refs, arrival):
        x, y, c, _ = _place()
        return [(src_refs[w].at[:, pl.ds((1 - c) * h, h), :], land_refs[w], (x, y, 1 - c)) for w, h in enumerate(halves)]
    return plan


def _share_plan(nw):
    def plan(src_refs, land_refs, arrival):
        x, y, c, _ = _place()
        return [(src_refs[w], land_refs[w], (x, y, 1 - c)) for w in range(nw)]
    return plan


def copies_start(srcs, land_shapes, plan, n_copies, *, name):
    lands = [lax.empty(s, a.dtype) for s, a in zip(land_shapes, srcs)]
    n_in = len(srcs) + len(lands)

    def body(*refs):
        src_refs, land_refs = refs[:len(srcs)], refs[len(srcs):n_in]
        send_sems, recv_sems, token = refs[n_in], refs[n_in + 1], refs[-1]
        for i, (src, dst, to) in enumerate(plan(src_refs, land_refs, False)):
            pltpu.make_async_remote_copy(src_ref=src, dst_ref=dst, send_sem=send_sems.at[i], recv_sem=recv_sems.at[i],
                                         device_id=to, device_id_type=MESH).start()
        token[...] = jnp.zeros_like(token)

    operands = list(srcs) + lands
    out = pl.pallas_call(
        body,
        name=name,
        out_shape=(pltpu.SemaphoreType.DMA((n_copies,)), pltpu.SemaphoreType.DMA((n_copies,)),
                   *[pltpu.HBM(a.shape, a.dtype) for a in operands], jax.ShapeDtypeStruct((8, LANES), F32)),
        in_specs=[_HBM] * n_in,
        out_specs=(_SEM, _SEM, *[_HBM] * n_in, pl.BlockSpec(memory_space=pltpu.VMEM)),
        input_output_aliases={i: 2 + i for i in range(n_in)},
        compiler_params=pltpu.CompilerParams(has_side_effects=_EFFECT),
    )(*[pltpu.with_memory_space_constraint(a, pltpu.HBM) for a in operands])
    return out[0], out[1], list(out[2:2 + len(srcs)]), list(out[2 + len(srcs):2 + n_in]), out[-1]


def copies_wait(send_sems, recv_sems, srcs, lands, plan, after, *, name):
    n_in = len(srcs) + len(lands)

    def body(*refs):
        src_refs, land_refs = refs[:len(srcs)], refs[len(srcs):n_in]
        send_ref, recv_ref, token = refs[n_in], refs[n_in + 1], refs[-1]
        token[...] = jnp.zeros_like(token)
        for i, (src, dst, to) in enumerate(plan(src_refs, land_refs, True)):
            copy = pltpu.make_async_remote_copy(src_ref=src, dst_ref=dst, send_sem=send_ref.at[i], recv_sem=recv_ref.at[i],
                                                device_id=to, device_id_type=MESH)
            copy.wait_send()
            copy.wait_recv()

    operands = list(srcs) + list(lands)
    out = pl.pallas_call(
        body,
        name=name,
        out_shape=(*[pltpu.HBM(a.shape, a.dtype) for a in operands], jax.ShapeDtypeStruct((8, LANES), F32)),
        in_specs=[_HBM] * n_in + [_SEM, _SEM, pl.BlockSpec(memory_space=pl.ANY)],
        out_specs=(*[_HBM] * n_in, pl.BlockSpec(memory_space=pltpu.VMEM)),
        input_output_aliases={i: i for i in range(n_in)},
        compiler_params=pltpu.CompilerParams(has_side_effects=_EFFECT),
    )(*operands, send_sems, recv_sems, after)
    return list(out[:len(srcs)]), list(out[len(srcs):n_in]), out[-1]


def forward_halves(lands, *, name):
    nw = len(lands)
    halves = [a.shape[1] // 2 for a in lands]

    def body(*refs):
        ins, outs = refs[:nw], refs[nw:2 * nw]
        send_sems, recv_sems = refs[2 * nw:]
        x, y, c, flips = _place()
        copies = []
        for w, h in enumerate(halves):
            for k, (fx, fy) in enumerate(flips):
                rows = (2 * fx + fy, pl.ds(c * h, h), slice(None))
                copies.append(pltpu.make_async_remote_copy(src_ref=ins[w].at[rows], dst_ref=outs[w].at[rows], send_sem=send_sems.at[3 * w + k],
                                                           recv_sem=recv_sems.at[3 * w + k], device_id=(x, y, 1 - c), device_id_type=MESH))
        for cp in copies:
            cp.start()
        for cp in copies:
            cp.wait()

    return pl.pallas_call(
        body,
        name=name,
        out_shape=tuple(jax.ShapeDtypeStruct(a.shape, a.dtype) for a in lands),
        in_specs=[_HBM] * nw,
        out_specs=tuple([_HBM] * nw),
        input_output_aliases={w: w for w in range(nw)},
        scratch_shapes=[pltpu.SemaphoreType.DMA((3 * nw,)), pltpu.SemaphoreType.DMA((3 * nw,))],
    )(*lands)


def share_halves(sums, *, name):
    nw = len(sums)

    def body(*refs):
        ins, outs = refs[:nw], refs[nw:2 * nw]
        send_sems, recv_sems = refs[2 * nw:]
        x, y, c, _ = _place()
        copies = [
            pltpu.make_async_remote_copy(src_ref=ins[w], dst_ref=outs[w], send_sem=send_sems.at[w], recv_sem=recv_sems.at[w],
                                         device_id=(x, y, 1 - c), device_id_type=MESH)
            for w in range(nw)
        ]
        for cp in copies:
            cp.start()
        for cp in copies:
            cp.wait()

    return pl.pallas_call(
        body,
        name=name,
        out_shape=tuple(jax.ShapeDtypeStruct(s.shape, s.dtype) for s in sums),
        in_specs=[_HBM] * nw,
        out_specs=tuple([_HBM] * nw),
        scratch_shapes=[pltpu.SemaphoreType.DMA((nw,)), pltpu.SemaphoreType.DMA((nw,))],
    )(*sums)


def gather_blocks(block, *, reduce, name):
    rows, cols = block.shape

    def body(x_ref, out_ref, *rest):
        if reduce:
            buf_ref, send_sems, recv_sems = rest
        else:
            send_sems, recv_sems = rest
            buf_ref = out_ref
        x, y, c, flips = _place()
        me, sibling = (x, y, c), (x, y, 1 - c)

        def slot(px, py, pc):
            return buf_ref.at[4 * px + 2 * py + pc]

        def copy(k, blk, to, src=None):
            return pltpu.make_async_remote_copy(src_ref=slot(*blk) if src is None else src, dst_ref=slot(*blk),
                                                send_sem=send_sems.at[k], recv_sem=recv_sems.at[k], device_id=to,
                                                device_id_type=MESH)

        buf_ref[4 * x + 2 * y + c] = x_ref[...]
        first = [copy(0, me, sibling, src=x_ref)]
        first += [copy(1 + j, me, (*chip, c), src=x_ref) for j, chip in enumerate(flips)]
        for cp in first:
            cp.start()
        passed = [copy(4 + j, (*chip, c), sibling) for j, chip in enumerate(flips)]
        for j, chip in enumerate(flips):
            copy(1 + j, (*chip, c), me).wait_recv()
            passed[j].start()
        copy(0, sibling, me).wait_recv()
        for j, chip in enumerate(flips):
            copy(4 + j, (*chip, 1 - c), me).wait_recv()
        for cp in first + passed:
            cp.wait_send()
        if reduce:
            acc = buf_ref[0]
            for dev in range(1, N_DEV):
                acc = acc + buf_ref[dev]
            out_ref[...] = acc

    vmem = pl.BlockSpec(memory_space=pltpu.VMEM)
    sems = [pltpu.SemaphoreType.DMA((7,)), pltpu.SemaphoreType.DMA((7,))]
    if reduce:
        out_shape = jax.ShapeDtypeStruct((rows, cols), block.dtype)
        scratch = [pltpu.VMEM((N_DEV, rows, cols), block.dtype)] + sems
    else:
        out_shape = jax.ShapeDtypeStruct((N_DEV, rows, cols), block.dtype)
        scratch = sems
    return pl.pallas_call(
        body,
        name=name,
        out_shape=out_shape,
        in_specs=[vmem],
        out_specs=vmem,
        scratch_shapes=scratch,
        compiler_params=pltpu.CompilerParams(vmem_limit_bytes=VMEM_LIMIT_BYTES),
    )(block)


def add_own_half(grad, recv, core, *, name):
    _, r2, cols = recv.shape
    tr = _tile(r2, max(16, (1 << 20) // (2 * cols) // 16 * 16), 16)
    nr = r2 // tr

    def body(core_ref, g_ref, r_ref, o_ref):
        o_ref[...] = (g_ref[...].astype(F32) + r_ref[...].astype(F32)).astype(o_ref.dtype)

    return pl.pallas_call(
        body,
        name=name,
        out_shape=jax.ShapeDtypeStruct(recv.shape, BF16),
        grid_spec=pltpu.PrefetchScalarGridSpec(
            num_scalar_prefetch=1,
            grid=(N_CHIPS, nr),
            in_specs=[
                pl.BlockSpec((None, tr, cols), lambda k, i, core_ref: (k, core_ref[0] * nr + i, 0)),
                pl.BlockSpec((None, tr, cols), lambda k, i, core_ref: (k, i, 0)),
            ],
            out_specs=pl.BlockSpec((None, tr, cols), lambda k, i, core_ref: (k, i, 0)),
        ),
        compiler_params=_cparams("parallel", "parallel"),
    )(core, grad, recv)


def sum_chips(part, recv, chip, sums, layer, *, name):
    _, r2, cols = part.shape
    tr = _tile(r2, max(16, (1 << 20) // (2 * cols) // 16 * 16), 16)

    def body(chip_ref, p_ref, r_ref, sums_ref, o_ref):
        acc = p_ref[...].astype(F32)
        for k in range(3):
            acc = acc + r_ref[k].astype(F32)
        o_ref[...] = acc

    return pl.pallas_call(
        body,
        name=name,
        out_shape=jax.ShapeDtypeStruct(sums.shape, F32),
        grid_spec=pltpu.PrefetchScalarGridSpec(
            num_scalar_prefetch=1,
            grid=(r2 // tr,),
            in_specs=[
                pl.BlockSpec((None, tr, cols), lambda i, chip_ref: (chip_ref[0], i, 0)),
                pl.BlockSpec((3, tr, cols), lambda i, chip_ref: (0, i, 0)),
                pl.BlockSpec(memory_space=pl.ANY),
            ],
            out_specs=pl.BlockSpec((None, tr, cols), lambda i, chip_ref: (layer, i, 0)),
        ),
        input_output_aliases={3: 0},
        compiler_params=_cparams("parallel"),
    )(chip, part, recv, sums)


WEIGHTS = ("w_in", "w_uq", "w_ukv", "w_branch", "w_out", "w_ffn_in", "w_ffn_out")


def layer_fwd(cfg, h, w, s, tabs, tag):
    m, d, hd = cfg.m, cfg.d, cfg.heads
    fox_blk = cfg.off_fox // LANES
    hn = rmsnorm_fwd(h, s["g_mix_pre"], BF16, name=f"norm_mix_pre{tag}")
    proj = matmul(hn, w["w_in"], "nn", BF16, tm=m, tn=_tile(cfg.d_inp, 512), tk=d, name=f"proj{tag}")
    fl = matmul(hn, w["w_in"][:, cfg.off_fl:cfg.off_fl + LANES], "nn", F32, tm=m, tn=LANES, tk=d, name=f"proj_forget{tag}")
    cqn = rmsnorm_fwd(proj, s["g_q_lat"], BF16, width=cfg.q_rank, col_blk=cfg.off_cq // cfg.q_rank, name=f"norm_q{tag}")
    ckvn = rmsnorm_fwd(proj, s["g_kv_lat"], BF16, width=cfg.kv_rank, col_blk=cfg.off_ckv // cfg.kv_rank, name=f"norm_kv{tag}")
    q = matmul(cqn, w["w_uq"], "nn", BF16, tm=m, tn=_tile(2 * cfg.width, 512), tk=cfg.q_rank, name=f"up_q{tag}")
    kv = matmul(ckvn, w["w_ukv"], "nn", BF16, tm=m, tn=_tile(2 * cfg.width, 512), tk=cfg.kv_rank, name=f"up_kv{tag}")
    qf, kf = mla_prep_fwd(cfg, q, kv, proj, tabs[0], name=f"mla_prep{tag}")
    o, lse_a = attn_fwd(qf, kf, kv, heads=hd, dk=2 * LANES, dv=LANES, qblk0=0, kblk0=0, vblk0=hd,
                        scale=(LANES + ROPE) ** -0.5, pad=cfg.pad, slot=0, name=f"mla_attn{tag}")
    o = conv_fwd(cfg, proj, s["conv_w"], o, 1, name=f"conv{tag}")
    b_pad = jnp.pad(s["b_forget"], (0, LANES - hd)).reshape(1, LANES)
    cum = fox_gate_fwd(cfg, fl, b_pad, name=f"fox_gate{tag}")
    cum_t = cum[:, :hd].T
    decay = (cum_t[:, :, None], cum_t[:, None, :])
    o, lse_c = attn_fwd(proj, proj, proj, heads=hd, dk=LANES, dv=LANES, qblk0=fox_blk, kblk0=fox_blk + hd, vblk0=fox_blk + 2 * hd,
                        scale=LANES ** -0.5, pad=cfg.pad, slot=2, branches=o, decay=decay, name=f"fox_attn{tag}")
    y = matmul(o, w["w_branch"], "nn", BF16, tm=m, tn=_tile(d, 512), tk=cfg.width, name=f"branch{tag}")
    merged = gate_merge_fwd(cfg, y, proj, name=f"merge{tag}")
    mix = matmul(merged, w["w_out"], "nn", F32, tm=m, tn=_tile(d, 256), tk=d, name=f"out_proj{tag}")
    h_mid = rmsnorm_fwd(mix, s["g_mix_post"], F32, res=h, name=f"norm_mix_post{tag}")
    if hasattr(w, "land_ffn"):
        s = w.land_ffn(h_mid, s)
    hn2 = rmsnorm_fwd(h_mid, s["g_ffn_pre"], BF16, name=f"norm_ffn_pre{tag}")
    gu = matmul(hn2, w["w_ffn_in"], "nn", BF16, tm=m, tn=_tile(2 * cfg.d_ff, 512), tk=d, name=f"ffn_in{tag}")
    act = swiglu_fwd(cfg, gu, name=f"swiglu{tag}")
    f = matmul(act, w["w_ffn_out"], "nn", F32, tm=m, tn=_tile(d, 512), tk=_tile(cfg.d_ff, 1408), name=f"ffn_out{tag}")
    h_next = rmsnorm_fwd(f, s["g_ffn_post"], F32, res=h_mid, name=f"norm_ffn_post{tag}")
    saved = dict(h=h, hn=hn, proj=proj, fl=fl, cqn=cqn, ckvn=ckvn, kv=kv, qf=qf, kf=kf, lse_a=lse_a,
                 b_pad=b_pad, decay=decay, lse_c=lse_c, o=o, y=y, merged=merged, mix=mix, h_mid=h_mid,
                 hn2=hn2, gu=gu, act=act, f=f)
    return h_next, s, saved


def layer_bwd(cfg, dh, w, s, r, tabs, tag, grads_done):
    m, d, hd = cfg.m, cfg.d, cfg.heads
    fox_blk = cfg.off_fox // LANES
    tk_m = m
    df, dg4 = rmsnorm_bwd(r["f"], s["g_ffn_post"], dh, BF16, name=f"norm_ffn_post_bwd{tag}")
    dact = matmul(df, w["w_ffn_out"], "nt", BF16, tm=m, tn=_tile(cfg.d_ff, 512), tk=d, name=f"ffn_out_dx{tag}")
    dw_fo = matmul(r["act"], df, "tn", BF16, tm=_tile(cfg.d_ff, 512), tn=_tile(d, 1024), tk=tk_m, name=f"ffn_out_dw{tag}")
    dgu = swiglu_bwd(cfg, dact, r["gu"], name=f"swiglu_bwd{tag}")
    dhn2 = matmul(dgu, w["w_ffn_in"], "nt", F32, tm=m, tn=_tile(d, 512), tk=_tile(2 * cfg.d_ff, 1408), name=f"ffn_in_dx{tag}")
    dw_fi = matmul(r["hn2"], dgu, "tn", BF16, tm=_tile(d, 1024), tn=_tile(2 * cfg.d_ff // N_CHIPS, 1408), tk=tk_m, chip_cols=True,
                   name=f"ffn_in_dw{tag}")
    token = grads_done(dict(w_ffn_in=dw_fi, w_ffn_out=dw_fo))
    if token is not None:
        s = {**s, "g_ffn_pre": s["g_ffn_pre"] + token[0, 0]}
    dh_mid, dg3 = rmsnorm_bwd(r["h_mid"], s["g_ffn_pre"], dhn2, F32, dres=dh, name=f"norm_ffn_pre_bwd{tag}")
    dmix, dg2 = rmsnorm_bwd(r["mix"], s["g_mix_post"], dh_mid, BF16, name=f"norm_mix_post_bwd{tag}")
    dmerged = matmul(dmix, w["w_out"], "nt", BF16, tm=m, tn=_tile(d, 512), tk=d, name=f"out_proj_dx{tag}")
    dw_out = matmul(r["merged"], dmix, "tn", BF16, tm=_tile(d, 1024), tn=_tile(d, 512), tk=tk_m, name=f"out_proj_dw{tag}")
    dy, dgl = gate_merge_bwd(cfg, dmerged, r["y"], r["proj"], name=f"merge_bwd{tag}")
    do = matmul(dy, w["w_branch"], "nt", BF16, tm=m, tn=_tile(cfg.width, 512), tk=d, name=f"branch_dx{tag}")
    dw_br = matmul(r["o"], dy, "tn", BF16, tm=_tile(cfg.width, 1024), tn=_tile(d // N_CHIPS, 512), tk=tk_m, chip_cols=True,
                   name=f"branch_dw{tag}")
    dqf, dkf, dv_a = attn_bwd(r["qf"], r["kf"], r["kv"], do, 0, r["lse_a"], heads=hd, dk=2 * LANES, dv=LANES,
                              qblk0=0, kblk0=0, vblk0=hd, scale=(LANES + ROPE) ** -0.5, pad=cfg.pad, name=f"mla_attn_bwd{tag}")
    dq, dkn, dkpe = mla_prep_bwd(cfg, dqf, dkf, tabs[1], name=f"mla_prep_bwd{tag}")
    dkv = jnp.concatenate([dkn, dv_a], axis=1)
    dcqn = matmul(dq, w["w_uq"], "nt", F32, tm=m, tn=cfg.q_rank, tk=2 * cfg.width, name=f"up_q_dx{tag}")
    dw_uq = matmul(r["cqn"], dq, "tn", BF16, tm=cfg.q_rank, tn=_tile(2 * cfg.width, 512), tk=tk_m, name=f"up_q_dw{tag}")
    dckvn = matmul(dkv, w["w_ukv"], "nt", F32, tm=m, tn=cfg.kv_rank, tk=2 * cfg.width, name=f"up_kv_dx{tag}")
    dw_ukv = matmul(r["ckvn"], dkv, "tn", BF16, tm=cfg.kv_rank, tn=_tile(2 * cfg.width, 512), tk=tk_m, name=f"up_kv_dw{tag}")
    dcq, dgq = rmsnorm_bwd(r["proj"], s["g_q_lat"], dcqn, BF16, width=cfg.q_rank, col_blk=cfg.off_cq // cfg.q_rank,
                           name=f"norm_q_bwd{tag}")
    dckv, dgkv = rmsnorm_bwd(r["proj"], s["g_kv_lat"], dckvn, BF16, width=cfg.kv_rank, col_blk=cfg.off_ckv // cfg.kv_rank,
                             name=f"norm_kv_bwd{tag}")
    dcb, dcc, dcx, dconv_w = conv_bwd(cfg, r["proj"], s["conv_w"], do, 1, name=f"conv_bwd{tag}")
    dfq, dfk, dfv, dck = attn_bwd(r["proj"], r["proj"], r["proj"], do, 2, r["lse_c"], heads=hd, dk=LANES, dv=LANES,
                                  qblk0=fox_blk, kblk0=fox_blk + hd, vblk0=fox_blk + 2 * hd, scale=LANES ** -0.5,
                                  pad=cfg.pad, decay=r["decay"], name=f"fox_attn_bwd{tag}")
    dc = jnp.pad(dck[:, 0, :].T, ((0, 0), (0, LANES - hd)))
    dfl, dbf = fox_gate_bwd(cfg, r["fl"], r["b_pad"], dc, name=f"fox_gate_bwd{tag}")
    tail = jnp.zeros((m, cfg.d_inp - cfg.off_fl - LANES), BF16)
    dproj = jnp.concatenate([dgl, dcq, dckv, dcb, dcc, dcx, dfq, dfk, dfv, dkpe.astype(BF16), dfl.astype(BF16), tail], axis=1)
    dhn = matmul(dproj, w["w_in"], "nt", F32, tm=m, tn=_tile(d, 512), tk=_tile(cfg.d_inp, 1536), name=f"proj_dx{tag}")
    dw_in = matmul(r["hn"], dproj, "tn", BF16, tm=_tile(d, 1024), tn=_tile(cfg.d_inp, 512), tk=tk_m, name=f"proj_dw{tag}")
    dh_in, dg1 = rmsnorm_bwd(r["h"], s["g_mix_pre"], dhn, F32, dres=dh_mid, name=f"norm_mix_pre_bwd{tag}")
    token = grads_done(dict(w_in=dw_in, w_uq=dw_uq, w_ukv=dw_ukv, w_branch=dw_br, w_out=dw_out))
    dsmall = dict(g_mix_pre=dg1[0], g_mix_post=dg2[0], g_ffn_pre=dg3[0], g_ffn_post=dg4[0], g_q_lat=dgq[0], g_kv_lat=dgkv[0],
                  b_forget=dbf[0, :hd], conv_w=dconv_w)
    return dh_in, dsmall, token


def local_step(cfg, x, target, meta, layer_params, grads_done):
    h = jnp.concatenate([jnp.zeros((cfg.pad, cfg.d), F32), meta, x], axis=0)
    cos, s1, s2 = rope_tables(cfg)
    tabs = ((cos, s1, s2), (cos, -s1, -s2))
    saved = []
    for l in range(cfg.depth):
        w, s = layer_params(l, h)
        h, s, r = layer_fwd(cfg, h, w, s, tabs, f"_{l}")
        saved.append((w, s, r))
    dh, loss = loss_head(cfg, h, target, name="loss_head")
    dsmalls, token = [None] * cfg.depth, None
    for l in reversed(range(cfg.depth)):
        w, s, r = saved[l]
        if token is not None:
            s = {**s, "g_ffn_post": s["g_ffn_post"] + token[0, 0]}
        dh, dsmalls[l], token = layer_bwd(cfg, dh, w, s, r, tabs, f"_{l}", functools.partial(grads_done, l))
    first = cfg.pad + cfg.n_meta
    return loss, dh[first:], dh[cfg.pad:first], dsmalls


def _cols_from_chips(g):
    return jnp.transpose(g, (1, 0, 2)).reshape(g.shape[1], N_CHIPS * g.shape[2])


def _cols_to_chips(w):
    r, c = w.shape
    return jnp.transpose(w.reshape(r, N_CHIPS, c // N_CHIPS), (1, 0, 2))


def _packed_segments(cfg):
    nat = [0] + _cumsum(cfg.nat_splits)
    w = cfg.width
    order = [(10, 0), (0, cfg.off_cq), (1, cfg.off_ckv), (3, cfg.off_conv), (4, cfg.off_conv + w), (5, cfg.off_conv + 2 * w),
             (6, cfg.off_fox), (7, cfg.off_fox + w), (8, cfg.off_fox + 2 * w), (2, cfg.off_kpe), (9, cfg.off_fl)]
    return [(pk, nat[i], cfg.nat_splits[i]) for i, pk in order]


def _chip_cols(cfg):
    n = cfg.d_in // N_CHIPS
    return n, -(-n // LANES) * LANES


def _lane_pieces(cfg, to_packed):
    n, n_pad = _chip_cols(cfg)
    tiles = [[] for _ in range(cfg.d_inp // LANES if to_packed else N_CHIPS * n_pad // LANES)]
    for pk, nat, width in _packed_segments(cfg):
        g = nat
        while g < nat + width:
            k, a = divmod(g, n)
            dst = (pk + g - nat) if to_packed else (k * n_pad + a)
            run = min(nat + width - g, n - a, LANES - dst % LANES)
            src = (k, a) if to_packed else (0, pk + g - nat)
            tiles[dst // LANES].append((dst % LANES, run, *src))
            g += run
    return tiles


def _fill_tiles(pieces, read, write, rows):
    lane = lax.broadcasted_iota(jnp.int32, (rows, LANES), 1)
    for t, parts in enumerate(pieces):
        tile = jnp.zeros((rows, LANES), F32)
        for dl, run, blk, col in parts:
            w0 = col // LANES * LANES
            off = col - w0
            span = LANES if off + run <= LANES else 2 * LANES
            win = read(blk, w0, span)
            shift = (dl - off) % span
            if shift:
                win = pltpu.roll(win, shift, 1)
            win = win[:, :LANES]
            tile = win if (dl == 0 and run == LANES) else jnp.where((lane >= dl) & (lane < dl + run), win, tile)
        write(t, tile)


def pack_w_in_blocks(cfg, lands, own, chip, *, name):
    d = cfg.d
    n, n_pad = _chip_cols(cfg)
    tr = _tile(d, 256, 16)
    pieces = _lane_pieces(cfg, True)

    def body(chip_ref, land_ref, own_ref, o_ref):
        def read(k, w0, span):
            theirs = land_ref[k, :, w0:w0 + span]
            return jnp.where(chip_ref[0] == k, own_ref[:, w0:w0 + span], theirs).astype(F32)

        def write(t, tile):
            o_ref[:, t * LANES:(t + 1) * LANES] = tile.astype(o_ref.dtype)

        _fill_tiles(pieces, read, write, tr)

    return pl.pallas_call(
        body,
        name=name,
        out_shape=jax.ShapeDtypeStruct((d, cfg.d_inp), BF16),
        grid_spec=pltpu.PrefetchScalarGridSpec(
            num_scalar_prefetch=1,
            grid=(d // tr,),
            in_specs=[pl.BlockSpec((N_CHIPS, tr, n_pad), lambda i, chip_ref: (0, i, 0)),
                      pl.BlockSpec((tr, n_pad), lambda i, chip_ref: (i, 0))],
            out_specs=pl.BlockSpec((tr, cfg.d_inp), lambda i, chip_ref: (i, 0)),
        ),
        compiler_params=_cparams("parallel"),
    )(chip, lands, own)


def unpack_w_in_blocks(cfg, dw, *, name):
    d = cfg.d
    n, n_pad = _chip_cols(cfg)
    tr = _tile(d, 256, 16)
    per_blk = n_pad // LANES
    pieces = _lane_pieces(cfg, False)

    def body(dw_ref, o_ref):
        def read(_, w0, span):
            return dw_ref[:, w0:w0 + span].astype(F32)

        def write(t, tile):
            k, i = divmod(t, per_blk)
            o_ref[k, :, i * LANES:(i + 1) * LANES] = tile.astype(o_ref.dtype)

        _fill_tiles(pieces, read, write, tr)

    return pl.pallas_call(
        body,
        name=name,
        out_shape=jax.ShapeDtypeStruct((N_CHIPS, d, n_pad), BF16),
        grid=(d // tr,),
        in_specs=[pl.BlockSpec((tr, cfg.d_inp), lambda i: (i, 0))],
        out_specs=pl.BlockSpec((N_CHIPS, tr, n_pad), lambda i: (0, i, 0)),
        compiler_params=_cparams("parallel"),
    )(dw)


def full_weights(cfg, g, w_in=None):
    make = dict(
        w_uq=lambda a: pack_w_uq(cfg, _cols_from_chips(a)),
        w_ukv=lambda a: pack_w_ukv(cfg, _cols_from_chips(a)),
        w_branch=lambda a: _cols_from_chips(a).reshape(3, cfg.width, cfg.d),
        w_out=lambda a: a.reshape(cfg.d, cfg.d),
        w_ffn_in=_cols_from_chips,
        w_ffn_out=lambda a: a.reshape(cfg.d_ff, cfg.d),
    )
    out = {n: make[n](a) for n, a in g.items()}
    if w_in is not None:
        out["w_in"] = w_in
    return out


def chip_grads(cfg, dw, tag):
    make = dict(
        w_in=lambda a: unpack_w_in_blocks(cfg, a, name=f"unpack_w_in{tag}"),
        w_uq=lambda a: _cols_to_chips(unpack_w_uq(cfg, a)),
        w_ukv=lambda a: _cols_to_chips(unpack_w_ukv(cfg, a)),
        w_branch=lambda a: a.reshape(N_CHIPS, 3 * cfg.width, cfg.d // N_CHIPS),
        w_out=lambda a: a.reshape(N_CHIPS, cfg.d // N_CHIPS, cfg.d),
        w_ffn_in=lambda a: a,
        w_ffn_out=lambda a: a.reshape(N_CHIPS, cfg.d_ff // N_CHIPS, cfg.d),
    )
    return {n: make[n](a) for n, a in dw.items()}


def _small_rows(cfg):
    return dict(g_mix_pre=cfg.d // LANES, g_mix_post=cfg.d // LANES, g_ffn_pre=cfg.d // LANES, g_ffn_post=cfg.d // LANES,
                g_q_lat=cfg.q_rank // LANES, g_kv_lat=cfg.kv_rank // LANES, b_forget=1, conv_w=3 * cfg.width // LANES)


def pack_small(cfg, loss, dmeta, dsmalls):
    parts = [loss[0:1, :], dmeta.reshape(-1, LANES)]
    for ds in dsmalls:
        for k in _small_rows(cfg):
            v = ds[k]
            if k == "b_forget":
                v = jnp.pad(v, (0, LANES - cfg.heads))
            parts.append(v.reshape(-1, LANES))
    rows = sum(p.shape[0] for p in parts)
    parts.append(jnp.zeros((-rows % 8, LANES), F32))
    return jnp.concatenate(parts, axis=0)


def unpack_small(cfg, block):
    loss = block[0, 0]
    n = cfg.n_meta * cfg.d // LANES
    dmeta = block[1:1 + n].reshape(cfg.n_meta, cfg.d)
    at = 1 + n
    out = []
    for _ in range(cfg.depth):
        ds = {}
        for k, rows in _small_rows(cfg).items():
            v = block[at:at + rows]
            at += rows
            if k == "b_forget":
                v = v[0, :cfg.heads]
            elif k == "conv_w":
                v = v.reshape(3, cfg.width)
            else:
                v = v.reshape(-1)
            ds[k] = v
        out.append(ds)
    return loss, dmeta, out


def kernel(x, meta, w_in, b_forget, g_q_lat, g_kv_lat, w_uq, w_ukv, conv_w, w_branch, w_out, w_ffn_in, w_ffn_out, g_mix_pre, g_mix_post, g_ffn_pre, g_ffn_post, loss_target, m_meta, m_w_in, m_b_forget, m_g_q_lat, m_g_kv_lat, m_w_uq, m_w_ukv, m_conv_w, m_w_branch, m_w_out, m_w_ffn_in, m_w_ffn_out, m_g_mix_pre, m_g_mix_post, m_g_ffn_pre, m_g_ffn_post, v_meta, v_w_in, v_b_forget, v_g_q_lat, v_g_kv_lat, v_w_uq, v_w_ukv, v_conv_w, v_w_branch, v_w_out, v_w_ffn_in, v_w_ffn_out, v_g_mix_pre, v_g_mix_post, v_g_ffn_pre, v_g_ffn_post):
    cfg = CFG
    names = ("meta", "w_in", "b_forget", "g_q_lat", "g_kv_lat", "w_uq", "w_ukv", "conv_w", "w_branch", "w_out", "w_ffn_in",
             "w_ffn_out", "g_mix_pre", "g_mix_post", "g_ffn_pre", "g_ffn_post")
    params = dict(zip(names, (meta, w_in, b_forget, g_q_lat, g_kv_lat, w_uq, w_ukv, conv_w, w_branch, w_out, w_ffn_in, w_ffn_out,
                              g_mix_pre, g_mix_post, g_ffn_pre, g_ffn_post)))
    mom1 = dict(zip(names, (m_meta, m_w_in, m_b_forget, m_g_q_lat, m_g_kv_lat, m_w_uq, m_w_ukv, m_conv_w, m_w_branch, m_w_out,
                            m_w_ffn_in, m_w_ffn_out, m_g_mix_pre, m_g_mix_post, m_g_ffn_pre, m_g_ffn_post)))
    mom2 = dict(zip(names, (v_meta, v_w_in, v_b_forget, v_g_q_lat, v_g_kv_lat, v_w_uq, v_w_ukv, v_conv_w, v_w_branch, v_w_out,
                            v_w_ffn_in, v_w_ffn_out, v_g_mix_pre, v_g_mix_post, v_g_ffn_pre, v_g_ffn_post)))
    xi, yi, ci = lax.axis_index("x"), lax.axis_index("y"), lax.axis_index("c")
    chip = 2 * xi + yi
    chip_arr = jnp.reshape(chip, (1,)).astype(jnp.int32)
    core_arr = jnp.reshape(ci, (1,)).astype(jnp.int32)

    meta_all = gather_blocks(meta, reduce=False, name="gather_meta")[0::2]
    meta_full = jnp.transpose(meta_all, (1, 0, 2)).reshape(cfg.n_meta, cfg.d)
    conv_rows = conv_w.reshape(cfg.depth * 3, cfg.width // N_CHIPS)
    conv_all = gather_blocks(conv_rows, reduce=False, name="gather_conv_w")[0::2]
    conv_full = jnp.transpose(conv_all, (1, 0, 2)).reshape(cfg.depth, 3, cfg.width)

    def shard2d(name, l, after=None):
        w = params[name][l]
        if after is not None:
            w = w + after
        w = w.reshape(-1, w.shape[-1]).astype(BF16)
        if name == "w_in":
            w = jnp.pad(w, ((0, 0), (0, _chip_cols(cfg)[1] - w.shape[1])))
        return w

    is_mine = (jnp.arange(N_CHIPS) == chip)[:, None, None]
    shard_shape = {n: shard2d(n, 0).shape for n in WEIGHTS}
    groups = (("w_in", "w_uq", "w_ukv", "w_branch", "w_out"), ("w_ffn_in", "w_ffn_out"))
    gather_plans = [_gather_plan([shard_shape[n][0] // 2 for n in g]) for g in groups]

    def gather_start(l, after):
        started = []
        for gi, g in enumerate(groups):
            started.append(copies_start([shard2d(n, l, after) for n in g], [(N_CHIPS,) + shard_shape[n] for n in g],
                                        gather_plans[gi], 3 * len(g), name=f"gather_start_{gi}_{l}"))
            after = started[-1][4][0, 0]
        return started

    def land(l, gi, started, after):
        send_sems, recv_sems, own, lands, _ = started
        own, lands, landed = copies_wait(send_sems, recv_sems, own, lands, gather_plans[gi], after, name=f"gather_wait_{gi}_{l}")
        return own, forward_halves(lands, name=f"forward_halves_{gi}_{l}"), landed

    in_flight = {0: gather_start(0, None)}

    class LayerWeights(dict):
        def __init__(self, l, h):
            own, lands, _ = land(l, 0, in_flight[l][0], h)
            got = {n: jnp.where(is_mine, o[None], g) for n, o, g in zip(groups[0][1:], own[1:], lands[1:])}
            super().__init__(full_weights(cfg, got, pack_w_in_blocks(cfg, lands[0], own[0], chip_arr, name=f"pack_w_in_{l}")))
            self.layer = l

        def land_ffn(self, after, s):
            l = self.layer
            own, lands, landed = land(l, 1, in_flight.pop(l)[1], after)
            self.update(full_weights(cfg, {n: jnp.where(is_mine, o[None], g) for n, o, g in zip(groups[1], own, lands)}))
            if l + 1 == cfg.depth:
                return s
            in_flight[l + 1] = gather_start(l + 1, landed[0, 0])
            return {**s, "g_ffn_pre": s["g_ffn_pre"] + in_flight[l + 1][1][4][0, 0]}

    def layer_params(l, h):
        s = dict(g_mix_pre=g_mix_pre[l], g_mix_post=g_mix_post[l], g_ffn_pre=g_ffn_pre[l], g_ffn_post=g_ffn_post[l],
                 g_q_lat=g_q_lat[l], g_kv_lat=g_kv_lat[l], b_forget=b_forget[l], conv_w=conv_full[l])
        s["g_mix_pre"] = s["g_mix_pre"] + in_flight[l][1][4][0, 0]
        return LayerWeights(l, h), s

    half_shape = {n: (shard_shape[n][0] // 2, shard_shape[n][1]) for n in WEIGHTS}
    sums_upper = {n: jnp.zeros((cfg.depth - 1,) + half_shape[n], F32) for n in WEIGHTS}
    sums_first = {n: jnp.zeros((1,) + half_shape[n], F32) for n in WEIGHTS}
    swapping, exchanging = [], []

    def finish_exchange(after):
        l, names_, (send_sems, recv_sems, parts, lands, _) = exchanging.pop(0)
        parts, others, _ = copies_wait(send_sems, recv_sems, parts, lands, _exchange_plan(len(names_)), after,
                                       name=f"exchange_wait_{names_[0]}_{l}")
        for n, p, o in zip(names_, parts, others):
            if l == 0:
                sums_first[n] = sum_chips(p, o, chip_arr, sums_first[n], 0, name=f"sum_chips_{n}_{l}")
            else:
                sums_upper[n] = sum_chips(p, o, chip_arr, sums_upper[n], l - 1, name=f"sum_chips_{n}_{l}")

    def finish_swap(after):
        l, names_, (send_sems, recv_sems, mine, lands, _) = swapping.pop(0)
        plan = _swap_plan([g.shape[1] // 2 for g in mine])
        mine, theirs, _ = copies_wait(send_sems, recv_sems, mine, lands, plan, after, name=f"swap_wait_{names_[0]}_{l}")
        parts = [add_own_half(g, t, core_arr, name=f"add_own_half_{n}_{l}") for n, g, t in zip(names_, mine, theirs)]
        started = copies_start(parts, [(3,) + p.shape[1:] for p in parts], _exchange_plan(len(names_)), 3 * len(names_),
                               name=f"exchange_start_{names_[0]}_{l}")
        exchanging.append((l, names_, started))
        return started[4]

    def grads_done(l, dws):
        names_ = [n for n in WEIGHTS if n in dws]
        send = chip_grads(cfg, dws, f"_{l}")
        mine = [send[n] for n in names_]
        halves = [g.shape[1] // 2 for g in mine]
        started = copies_start(mine, [(N_CHIPS, h, g.shape[2]) for g, h in zip(mine, halves)], _swap_plan(halves), len(mine),
                               name=f"swap_start_{names_[0]}_{l}")
        token = started[4]
        if swapping:
            token = finish_swap(token)
            if len(exchanging) > 1:
                finish_exchange(token)
        swapping.append((l, names_, started))
        return token

    loss, grad_x, dmeta, dsmalls = local_step(cfg, x[0], loss_target[0], meta_full, layer_params, grads_done)
    share_plan = _share_plan(len(WEIGHTS))
    sharing = copies_start([sums_upper[n] for n in WEIGHTS], [sums_upper[n].shape for n in WEIGHTS], share_plan, len(WEIGHTS),
                           name="share_start_upper")
    last = finish_swap(sharing[4])
    while exchanging[0][0] > 0:
        finish_exchange(last)

    def update(own, other, first_layer, prev, tag):
        out = {}
        for n, mine_, theirs_ in zip(WEIGHTS, own, other):
            three_d = lambda a: a.reshape(cfg.depth, -1, params[n].shape[-1])
            out[n] = adamw_halves(three_d(params[n]), mine_, theirs_, core_arr, three_d(mom1[n]), three_d(mom2[n]),
                                  first_layer=first_layer, prev=prev and prev[n], name=f"adamw_{tag}_{n}")
        return out

    own_upper, other_upper, _ = copies_wait(sharing[0], sharing[1], sharing[2], sharing[3], share_plan, last, name="share_wait_upper")
    upper = update(own_upper, other_upper, 1, None, "upper")
    busy = sum(upper[n][1][1, 0, :LANES] for n in WEIGHTS)
    while exchanging:
        finish_exchange(busy)
    own_first = [sums_first[n] for n in WEIGHTS]
    done = update(own_first, share_halves(own_first, name="share_halves_first"), 0, upper, "first")
    grad, delta, new_m, new_v = ({n: done[n][k].reshape(params[n].shape) for n in WEIGHTS} for k in range(4))

    total = gather_blocks(pack_small(cfg, loss, dmeta, dsmalls), reduce=True, name="reduce_small")
    loss_sum, dmeta_sum, dsmall_sum = unpack_small(cfg, total)
    for k in _small_rows(cfg):
        grad[k] = jnp.stack([ds[k] for ds in dsmall_sum])
    grad["conv_w"] = lax.dynamic_slice_in_dim(grad["conv_w"], chip * (cfg.width // N_CHIPS), cfg.width // N_CHIPS, axis=2)
    grad["meta"] = lax.dynamic_slice_in_dim(dmeta_sum, chip * (cfg.d // N_CHIPS), cfg.d // N_CHIPS, axis=1)

    for n in names:
        if n in WEIGHTS:
            continue
        shp = params[n].shape
        two_d = lambda a: a.reshape(-1, shp[-1])
        dl, nm, nv = adamw(two_d(params[n]), two_d(grad[n]), two_d(mom1[n]), two_d(mom2[n]), name=f"adamw_{n}")
        delta[n], new_m[n], new_v[n] = dl.reshape(shp), nm.reshape(shp), nv.reshape(shp)

    return (loss_sum, grad_x[None], *[grad[n] for n in names], *[delta[n] for n in names], *[new_m[n] for n in names],
            *[new_v[n] for n in names])
```

```python
import functools
from typing import NamedTuple

import jax
import jax.numpy as jnp
from jax import lax
from jax.experimental import pallas as pl
from jax.experimental.pallas import tpu as pltpu

F32 = jnp.float32
BF16 = jnp.bfloat16
MESH = pl.DeviceIdType.MESH

EPS = 1e-6
NEG_INF = -1e30
ROPE_THETA = 10000.0
LANES = 128
ROPE = 64
N_CHIPS = 4
N_DEV = 8

ADAM_LR = 0.001
ADAM_B1 = 0.9
ADAM_B2 = 0.999
ADAM_EPS = 1e-08
ADAM_WD = 0.01
ADAM_STEP = 10

VMEM_LIMIT_BYTES = 48 * 1024 * 1024


class Cfg(NamedTuple):
    d: int = 2048
    seq: int = 2048
    depth: int = 4
    n_meta: int = 16
    heads: int = 8
    q_rank: int = 512
    kv_rank: int = 512
    d_ff: int = 5632

    @property
    def width(self):
        return self.heads * LANES

    @property
    def pad(self):
        return (-(self.n_meta + self.seq)) % LANES

    @property
    def m(self):
        return self.pad + self.n_meta + self.seq

    @property
    def nat_splits(self):
        w = self.width
        return (self.q_rank, self.kv_rank, ROPE, w, w, w, w, w, w, self.heads, 3 * self.d)

    @property
    def d_in(self):
        return sum(self.nat_splits)

    @property
    def off_cq(self):
        return 3 * self.d

    @property
    def off_ckv(self):
        return self.off_cq + self.q_rank

    @property
    def off_conv(self):
        return self.off_ckv + self.kv_rank

    @property
    def off_fox(self):
        return self.off_conv + 3 * self.width

    @property
    def off_kpe(self):
        return self.off_fox + 3 * self.width

    @property
    def off_fl(self):
        return self.off_kpe + LANES

    @property
    def d_inp(self):
        return -(-(self.off_fl + LANES) // 512) * 512


CFG = Cfg()


def _tile(n, target, mult=LANES):
    best = None
    t = mult
    while t <= min(n, target):
        if n % t == 0:
            best = t
        t += mult
    return best or n


def _cparams(*sem):
    return pltpu.CompilerParams(dimension_semantics=sem, vmem_limit_bytes=VMEM_LIMIT_BYTES)


def _cumsum(xs):
    out, s = [], 0
    for v in xs:
        s += v
        out.append(s)
    return out


def pack_w_uq(cfg, w):
    r = w.shape[0]
    w3 = w.reshape(r, cfg.heads, LANES + ROPE)
    w3 = jnp.pad(w3, ((0, 0), (0, 0), (0, LANES - ROPE)))
    return w3.reshape(r, cfg.heads * 2 * LANES)


def unpack_w_uq(cfg, wp):
    r = wp.shape[0]
    return wp.reshape(r, cfg.heads, 2 * LANES)[:, :, : LANES + ROPE].reshape(r, cfg.heads * (LANES + ROPE))


def pack_w_ukv(cfg, w):
    r = w.shape[0]
    w4 = w.reshape(r, cfg.heads, 2, LANES)
    return jnp.transpose(w4, (0, 2, 1, 3)).reshape(r, 2 * cfg.heads * LANES)


def unpack_w_ukv(cfg, wp):
    r = wp.shape[0]
    w4 = wp.reshape(r, 2, cfg.heads, LANES)
    return jnp.transpose(w4, (0, 2, 1, 3)).reshape(r, 2 * cfg.heads * LANES)


_DIMS = {
    "nn": (((1,), (0,)), ((), ())),
    "nt": (((1,), (1,)), ((), ())),
    "tn": (((0,), (0,)), ((), ())),
}


def matmul(a, b, mode, out_dtype, *, tm, tn, tk, name, chip_cols=False):
    batched = a.ndim == 3
    if mode == "nn":
        (m, kc), n = a.shape[-2:], b.shape[-1]
        a_blk, a_idx = (tm, tk), lambda i, j, k: (i, k)
        b_blk, b_idx = (tk, tn), lambda i, j, k: (k, j)
    elif mode == "nt":
        (m, kc), n = a.shape[-2:], b.shape[-2]
        a_blk, a_idx = (tm, tk), lambda i, j, k: (i, k)
        b_blk, b_idx = (tn, tk), lambda i, j, k: (j, k)
    else:
        (kc, m), n = a.shape[-2:], b.shape[-1]
        a_blk, a_idx = (tk, tm), lambda i, j, k: (k, i)
        b_blk, b_idx = (tk, tn), lambda i, j, k: (k, j)
    assert m % tm == 0 and n % tn == 0 and kc % tk == 0, (name, m, n, kc, tm, tn, tk)
    nk = kc // tk
    dims = _DIMS[mode]
    o_blk, o_idx = (tm, tn), lambda i, j, k: (i, j)
    grid = (m // tm, n // tn, nk)
    per = n // N_CHIPS // tn
    assert not chip_cols or n // N_CHIPS % tn == 0
    if batched:
        nb = a.shape[0]
        grid = (nb,) + grid
        wrap = lambda f: (lambda bb, i, j, k: (bb,) + f(i, j, k))
        a_blk, b_blk, o_blk = (None,) + a_blk, (None,) + b_blk, (None,) + o_blk
        a_idx, b_idx, o_idx = wrap(a_idx), wrap(b_idx), wrap(o_idx)
        out_shape = (nb, m, n)
        if chip_cols:
            o_blk, o_idx = (None,) + o_blk, lambda bb, i, j, k: (j // per, bb, i, j % per)
            out_shape = (N_CHIPS, nb, m, n // N_CHIPS)
        sem = ("parallel", "parallel", "parallel", "arbitrary")
    else:
        out_shape = (m, n)
        if chip_cols:
            o_blk, o_idx = (None,) + o_blk, lambda i, j, k: (j // per, i, j % per)
            out_shape = (N_CHIPS, m, n // N_CHIPS)
        sem = ("parallel", "parallel", "arbitrary")
    k_axis = len(grid) - 1

    def body(a_ref, b_ref, o_ref, *scratch):
        prod = lax.dot_general(a_ref[...], b_ref[...], dims, preferred_element_type=F32)
        if nk == 1:
            o_ref[...] = prod.astype(o_ref.dtype)
        else:
            acc_ref = scratch[0] if scratch else o_ref
            k = pl.program_id(k_axis)

            @pl.when(k == 0)
            def _():
                acc_ref[...] = prod

            @pl.when(k > 0)
            def _():
                acc_ref[...] += prod

            if scratch:
                @pl.when(k == nk - 1)
                def _():
                    o_ref[...] = acc_ref[...].astype(o_ref.dtype)

    return pl.pallas_call(
        body,
        name=name,
        out_shape=jax.ShapeDtypeStruct(out_shape, out_dtype),
        grid=grid,
        in_specs=[pl.BlockSpec(a_blk, a_idx), pl.BlockSpec(b_blk, b_idx)],
        out_specs=pl.BlockSpec(o_blk, o_idx),
        scratch_shapes=[] if nk == 1 or out_dtype == F32 else [pltpu.VMEM((tm, tn), F32)],
        compiler_params=_cparams(*sem),
    )(a, b)


def _row_tile(m):
    return _tile(m, 272, 16)


def rmsnorm_fwd(x, g, out_dtype, *, name, width=None, col_blk=0, res=None):
    m = x.shape[0]
    n = width or x.shape[1]
    tm = _row_tile(m)
    has_res = res is not None

    def body(x_ref, g_ref, *rest):
        o_ref = rest[-1]
        xf = x_ref[...].astype(F32)
        r = lax.rsqrt(jnp.mean(xf * xf, axis=-1, keepdims=True) + EPS)
        y = xf * r * g_ref[...]
        if has_res:
            y = rest[0][...] + y
        o_ref[...] = y.astype(o_ref.dtype)

    in_specs = [pl.BlockSpec((tm, n), lambda i: (i, col_blk)), pl.BlockSpec((1, n), lambda i: (0, 0))]
    args = [x, g.reshape(1, n)]
    if has_res:
        in_specs.append(pl.BlockSpec((tm, n), lambda i: (i, 0)))
        args.append(res)
    return pl.pallas_call(
        body,
        name=name,
        out_shape=jax.ShapeDtypeStruct((m, n), out_dtype),
        grid=(m // tm,),
        in_specs=in_specs,
        out_specs=pl.BlockSpec((tm, n), lambda i: (i, 0)),
        compiler_params=_cparams("parallel"),
    )(*args)


def rmsnorm_bwd(x, g, dy, out_dtype, *, name, width=None, col_blk=0, dres=None):
    m = x.shape[0]
    n = width or x.shape[1]
    tm = _row_tile(m)
    has_res = dres is not None

    def body(x_ref, g_ref, dy_ref, *rest):
        dx_ref, dg_ref = rest[-2:]
        i = pl.program_id(0)
        xf = x_ref[...].astype(F32)
        r = lax.rsqrt(jnp.mean(xf * xf, axis=-1, keepdims=True) + EPS)
        xhat = xf * r
        dyf = dy_ref[...].astype(F32)
        dxh = dyf * g_ref[...]
        dx = r * (dxh - xhat * jnp.mean(dxh * xhat, axis=-1, keepdims=True))
        if has_res:
            dx = dx + rest[0][...]
        dx_ref[...] = dx.astype(dx_ref.dtype)
        part = jnp.sum(dyf * xhat, axis=0, keepdims=True)

        @pl.when(i == 0)
        def _():
            dg_ref[...] = part

        @pl.when(i > 0)
        def _():
            dg_ref[...] += part

    in_specs = [
        pl.BlockSpec((tm, n), lambda i: (i, col_blk)),
        pl.BlockSpec((1, n), lambda i: (0, 0)),
        pl.BlockSpec((tm, n), lambda i: (i, 0)),
    ]
    args = [x, g.reshape(1, n), dy]
    if has_res:
        in_specs.append(pl.BlockSpec((tm, n), lambda i: (i, 0)))
        args.append(dres)
    return pl.pallas_call(
        body,
        name=name,
        out_shape=(jax.ShapeDtypeStruct((m, n), out_dtype), jax.ShapeDtypeStruct((1, n), F32)),
        grid=(m // tm,),
        in_specs=in_specs,
        out_specs=(pl.BlockSpec((tm, n), lambda i: (i, 0)), pl.BlockSpec((1, n), lambda i: (0, 0))),
        compiler_params=_cparams("arbitrary"),
    )(*args)


_NT = (((1,), (1,)), ((), ()))
_NN = (((1,), (0,)), ((), ()))
_TN = (((0,), (0,)), ((), ()))


def _attn_scores(q, k, scale, decay_refs, i, tq, kn, pad):
    s = lax.dot_general(q, k, _NT, preferred_element_type=F32) * scale
    if decay_refs is not None:
        cq_ref, ck_ref = decay_refs
        s = s + (cq_ref[0] - ck_ref[0][:, :kn])
    t_idx = i * tq + lax.broadcasted_iota(jnp.int32, (tq, 1), 0)
    s_idx = lax.broadcasted_iota(jnp.int32, (1, kn), 1)
    mask = (s_idx <= t_idx) & (s_idx >= pad)
    return s, mask, t_idx


def _keys_needed(i, tq, m):
    return min(m, -(-((i + 1) * tq) // LANES) * LANES)


def attn_fwd(q, k, v, *, heads, dk, dv, qblk0, kblk0, vblk0, scale, pad, slot, branches=None, decay=None, name):
    m = q.shape[0]
    tq = _row_tile(m)
    has_decay = decay is not None

    def body(q_ref, k_ref, v_ref, *rest):
        o_ref, lse_ref = rest[-2:]
        decay_refs = rest[:2] if has_decay else None

        def block(i):
            kn = _keys_needed(i, tq, m)
            s, mask, t_idx = _attn_scores(q_ref[...], k_ref[0:kn, :], scale, decay_refs, i, tq, kn, pad)
            s = jnp.where(mask, s, NEG_INF)
            mx = jnp.max(s, axis=1, keepdims=True)
            p = jnp.exp(s - mx)
            l = jnp.sum(p, axis=1, keepdims=True)
            o = lax.dot_general(p.astype(BF16), v_ref[0:kn, :], _NN, preferred_element_type=F32) / l
            o_ref[...] = jnp.where(t_idx >= pad, o, 0.0).astype(o_ref.dtype)
            lse_ref[0] = mx + jnp.log(l)

        for i in range(m // tq):
            pl.when(pl.program_id(1) == i)(functools.partial(block, i))

    in_specs = [
        pl.BlockSpec((tq, dk), lambda h, i: (i, qblk0 + h)),
        pl.BlockSpec((m, dk), lambda h, i: (0, kblk0 + h)),
        pl.BlockSpec((m, dv), lambda h, i: (0, vblk0 + h)),
    ]
    args = [q, k, v]
    if has_decay:
        in_specs += [pl.BlockSpec((1, tq, 1), lambda h, i: (h, i, 0)), pl.BlockSpec((1, 1, m), lambda h, i: (h, 0, 0))]
        args += list(decay)
    aliases = {}
    if branches is not None:
        aliases = {len(args): 0}
        in_specs.append(pl.BlockSpec(memory_space=pl.ANY))
        args.append(branches)
    return pl.pallas_call(
        body,
        name=name,
        out_shape=(jax.ShapeDtypeStruct((3, m, heads * dv), BF16), jax.ShapeDtypeStruct((heads, m, 1), F32)),
        grid=(heads, m // tq),
        in_specs=in_specs,
        out_specs=(pl.BlockSpec((None, tq, dv), lambda h, i: (slot, i, h)), pl.BlockSpec((1, tq, 1), lambda h, i: (h, i, 0))),
        input_output_aliases=aliases,
        compiler_params=_cparams("parallel", "parallel"),
    )(*args)


def attn_bwd(q, k, v, do, do_sel, lse, *, heads, dk, dv, qblk0, kblk0, vblk0, scale, pad, decay=None, name):
    m = q.shape[0]
    tq = _row_tile(m)
    nq = m // tq
    has_decay = decay is not None

    def body(q_ref, k_ref, v_ref, do_ref, lse_ref, *rest):
        if has_decay:
            cq_ref, ck_ref, dq_ref, dk_ref, dv_ref, dck_ref, dk_acc, dv_acc = rest
            decay_refs = (cq_ref, ck_ref)
        else:
            dq_ref, dk_ref, dv_ref, dk_acc, dv_acc = rest
            decay_refs = None
        @pl.when(pl.program_id(1) == 0)
        def _():
            dk_acc[...] = jnp.zeros_like(dk_acc)
            dv_acc[...] = jnp.zeros_like(dv_acc)
            if has_decay:
                dck_ref[...] = jnp.zeros_like(dck_ref)

        def block(i):
            kn = _keys_needed(i, tq, m)
            qb, kb, dob = q_ref[...], k_ref[0:kn, :], do_ref[...]
            s, mask, _ = _attn_scores(qb, kb, scale, decay_refs, i, tq, kn, pad)
            p = jnp.where(mask, jnp.exp(s - lse_ref[0]), 0.0)
            dp = lax.dot_general(dob, v_ref[0:kn, :], _NT, preferred_element_type=F32)
            ds = p * (dp - jnp.sum(p * dp, axis=1, keepdims=True))
            dsb = ds.astype(BF16)
            dq_ref[...] = (lax.dot_general(dsb, kb, _NN, preferred_element_type=F32) * scale).astype(dq_ref.dtype)
            dk_acc[0:kn, :] += lax.dot_general(dsb, qb, _TN, preferred_element_type=F32) * scale
            dv_acc[0:kn, :] += lax.dot_general(p.astype(BF16), dob, _TN, preferred_element_type=F32)
            if has_decay:
                dck_ref[0, :, 0:kn] -= jnp.sum(ds, axis=0, keepdims=True)

        for i in range(nq):
            pl.when(pl.program_id(1) == i)(functools.partial(block, i))

        @pl.when(pl.program_id(1) == nq - 1)
        def _():
            dk_ref[...] = dk_acc[...].astype(dk_ref.dtype)
            dv_ref[...] = dv_acc[...].astype(dv_ref.dtype)

    in_specs = [
        pl.BlockSpec((tq, dk), lambda h, i: (i, qblk0 + h)),
        pl.BlockSpec((m, dk), lambda h, i: (0, kblk0 + h)),
        pl.BlockSpec((m, dv), lambda h, i: (0, vblk0 + h)),
        pl.BlockSpec((None, tq, dv), lambda h, i: (do_sel, i, h)),
        pl.BlockSpec((1, tq, 1), lambda h, i: (h, i, 0)),
    ]
    args = [q, k, v, do, lse]
    out_shape = [
        jax.ShapeDtypeStruct((m, heads * dk), BF16),
        jax.ShapeDtypeStruct((m, heads * dk), BF16),
        jax.ShapeDtypeStruct((m, heads * dv), BF16),
    ]
    out_specs = [
        pl.BlockSpec((tq, dk), lambda h, i: (i, h)),
        pl.BlockSpec((m, dk), lambda h, i: (0, h)),
        pl.BlockSpec((m, dv), lambda h, i: (0, h)),
    ]
    if has_decay:
        in_specs += [pl.BlockSpec((1, tq, 1), lambda h, i: (h, i, 0)), pl.BlockSpec((1, 1, m), lambda h, i: (h, 0, 0))]
        args += list(decay)
        out_shape.append(jax.ShapeDtypeStruct((heads, 1, m), F32))
        out_specs.append(pl.BlockSpec((1, 1, m), lambda h, i: (h, 0, 0)))
    return pl.pallas_call(
        body,
        name=name,
        out_shape=tuple(out_shape),
        grid=(heads, nq),
        in_specs=in_specs,
        out_specs=tuple(out_specs),
        scratch_shapes=[pltpu.VMEM((m, dk), F32), pltpu.VMEM((m, dv), F32)],
        compiler_params=_cparams("parallel", "arbitrary"),
    )(*args)


def rope_tables(cfg):
    half = ROPE // 2
    inv_freq = 1.0 / (ROPE_THETA ** (jnp.arange(0, ROPE, 2, dtype=F32) / ROPE))
    pos = (jnp.arange(cfg.m, dtype=jnp.int32) - cfg.pad).astype(F32)
    ang = pos[:, None] * inv_freq[None, :]
    cos, sin = jnp.cos(ang), jnp.sin(ang)
    z = jnp.zeros((cfg.m, half), F32)
    zz = jnp.zeros((cfg.m, LANES - ROPE), F32)
    return (
        jnp.concatenate([cos, cos, zz], axis=1),
        jnp.concatenate([-sin, z, zz], axis=1),
        jnp.concatenate([z, sin, zz], axis=1),
    )


def _rope(x, cos, s1, s2):
    return x * cos + pltpu.roll(x, LANES - ROPE // 2, 1) * s1 + pltpu.roll(x, ROPE // 2, 1) * s2


def mla_prep_fwd(cfg, q, kv, proj, tabs, *, name):
    m, h2 = cfg.m, 2 * LANES
    tm = _tile(m, 544, 16)
    kpe_blk = cfg.off_kpe // LANES

    def body(q_ref, kn_ref, kpe_ref, cos_ref, s1_ref, s2_ref, qf_ref, kf_ref):
        cos, s1, s2 = cos_ref[...], s1_ref[...], s2_ref[...]
        qv = q_ref[...]
        qf_ref[:, :LANES] = qv[:, :LANES]
        qf_ref[:, LANES:] = _rope(qv[:, LANES:].astype(F32), cos, s1, s2).astype(qf_ref.dtype)
        kf_ref[:, :LANES] = kn_ref[...]
        kf_ref[:, LANES:] = _rope(kpe_ref[...].astype(F32), cos, s1, s2).astype(kf_ref.dtype)

    tab = pl.BlockSpec((tm, LANES), lambda i, h: (i, 0))
    return pl.pallas_call(
        body,
        name=name,
        out_shape=(jax.ShapeDtypeStruct((m, cfg.heads * h2), BF16), jax.ShapeDtypeStruct((m, cfg.heads * h2), BF16)),
        grid=(m // tm, cfg.heads),
        in_specs=[
            pl.BlockSpec((tm, h2), lambda i, h: (i, h)),
            pl.BlockSpec((tm, LANES), lambda i, h: (i, h)),
            pl.BlockSpec((tm, LANES), lambda i, h: (i, kpe_blk)),
            tab, tab, tab,
        ],
        out_specs=(pl.BlockSpec((tm, h2), lambda i, h: (i, h)), pl.BlockSpec((tm, h2), lambda i, h: (i, h))),
        compiler_params=_cparams("parallel", "parallel"),
    )(q, kv, proj, *tabs)


def mla_prep_bwd(cfg, dqf, dkf, tabs_t, *, name):
    m, h2 = cfg.m, 2 * LANES
    tm = _tile(m, 544, 16)

    def body(dqf_ref, dkf_ref, cos_ref, s1_ref, s2_ref, dq_ref, dkn_ref, dkpe_ref):
        h = pl.program_id(1)
        cos, s1, s2 = cos_ref[...], s1_ref[...], s2_ref[...]
        dqv, dkv = dqf_ref[...], dkf_ref[...]
        dq_ref[:, :LANES] = dqv[:, :LANES]
        dq_ref[:, LANES:] = _rope(dqv[:, LANES:].astype(F32), cos, s1, s2).astype(dq_ref.dtype)
        dkn_ref[...] = dkv[:, :LANES]
        part = _rope(dkv[:, LANES:].astype(F32), cos, s1, s2)

        @pl.when(h == 0)
        def _():
            dkpe_ref[...] = part

        @pl.when(h > 0)
        def _():
            dkpe_ref[...] += part

    tab = pl.BlockSpec((tm, LANES), lambda i, h: (i, 0))
    return pl.pallas_call(
        body,
        name=name,
        out_shape=(
            jax.ShapeDtypeStruct((m, cfg.heads * h2), BF16),
            jax.ShapeDtypeStruct((m, cfg.heads * LANES), BF16),
            jax.ShapeDtypeStruct((m, LANES), F32),
        ),
        grid=(m // tm, cfg.heads),
        in_specs=[pl.BlockSpec((tm, h2), lambda i, h: (i, h)), pl.BlockSpec((tm, h2), lambda i, h: (i, h)), tab, tab, tab],
        out_specs=(
            pl.BlockSpec((tm, h2), lambda i, h: (i, h)),
            pl.BlockSpec((tm, LANES), lambda i, h: (i, h)),
            pl.BlockSpec((tm, LANES), lambda i, h: (i, 0)),
        ),
        compiler_params=_cparams("parallel", "arbitrary"),
    )(dqf, dkf, *tabs_t)


def _conv_parts(b_ref, c_ref, x_ref, w_ref, m):
    b, c, x = b_ref[...].astype(F32), c_ref[...].astype(F32), x_ref[...].astype(F32)
    u = c * x
    row = lax.broadcasted_iota(jnp.int32, (m, 1), 0)
    u1 = jnp.where(row >= 1, pltpu.roll(u, 1, 0), 0.0)
    u2 = jnp.where(row >= 2, pltpu.roll(u, 2, 0), 0.0)
    w0, w1, w2 = w_ref[0:1, :], w_ref[1:2, :], w_ref[2:3, :]
    uc = w0 * u2 + w1 * u1 + w2 * u
    return b, c, x, u, u1, u2, uc, (w0, w1, w2), row


def _conv_specs(cfg, tn):
    m, nb, blk0 = cfg.m, cfg.width // tn, cfg.off_conv // tn
    return [
        pl.BlockSpec((m, tn), lambda j: (0, blk0 + j)),
        pl.BlockSpec((m, tn), lambda j: (0, blk0 + nb + j)),
        pl.BlockSpec((m, tn), lambda j: (0, blk0 + 2 * nb + j)),
        pl.BlockSpec((3, tn), lambda j: (0, j)),
    ]


def conv_fwd(cfg, proj, conv_w, branches, slot, *, name):
    m, tn = cfg.m, LANES

    def body(b_ref, c_ref, x_ref, w_ref, kept_ref, o_ref):
        b, _, _, _, _, _, uc, _, _ = _conv_parts(b_ref, c_ref, x_ref, w_ref, m)
        o_ref[...] = (b * uc).astype(o_ref.dtype)

    return pl.pallas_call(
        body,
        name=name,
        out_shape=jax.ShapeDtypeStruct((3, m, cfg.width), BF16),
        grid=(cfg.width // tn,),
        in_specs=_conv_specs(cfg, tn) + [pl.BlockSpec(memory_space=pl.ANY)],
        out_specs=pl.BlockSpec((None, m, tn), lambda j: (slot, 0, j)),
        input_output_aliases={4: 0},
        compiler_params=_cparams("parallel"),
    )(proj, proj, proj, conv_w, branches)


def conv_bwd(cfg, proj, conv_w, do, do_sel, *, name):
    m, tn = cfg.m, LANES

    def body(b_ref, c_ref, x_ref, w_ref, do_ref, db_ref, dc_ref, dx_ref, dw_ref):
        b, c, x, u, u1, u2, uc, (w0, w1, w2), row = _conv_parts(b_ref, c_ref, x_ref, w_ref, m)
        dob = do_ref[...].astype(F32)
        db_ref[...] = (dob * uc).astype(db_ref.dtype)
        duc = dob * b
        up1 = jnp.where(row <= m - 2, pltpu.roll(duc, m - 1, 0), 0.0)
        up2 = jnp.where(row <= m - 3, pltpu.roll(duc, m - 2, 0), 0.0)
        du = w2 * duc + w1 * up1 + w0 * up2
        dc_ref[...] = (du * x).astype(dc_ref.dtype)
        dx_ref[...] = (du * c).astype(dx_ref.dtype)
        dw_ref[0:1, :] = jnp.sum(duc * u2, axis=0, keepdims=True)
        dw_ref[1:2, :] = jnp.sum(duc * u1, axis=0, keepdims=True)
        dw_ref[2:3, :] = jnp.sum(duc * u, axis=0, keepdims=True)

    act = jax.ShapeDtypeStruct((m, cfg.width), BF16)
    blk = pl.BlockSpec((m, tn), lambda j: (0, j))
    return pl.pallas_call(
        body,
        name=name,
        out_shape=(act, act, act, jax.ShapeDtypeStruct((3, cfg.width), F32)),
        grid=(cfg.width // tn,),
        in_specs=_conv_specs(cfg, tn) + [pl.BlockSpec((None, m, tn), lambda j: (do_sel, 0, j))],
        out_specs=(blk, blk, blk, pl.BlockSpec((3, tn), lambda j: (0, j))),
        compiler_params=_cparams("parallel"),
    )(proj, proj, proj, conv_w, do)


def _tri(lower):
    r = lax.broadcasted_iota(jnp.int32, (LANES, LANES), 0)
    c = lax.broadcasted_iota(jnp.int32, (LANES, LANES), 1)
    return jnp.where((r >= c) if lower else (r <= c), 1.0, 0.0).astype(F32)


def fox_gate_fwd(cfg, fl, b_pad, *, name):
    m = cfg.m
    nblk = m // LANES

    def body(fl_ref, b_ref, c_ref):
        z = fl_ref[...] + b_ref[...]
        logf = jnp.minimum(z, 0.0) - jnp.log(1.0 + jnp.exp(-jnp.abs(z)))
        row = lax.broadcasted_iota(jnp.int32, (m, 1), 0)
        logf = jnp.where(row >= cfg.pad, logf, 0.0)
        tri = _tri(True)
        carry = jnp.zeros((1, LANES), F32)
        for blk in range(nblk):
            cb = jnp.dot(tri, logf[blk * LANES:(blk + 1) * LANES, :], precision=lax.Precision.HIGHEST,
                         preferred_element_type=F32) + carry
            c_ref[blk * LANES:(blk + 1) * LANES, :] = cb
            carry = cb[LANES - 1:LANES, :]

    full = pl.BlockSpec((m, LANES), lambda: (0, 0))
    return pl.pallas_call(
        body,
        name=name,
        out_shape=jax.ShapeDtypeStruct((m, LANES), F32),
        in_specs=[full, pl.BlockSpec((1, LANES), lambda: (0, 0))],
        out_specs=full,
        compiler_params=pltpu.CompilerParams(vmem_limit_bytes=VMEM_LIMIT_BYTES),
    )(fl, b_pad)


def fox_gate_bwd(cfg, fl, b_pad, dc, *, name):
    m = cfg.m
    nblk = m // LANES

    def body(fl_ref, b_ref, dc_ref, dfl_ref, db_ref):
        z = fl_ref[...] + b_ref[...]
        dlogsig = 1.0 / (1.0 + jnp.exp(z))
        row = lax.broadcasted_iota(jnp.int32, (m, 1), 0)
        gate = jnp.where(row >= cfg.pad, dlogsig, 0.0)
        dcv = dc_ref[...]
        tri = _tri(False)
        carry = jnp.zeros((1, LANES), F32)
        db = jnp.zeros((1, LANES), F32)
        for blk in reversed(range(nblk)):
            sl = slice(blk * LANES, (blk + 1) * LANES)
            rb = jnp.dot(tri, dcv[sl, :], precision=lax.Precision.HIGHEST, preferred_element_type=F32) + carry
            carry = rb[0:1, :]
            dfl = rb * gate[sl, :]
            dfl_ref[sl, :] = dfl
            db = db + jnp.sum(dfl, axis=0, keepdims=True)
        db_ref[...] = db

    full = pl.BlockSpec((m, LANES), lambda: (0, 0))
    one = pl.BlockSpec((1, LANES), lambda: (0, 0))
    return pl.pallas_call(
        body,
        name=name,
        out_shape=(jax.ShapeDtypeStruct((m, LANES), F32), jax.ShapeDtypeStruct((1, LANES), F32)),
        in_specs=[full, one, full],
        out_specs=(full, one),
        compiler_params=pltpu.CompilerParams(vmem_limit_bytes=VMEM_LIMIT_BYTES),
    )(fl, b_pad, dc)


def _sigmoid(x):
    return 1.0 / (1.0 + jnp.exp(-x))


def gate_merge_fwd(cfg, y, proj, *, name):
    m, d = cfg.m, cfg.d
    tm, tn = _tile(m, 1088, 16), _tile(d, 512)
    nd = d // tn

    def body(y_ref, g0_ref, g1_ref, g2_ref, o_ref):
        acc = None
        for n, g_ref in enumerate((g0_ref, g1_ref, g2_ref)):
            t = _sigmoid(g_ref[...].astype(F32)) * y_ref[n].astype(F32)
            acc = t if acc is None else acc + t
        o_ref[...] = acc.astype(o_ref.dtype)

    gate = lambda n: pl.BlockSpec((tm, tn), lambda i, j: (i, n * nd + j))
    return pl.pallas_call(
        body,
        name=name,
        out_shape=jax.ShapeDtypeStruct((m, d), BF16),
        grid=(m // tm, nd),
        in_specs=[pl.BlockSpec((3, tm, tn), lambda i, j: (0, i, j)), gate(0), gate(1), gate(2)],
        out_specs=pl.BlockSpec((tm, tn), lambda i, j: (i, j)),
        compiler_params=_cparams("parallel", "parallel"),
    )(y, proj, proj, proj)


def gate_merge_bwd(cfg, dm, y, proj, *, name):
    m, d = cfg.m, cfg.d
    tm, tn = _tile(m, 1088, 16), _tile(d, 512)
    nd = d // tn

    def body(dm_ref, y_ref, g_ref, dy_ref, dg_ref):
        sg = _sigmoid(g_ref[...].astype(F32))
        dmv = dm_ref[...].astype(F32)
        dy_ref[...] = (sg * dmv).astype(dy_ref.dtype)
        dg_ref[...] = (dmv * y_ref[...].astype(F32) * sg * (1.0 - sg)).astype(dg_ref.dtype)

    return pl.pallas_call(
        body,
        name=name,
        out_shape=(jax.ShapeDtypeStruct((3, m, d), BF16), jax.ShapeDtypeStruct((m, 3 * d), BF16)),
        grid=(m // tm, nd, 3),
        in_specs=[
            pl.BlockSpec((tm, tn), lambda i, j, n: (i, j)),
            pl.BlockSpec((None, tm, tn), lambda i, j, n: (n, i, j)),
            pl.BlockSpec((tm, tn), lambda i, j, n: (i, n * nd + j)),
        ],
        out_specs=(
            pl.BlockSpec((None, tm, tn), lambda i, j, n: (n, i, j)),
            pl.BlockSpec((tm, tn), lambda i, j, n: (i, n * nd + j)),
        ),
        compiler_params=_cparams("parallel", "parallel", "parallel"),
    )(dm, y, proj)


def swiglu_fwd(cfg, gu, *, name):
    m, f = cfg.m, cfg.d_ff
    tm, tn = _tile(m, 1088, 16), _tile(f, 512)
    nf = f // tn

    def body(g_ref, u_ref, o_ref):
        g = g_ref[...].astype(F32)
        o_ref[...] = (g * _sigmoid(g) * u_ref[...].astype(F32)).astype(o_ref.dtype)

    return pl.pallas_call(
        body,
        name=name,
        out_shape=jax.ShapeDtypeStruct((m, f), BF16),
        grid=(m // tm, nf),
        in_specs=[pl.BlockSpec((tm, tn), lambda i, j: (i, j)), pl.BlockSpec((tm, tn), lambda i, j: (i, nf + j))],
        out_specs=pl.BlockSpec((tm, tn), lambda i, j: (i, j)),
        compiler_params=_cparams("parallel", "parallel"),
    )(gu, gu)


def swiglu_bwd(cfg, dact, gu, *, name):
    m, f = cfg.m, cfg.d_ff
    tm, tn = _tile(m, 1088, 16), _tile(f, 512)
    nf = f // tn

    def body(da_ref, g_ref, u_ref, o_ref):
        j = pl.program_id(1)
        g, u, da = g_ref[...].astype(F32), u_ref[...].astype(F32), da_ref[...].astype(F32)
        sg = _sigmoid(g)
        dg = da * u * sg * (1.0 + g * (1.0 - sg))
        du = da * g * sg
        o_ref[...] = jnp.where(j < nf, dg, du).astype(o_ref.dtype)

    return pl.pallas_call(
        body,
        name=name,
        out_shape=jax.ShapeDtypeStruct((m, 2 * f), BF16),
        grid=(m // tm, 2 * nf),
        in_specs=[
            pl.BlockSpec((tm, tn), lambda i, j: (i, j % nf)),
            pl.BlockSpec((tm, tn), lambda i, j: (i, j % nf)),
            pl.BlockSpec((tm, tn), lambda i, j: (i, nf + j % nf)),
        ],
        out_specs=pl.BlockSpec((tm, tn), lambda i, j: (i, j)),
        compiler_params=_cparams("parallel", "parallel"),
    )(dact, gu, gu)


def loss_head(cfg, h, target, *, name):
    m, d = cfg.m, cfg.d
    assert cfg.pad + cfg.n_meta == LANES
    tm = LANES
    inv_d = 1.0 / d

    def body(h_ref, t_ref, dh_ref, loss_ref):
        i = pl.program_id(0)

        @pl.when(i == 0)
        def _():
            dh_ref[...] = jnp.zeros_like(dh_ref)
            loss_ref[...] = jnp.zeros_like(loss_ref)

        @pl.when(i > 0)
        def _():
            err = h_ref[...] - t_ref[...]
            dh_ref[...] = err * inv_d
            loss_ref[...] += 0.5 * inv_d * jnp.sum(err * err)

    return pl.pallas_call(
        body,
        name=name,
        out_shape=(jax.ShapeDtypeStruct((m, d), F32), jax.ShapeDtypeStruct((8, LANES), F32)),
        grid=(m // tm,),
        in_specs=[pl.BlockSpec((tm, d), lambda i: (i, 0)), pl.BlockSpec((tm, d), lambda i: (jnp.maximum(i - 1, 0), 0))],
        out_specs=(pl.BlockSpec((tm, d), lambda i: (i, 0)), pl.BlockSpec((8, LANES), lambda i: (0, 0))),
        compiler_params=_cparams("arbitrary"),
    )(h, target)


def adamw(w, g, m_, v_, *, name):
    r, c = w.shape
    c_pad = -(-c // LANES) * LANES
    tr = r
    if r % 8 == 0:
        tr = _tile(r, max(8, (3 << 19) // (4 * c_pad) // 8 * 8), 8)
    bc1 = 1.0 - ADAM_B1 ** ADAM_STEP
    bc2 = 1.0 - ADAM_B2 ** ADAM_STEP

    def body(w_ref, g_ref, m_ref, v_ref, d_ref, nm_ref, nv_ref):
        gv = g_ref[...]
        nm = ADAM_B1 * m_ref[...] + (1.0 - ADAM_B1) * gv
        nv = ADAM_B2 * v_ref[...] + (1.0 - ADAM_B2) * (gv * gv)
        d_ref[...] = -ADAM_LR * ((nm / bc1) / (jnp.sqrt(nv / bc2) + ADAM_EPS) + ADAM_WD * w_ref[...])
        nm_ref[...] = nm
        nv_ref[...] = nv

    blk = pl.BlockSpec((tr, c), lambda i: (i, 0))
    shp = jax.ShapeDtypeStruct((r, c), F32)
    return pl.pallas_call(
        body,
        name=name,
        out_shape=(shp, shp, shp),
        grid=(r // tr,),
        in_specs=[blk, blk, blk, blk],
        out_specs=(blk, blk, blk),
        compiler_params=_cparams("parallel"),
    )(w, g, m_, v_)


def adamw_halves(w, g_own, g_other, core, m_, v_, *, first_layer=0, prev=None, name):
    _, r, c = w.shape
    nl = g_own.shape[0]
    r2 = r // 2
    c_pad = -(-c // LANES) * LANES
    tr = _tile(r2, max(8, (3 << 19) // (4 * c_pad) // 8 * 8), 8)
    nr = r2 // tr
    bc1 = 1.0 - ADAM_B1 ** ADAM_STEP
    bc2 = 1.0 - ADAM_B2 ** ADAM_STEP

    def body(core_ref, w_ref, go_ref, gr_ref, m_ref, v_ref, *rest):
        g_ref, d_ref, nm_ref, nv_ref = rest[-4:]
        gv = jnp.where(pl.program_id(2) == core_ref[0], go_ref[...], gr_ref[...])[:, :c]
        nm = ADAM_B1 * m_ref[...] + (1.0 - ADAM_B1) * gv
        nv = ADAM_B2 * v_ref[...] + (1.0 - ADAM_B2) * (gv * gv)
        d_ref[...] = -ADAM_LR * ((nm / bc1) / (jnp.sqrt(nv / bc2) + ADAM_EPS) + ADAM_WD * w_ref[...])
        g_ref[...] = gv
        nm_ref[...] = nm
        nv_ref[...] = nv

    full = pl.BlockSpec((None, tr, c), lambda l, i, hf, core_ref: (first_layer + l, hf * nr + i, 0))
    half = pl.BlockSpec((None, tr, g_own.shape[2]), lambda l, i, hf, core_ref: (l, i, 0))
    shp = jax.ShapeDtypeStruct(w.shape, F32)
    kept = list(prev or ())
    return pl.pallas_call(
        body,
        name=name,
        out_shape=(shp, shp, shp, shp),
        grid_spec=pltpu.PrefetchScalarGridSpec(
            num_scalar_prefetch=1,
            grid=(nl, nr, 2),
            in_specs=[full, half, half, full, full] + [pl.BlockSpec(memory_space=pl.ANY)] * len(kept),
            out_specs=(full, full, full, full),
        ),
        input_output_aliases={6 + k: k for k in range(len(kept))},
        compiler_params=_cparams("parallel", "parallel", "arbitrary"),
    )(core, w, g_own, g_other, m_, v_, *kept)


_HBM = pl.BlockSpec(memory_space=pltpu.HBM)


def _place():
    x, y, c = lax.axis_index("x"), lax.axis_index("y"), lax.axis_index("c")
    flips = [(1 - x, y), (x, 1 - y), (1 - x, 1 - y)]
    return x, y, c, flips


_SEM = pl.BlockSpec(memory_space=pltpu.SEMAPHORE)
_EFFECT = pltpu.SideEffectType.DATAFLOW_SIDE_EFFECTING


def _gather_plan(halves):
    def plan(src_refs, land_refs, arrival):
        x, y, c, flips = _place()
        mine = 2 * x + y
        out = []
        for w, h in enumerate(halves):
            for fx, fy in flips:
                slot = (2 * fx + fy) if arrival else mine
                out.append((src_refs[w].at[pl.ds(c * h, h), :], land_refs[w].at[slot, pl.ds(c * h, h), :], (fx, fy, c)))
        return out
    return plan


def _exchange_plan(nw):
    def plan(src_refs, land_refs, arrival):
        _, _, c, flips = _place()
        return [(src_refs[w].at[2 * fx + fy], land_refs[w].at[k], (fx, fy, c)) for w in range(nw) for k, (fx, fy) in enumerate(flips)]
    return plan


def _swap_plan(halves):
    def plan(src_refs, land_refs, arrival):
        x, y, c, _ = _place()
        return [(src_refs[w].at[:, pl.ds((1 - c) * h, h), :], land_refs[w], (x, y, 1 - c)) for w, h in enumerate(halves)]
    return plan


def _share_plan(nw):
    def plan(src_refs, land_refs, arrival):
        x, y, c, _ = _place()
        return [(src_refs[w], land_refs[w], (x, y, 1 - c)) for w in range(nw)]
    return plan


def copies_start(srcs, land_shapes, plan, n_copies, *, name):
    lands = [lax.empty(s, a.dtype) for s, a in zip(land_shapes, srcs)]
    n_in = len(srcs) + len(lands)

    def body(*refs):
        src_refs, land_refs = refs[:len(srcs)], refs[len(srcs):n_in]
        send_sems, recv_sems, token = refs[n_in], refs[n_in + 1], refs[-1]
        for i, (src, dst, to) in enumerate(plan(src_refs, land_refs, False)):
            pltpu.make_async_remote_copy(src_ref=src, dst_ref=dst, send_sem=send_sems.at[i], recv_sem=recv_sems.at[i],
                                         device_id=to, device_id_type=MESH).start()
        token[...] = jnp.zeros_like(token)

    operands = list(srcs) + lands
    out = pl.pallas_call(
        body,
        name=name,
        out_shape=(pltpu.SemaphoreType.DMA((n_copies,)), pltpu.SemaphoreType.DMA((n_copies,)),
                   *[pltpu.HBM(a.shape, a.dtype) for a in operands], jax.ShapeDtypeStruct((8, LANES), F32)),
        in_specs=[_HBM] * n_in,
        out_specs=(_SEM, _SEM, *[_HBM] * n_in, pl.BlockSpec(memory_space=pltpu.VMEM)),
        input_output_aliases={i: 2 + i for i in range(n_in)},
        compiler_params=pltpu.CompilerParams(has_side_effects=_EFFECT),
    )(*[pltpu.with_memory_space_constraint(a, pltpu.HBM) for a in operands])
    return out[0], out[1], list(out[2:2 + len(srcs)]), list(out[2 + len(srcs):2 + n_in]), out[-1]


def copies_wait(send_sems, recv_sems, srcs, lands, plan, after, *, name):
    n_in = len(srcs) + len(lands)

    def body(*refs):
        src_refs, land_refs = refs[:len(srcs)], refs[len(srcs):n_in]
        send_ref, recv_ref, token = refs[n_in], refs[n_in + 1], refs[-1]
        token[...] = jnp.zeros_like(token)
        for i, (src, dst, to) in enumerate(plan(src_refs, land_refs, True)):
            copy = pltpu.make_async_remote_copy(src_ref=src, dst_ref=dst, send_sem=send_ref.at[i], recv_sem=recv_ref.at[i],
                                                device_id=to, device_id_type=MESH)
            copy.wait_send()
            copy.wait_recv()

    operands = list(srcs) + list(lands)
    out = pl.pallas_call(
        body,
        name=name,
        out_shape=(*[pltpu.HBM(a.shape, a.dtype) for a in operands], jax.ShapeDtypeStruct((8, LANES), F32)),
        in_specs=[_HBM] * n_in + [_SEM, _SEM, pl.BlockSpec(memory_space=pl.ANY)],
        out_specs=(*[_HBM] * n_in, pl.BlockSpec(memory_space=pltpu.VMEM)),
        input_output_aliases={i: i for i in range(n_in)},
        compiler_params=pltpu.CompilerParams(has_side_effects=_EFFECT),
    )(*operands, send_sems, recv_sems, after)
    return list(out[:len(srcs)]), list(out[len(srcs):n_in]), out[-1]


def forward_halves(lands, *, name):
    nw = len(lands)
    halves = [a.shape[1] // 2 for a in lands]

    def body(*refs):
        ins, outs = refs[:nw], refs[nw:2 * nw]
        send_sems, recv_sems = refs[2 * nw:]
        x, y, c, flips = _place()
        copies = []
        for w, h in enumerate(halves):
            for k, (fx, fy) in enumerate(flips):
                rows = (2 * fx + fy, pl.ds(c * h, h), slice(None))
                copies.append(pltpu.make_async_remote_copy(src_ref=ins[w].at[rows], dst_ref=outs[w].at[rows], send_sem=send_sems.at[3 * w + k],
                                                           recv_sem=recv_sems.at[3 * w + k], device_id=(x, y, 1 - c), device_id_type=MESH))
        for cp in copies:
            cp.start()
        for cp in copies:
            cp.wait()

    return pl.pallas_call(
        body,
        name=name,
        out_shape=tuple(jax.ShapeDtypeStruct(a.shape, a.dtype) for a in lands),
        in_specs=[_HBM] * nw,
        out_specs=tuple([_HBM] * nw),
        input_output_aliases={w: w for w in range(nw)},
        scratch_shapes=[pltpu.SemaphoreType.DMA((3 * nw,)), pltpu.SemaphoreType.DMA((3 * nw,))],
    )(*lands)


def share_halves(sums, *, name):
    nw = len(sums)

    def body(*refs):
        ins, outs = refs[:nw], refs[nw:2 * nw]
        send_sems, recv_sems = refs[2 * nw:]
        x, y, c, _ = _place()
        copies = [
            pltpu.make_async_remote_copy(src_ref=ins[w], dst_ref=outs[w], send_sem=send_sems.at[w], recv_sem=recv_sems.at[w],
                                         device_id=(x, y, 1 - c), device_id_type=MESH)
            for w in range(nw)
        ]
        for cp in copies:
            cp.start()
        for cp in copies:
            cp.wait()

    return pl.pallas_call(
        body,
        name=name,
        out_shape=tuple(jax.ShapeDtypeStruct(s.shape, s.dtype) for s in sums),
        in_specs=[_HBM] * nw,
        out_specs=tuple([_HBM] * nw),
        scratch_shapes=[pltpu.SemaphoreType.DMA((nw,)), pltpu.SemaphoreType.DMA((nw,))],
    )(*sums)


def gather_blocks(block, *, reduce, name):
    rows, cols = block.shape

    def body(x_ref, out_ref, *rest):
        if reduce:
            buf_ref, send_sems, recv_sems = rest
        else:
            send_sems, recv_sems = rest
            buf_ref = out_ref
        x, y, c, flips = _place()
        me, sibling = (x, y, c), (x, y, 1 - c)

        def slot(px, py, pc):
            return buf_ref.at[4 * px + 2 * py + pc]

        def copy(k, blk, to, src=None):
            return pltpu.make_async_remote_copy(src_ref=slot(*blk) if src is None else src, dst_ref=slot(*blk),
                                                send_sem=send_sems.at[k], recv_sem=recv_sems.at[k], device_id=to,
                                                device_id_type=MESH)

        buf_ref[4 * x + 2 * y + c] = x_ref[...]
        first = [copy(0, me, sibling, src=x_ref)]
        first += [copy(1 + j, me, (*chip, c), src=x_ref) for j, chip in enumerate(flips)]
        for cp in first:
            cp.start()
        passed = [copy(4 + j, (*chip, c), sibling) for j, chip in enumerate(flips)]
        for j, chip in enumerate(flips):
            copy(1 + j, (*chip, c), me).wait_recv()
            passed[j].start()
        copy(0, sibling, me).wait_recv()
        for j, chip in enumerate(flips):
            copy(4 + j, (*chip, 1 - c), me).wait_recv()
        for cp in first + passed:
            cp.wait_send()
        if reduce:
            acc = buf_ref[0]
            for dev in range(1, N_DEV):
                acc = acc + buf_ref[dev]
            out_ref[...] = acc

    vmem = pl.BlockSpec(memory_space=pltpu.VMEM)
    sems = [pltpu.SemaphoreType.DMA((7,)), pltpu.SemaphoreType.DMA((7,))]
    if reduce:
        out_shape = jax.ShapeDtypeStruct((rows, cols), block.dtype)
        scratch = [pltpu.VMEM((N_DEV, rows, cols), block.dtype)] + sems
    else:
        out_shape = jax.ShapeDtypeStruct((N_DEV, rows, cols), block.dtype)
        scratch = sems
    return pl.pallas_call(
        body,
        name=name,
        out_shape=out_shape,
        in_specs=[vmem],
        out_specs=vmem,
        scratch_shapes=scratch,
        compiler_params=pltpu.CompilerParams(vmem_limit_bytes=VMEM_LIMIT_BYTES),
    )(block)


def add_own_half(grad, recv, core, *, name):
    _, r2, cols = recv.shape
    tr = _tile(r2, max(16, (1 << 20) // (2 * cols) // 16 * 16), 16)
    nr = r2 // tr

    def body(core_ref, g_ref, r_ref, o_ref):
        o_ref[...] = (g_ref[...].astype(F32) + r_ref[...].astype(F32)).astype(o_ref.dtype)

    return pl.pallas_call(
        body,
        name=name,
        out_shape=jax.ShapeDtypeStruct(recv.shape, BF16),
        grid_spec=pltpu.PrefetchScalarGridSpec(
            num_scalar_prefetch=1,
            grid=(N_CHIPS, nr),
            in_specs=[
                pl.BlockSpec((None, tr, cols), lambda k, i, core_ref: (k, core_ref[0] * nr + i, 0)),
                pl.BlockSpec((None, tr, cols), lambda k, i, core_ref: (k, i, 0)),
            ],
            out_specs=pl.BlockSpec((None, tr, cols), lambda k, i, core_ref: (k, i, 0)),
        ),
        compiler_params=_cparams("parallel", "parallel"),
    )(core, grad, recv)


def sum_chips(part, recv, chip, sums, layer, *, name):
    _, r2, cols = part.shape
    tr = _tile(r2, max(16, (1 << 20) // (2 * cols) // 16 * 16), 16)

    def body(chip_ref, p_ref, r_ref, sums_ref, o_ref):
        acc = p_ref[...].astype(F32)
        for k in range(3):
            acc = acc + r_ref[k].astype(F32)
        o_ref[...] = acc

    return pl.pallas_call(
        body,
        name=name,
        out_shape=jax.ShapeDtypeStruct(sums.shape, F32),
        grid_spec=pltpu.PrefetchScalarGridSpec(
            num_scalar_prefetch=1,
            grid=(r2 // tr,),
            in_specs=[
                pl.BlockSpec((None, tr, cols), lambda i, chip_ref: (chip_ref[0], i, 0)),
                pl.BlockSpec((3, tr, cols), lambda i, chip_ref: (0, i, 0)),
                pl.BlockSpec(memory_space=pl.ANY),
            ],
            out_specs=pl.BlockSpec((None, tr, cols), lambda i, chip_ref: (layer, i, 0)),
        ),
        input_output_aliases={3: 0},
        compiler_params=_cparams("parallel"),
    )(chip, part, recv, sums)


WEIGHTS = ("w_in", "w_uq", "w_ukv", "w_branch", "w_out", "w_ffn_in", "w_ffn_out")


def layer_fwd(cfg, h, w, s, tabs, tag):
    m, d, hd = cfg.m, cfg.d, cfg.heads
    fox_blk = cfg.off_fox // LANES
    hn = rmsnorm_fwd(h, s["g_mix_pre"], BF16, name=f"norm_mix_pre{tag}")
    proj = matmul(hn, w["w_in"], "nn", BF16, tm=m, tn=_tile(cfg.d_inp, 512), tk=d, name=f"proj{tag}")
    fl = matmul(hn, w["w_in"][:, cfg.off_fl:cfg.off_fl + LANES], "nn", F32, tm=m, tn=LANES, tk=d, name=f"proj_forget{tag}")
    cqn = rmsnorm_fwd(proj, s["g_q_lat"], BF16, width=cfg.q_rank, col_blk=cfg.off_cq // cfg.q_rank, name=f"norm_q{tag}")
    ckvn = rmsnorm_fwd(proj, s["g_kv_lat"], BF16, width=cfg.kv_rank, col_blk=cfg.off_ckv // cfg.kv_rank, name=f"norm_kv{tag}")
    q = matmul(cqn, w["w_uq"], "nn", BF16, tm=m, tn=_tile(2 * cfg.width, 512), tk=cfg.q_rank, name=f"up_q{tag}")
    kv = matmul(ckvn, w["w_ukv"], "nn", BF16, tm=m, tn=_tile(2 * cfg.width, 512), tk=cfg.kv_rank, name=f"up_kv{tag}")
    qf, kf = mla_prep_fwd(cfg, q, kv, proj, tabs[0], name=f"mla_prep{tag}")
    o, lse_a = attn_fwd(qf, kf, kv, heads=hd, dk=2 * LANES, dv=LANES, qblk0=0, kblk0=0, vblk0=hd,
                        scale=(LANES + ROPE) ** -0.5, pad=cfg.pad, slot=0, name=f"mla_attn{tag}")
    o = conv_fwd(cfg, proj, s["conv_w"], o, 1, name=f"conv{tag}")
    b_pad = jnp.pad(s["b_forget"], (0, LANES - hd)).reshape(1, LANES)
    cum = fox_gate_fwd(cfg, fl, b_pad, name=f"fox_gate{tag}")
    cum_t = cum[:, :hd].T
    decay = (cum_t[:, :, None], cum_t[:, None, :])
    o, lse_c = attn_fwd(proj, proj, proj, heads=hd, dk=LANES, dv=LANES, qblk0=fox_blk, kblk0=fox_blk + hd, vblk0=fox_blk + 2 * hd,
                        scale=LANES ** -0.5, pad=cfg.pad, slot=2, branches=o, decay=decay, name=f"fox_attn{tag}")
    y = matmul(o, w["w_branch"], "nn", BF16, tm=m, tn=_tile(d, 512), tk=cfg.width, name=f"branch{tag}")
    merged = gate_merge_fwd(cfg, y, proj, name=f"merge{tag}")
    mix = matmul(merged, w["w_out"], "nn", F32, tm=m, tn=_tile(d, 256), tk=d, name=f"out_proj{tag}")
    h_mid = rmsnorm_fwd(mix, s["g_mix_post"], F32, res=h, name=f"norm_mix_post{tag}")
    if hasattr(w, "land_ffn"):
        s = w.land_ffn(h_mid, s)
    hn2 = rmsnorm_fwd(h_mid, s["g_ffn_pre"], BF16, name=f"norm_ffn_pre{tag}")
    gu = matmul(hn2, w["w_ffn_in"], "nn", BF16, tm=m, tn=_tile(2 * cfg.d_ff, 512), tk=d, name=f"ffn_in{tag}")
    act = swiglu_fwd(cfg, gu, name=f"swiglu{tag}")
    f = matmul(act, w["w_ffn_out"], "nn", F32, tm=m, tn=_tile(d, 512), tk=_tile(cfg.d_ff, 2816), name=f"ffn_out{tag}")
    h_next = rmsnorm_fwd(f, s["g_ffn_post"], F32, res=h_mid, name=f"norm_ffn_post{tag}")
    saved = dict(h=h, hn=hn, proj=proj, fl=fl, cqn=cqn, ckvn=ckvn, kv=kv, qf=qf, kf=kf, lse_a=lse_a,
                 b_pad=b_pad, decay=decay, lse_c=lse_c, o=o, y=y, merged=merged, mix=mix, h_mid=h_mid,
                 hn2=hn2, gu=gu, act=act, f=f)
    return h_next, s, saved


def layer_bwd(cfg, dh, w, s, r, tabs, tag, grads_done):
    m, d, hd = cfg.m, cfg.d, cfg.heads
    fox_blk = cfg.off_fox // LANES
    tk_m = m
    df, dg4 = rmsnorm_bwd(r["f"], s["g_ffn_post"], dh, BF16, name=f"norm_ffn_post_bwd{tag}")
    dact = matmul(df, w["w_ffn_out"], "nt", BF16, tm=m, tn=_tile(cfg.d_ff, 512), tk=d, name=f"ffn_out_dx{tag}")
    dw_fo = matmul(r["act"], df, "tn", BF16, tm=_tile(cfg.d_ff, 512), tn=_tile(d, 1024), tk=tk_m, name=f"ffn_out_dw{tag}")
    dgu = swiglu_bwd(cfg, dact, r["gu"], name=f"swiglu_bwd{tag}")
    dhn2 = matmul(dgu, w["w_ffn_in"], "nt", F32, tm=m, tn=_tile(d, 512), tk=_tile(2 * cfg.d_ff, 2816), name=f"ffn_in_dx{tag}")
    dw_fi = matmul(r["hn2"], dgu, "tn", BF16, tm=_tile(d, 1024), tn=_tile(2 * cfg.d_ff // N_CHIPS, 1408), tk=tk_m, chip_cols=True,
                   name=f"ffn_in_dw{tag}")
    token = grads_done(dict(w_ffn_in=dw_fi, w_ffn_out=dw_fo))
    if token is not None:
        s = {**s, "g_ffn_pre": s["g_ffn_pre"] + token[0, 0]}
    dh_mid, dg3 = rmsnorm_bwd(r["h_mid"], s["g_ffn_pre"], dhn2, F32, dres=dh, name=f"norm_ffn_pre_bwd{tag}")
    dmix, dg2 = rmsnorm_bwd(r["mix"], s["g_mix_post"], dh_mid, BF16, name=f"norm_mix_post_bwd{tag}")
    dmerged = matmul(dmix, w["w_out"], "nt", BF16, tm=m, tn=_tile(d, 512), tk=d, name=f"out_proj_dx{tag}")
    dw_out = matmul(r["merged"], dmix, "tn", BF16, tm=_tile(d, 1024), tn=_tile(d, 512), tk=tk_m, name=f"out_proj_dw{tag}")
    dy, dgl = gate_merge_bwd(cfg, dmerged, r["y"], r["proj"], name=f"merge_bwd{tag}")
    do = matmul(dy, w["w_branch"], "nt", BF16, tm=m, tn=_tile(cfg.width, 512), tk=d, name=f"branch_dx{tag}")
    dw_br = matmul(r["o"], dy, "tn", BF16, tm=_tile(cfg.width, 1024), tn=_tile(d // N_CHIPS, 512), tk=tk_m, chip_cols=True,
                   name=f"branch_dw{tag}")
    dqf, dkf, dv_a = attn_bwd(r["qf"], r["kf"], r["kv"], do, 0, r["lse_a"], heads=hd, dk=2 * LANES, dv=LANES,
                              qblk0=0, kblk0=0, vblk0=hd, scale=(LANES + ROPE) ** -0.5, pad=cfg.pad, name=f"mla_attn_bwd{tag}")
    dq, dkn, dkpe = mla_prep_bwd(cfg, dqf, dkf, tabs[1], name=f"mla_prep_bwd{tag}")
    dkv = jnp.concatenate([dkn, dv_a], axis=1)
    dcqn = matmul(dq, w["w_uq"], "nt", F32, tm=m, tn=cfg.q_rank, tk=2 * cfg.width, name=f"up_q_dx{tag}")
    dw_uq = matmul(r["cqn"], dq, "tn", BF16, tm=cfg.q_rank, tn=_tile(2 * cfg.width, 512), tk=tk_m, name=f"up_q_dw{tag}")
    dckvn = matmul(dkv, w["w_ukv"], "nt", F32, tm=m, tn=cfg.kv_rank, tk=2 * cfg.width, name=f"up_kv_dx{tag}")
    dw_ukv = matmul(r["ckvn"], dkv, "tn", BF16, tm=cfg.kv_rank, tn=_tile(2 * cfg.width, 512), tk=tk_m, name=f"up_kv_dw{tag}")
    dcq, dgq = rmsnorm_bwd(r["proj"], s["g_q_lat"], dcqn, BF16, width=cfg.q_rank, col_blk=cfg.off_cq // cfg.q_rank,
                           name=f"norm_q_bwd{tag}")
    dckv, dgkv = rmsnorm_bwd(r["proj"], s["g_kv_lat"], dckvn, BF16, width=cfg.kv_rank, col_blk=cfg.off_ckv // cfg.kv_rank,
                             name=f"norm_kv_bwd{tag}")
    dcb, dcc, dcx, dconv_w = conv_bwd(cfg, r["proj"], s["conv_w"], do, 1, name=f"conv_bwd{tag}")
    dfq, dfk, dfv, dck = attn_bwd(r["proj"], r["proj"], r["proj"], do, 2, r["lse_c"], heads=hd, dk=LANES, dv=LANES,
                                  qblk0=fox_blk, kblk0=fox_blk + hd, vblk0=fox_blk + 2 * hd, scale=LANES ** -0.5,
                                  pad=cfg.pad, decay=r["decay"], name=f"fox_attn_bwd{tag}")
    dc = jnp.pad(dck[:, 0, :].T, ((0, 0), (0, LANES - hd)))
    dfl, dbf = fox_gate_bwd(cfg, r["fl"], r["b_pad"], dc, name=f"fox_gate_bwd{tag}")
    tail = jnp.zeros((m, cfg.d_inp - cfg.off_fl - LANES), BF16)
    dproj = jnp.concatenate([dgl, dcq, dckv, dcb, dcc, dcx, dfq, dfk, dfv, dkpe.astype(BF16), dfl.astype(BF16), tail], axis=1)
    dhn = matmul(dproj, w["w_in"], "nt", F32, tm=m, tn=_tile(d, 512), tk=_tile(cfg.d_inp, 2304), name=f"proj_dx{tag}")
    dw_in = matmul(r["hn"], dproj, "tn", BF16, tm=_tile(d, 1024), tn=_tile(cfg.d_inp, 512), tk=tk_m, name=f"proj_dw{tag}")
    dh_in, dg1 = rmsnorm_bwd(r["h"], s["g_mix_pre"], dhn, F32, dres=dh_mid, name=f"norm_mix_pre_bwd{tag}")
    token = grads_done(dict(w_in=dw_in, w_uq=dw_uq, w_ukv=dw_ukv, w_branch=dw_br, w_out=dw_out))
    dsmall = dict(g_mix_pre=dg1[0], g_mix_post=dg2[0], g_ffn_pre=dg3[0], g_ffn_post=dg4[0], g_q_lat=dgq[0], g_kv_lat=dgkv[0],
                  b_forget=dbf[0, :hd], conv_w=dconv_w)
    return dh_in, dsmall, token


def local_step(cfg, x, target, meta, layer_params, grads_done):
    h = jnp.concatenate([jnp.zeros((cfg.pad, cfg.d), F32), meta, x], axis=0)
    cos, s1, s2 = rope_tables(cfg)
    tabs = ((cos, s1, s2), (cos, -s1, -s2))
    saved = []
    for l in range(cfg.depth):
        w, s = layer_params(l, h)
        h, s, r = layer_fwd(cfg, h, w, s, tabs, f"_{l}")
        saved.append((w, s, r))
    dh, loss = loss_head(cfg, h, target, name="loss_head")
    dsmalls, token = [None] * cfg.depth, None
    for l in reversed(range(cfg.depth)):
        w, s, r = saved[l]
        if token is not None:
            s = {**s, "g_ffn_post": s["g_ffn_post"] + token[0, 0]}
        dh, dsmalls[l], token = layer_bwd(cfg, dh, w, s, r, tabs, f"_{l}", functools.partial(grads_done, l))
    first = cfg.pad + cfg.n_meta
    return loss, dh[first:], dh[cfg.pad:first], dsmalls


def _cols_from_chips(g):
    return jnp.transpose(g, (1, 0, 2)).reshape(g.shape[1], N_CHIPS * g.shape[2])


def _cols_to_chips(w):
    r, c = w.shape
    return jnp.transpose(w.reshape(r, N_CHIPS, c // N_CHIPS), (1, 0, 2))


def _packed_segments(cfg):
    nat = [0] + _cumsum(cfg.nat_splits)
    w = cfg.width
    order = [(10, 0), (0, cfg.off_cq), (1, cfg.off_ckv), (3, cfg.off_conv), (4, cfg.off_conv + w), (5, cfg.off_conv + 2 * w),
             (6, cfg.off_fox), (7, cfg.off_fox + w), (8, cfg.off_fox + 2 * w), (2, cfg.off_kpe), (9, cfg.off_fl)]
    return [(pk, nat[i], cfg.nat_splits[i]) for i, pk in order]


def _chip_cols(cfg):
    n = cfg.d_in // N_CHIPS
    return n, -(-n // LANES) * LANES


def _lane_pieces(cfg, to_packed):
    n, n_pad = _chip_cols(cfg)
    tiles = [[] for _ in range(cfg.d_inp // LANES if to_packed else N_CHIPS * n_pad // LANES)]
    for pk, nat, width in _packed_segments(cfg):
        g = nat
        while g < nat + width:
            k, a = divmod(g, n)
            dst = (pk + g - nat) if to_packed else (k * n_pad + a)
            run = min(nat + width - g, n - a, LANES - dst % LANES)
            src = (k, a) if to_packed else (0, pk + g - nat)
            tiles[dst // LANES].append((dst % LANES, run, *src))
            g += run
    return tiles


def _fill_tiles(pieces, read, write, rows):
    lane = lax.broadcasted_iota(jnp.int32, (rows, LANES), 1)
    for t, parts in enumerate(pieces):
        tile = jnp.zeros((rows, LANES), F32)
        for dl, run, blk, col in parts:
            w0 = col // LANES * LANES
            off = col - w0
            span = LANES if off + run <= LANES else 2 * LANES
            win = read(blk, w0, span)
            shift = (dl - off) % span
            if shift:
                win = pltpu.roll(win, shift, 1)
            win = win[:, :LANES]
            tile = win if (dl == 0 and run == LANES) else jnp.where((lane >= dl) & (lane < dl + run), win, tile)
        write(t, tile)


def pack_w_in_blocks(cfg, lands, own, chip, *, name):
    d = cfg.d
    n, n_pad = _chip_cols(cfg)
    tr = _tile(d, 256, 16)
    pieces = _lane_pieces(cfg, True)

    def body(chip_ref, land_ref, own_ref, o_ref):
        def read(k, w0, span):
            theirs = land_ref[k, :, w0:w0 + span]
            return jnp.where(chip_ref[0] == k, own_ref[:, w0:w0 + span], theirs).astype(F32)

        def write(t, tile):
            o_ref[:, t * LANES:(t + 1) * LANES] = tile.astype(o_ref.dtype)

        _fill_tiles(pieces, read, write, tr)

    return pl.pallas_call(
        body,
        name=name,
        out_shape=jax.ShapeDtypeStruct((d, cfg.d_inp), BF16),
        grid_spec=pltpu.PrefetchScalarGridSpec(
            num_scalar_prefetch=1,
            grid=(d // tr,),
            in_specs=[pl.BlockSpec((N_CHIPS, tr, n_pad), lambda i, chip_ref: (0, i, 0)),
                      pl.BlockSpec((tr, n_pad), lambda i, chip_ref: (i, 0))],
            out_specs=pl.BlockSpec((tr, cfg.d_inp), lambda i, chip_ref: (i, 0)),
        ),
        compiler_params=_cparams("parallel"),
    )(chip, lands, own)


def unpack_w_in_blocks(cfg, dw, *, name):
    d = cfg.d
    n, n_pad = _chip_cols(cfg)
    tr = _tile(d, 256, 16)
    per_blk = n_pad // LANES
    pieces = _lane_pieces(cfg, False)

    def body(dw_ref, o_ref):
        def read(_, w0, span):
            return dw_ref[:, w0:w0 + span].astype(F32)

        def write(t, tile):
            k, i = divmod(t, per_blk)
            o_ref[k, :, i * LANES:(i + 1) * LANES] = tile.astype(o_ref.dtype)

        _fill_tiles(pieces, read, write, tr)

    return pl.pallas_call(
        body,
        name=name,
        out_shape=jax.ShapeDtypeStruct((N_CHIPS, d, n_pad), BF16),
        grid=(d // tr,),
        in_specs=[pl.BlockSpec((tr, cfg.d_inp), lambda i: (i, 0))],
        out_specs=pl.BlockSpec((N_CHIPS, tr, n_pad), lambda i: (0, i, 0)),
        compiler_params=_cparams("parallel"),
    )(dw)


def full_weights(cfg, g, w_in=None):
    make = dict(
        w_uq=lambda a: pack_w_uq(cfg, _cols_from_chips(a)),
        w_ukv=lambda a: pack_w_ukv(cfg, _cols_from_chips(a)),
        w_branch=lambda a: _cols_from_chips(a).reshape(3, cfg.width, cfg.d),
        w_out=lambda a: a.reshape(cfg.d, cfg.d),
        w_ffn_in=_cols_from_chips,
        w_ffn_out=lambda a: a.reshape(cfg.d_ff, cfg.d),
    )
    out = {n: make[n](a) for n, a in g.items()}
    if w_in is not None:
        out["w_in"] = w_in
    return out


def chip_grads(cfg, dw, tag):
    make = dict(
        w_in=lambda a: unpack_w_in_blocks(cfg, a, name=f"unpack_w_in{tag}"),
        w_uq=lambda a: _cols_to_chips(unpack_w_uq(cfg, a)),
        w_ukv=lambda a: _cols_to_chips(unpack_w_ukv(cfg, a)),
        w_branch=lambda a: a.reshape(N_CHIPS, 3 * cfg.width, cfg.d // N_CHIPS),
        w_out=lambda a: a.reshape(N_CHIPS, cfg.d // N_CHIPS, cfg.d),
        w_ffn_in=lambda a: a,
        w_ffn_out=lambda a: a.reshape(N_CHIPS, cfg.d_ff // N_CHIPS, cfg.d),
    )
    return {n: make[n](a) for n, a in dw.items()}


def _small_rows(cfg):
    return dict(g_mix_pre=cfg.d // LANES, g_mix_post=cfg.d // LANES, g_ffn_pre=cfg.d // LANES, g_ffn_post=cfg.d // LANES,
                g_q_lat=cfg.q_rank // LANES, g_kv_lat=cfg.kv_rank // LANES, b_forget=1, conv_w=3 * cfg.width // LANES)


def pack_small(cfg, loss, dmeta, dsmalls):
    parts = [loss[0:1, :], dmeta.reshape(-1, LANES)]
    for ds in dsmalls:
        for k in _small_rows(cfg):
            v = ds[k]
            if k == "b_forget":
                v = jnp.pad(v, (0, LANES - cfg.heads))
            parts.append(v.reshape(-1, LANES))
    rows = sum(p.shape[0] for p in parts)
    parts.append(jnp.zeros((-rows % 8, LANES), F32))
    return jnp.concatenate(parts, axis=0)


def unpack_small(cfg, block):
    loss = block[0, 0]
    n = cfg.n_meta * cfg.d // LANES
    dmeta = block[1:1 + n].reshape(cfg.n_meta, cfg.d)
    at = 1 + n
    out = []
    for _ in range(cfg.depth):
        ds = {}
        for k, rows in _small_rows(cfg).items():
            v = block[at:at + rows]
            at += rows
            if k == "b_forget":
                v = v[0, :cfg.heads]
            elif k == "conv_w":
                v = v.reshape(3, cfg.width)
            else:
                v = v.reshape(-1)
            ds[k] = v
        out.append(ds)
    return loss, dmeta, out


def kernel(x, meta, w_in, b_forget, g_q_lat, g_kv_lat, w_uq, w_ukv, conv_w, w_branch, w_out, w_ffn_in, w_ffn_out, g_mix_pre, g_mix_post, g_ffn_pre, g_ffn_post, loss_target, m_meta, m_w_in, m_b_forget, m_g_q_lat, m_g_kv_lat, m_w_uq, m_w_ukv, m_conv_w, m_w_branch, m_w_out, m_w_ffn_in, m_w_ffn_out, m_g_mix_pre, m_g_mix_post, m_g_ffn_pre, m_g_ffn_post, v_meta, v_w_in, v_b_forget, v_g_q_lat, v_g_kv_lat, v_w_uq, v_w_ukv, v_conv_w, v_w_branch, v_w_out, v_w_ffn_in, v_w_ffn_out, v_g_mix_pre, v_g_mix_post, v_g_ffn_pre, v_g_ffn_post):
    cfg = CFG
    names = ("meta", "w_in", "b_forget", "g_q_lat", "g_kv_lat", "w_uq", "w_ukv", "conv_w", "w_branch", "w_out", "w_ffn_in",
             "w_ffn_out", "g_mix_pre", "g_mix_post", "g_ffn_pre", "g_ffn_post")
    params = dict(zip(names, (meta, w_in, b_forget, g_q_lat, g_kv_lat, w_uq, w_ukv, conv_w, w_branch, w_out, w_ffn_in, w_ffn_out,
                              g_mix_pre, g_mix_post, g_ffn_pre, g_ffn_post)))
    mom1 = dict(zip(names, (m_meta, m_w_in, m_b_forget, m_g_q_lat, m_g_kv_lat, m_w_uq, m_w_ukv, m_conv_w, m_w_branch, m_w_out,
                            m_w_ffn_in, m_w_ffn_out, m_g_mix_pre, m_g_mix_post, m_g_ffn_pre, m_g_ffn_post)))
    mom2 = dict(zip(names, (v_meta, v_w_in, v_b_forget, v_g_q_lat, v_g_kv_lat, v_w_uq, v_w_ukv, v_conv_w, v_w_branch, v_w_out,
                            v_w_ffn_in, v_w_ffn_out, v_g_mix_pre, v_g_mix_post, v_g_ffn_pre, v_g_ffn_post)))
    xi, yi, ci = lax.axis_index("x"), lax.axis_index("y"), lax.axis_index("c")
    chip = 2 * xi + yi
    chip_arr = jnp.reshape(chip, (1,)).astype(jnp.int32)
    core_arr = jnp.reshape(ci, (1,)).astype(jnp.int32)

    meta_all = gather_blocks(meta, reduce=False, name="gather_meta")[0::2]
    meta_full = jnp.transpose(meta_all, (1, 0, 2)).reshape(cfg.n_meta, cfg.d)
    conv_rows = conv_w.reshape(cfg.depth * 3, cfg.width // N_CHIPS)
    conv_all = gather_blocks(conv_rows, reduce=False, name="gather_conv_w")[0::2]
    conv_full = jnp.transpose(conv_all, (1, 0, 2)).reshape(cfg.depth, 3, cfg.width)

    def shard2d(name, l, after=None):
        w = params[name][l]
        if after is not None:
            w = w + after
        w = w.reshape(-1, w.shape[-1]).astype(BF16)
        if name == "w_in":
            w = jnp.pad(w, ((0, 0), (0, _chip_cols(cfg)[1] - w.shape[1])))
        return w

    is_mine = (jnp.arange(N_CHIPS) == chip)[:, None, None]
    shard_shape = {n: shard2d(n, 0).shape for n in WEIGHTS}
    groups = (("w_in", "w_uq", "w_ukv", "w_branch", "w_out"), ("w_ffn_in", "w_ffn_out"))
    gather_plans = [_gather_plan([shard_shape[n][0] // 2 for n in g]) for g in groups]

    def gather_start(l, after):
        started = []
        for gi, g in enumerate(groups):
            started.append(copies_start([shard2d(n, l, after) for n in g], [(N_CHIPS,) + shard_shape[n] for n in g],
                                        gather_plans[gi], 3 * len(g), name=f"gather_start_{gi}_{l}"))
            after = started[-1][4][0, 0]
        return started

    def land(l, gi, started, after):
        send_sems, recv_sems, own, lands, _ = started
        own, lands, landed = copies_wait(send_sems, recv_sems, own, lands, gather_plans[gi], after, name=f"gather_wait_{gi}_{l}")
        return own, forward_halves(lands, name=f"forward_halves_{gi}_{l}"), landed

    in_flight = {0: gather_start(0, None)}

    class LayerWeights(dict):
        def __init__(self, l, h):
            own, lands, _ = land(l, 0, in_flight[l][0], h)
            got = {n: jnp.where(is_mine, o[None], g) for n, o, g in zip(groups[0][1:], own[1:], lands[1:])}
            super().__init__(full_weights(cfg, got, pack_w_in_blocks(cfg, lands[0], own[0], chip_arr, name=f"pack_w_in_{l}")))
            self.layer = l

        def land_ffn(self, after, s):
            l = self.layer
            own, lands, landed = land(l, 1, in_flight.pop(l)[1], after)
            self.update(full_weights(cfg, {n: jnp.where(is_mine, o[None], g) for n, o, g in zip(groups[1], own, lands)}))
            if l + 1 == cfg.depth:
                return s
            in_flight[l + 1] = gather_start(l + 1, landed[0, 0])
            return {**s, "g_ffn_pre": s["g_ffn_pre"] + in_flight[l + 1][1][4][0, 0]}

    def layer_params(l, h):
        s = dict(g_mix_pre=g_mix_pre[l], g_mix_post=g_mix_post[l], g_ffn_pre=g_ffn_pre[l], g_ffn_post=g_ffn_post[l],
                 g_q_lat=g_q_lat[l], g_kv_lat=g_kv_lat[l], b_forget=b_forget[l], conv_w=conv_full[l])
        s["g_mix_pre"] = s["g_mix_pre"] + in_flight[l][1][4][0, 0]
        return LayerWeights(l, h), s

    half_shape = {n: (shard_shape[n][0] // 2, shard_shape[n][1]) for n in WEIGHTS}
    sums_upper = {n: jnp.zeros((cfg.depth - 1,) + half_shape[n], F32) for n in WEIGHTS}
    sums_first = {n: jnp.zeros((1,) + half_shape[n], F32) for n in WEIGHTS}
    swapping, exchanging = [], []

    def finish_exchange(after):
        l, names_, (send_sems, recv_sems, parts, lands, _) = exchanging.pop(0)
        parts, others, _ = copies_wait(send_sems, recv_sems, parts, lands, _exchange_plan(len(names_)), after,
                                       name=f"exchange_wait_{names_[0]}_{l}")
        for n, p, o in zip(names_, parts, others):
            if l == 0:
                sums_first[n] = sum_chips(p, o, chip_arr, sums_first[n], 0, name=f"sum_chips_{n}_{l}")
            else:
                sums_upper[n] = sum_chips(p, o, chip_arr, sums_upper[n], l - 1, name=f"sum_chips_{n}_{l}")

    def finish_swap(after):
        l, names_, (send_sems, recv_sems, mine, lands, _) = swapping.pop(0)
        plan = _swap_plan([g.shape[1] // 2 for g in mine])
        mine, theirs, _ = copies_wait(send_sems, recv_sems, mine, lands, plan, after, name=f"swap_wait_{names_[0]}_{l}")
        parts = [add_own_half(g, t, core_arr, name=f"add_own_half_{n}_{l}") for n, g, t in zip(names_, mine, theirs)]
        started = copies_start(parts, [(3,) + p.shape[1:] for p in parts], _exchange_plan(len(names_)), 3 * len(names_),
                               name=f"exchange_start_{names_[0]}_{l}")
        exchanging.append((l, names_, started))
        return started[4]

    def grads_done(l, dws):
        names_ = [n for n in WEIGHTS if n in dws]
        send = chip_grads(cfg, dws, f"_{l}")
        mine = [send[n] for n in names_]
        halves = [g.shape[1] // 2 for g in mine]
        started = copies_start(mine, [(N_CHIPS, h, g.shape[2]) for g, h in zip(mine, halves)], _swap_plan(halves), len(mine),
                               name=f"swap_start_{names_[0]}_{l}")
        token = started[4]
        if swapping:
            token = finish_swap(token)
            if len(exchanging) > 1:
                finish_exchange(token)
        swapping.append((l, names_, started))
        return token

    loss, grad_x, dmeta, dsmalls = local_step(cfg, x[0], loss_target[0], meta_full, layer_params, grads_done)
    share_plan = _share_plan(len(WEIGHTS))
    sharing = copies_start([sums_upper[n] for n in WEIGHTS], [sums_upper[n].shape for n in WEIGHTS], share_plan, len(WEIGHTS),
                           name="share_start_upper")
    last = finish_swap(sharing[4])
    while exchanging[0][0] > 0:
        finish_exchange(last)

    def update(own, other, first_layer, prev, tag):
        out = {}
        for n, mine_, theirs_ in zip(WEIGHTS, own, other):
            three_d = lambda a: a.reshape(cfg.depth, -1, params[n].shape[-1])
            out[n] = adamw_halves(three_d(params[n]), mine_, theirs_, core_arr, three_d(mom1[n]), three_d(mom2[n]),
                                  first_layer=first_layer, prev=prev and prev[n], name=f"adamw_{tag}_{n}")
        return out

    own_upper, other_upper, _ = copies_wait(sharing[0], sharing[1], sharing[2], sharing[3], share_plan, last, name="share_wait_upper")
    upper = update(own_upper, other_upper, 1, None, "upper")
    busy = sum(upper[n][1][1, 0, :LANES] for n in WEIGHTS)
    while exchanging:
        finish_exchange(busy)
    own_first = [sums_first[n] for n in WEIGHTS]
    done = update(own_first, share_halves(own_first, name="share_halves_first"), 0, upper, "first")
    grad, delta, new_m, new_v = ({n: done[n][k].reshape(params[n].shape) for n in WEIGHTS} for k in range(4))

    total = gather_blocks(pack_small(cfg, loss, dmeta, dsmalls), reduce=True, name="reduce_small")
    loss_sum, dmeta_sum, dsmall_sum = unpack_small(cfg, total)
    for k in _small_rows(cfg):
        grad[k] = jnp.stack([ds[k] for ds in dsmall_sum])
    grad["conv_w"] = lax.dynamic_slice_in_dim(grad["conv_w"], chip * (cfg.width // N_CHIPS), cfg.width // N_CHIPS, axis=2)
    grad["meta"] = lax.dynamic_slice_in_dim(dmeta_sum, chip * (cfg.d // N_CHIPS), cfg.d // N_CHIPS, axis=1)

    for n in names:
        if n in WEIGHTS:
            continue
        shp = params[n].shape
        two_d = lambda a: a.reshape(-1, shp[-1])
        dl, nm, nv = adamw(two_d(params[n]), two_d(grad[n]), two_d(mom1[n]), two_d(mom2[n]), name=f"adamw_{n}")
        delta[n], new_m[n], new_v[n] = dl.reshape(shp), nm.reshape(shp), nv.reshape(shp)

    return (loss_sum, grad_x[None], *[grad[n] for n in names], *[delta[n] for n in names], *[new_m[n] for n in names],
            *[new_v[n] for n in names])
```

```python
import functools
from typing import NamedTuple

import jax
import jax.numpy as jnp
from jax import lax
from jax.experimental import pallas as pl
from jax.experimental.pallas import tpu as pltpu

F32 = jnp.float32
BF16 = jnp.bfloat16
MESH = pl.DeviceIdType.MESH

EPS = 1e-6
NEG_INF = -1e30
ROPE_THETA = 10000.0
LANES = 128
ROPE = 64
N_CHIPS = 4
N_DEV = 8

ADAM_LR = 0.001
ADAM_B1 = 0.9
ADAM_B2 = 0.999
ADAM_EPS = 1e-08
ADAM_WD = 0.01
ADAM_STEP = 10

VMEM_LIMIT_BYTES = 48 * 1024 * 1024


class Cfg(NamedTuple):
    d: int = 2048
    seq: int = 2048
    depth: int = 4
    n_meta: int = 16
    heads: int = 8
    q_rank: int = 512
    kv_rank: int = 512
    d_ff: int = 5632

    @property
    def width(self):
        return self.heads * LANES

    @property
    def pad(self):
        return (-(self.n_meta + self.seq)) % LANES

    @property
    def m(self):
        return self.pad + self.n_meta + self.seq

    @property
    def nat_splits(self):
        w = self.width
        return (self.q_rank, self.kv_rank, ROPE, w, w, w, w, w, w, self.heads, 3 * self.d)

    @property
    def d_in(self):
        return sum(self.nat_splits)

    @property
    def off_cq(self):
        return 3 * self.d

    @property
    def off_ckv(self):
        return self.off_cq + self.q_rank

    @property
    def off_conv(self):
        return self.off_ckv + self.kv_rank

    @property
    def off_fox(self):
        return self.off_conv + 3 * self.width

    @property
    def off_kpe(self):
        return self.off_fox + 3 * self.width

    @property
    def off_fl(self):
        return self.off_kpe + LANES

    @property
    def d_inp(self):
        return -(-(self.off_fl + LANES) // 512) * 512


CFG = Cfg()


def _tile(n, target, mult=LANES):
    best = None
    t = mult
    while t <= min(n, target):
        if n % t == 0:
            best = t
        t += mult
    return best or n


def _cparams(*sem):
    return pltpu.CompilerParams(dimension_semantics=sem, vmem_limit_bytes=VMEM_LIMIT_BYTES)


def _cumsum(xs):
    out, s = [], 0
    for v in xs:
        s += v
        out.append(s)
    return out


def pack_w_uq(cfg, w):
    r = w.shape[0]
    w3 = w.reshape(r, cfg.heads, LANES + ROPE)
    w3 = jnp.pad(w3, ((0, 0), (0, 0), (0, LANES - ROPE)))
    return w3.reshape(r, cfg.heads * 2 * LANES)


def unpack_w_uq(cfg, wp):
    r = wp.shape[0]
    return wp.reshape(r, cfg.heads, 2 * LANES)[:, :, : LANES + ROPE].reshape(r, cfg.heads * (LANES + ROPE))


def pack_w_ukv(cfg, w):
    r = w.shape[0]
    w4 = w.reshape(r, cfg.heads, 2, LANES)
    return jnp.transpose(w4, (0, 2, 1, 3)).reshape(r, 2 * cfg.heads * LANES)


def unpack_w_ukv(cfg, wp):
    r = wp.shape[0]
    w4 = wp.reshape(r, 2, cfg.heads, LANES)
    return jnp.transpose(w4, (0, 2, 1, 3)).reshape(r, 2 * cfg.heads * LANES)


_DIMS = {
    "nn": (((1,), (0,)), ((), ())),
    "nt": (((1,), (1,)), ((), ())),
    "tn": (((0,), (0,)), ((), ())),
}


def matmul(a, b, mode, out_dtype, *, tm, tn, tk, name, chip_cols=False):
    batched = a.ndim == 3
    if mode == "nn":
        (m, kc), n = a.shape[-2:], b.shape[-1]
        a_blk, a_idx = (tm, tk), lambda i, j, k: (i, k)
        b_blk, b_idx = (tk, tn), lambda i, j, k: (k, j)
    elif mode == "nt":
        (m, kc), n = a.shape[-2:], b.shape[-2]
        a_blk, a_idx = (tm, tk), lambda i, j, k: (i, k)
        b_blk, b_idx = (tn, tk), lambda i, j, k: (j, k)
    else:
        (kc, m), n = a.shape[-2:], b.shape[-1]
        a_blk, a_idx = (tk, tm), lambda i, j, k: (k, i)
        b_blk, b_idx = (tk, tn), lambda i, j, k: (k, j)
    assert m % tm == 0 and n % tn == 0 and kc % tk == 0, (name, m, n, kc, tm, tn, tk)
    nk = kc // tk
    dims = _DIMS[mode]
    o_blk, o_idx = (tm, tn), lambda i, j, k: (i, j)
    grid = (m // tm, n // tn, nk)
    per = n // N_CHIPS // tn
    assert not chip_cols or n // N_CHIPS % tn == 0
    if batched:
        nb = a.shape[0]
        grid = (nb,) + grid
        wrap = lambda f: (lambda bb, i, j, k: (bb,) + f(i, j, k))
        a_blk, b_blk, o_blk = (None,) + a_blk, (None,) + b_blk, (None,) + o_blk
        a_idx, b_idx, o_idx = wrap(a_idx), wrap(b_idx), wrap(o_idx)
        out_shape = (nb, m, n)
        if chip_cols:
            o_blk, o_idx = (None,) + o_blk, lambda bb, i, j, k: (j // per, bb, i, j % per)
            out_shape = (N_CHIPS, nb, m, n // N_CHIPS)
        sem = ("parallel", "parallel", "parallel", "arbitrary")
    else:
        out_shape = (m, n)
        if chip_cols:
            o_blk, o_idx = (None,) + o_blk, lambda i, j, k: (j // per, i, j % per)
            out_shape = (N_CHIPS, m, n // N_CHIPS)
        sem = ("parallel", "parallel", "arbitrary")
    k_axis = len(grid) - 1

    def body(a_ref, b_ref, o_ref, *scratch):
        prod = lax.dot_general(a_ref[...], b_ref[...], dims, preferred_element_type=F32)
        if nk == 1:
            o_ref[...] = prod.astype(o_ref.dtype)
        else:
            acc_ref = scratch[0] if scratch else o_ref
            k = pl.program_id(k_axis)

            @pl.when(k == 0)
            def _():
                acc_ref[...] = prod

            @pl.when(k > 0)
            def _():
                acc_ref[...] += prod

            if scratch:
                @pl.when(k == nk - 1)
                def _():
                    o_ref[...] = acc_ref[...].astype(o_ref.dtype)

    return pl.pallas_call(
        body,
        name=name,
        out_shape=jax.ShapeDtypeStruct(out_shape, out_dtype),
        grid=grid,
        in_specs=[pl.BlockSpec(a_blk, a_idx), pl.BlockSpec(b_blk, b_idx)],
        out_specs=pl.BlockSpec(o_blk, o_idx),
        scratch_shapes=[] if nk == 1 or out_dtype == F32 else [pltpu.VMEM((tm, tn), F32)],
        compiler_params=_cparams(*sem),
    )(a, b)


def _row_tile(m):
    return _tile(m, 272, 16)


def rmsnorm_fwd(x, g, out_dtype, *, name, width=None, col_blk=0, res=None):
    m = x.shape[0]
    n = width or x.shape[1]
    tm = _row_tile(m)
    has_res = res is not None

    def body(x_ref, g_ref, *rest):
        o_ref = rest[-1]
        xf = x_ref[...].astype(F32)
        r = lax.rsqrt(jnp.mean(xf * xf, axis=-1, keepdims=True) + EPS)
        y = xf * r * g_ref[...]
        if has_res:
            y = rest[0][...] + y
        o_ref[...] = y.astype(o_ref.dtype)

    in_specs = [pl.BlockSpec((tm, n), lambda i: (i, col_blk)), pl.BlockSpec((1, n), lambda i: (0, 0))]
    args = [x, g.reshape(1, n)]
    if has_res:
        in_specs.append(pl.BlockSpec((tm, n), lambda i: (i, 0)))
        args.append(res)
    return pl.pallas_call(
        body,
        name=name,
        out_shape=jax.ShapeDtypeStruct((m, n), out_dtype),
        grid=(m // tm,),
        in_specs=in_specs,
        out_specs=pl.BlockSpec((tm, n), lambda i: (i, 0)),
        compiler_params=_cparams("parallel"),
    )(*args)


def rmsnorm_bwd(x, g, dy, out_dtype, *, name, width=None, col_blk=0, dres=None):
    m = x.shape[0]
    n = width or x.shape[1]
    tm = _row_tile(m)
    has_res = dres is not None

    def body(x_ref, g_ref, dy_ref, *rest):
        dx_ref, dg_ref = rest[-2:]
        i = pl.program_id(0)
        xf = x_ref[...].astype(F32)
        r = lax.rsqrt(jnp.mean(xf * xf, axis=-1, keepdims=True) + EPS)
        xhat = xf * r
        dyf = dy_ref[...].astype(F32)
        dxh = dyf * g_ref[...]
        dx = r * (dxh - xhat * jnp.mean(dxh * xhat, axis=-1, keepdims=True))
        if has_res:
            dx = dx + rest[0][...]
        dx_ref[...] = dx.astype(dx_ref.dtype)
        part = jnp.sum(dyf * xhat, axis=0, keepdims=True)

        @pl.when(i == 0)
        def _():
            dg_ref[...] = part

        @pl.when(i > 0)
        def _():
            dg_ref[...] += part

    in_specs = [
        pl.BlockSpec((tm, n), lambda i: (i, col_blk)),
        pl.BlockSpec((1, n), lambda i: (0, 0)),
        pl.BlockSpec((tm, n), lambda i: (i, 0)),
    ]
    args = [x, g.reshape(1, n), dy]
    if has_res:
        in_specs.append(pl.BlockSpec((tm, n), lambda i: (i, 0)))
        args.append(dres)
    return pl.pallas_call(
        body,
        name=name,
        out_shape=(jax.ShapeDtypeStruct((m, n), out_dtype), jax.ShapeDtypeStruct((1, n), F32)),
        grid=(m // tm,),
        in_specs=in_specs,
        out_specs=(pl.BlockSpec((tm, n), lambda i: (i, 0)), pl.BlockSpec((1, n), lambda i: (0, 0))),
        compiler_params=_cparams("arbitrary"),
    )(*args)


_NT = (((1,), (1,)), ((), ()))
_NN = (((1,), (0,)), ((), ()))
_TN = (((0,), (0,)), ((), ()))


def _attn_scores(q, k, scale, decay_refs, i, tq, kn, pad):
    s = lax.dot_general(q, k, _NT, preferred_element_type=F32) * scale
    if decay_refs is not None:
        cq_ref, ck_ref = decay_refs
        s = s + (cq_ref[0] - ck_ref[0][:, :kn])
    t_idx = i * tq + lax.broadcasted_iota(jnp.int32, (tq, 1), 0)
    s_idx = lax.broadcasted_iota(jnp.int32, (1, kn), 1)
    mask = (s_idx <= t_idx) & (s_idx >= pad)
    return s, mask, t_idx


def _keys_needed(i, tq, m):
    return min(m, -(-((i + 1) * tq) // LANES) * LANES)


def attn_fwd(q, k, v, *, heads, dk, dv, qblk0, kblk0, vblk0, scale, pad, slot, branches=None, decay=None, name):
    m = q.shape[0]
    tq = _tile(m, 544, 16)
    has_decay = decay is not None

    def body(q_ref, k_ref, v_ref, *rest):
        o_ref, lse_ref = rest[-2:]
        decay_refs = rest[:2] if has_decay else None

        def block(i):
            kn = _keys_needed(i, tq, m)
            s, mask, t_idx = _attn_scores(q_ref[...], k_ref[0:kn, :], scale, decay_refs, i, tq, kn, pad)
            s = jnp.where(mask, s, NEG_INF)
            mx = jnp.max(s, axis=1, keepdims=True)
            p = jnp.exp(s - mx)
            l = jnp.sum(p, axis=1, keepdims=True)
            o = lax.dot_general(p.astype(BF16), v_ref[0:kn, :], _NN, preferred_element_type=F32) / l
            o_ref[...] = jnp.where(t_idx >= pad, o, 0.0).astype(o_ref.dtype)
            lse_ref[0] = mx + jnp.log(l)

        for i in range(m // tq):
            pl.when(pl.program_id(1) == i)(functools.partial(block, i))

    in_specs = [
        pl.BlockSpec((tq, dk), lambda h, i: (i, qblk0 + h)),
        pl.BlockSpec((m, dk), lambda h, i: (0, kblk0 + h)),
        pl.BlockSpec((m, dv), lambda h, i: (0, vblk0 + h)),
    ]
    args = [q, k, v]
    if has_decay:
        in_specs += [pl.BlockSpec((1, tq, 1), lambda h, i: (h, i, 0)), pl.BlockSpec((1, 1, m), lambda h, i: (h, 0, 0))]
        args += list(decay)
    aliases = {}
    if branches is not None:
        aliases = {len(args): 0}
        in_specs.append(pl.BlockSpec(memory_space=pl.ANY))
        args.append(branches)
    return pl.pallas_call(
        body,
        name=name,
        out_shape=(jax.ShapeDtypeStruct((3, m, heads * dv), BF16), jax.ShapeDtypeStruct((heads, m, 1), F32)),
        grid=(heads, m // tq),
        in_specs=in_specs,
        out_specs=(pl.BlockSpec((None, tq, dv), lambda h, i: (slot, i, h)), pl.BlockSpec((1, tq, 1), lambda h, i: (h, i, 0))),
        input_output_aliases=aliases,
        compiler_params=_cparams("parallel", "parallel"),
    )(*args)


def attn_bwd(q, k, v, do, do_sel, lse, *, heads, dk, dv, qblk0, kblk0, vblk0, scale, pad, decay=None, name):
    m = q.shape[0]
    tq = _tile(m, 544, 16)
    nq = m // tq
    has_decay = decay is not None

    def body(q_ref, k_ref, v_ref, do_ref, lse_ref, *rest):
        if has_decay:
            cq_ref, ck_ref, dq_ref, dk_ref, dv_ref, dck_ref, dk_acc, dv_acc = rest
            decay_refs = (cq_ref, ck_ref)
        else:
            dq_ref, dk_ref, dv_ref, dk_acc, dv_acc = rest
            decay_refs = None
        @pl.when(pl.program_id(1) == 0)
        def _():
            dk_acc[...] = jnp.zeros_like(dk_acc)
            dv_acc[...] = jnp.zeros_like(dv_acc)
            if has_decay:
                dck_ref[...] = jnp.zeros_like(dck_ref)

        def block(i):
            kn = _keys_needed(i, tq, m)
            qb, kb, dob = q_ref[...], k_ref[0:kn, :], do_ref[...]
            s, mask, _ = _attn_scores(qb, kb, scale, decay_refs, i, tq, kn, pad)
            p = jnp.where(mask, jnp.exp(s - lse_ref[0]), 0.0)
            dp = lax.dot_general(dob, v_ref[0:kn, :], _NT, preferred_element_type=F32)
            ds = p * (dp - jnp.sum(p * dp, axis=1, keepdims=True))
            dsb = ds.astype(BF16)
            dq_ref[...] = (lax.dot_general(dsb, kb, _NN, preferred_element_type=F32) * scale).astype(dq_ref.dtype)
            dk_acc[0:kn, :] += lax.dot_general(dsb, qb, _TN, preferred_element_type=F32) * scale
            dv_acc[0:kn, :] += lax.dot_general(p.astype(BF16), dob, _TN, preferred_element_type=F32)
            if has_decay:
                dck_ref[0, :, 0:kn] -= jnp.sum(ds, axis=0, keepdims=True)

        for i in range(nq):
            pl.when(pl.program_id(1) == i)(functools.partial(block, i))

        @pl.when(pl.program_id(1) == nq - 1)
        def _():
            dk_ref[...] = dk_acc[...].astype(dk_ref.dtype)
            dv_ref[...] = dv_acc[...].astype(dv_ref.dtype)

    in_specs = [
        pl.BlockSpec((tq, dk), lambda h, i: (i, qblk0 + h)),
        pl.BlockSpec((m, dk), lambda h, i: (0, kblk0 + h)),
        pl.BlockSpec((m, dv), lambda h, i: (0, vblk0 + h)),
        pl.BlockSpec((None, tq, dv), lambda h, i: (do_sel, i, h)),
        pl.BlockSpec((1, tq, 1), lambda h, i: (h, i, 0)),
    ]
    args = [q, k, v, do, lse]
    out_shape = [
        jax.ShapeDtypeStruct((m, heads * dk), BF16),
        jax.ShapeDtypeStruct((m, heads * dk), BF16),
        jax.ShapeDtypeStruct((m, heads * dv), BF16),
    ]
    out_specs = [
        pl.BlockSpec((tq, dk), lambda h, i: (i, h)),
        pl.BlockSpec((m, dk), lambda h, i: (0, h)),
        pl.BlockSpec((m, dv), lambda h, i: (0, h)),
    ]
    if has_decay:
        in_specs += [pl.BlockSpec((1, tq, 1), lambda h, i: (h, i, 0)), pl.BlockSpec((1, 1, m), lambda h, i: (h, 0, 0))]
        args += list(decay)
        out_shape.append(jax.ShapeDtypeStruct((heads, 1, m), F32))
        out_specs.append(pl.BlockSpec((1, 1, m), lambda h, i: (h, 0, 0)))
    return pl.pallas_call(
        body,
        name=name,
        out_shape=tuple(out_shape),
        grid=(heads, nq),
        in_specs=in_specs,
        out_specs=tuple(out_specs),
        scratch_shapes=[pltpu.VMEM((m, dk), F32), pltpu.VMEM((m, dv), F32)],
        compiler_params=_cparams("parallel", "arbitrary"),
    )(*args)


def rope_tables(cfg):
    half = ROPE // 2
    inv_freq = 1.0 / (ROPE_THETA ** (jnp.arange(0, ROPE, 2, dtype=F32) / ROPE))
    pos = (jnp.arange(cfg.m, dtype=jnp.int32) - cfg.pad).astype(F32)
    ang = pos[:, None] * inv_freq[None, :]
    cos, sin = jnp.cos(ang), jnp.sin(ang)
    z = jnp.zeros((cfg.m, half), F32)
    zz = jnp.zeros((cfg.m, LANES - ROPE), F32)
    return (
        jnp.concatenate([cos, cos, zz], axis=1),
        jnp.concatenate([-sin, z, zz], axis=1),
        jnp.concatenate([z, sin, zz], axis=1),
    )


def _rope(x, cos, s1, s2):
    return x * cos + pltpu.roll(x, LANES - ROPE // 2, 1) * s1 + pltpu.roll(x, ROPE // 2, 1) * s2


def mla_prep_fwd(cfg, q, kv, proj, tabs, *, name):
    m, h2 = cfg.m, 2 * LANES
    tm = _tile(m, 544, 16)
    kpe_blk = cfg.off_kpe // LANES

    def body(q_ref, kn_ref, kpe_ref, cos_ref, s1_ref, s2_ref, qf_ref, kf_ref):
        cos, s1, s2 = cos_ref[...], s1_ref[...], s2_ref[...]
        qv = q_ref[...]
        qf_ref[:, :LANES] = qv[:, :LANES]
        qf_ref[:, LANES:] = _rope(qv[:, LANES:].astype(F32), cos, s1, s2).astype(qf_ref.dtype)
        kf_ref[:, :LANES] = kn_ref[...]
        kf_ref[:, LANES:] = _rope(kpe_ref[...].astype(F32), cos, s1, s2).astype(kf_ref.dtype)

    tab = pl.BlockSpec((tm, LANES), lambda i, h: (i, 0))
    return pl.pallas_call(
        body,
        name=name,
        out_shape=(jax.ShapeDtypeStruct((m, cfg.heads * h2), BF16), jax.ShapeDtypeStruct((m, cfg.heads * h2), BF16)),
        grid=(m // tm, cfg.heads),
        in_specs=[
            pl.BlockSpec((tm, h2), lambda i, h: (i, h)),
            pl.BlockSpec((tm, LANES), lambda i, h: (i, h)),
            pl.BlockSpec((tm, LANES), lambda i, h: (i, kpe_blk)),
            tab, tab, tab,
        ],
        out_specs=(pl.BlockSpec((tm, h2), lambda i, h: (i, h)), pl.BlockSpec((tm, h2), lambda i, h: (i, h))),
        compiler_params=_cparams("parallel", "parallel"),
    )(q, kv, proj, *tabs)


def mla_prep_bwd(cfg, dqf, dkf, tabs_t, *, name):
    m, h2 = cfg.m, 2 * LANES
    tm = _tile(m, 544, 16)

    def body(dqf_ref, dkf_ref, cos_ref, s1_ref, s2_ref, dq_ref, dkn_ref, dkpe_ref):
        h = pl.program_id(1)
        cos, s1, s2 = cos_ref[...], s1_ref[...], s2_ref[...]
        dqv, dkv = dqf_ref[...], dkf_ref[...]
        dq_ref[:, :LANES] = dqv[:, :LANES]
        dq_ref[:, LANES:] = _rope(dqv[:, LANES:].astype(F32), cos, s1, s2).astype(dq_ref.dtype)
        dkn_ref[...] = dkv[:, :LANES]
        part = _rope(dkv[:, LANES:].astype(F32), cos, s1, s2)

        @pl.when(h == 0)
        def _():
            dkpe_ref[...] = part

        @pl.when(h > 0)
        def _():
            dkpe_ref[...] += part

    tab = pl.BlockSpec((tm, LANES), lambda i, h: (i, 0))
    return pl.pallas_call(
        body,
        name=name,
        out_shape=(
            jax.ShapeDtypeStruct((m, cfg.heads * h2), BF16),
            jax.ShapeDtypeStruct((m, cfg.heads * LANES), BF16),
            jax.ShapeDtypeStruct((m, LANES), F32),
        ),
        grid=(m // tm, cfg.heads),
        in_specs=[pl.BlockSpec((tm, h2), lambda i, h: (i, h)), pl.BlockSpec((tm, h2), lambda i, h: (i, h)), tab, tab, tab],
        out_specs=(
            pl.BlockSpec((tm, h2), lambda i, h: (i, h)),
            pl.BlockSpec((tm, LANES), lambda i, h: (i, h)),
            pl.BlockSpec((tm, LANES), lambda i, h: (i, 0)),
        ),
        compiler_params=_cparams("parallel", "arbitrary"),
    )(dqf, dkf, *tabs_t)


def _conv_parts(b_ref, c_ref, x_ref, w_ref, m):
    b, c, x = b_ref[...].astype(F32), c_ref[...].astype(F32), x_ref[...].astype(F32)
    u = c * x
    row = lax.broadcasted_iota(jnp.int32, (m, 1), 0)
    u1 = jnp.where(row >= 1, pltpu.roll(u, 1, 0), 0.0)
    u2 = jnp.where(row >= 2, pltpu.roll(u, 2, 0), 0.0)
    w0, w1, w2 = w_ref[0:1, :], w_ref[1:2, :], w_ref[2:3, :]
    uc = w0 * u2 + w1 * u1 + w2 * u
    return b, c, x, u, u1, u2, uc, (w0, w1, w2), row


def _conv_specs(cfg, tn):
    m, nb, blk0 = cfg.m, cfg.width // tn, cfg.off_conv // tn
    return [
        pl.BlockSpec((m, tn), lambda j: (0, blk0 + j)),
        pl.BlockSpec((m, tn), lambda j: (0, blk0 + nb + j)),
        pl.BlockSpec((m, tn), lambda j: (0, blk0 + 2 * nb + j)),
        pl.BlockSpec((3, tn), lambda j: (0, j)),
    ]


def conv_fwd(cfg, proj, conv_w, branches, slot, *, name):
    m, tn = cfg.m, LANES

    def body(b_ref, c_ref, x_ref, w_ref, kept_ref, o_ref):
        b, _, _, _, _, _, uc, _, _ = _conv_parts(b_ref, c_ref, x_ref, w_ref, m)
        o_ref[...] = (b * uc).astype(o_ref.dtype)

    return pl.pallas_call(
        body,
        name=name,
        out_shape=jax.ShapeDtypeStruct((3, m, cfg.width), BF16),
        grid=(cfg.width // tn,),
        in_specs=_conv_specs(cfg, tn) + [pl.BlockSpec(memory_space=pl.ANY)],
        out_specs=pl.BlockSpec((None, m, tn), lambda j: (slot, 0, j)),
        input_output_aliases={4: 0},
        compiler_params=_cparams("parallel"),
    )(proj, proj, proj, conv_w, branches)


def conv_bwd(cfg, proj, conv_w, do, do_sel, *, name):
    m, tn = cfg.m, LANES

    def body(b_ref, c_ref, x_ref, w_ref, do_ref, db_ref, dc_ref, dx_ref, dw_ref):
        b, c, x, u, u1, u2, uc, (w0, w1, w2), row = _conv_parts(b_ref, c_ref, x_ref, w_ref, m)
        dob = do_ref[...].astype(F32)
        db_ref[...] = (dob * uc).astype(db_ref.dtype)
        duc = dob * b
        up1 = jnp.where(row <= m - 2, pltpu.roll(duc, m - 1, 0), 0.0)
        up2 = jnp.where(row <= m - 3, pltpu.roll(duc, m - 2, 0), 0.0)
        du = w2 * duc + w1 * up1 + w0 * up2
        dc_ref[...] = (du * x).astype(dc_ref.dtype)
        dx_ref[...] = (du * c).astype(dx_ref.dtype)
        dw_ref[0:1, :] = jnp.sum(duc * u2, axis=0, keepdims=True)
        dw_ref[1:2, :] = jnp.sum(duc * u1, axis=0, keepdims=True)
        dw_ref[2:3, :] = jnp.sum(duc * u, axis=0, keepdims=True)

    act = jax.ShapeDtypeStruct((m, cfg.width), BF16)
    blk = pl.BlockSpec((m, tn), lambda j: (0, j))
    return pl.pallas_call(
        body,
        name=name,
        out_shape=(act, act, act, jax.ShapeDtypeStruct((3, cfg.width), F32)),
        grid=(cfg.width // tn,),
        in_specs=_conv_specs(cfg, tn) + [pl.BlockSpec((None, m, tn), lambda j: (do_sel, 0, j))],
        out_specs=(blk, blk, blk, pl.BlockSpec((3, tn), lambda j: (0, j))),
        compiler_params=_cparams("parallel"),
    )(proj, proj, proj, conv_w, do)


def _tri(lower):
    r = lax.broadcasted_iota(jnp.int32, (LANES, LANES), 0)
    c = lax.broadcasted_iota(jnp.int32, (LANES, LANES), 1)
    return jnp.where((r >= c) if lower else (r <= c), 1.0, 0.0).astype(F32)


def fox_gate_fwd(cfg, fl, b_pad, *, name):
    m = cfg.m
    nblk = m // LANES

    def body(fl_ref, b_ref, c_ref):
        z = fl_ref[...] + b_ref[...]
        logf = jnp.minimum(z, 0.0) - jnp.log(1.0 + jnp.exp(-jnp.abs(z)))
        row = lax.broadcasted_iota(jnp.int32, (m, 1), 0)
        logf = jnp.where(row >= cfg.pad, logf, 0.0)
        tri = _tri(True)
        carry = jnp.zeros((1, LANES), F32)
        for blk in range(nblk):
            cb = jnp.dot(tri, logf[blk * LANES:(blk + 1) * LANES, :], precision=lax.Precision.HIGHEST,
                         preferred_element_type=F32) + carry
            c_ref[blk * LANES:(blk + 1) * LANES, :] = cb
            carry = cb[LANES - 1:LANES, :]

    full = pl.BlockSpec((m, LANES), lambda: (0, 0))
    return pl.pallas_call(
        body,
        name=name,
        out_shape=jax.ShapeDtypeStruct((m, LANES), F32),
        in_specs=[full, pl.BlockSpec((1, LANES), lambda: (0, 0))],
        out_specs=full,
        compiler_params=pltpu.CompilerParams(vmem_limit_bytes=VMEM_LIMIT_BYTES),
    )(fl, b_pad)


def fox_gate_bwd(cfg, fl, b_pad, dc, *, name):
    m = cfg.m
    nblk = m // LANES

    def body(fl_ref, b_ref, dc_ref, dfl_ref, db_ref):
        z = fl_ref[...] + b_ref[...]
        dlogsig = 1.0 / (1.0 + jnp.exp(z))
        row = lax.broadcasted_iota(jnp.int32, (m, 1), 0)
        gate = jnp.where(row >= cfg.pad, dlogsig, 0.0)
        dcv = dc_ref[...]
        tri = _tri(False)
        carry = jnp.zeros((1, LANES), F32)
        db = jnp.zeros((1, LANES), F32)
        for blk in reversed(range(nblk)):
            sl = slice(blk * LANES, (blk + 1) * LANES)
            rb = jnp.dot(tri, dcv[sl, :], precision=lax.Precision.HIGHEST, preferred_element_type=F32) + carry
            carry = rb[0:1, :]
            dfl = rb * gate[sl, :]
            dfl_ref[sl, :] = dfl
            db = db + jnp.sum(dfl, axis=0, keepdims=True)
        db_ref[...] = db

    full = pl.BlockSpec((m, LANES), lambda: (0, 0))
    one = pl.BlockSpec((1, LANES), lambda: (0, 0))
    return pl.pallas_call(
        body,
        name=name,
        out_shape=(jax.ShapeDtypeStruct((m, LANES), F32), jax.ShapeDtypeStruct((1, LANES), F32)),
        in_specs=[full, one, full],
        out_specs=(full, one),
        compiler_params=pltpu.CompilerParams(vmem_limit_bytes=VMEM_LIMIT_BYTES),
    )(fl, b_pad, dc)


def _sigmoid(x):
    return 1.0 / (1.0 + jnp.exp(-x))


def gate_merge_fwd(cfg, y, proj, *, name):
    m, d = cfg.m, cfg.d
    tm, tn = _tile(m, 1088, 16), _tile(d, 512)
    nd = d // tn

    def body(y_ref, g0_ref, g1_ref, g2_ref, o_ref):
        acc = None
        for n, g_ref in enumerate((g0_ref, g1_ref, g2_ref)):
            t = _sigmoid(g_ref[...].astype(F32)) * y_ref[n].astype(F32)
            acc = t if acc is None else acc + t
        o_ref[...] = acc.astype(o_ref.dtype)

    gate = lambda n: pl.BlockSpec((tm, tn), lambda i, j: (i, n * nd + j))
    return pl.pallas_call(
        body,
        name=name,
        out_shape=jax.ShapeDtypeStruct((m, d), BF16),
        grid=(m // tm, nd),
        in_specs=[pl.BlockSpec((3, tm, tn), lambda i, j: (0, i, j)), gate(0), gate(1), gate(2)],
        out_specs=pl.BlockSpec((tm, tn), lambda i, j: (i, j)),
        compiler_params=_cparams("parallel", "parallel"),
    )(y, proj, proj, proj)


def gate_merge_bwd(cfg, dm, y, proj, *, name):
    m, d = cfg.m, cfg.d
    tm, tn = _tile(m, 1088, 16), _tile(d, 512)
    nd = d // tn

    def body(dm_ref, y_ref, g_ref, dy_ref, dg_ref):
        sg = _sigmoid(g_ref[...].astype(F32))
        dmv = dm_ref[...].astype(F32)
        dy_ref[...] = (sg * dmv).astype(dy_ref.dtype)
        dg_ref[...] = (dmv * y_ref[...].astype(F32) * sg * (1.0 - sg)).astype(dg_ref.dtype)

    return pl.pallas_call(
        body,
        name=name,
        out_shape=(jax.ShapeDtypeStruct((3, m, d), BF16), jax.ShapeDtypeStruct((m, 3 * d), BF16)),
        grid=(m // tm, nd, 3),
        in_specs=[
            pl.BlockSpec((tm, tn), lambda i, j, n: (i, j)),
            pl.BlockSpec((None, tm, tn), lambda i, j, n: (n, i, j)),
            pl.BlockSpec((tm, tn), lambda i, j, n: (i, n * nd + j)),
        ],
        out_specs=(
            pl.BlockSpec((None, tm, tn), lambda i, j, n: (n, i, j)),
            pl.BlockSpec((tm, tn), lambda i, j, n: (i, n * nd + j)),
        ),
        compiler_params=_cparams("parallel", "parallel", "parallel"),
    )(dm, y, proj)


def swiglu_fwd(cfg, gu, *, name):
    m, f = cfg.m, cfg.d_ff
    tm, tn = _tile(m, 1088, 16), _tile(f, 512)
    nf = f // tn

    def body(g_ref, u_ref, o_ref):
        g = g_ref[...].astype(F32)
        o_ref[...] = (g * _sigmoid(g) * u_ref[...].astype(F32)).astype(o_ref.dtype)

    return pl.pallas_call(
        body,
        name=name,
        out_shape=jax.ShapeDtypeStruct((m, f), BF16),
        grid=(m // tm, nf),
        in_specs=[pl.BlockSpec((tm, tn), lambda i, j: (i, j)), pl.BlockSpec((tm, tn), lambda i, j: (i, nf + j))],
        out_specs=pl.BlockSpec((tm, tn), lambda i, j: (i, j)),
        compiler_params=_cparams("parallel", "parallel"),
    )(gu, gu)


def swiglu_bwd(cfg, dact, gu, *, name):
    m, f = cfg.m, cfg.d_ff
    tm, tn = _tile(m, 1088, 16), _tile(f, 512)
    nf = f // tn

    def body(da_ref, g_ref, u_ref, o_ref):
        j = pl.program_id(1)
        g, u, da = g_ref[...].astype(F32), u_ref[...].astype(F32), da_ref[...].astype(F32)
        sg = _sigmoid(g)
        dg = da * u * sg * (1.0 + g * (1.0 - sg))
        du = da * g * sg
        o_ref[...] = jnp.where(j < nf, dg, du).astype(o_ref.dtype)

    return pl.pallas_call(
        body,
        name=name,
        out_shape=jax.ShapeDtypeStruct((m, 2 * f), BF16),
        grid=(m // tm, 2 * nf),
        in_specs=[
            pl.BlockSpec((tm, tn), lambda i, j: (i, j % nf)),
            pl.BlockSpec((tm, tn), lambda i, j: (i, j % nf)),
            pl.BlockSpec((tm, tn), lambda i, j: (i, nf + j % nf)),
        ],
        out_specs=pl.BlockSpec((tm, tn), lambda i, j: (i, j)),
        compiler_params=_cparams("parallel", "parallel"),
    )(dact, gu, gu)


def loss_head(cfg, h, target, *, name):
    m, d = cfg.m, cfg.d
    assert cfg.pad + cfg.n_meta == LANES
    tm = LANES
    inv_d = 1.0 / d

    def body(h_ref, t_ref, dh_ref, loss_ref):
        i = pl.program_id(0)

        @pl.when(i == 0)
        def _():
            dh_ref[...] = jnp.zeros_like(dh_ref)
            loss_ref[...] = jnp.zeros_like(loss_ref)

        @pl.when(i > 0)
        def _():
            err = h_ref[...] - t_ref[...]
            dh_ref[...] = err * inv_d
            loss_ref[...] += 0.5 * inv_d * jnp.sum(err * err)

    return pl.pallas_call(
        body,
        name=name,
        out_shape=(jax.ShapeDtypeStruct((m, d), F32), jax.ShapeDtypeStruct((8, LANES), F32)),
        grid=(m // tm,),
        in_specs=[pl.BlockSpec((tm, d), lambda i: (i, 0)), pl.BlockSpec((tm, d), lambda i: (jnp.maximum(i - 1, 0), 0))],
        out_specs=(pl.BlockSpec((tm, d), lambda i: (i, 0)), pl.BlockSpec((8, LANES), lambda i: (0, 0))),
        compiler_params=_cparams("arbitrary"),
    )(h, target)


def adamw(w, g, m_, v_, *, name):
    r, c = w.shape
    c_pad = -(-c // LANES) * LANES
    tr = r
    if r % 8 == 0:
        tr = _tile(r, max(8, (3 << 19) // (4 * c_pad) // 8 * 8), 8)
    bc1 = 1.0 - ADAM_B1 ** ADAM_STEP
    bc2 = 1.0 - ADAM_B2 ** ADAM_STEP

    def body(w_ref, g_ref, m_ref, v_ref, d_ref, nm_ref, nv_ref):
        gv = g_ref[...]
        nm = ADAM_B1 * m_ref[...] + (1.0 - ADAM_B1) * gv
        nv = ADAM_B2 * v_ref[...] + (1.0 - ADAM_B2) * (gv * gv)
        d_ref[...] = -ADAM_LR * ((nm / bc1) / (jnp.sqrt(nv / bc2) + ADAM_EPS) + ADAM_WD * w_ref[...])
        nm_ref[...] = nm
        nv_ref[...] = nv

    blk = pl.BlockSpec((tr, c), lambda i: (i, 0))
    shp = jax.ShapeDtypeStruct((r, c), F32)
    return pl.pallas_call(
        body,
        name=name,
        out_shape=(shp, shp, shp),
        grid=(r // tr,),
        in_specs=[blk, blk, blk, blk],
        out_specs=(blk, blk, blk),
        compiler_params=_cparams("parallel"),
    )(w, g, m_, v_)


def adamw_halves(w, g_own, g_other, core, m_, v_, *, first_layer=0, prev=None, name):
    _, r, c = w.shape
    nl = g_own.shape[0]
    r2 = r // 2
    c_pad = -(-c // LANES) * LANES
    tr = _tile(r2, max(8, (3 << 19) // (4 * c_pad) // 8 * 8), 8)
    nr = r2 // tr
    bc1 = 1.0 - ADAM_B1 ** ADAM_STEP
    bc2 = 1.0 - ADAM_B2 ** ADAM_STEP

    def body(core_ref, w_ref, go_ref, gr_ref, m_ref, v_ref, *rest):
        g_ref, d_ref, nm_ref, nv_ref = rest[-4:]
        gv = jnp.where(pl.program_id(2) == core_ref[0], go_ref[...], gr_ref[...])[:, :c]
        nm = ADAM_B1 * m_ref[...] + (1.0 - ADAM_B1) * gv
        nv = ADAM_B2 * v_ref[...] + (1.0 - ADAM_B2) * (gv * gv)
        d_ref[...] = -ADAM_LR * ((nm / bc1) / (jnp.sqrt(nv / bc2) + ADAM_EPS) + ADAM_WD * w_ref[...])
        g_ref[...] = gv
        nm_ref[...] = nm
        nv_ref[...] = nv

    full = pl.BlockSpec((None, tr, c), lambda l, i, hf, core_ref: (first_layer + l, hf * nr + i, 0))
    half = pl.BlockSpec((None, tr, g_own.shape[2]), lambda l, i, hf, core_ref: (l, i, 0))
    shp = jax.ShapeDtypeStruct(w.shape, F32)
    kept = list(prev or ())
    return pl.pallas_call(
        body,
        name=name,
        out_shape=(shp, shp, shp, shp),
        grid_spec=pltpu.PrefetchScalarGridSpec(
            num_scalar_prefetch=1,
            grid=(nl, nr, 2),
            in_specs=[full, half, half, full, full] + [pl.BlockSpec(memory_space=pl.ANY)] * len(kept),
            out_specs=(full, full, full, full),
        ),
        input_output_aliases={6 + k: k for k in range(len(kept))},
        compiler_params=_cparams("parallel", "parallel", "arbitrary"),
    )(core, w, g_own, g_other, m_, v_, *kept)


_HBM = pl.BlockSpec(memory_space=pltpu.HBM)


def _place():
    x, y, c = lax.axis_index("x"), lax.axis_index("y"), lax.axis_index("c")
    flips = [(1 - x, y), (x, 1 - y), (1 - x, 1 - y)]
    return x, y, c, flips


_SEM = pl.BlockSpec(memory_space=pltpu.SEMAPHORE)
_EFFECT = pltpu.SideEffectType.DATAFLOW_SIDE_EFFECTING


def _gather_plan(halves):
    def plan(src_refs, land_refs, arrival):
        x, y, c, flips = _place()
        mine = 2 * x + y
        out = []
        for w, h in enumerate(halves):
            for fx, fy in flips:
                slot = (2 * fx + fy) if arrival else mine
                out.append((src_refs[w].at[pl.ds(c * h, h), :], land_refs[w].at[slot, pl.ds(c * h, h), :], (fx, fy, c)))
        return out
    return plan


def _exchange_plan(nw):
    def plan(src_refs, land_refs, arrival):
        _, _, c, flips = _place()
        return [(src_refs[w].at[2 * fx + fy], land_refs[w].at[k], (fx, fy, c)) for w in range(nw) for k, (fx, fy) in enumerate(flips)]
    return plan


def _swap_plan(halves):
    def plan(src_refs, land_refs, arrival):
        x, y, c, _ = _place()
        return [(src_refs[w].at[:, pl.ds((1 - c) * h, h), :], land_refs[w], (x, y, 1 - c)) for w, h in enumerate(halves)]
    return plan


def _share_plan(nw):
    def plan(src_refs, land_refs, arrival):
        x, y, c, _ = _place()
        return [(src_refs[w], land_refs[w], (x, y, 1 - c)) for w in range(nw)]
    return plan


def copies_start(srcs, land_shapes, plan, n_copies, *, name):
    lands = [lax.empty(s, a.dtype) for s, a in zip(land_shapes, srcs)]
    n_in = len(srcs) + len(lands)

    def body(*refs):
        src_refs, land_refs = refs[:len(srcs)], refs[len(srcs):n_in]
        send_sems, recv_sems, token = refs[n_in], refs[n_in + 1], refs[-1]
        for i, (src, dst, to) in enumerate(plan(src_refs, land_refs, False)):
            pltpu.make_async_remote_copy(src_ref=src, dst_ref=dst, send_sem=send_sems.at[i], recv_sem=recv_sems.at[i],
                                         device_id=to, device_id_type=MESH).start()
        token[...] = jnp.zeros_like(token)

    operands = list(srcs) + lands
    out = pl.pallas_call(
        body,
        name=name,
        out_shape=(pltpu.SemaphoreType.DMA((n_copies,)), pltpu.SemaphoreType.DMA((n_copies,)),
                   *[pltpu.HBM(a.shape, a.dtype) for a in operands], jax.ShapeDtypeStruct((8, LANES), F32)),
        in_specs=[_HBM] * n_in,
        out_specs=(_SEM, _SEM, *[_HBM] * n_in, pl.BlockSpec(memory_space=pltpu.VMEM)),
        input_output_aliases={i: 2 + i for i in range(n_in)},
        compiler_params=pltpu.CompilerParams(has_side_effects=_EFFECT),
    )(*[pltpu.with_memory_space_constraint(a, pltpu.HBM) for a in operands])
    return out[0], out[1], list(out[2:2 + len(srcs)]), list(out[2 + len(srcs):2 + n_in]), out[-1]


def copies_wait(send_sems, recv_sems, srcs, lands, plan, after, *, name):
    n_in = len(srcs) + len(lands)

    def body(*refs):
        src_refs, land_refs = refs[:len(srcs)], refs[len(srcs):n_in]
        send_ref, recv_ref, token = refs[n_in], refs[n_in + 1], refs[-1]
        token[...] = jnp.zeros_like(token)
        for i, (src, dst, to) in enumerate(plan(src_refs, land_refs, True)):
            copy = pltpu.make_async_remote_copy(src_ref=src, dst_ref=dst, send_sem=send_ref.at[i], recv_sem=recv_ref.at[i],
                                                device_id=to, device_id_type=MESH)
            copy.wait_send()
            copy.wait_recv()

    operands = list(srcs) + list(lands)
    out = pl.pallas_call(
        body,
        name=name,
        out_shape=(*[pltpu.HBM(a.shape, a.dtype) for a in operands], jax.ShapeDtypeStruct((8, LANES), F32)),
        in_specs=[_HBM] * n_in + [_SEM, _SEM, pl.BlockSpec(memory_space=pl.ANY)],
        out_specs=(*[_HBM] * n_in, pl.BlockSpec(memory_space=pltpu.VMEM)),
        input_output_aliases={i: i for i in range(n_in)},
        compiler_params=pltpu.CompilerParams(has_side_effects=_EFFECT),
    )(*operands, send_sems, recv_sems, after)
    return list(out[:len(srcs)]), list(out[len(srcs):n_in]), out[-1]


def forward_halves(lands, *, name):
    nw = len(lands)
    halves = [a.shape[1] // 2 for a in lands]

    def body(*refs):
        ins, outs = refs[:nw], refs[nw:2 * nw]
        send_sems, recv_sems = refs[2 * nw:]
        x, y, c, flips = _place()
        copies = []
        for w, h in enumerate(halves):
            for k, (fx, fy) in enumerate(flips):
                rows = (2 * fx + fy, pl.ds(c * h, h), slice(None))
                copies.append(pltpu.make_async_remote_copy(src_ref=ins[w].at[rows], dst_ref=outs[w].at[rows], send_sem=send_sems.at[3 * w + k],
                                                           recv_sem=recv_sems.at[3 * w + k], device_id=(x, y, 1 - c), device_id_type=MESH))
        for cp in copies:
            cp.start()
        for cp in copies:
            cp.wait()

    return pl.pallas_call(
        body,
        name=name,
        out_shape=tuple(jax.ShapeDtypeStruct(a.shape, a.dtype) for a in lands),
        in_specs=[_HBM] * nw,
        out_specs=tuple([_HBM] * nw),
        input_output_aliases={w: w for w in range(nw)},
        scratch_shapes=[pltpu.SemaphoreType.DMA((3 * nw,)), pltpu.SemaphoreType.DMA((3 * nw,))],
    )(*lands)


def share_halves(sums, *, name):
    nw = len(sums)

    def body(*refs):
        ins, outs = refs[:nw], refs[nw:2 * nw]
        send_sems, recv_sems = refs[2 * nw:]
        x, y, c, _ = _place()
        copies = [
            pltpu.make_async_remote_copy(src_ref=ins[w], dst_ref=outs[w], send_sem=send_sems.at[w], recv_sem=recv_sems.at[w],
                                         device_id=(x, y, 1 - c), device_id_type=MESH)
            for w in range(nw)
        ]
        for cp in copies:
            cp.start()
        for cp in copies:
            cp.wait()

    return pl.pallas_call(
        body,
        name=name,
        out_shape=tuple(jax.ShapeDtypeStruct(s.shape, s.dtype) for s in sums),
        in_specs=[_HBM] * nw,
        out_specs=tuple([_HBM] * nw),
        scratch_shapes=[pltpu.SemaphoreType.DMA((nw,)), pltpu.SemaphoreType.DMA((nw,))],
    )(*sums)


def gather_blocks(block, *, reduce, name):
    rows, cols = block.shape

    def body(x_ref, out_ref, *rest):
        if reduce:
            buf_ref, send_sems, recv_sems = rest
        else:
            send_sems, recv_sems = rest
            buf_ref = out_ref
        x, y, c, flips = _place()
        me, sibling = (x, y, c), (x, y, 1 - c)

        def slot(px, py, pc):
            return buf_ref.at[4 * px + 2 * py + pc]

        def copy(k, blk, to, src=None):
            return pltpu.make_async_remote_copy(src_ref=slot(*blk) if src is None else src, dst_ref=slot(*blk),
                                                send_sem=send_sems.at[k], recv_sem=recv_sems.at[k], device_id=to,
                                                device_id_type=MESH)

        buf_ref[4 * x + 2 * y + c] = x_ref[...]
        first = [copy(0, me, sibling, src=x_ref)]
        first += [copy(1 + j, me, (*chip, c), src=x_ref) for j, chip in enumerate(flips)]
        for cp in first:
            cp.start()
        passed = [copy(4 + j, (*chip, c), sibling) for j, chip in enumerate(flips)]
        for j, chip in enumerate(flips):
            copy(1 + j, (*chip, c), me).wait_recv()
            passed[j].start()
        copy(0, sibling, me).wait_recv()
        for j, chip in enumerate(flips):
            copy(4 + j, (*chip, 1 - c), me).wait_recv()
        for cp in first + passed:
            cp.wait_send()
        if reduce:
            acc = buf_ref[0]
            for dev in range(1, N_DEV):
                acc = acc + buf_ref[dev]
            out_ref[...] = acc

    vmem = pl.BlockSpec(memory_space=pltpu.VMEM)
    sems = [pltpu.SemaphoreType.DMA((7,)), pltpu.SemaphoreType.DMA((7,))]
    if reduce:
        out_shape = jax.ShapeDtypeStruct((rows, cols), block.dtype)
        scratch = [pltpu.VMEM((N_DEV, rows, cols), block.dtype)] + sems
    else:
        out_shape = jax.ShapeDtypeStruct((N_DEV, rows, cols), block.dtype)
        scratch = sems
    return pl.pallas_call(
        body,
        name=name,
        out_shape=out_shape,
        in_specs=[vmem],
        out_specs=vmem,
        scratch_shapes=scratch,
        compiler_params=pltpu.CompilerParams(vmem_limit_bytes=VMEM_LIMIT_BYTES),
    )(block)


def add_own_half(grad, recv, core, *, name):
    _, r2, cols = recv.shape
    tr = _tile(r2, max(16, (1 << 20) // (2 * cols) // 16 * 16), 16)
    nr = r2 // tr

    def body(core_ref, g_ref, r_ref, o_ref):
        o_ref[...] = (g_ref[...].astype(F32) + r_ref[...].astype(F32)).astype(o_ref.dtype)

    return pl.pallas_call(
        body,
        name=name,
        out_shape=jax.ShapeDtypeStruct(recv.shape, BF16),
        grid_spec=pltpu.PrefetchScalarGridSpec(
            num_scalar_prefetch=1,
            grid=(N_CHIPS, nr),
            in_specs=[
                pl.BlockSpec((None, tr, cols), lambda k, i, core_ref: (k, core_ref[0] * nr + i, 0)),
                pl.BlockSpec((None, tr, cols), lambda k, i, core_ref: (k, i, 0)),
            ],
            out_specs=pl.BlockSpec((None, tr, cols), lambda k, i, core_ref: (k, i, 0)),
        ),
        compiler_params=_cparams("parallel", "parallel"),
    )(core, grad, recv)


def sum_chips(part, recv, chip, sums, layer, *, name):
    _, r2, cols = part.shape
    tr = _tile(r2, max(16, (1 << 20) // (2 * cols) // 16 * 16), 16)

    def body(chip_ref, p_ref, r_ref, sums_ref, o_ref):
        acc = p_ref[...].astype(F32)
        for k in range(3):
            acc = acc + r_ref[k].astype(F32)
        o_ref[...] = acc

    return pl.pallas_call(
        body,
        name=name,
        out_shape=jax.ShapeDtypeStruct(sums.shape, F32),
        grid_spec=pltpu.PrefetchScalarGridSpec(
            num_scalar_prefetch=1,
            grid=(r2 // tr,),
            in_specs=[
                pl.BlockSpec((None, tr, cols), lambda i, chip_ref: (chip_ref[0], i, 0)),
                pl.BlockSpec((3, tr, cols), lambda i, chip_ref: (0, i, 0)),
                pl.BlockSpec(memory_space=pl.ANY),
            ],
            out_specs=pl.BlockSpec((None, tr, cols), lambda i, chip_ref: (layer, i, 0)),
        ),
        input_output_aliases={3: 0},
        compiler_params=_cparams("parallel"),
    )(chip, part, recv, sums)


WEIGHTS = ("w_in", "w_uq", "w_ukv", "w_branch", "w_out", "w_ffn_in", "w_ffn_out")


def layer_fwd(cfg, h, w, s, tabs, tag):
    m, d, hd = cfg.m, cfg.d, cfg.heads
    fox_blk = cfg.off_fox // LANES
    hn = rmsnorm_fwd(h, s["g_mix_pre"], BF16, name=f"norm_mix_pre{tag}")
    proj = matmul(hn, w["w_in"], "nn", BF16, tm=m, tn=_tile(cfg.d_inp, 512), tk=d, name=f"proj{tag}")
    fl = matmul(hn, w["w_in"][:, cfg.off_fl:cfg.off_fl + LANES], "nn", F32, tm=m, tn=LANES, tk=d, name=f"proj_forget{tag}")
    cqn = rmsnorm_fwd(proj, s["g_q_lat"], BF16, width=cfg.q_rank, col_blk=cfg.off_cq // cfg.q_rank, name=f"norm_q{tag}")
    ckvn = rmsnorm_fwd(proj, s["g_kv_lat"], BF16, width=cfg.kv_rank, col_blk=cfg.off_ckv // cfg.kv_rank, name=f"norm_kv{tag}")
    q = matmul(cqn, w["w_uq"], "nn", BF16, tm=m, tn=_tile(2 * cfg.width, 512), tk=cfg.q_rank, name=f"up_q{tag}")
    kv = matmul(ckvn, w["w_ukv"], "nn", BF16, tm=m, tn=_tile(2 * cfg.width, 512), tk=cfg.kv_rank, name=f"up_kv{tag}")
    qf, kf = mla_prep_fwd(cfg, q, kv, proj, tabs[0], name=f"mla_prep{tag}")
    o, lse_a = attn_fwd(qf, kf, kv, heads=hd, dk=2 * LANES, dv=LANES, qblk0=0, kblk0=0, vblk0=hd,
                        scale=(LANES + ROPE) ** -0.5, pad=cfg.pad, slot=0, name=f"mla_attn{tag}")
    o = conv_fwd(cfg, proj, s["conv_w"], o, 1, name=f"conv{tag}")
    b_pad = jnp.pad(s["b_forget"], (0, LANES - hd)).reshape(1, LANES)
    cum = fox_gate_fwd(cfg, fl, b_pad, name=f"fox_gate{tag}")
    cum_t = cum[:, :hd].T
    decay = (cum_t[:, :, None], cum_t[:, None, :])
    o, lse_c = attn_fwd(proj, proj, proj, heads=hd, dk=LANES, dv=LANES, qblk0=fox_blk, kblk0=fox_blk + hd, vblk0=fox_blk + 2 * hd,
                        scale=LANES ** -0.5, pad=cfg.pad, slot=2, branches=o, decay=decay, name=f"fox_attn{tag}")
    y = matmul(o, w["w_branch"], "nn", BF16, tm=m, tn=_tile(d, 512), tk=cfg.width, name=f"branch{tag}")
    merged = gate_merge_fwd(cfg, y, proj, name=f"merge{tag}")
    mix = matmul(merged, w["w_out"], "nn", F32, tm=m, tn=_tile(d, 256), tk=d, name=f"out_proj{tag}")
    h_mid = rmsnorm_fwd(mix, s["g_mix_post"], F32, res=h, name=f"norm_mix_post{tag}")
    if hasattr(w, "land_ffn"):
        s = w.land_ffn(h_mid, s)
    hn2 = rmsnorm_fwd(h_mid, s["g_ffn_pre"], BF16, name=f"norm_ffn_pre{tag}")
    gu = matmul(hn2, w["w_ffn_in"], "nn", BF16, tm=m, tn=_tile(2 * cfg.d_ff, 512), tk=d, name=f"ffn_in{tag}")
    act = swiglu_fwd(cfg, gu, name=f"swiglu{tag}")
    f = matmul(act, w["w_ffn_out"], "nn", F32, tm=m, tn=_tile(d, 512), tk=_tile(cfg.d_ff, 2816), name=f"ffn_out{tag}")
    h_next = rmsnorm_fwd(f, s["g_ffn_post"], F32, res=h_mid, name=f"norm_ffn_post{tag}")
    saved = dict(h=h, hn=hn, proj=proj, fl=fl, cqn=cqn, ckvn=ckvn, kv=kv, qf=qf, kf=kf, lse_a=lse_a,
                 b_pad=b_pad, decay=decay, lse_c=lse_c, o=o, y=y, merged=merged, mix=mix, h_mid=h_mid,
                 hn2=hn2, gu=gu, act=act, f=f)
    return h_next, s, saved


def layer_bwd(cfg, dh, w, s, r, tabs, tag, grads_done):
    m, d, hd = cfg.m, cfg.d, cfg.heads
    fox_blk = cfg.off_fox // LANES
    tk_m = m
    df, dg4 = rmsnorm_bwd(r["f"], s["g_ffn_post"], dh, BF16, name=f"norm_ffn_post_bwd{tag}")
    dact = matmul(df, w["w_ffn_out"], "nt", BF16, tm=m, tn=_tile(cfg.d_ff, 512), tk=d, name=f"ffn_out_dx{tag}")
    dw_fo = matmul(r["act"], df, "tn", BF16, tm=_tile(cfg.d_ff, 512), tn=_tile(d, 1024), tk=tk_m, name=f"ffn_out_dw{tag}")
    dgu = swiglu_bwd(cfg, dact, r["gu"], name=f"swiglu_bwd{tag}")
    dhn2 = matmul(dgu, w["w_ffn_in"], "nt", F32, tm=m, tn=_tile(d, 512), tk=_tile(2 * cfg.d_ff, 2816), name=f"ffn_in_dx{tag}")
    dw_fi = matmul(r["hn2"], dgu, "tn", BF16, tm=_tile(d, 1024), tn=_tile(2 * cfg.d_ff // N_CHIPS, 1408), tk=tk_m, chip_cols=True,
                   name=f"ffn_in_dw{tag}")
    token = grads_done(dict(w_ffn_in=dw_fi, w_ffn_out=dw_fo))
    if token is not None:
        s = {**s, "g_ffn_pre": s["g_ffn_pre"] + token[0, 0]}
    dh_mid, dg3 = rmsnorm_bwd(r["h_mid"], s["g_ffn_pre"], dhn2, F32, dres=dh, name=f"norm_ffn_pre_bwd{tag}")
    dmix, dg2 = rmsnorm_bwd(r["mix"], s["g_mix_post"], dh_mid, BF16, name=f"norm_mix_post_bwd{tag}")
    dmerged = matmul(dmix, w["w_out"], "nt", BF16, tm=m, tn=_tile(d, 512), tk=d, name=f"out_proj_dx{tag}")
    dw_out = matmul(r["merged"], dmix, "tn", BF16, tm=_tile(d, 1024), tn=_tile(d, 512), tk=tk_m, name=f"out_proj_dw{tag}")
    dy, dgl = gate_merge_bwd(cfg, dmerged, r["y"], r["proj"], name=f"merge_bwd{tag}")
    do = matmul(dy, w["w_branch"], "nt", BF16, tm=m, tn=_tile(cfg.width, 512), tk=d, name=f"branch_dx{tag}")
    dw_br = matmul(r["o"], dy, "tn", BF16, tm=_tile(cfg.width, 1024), tn=_tile(d // N_CHIPS, 512), tk=tk_m, chip_cols=True,
                   name=f"branch_dw{tag}")
    dqf, dkf, dv_a = attn_bwd(r["qf"], r["kf"], r["kv"], do, 0, r["lse_a"], heads=hd, dk=2 * LANES, dv=LANES,
                              qblk0=0, kblk0=0, vblk0=hd, scale=(LANES + ROPE) ** -0.5, pad=cfg.pad, name=f"mla_attn_bwd{tag}")
    dq, dkn, dkpe = mla_prep_bwd(cfg, dqf, dkf, tabs[1], name=f"mla_prep_bwd{tag}")
    dkv = jnp.concatenate([dkn, dv_a], axis=1)
    dcqn = matmul(dq, w["w_uq"], "nt", F32, tm=m, tn=cfg.q_rank, tk=2 * cfg.width, name=f"up_q_dx{tag}")
    dw_uq = matmul(r["cqn"], dq, "tn", BF16, tm=cfg.q_rank, tn=_tile(2 * cfg.width, 512), tk=tk_m, name=f"up_q_dw{tag}")
    dckvn = matmul(dkv, w["w_ukv"], "nt", F32, tm=m, tn=cfg.kv_rank, tk=2 * cfg.width, name=f"up_kv_dx{tag}")
    dw_ukv = matmul(r["ckvn"], dkv, "tn", BF16, tm=cfg.kv_rank, tn=_tile(2 * cfg.width, 512), tk=tk_m, name=f"up_kv_dw{tag}")
    dcq, dgq = rmsnorm_bwd(r["proj"], s["g_q_lat"], dcqn, BF16, width=cfg.q_rank, col_blk=cfg.off_cq // cfg.q_rank,
                           name=f"norm_q_bwd{tag}")
    dckv, dgkv = rmsnorm_bwd(r["proj"], s["g_kv_lat"], dckvn, BF16, width=cfg.kv_rank, col_blk=cfg.off_ckv // cfg.kv_rank,
                             name=f"norm_kv_bwd{tag}")
    dcb, dcc, dcx, dconv_w = conv_bwd(cfg, r["proj"], s["conv_w"], do, 1, name=f"conv_bwd{tag}")
    dfq, dfk, dfv, dck = attn_bwd(r["proj"], r["proj"], r["proj"], do, 2, r["lse_c"], heads=hd, dk=LANES, dv=LANES,
                                  qblk0=fox_blk, kblk0=fox_blk + hd, vblk0=fox_blk + 2 * hd, scale=LANES ** -0.5,
                                  pad=cfg.pad, decay=r["decay"], name=f"fox_attn_bwd{tag}")
    dc = jnp.pad(dck[:, 0, :].T, ((0, 0), (0, LANES - hd)))
    dfl, dbf = fox_gate_bwd(cfg, r["fl"], r["b_pad"], dc, name=f"fox_gate_bwd{tag}")
    tail = jnp.zeros((m, cfg.d_inp - cfg.off_fl - LANES), BF16)
    dproj = jnp.concatenate([dgl, dcq, dckv, dcb, dcc, dcx, dfq, dfk, dfv, dkpe.astype(BF16), dfl.astype(BF16), tail], axis=1)
    dhn = matmul(dproj, w["w_in"], "nt", F32, tm=m, tn=_tile(d, 512), tk=_tile(cfg.d_inp, 2304), name=f"proj_dx{tag}")
    dw_in = matmul(r["hn"], dproj, "tn", BF16, tm=_tile(d, 1024), tn=_tile(cfg.d_inp, 512), tk=tk_m, name=f"proj_dw{tag}")
    dh_in, dg1 = rmsnorm_bwd(r["h"], s["g_mix_pre"], dhn, F32, dres=dh_mid, name=f"norm_mix_pre_bwd{tag}")
    token = grads_done(dict(w_in=dw_in, w_uq=dw_uq, w_ukv=dw_ukv, w_branch=dw_br, w_out=dw_out))
    dsmall = dict(g_mix_pre=dg1[0], g_mix_post=dg2[0], g_ffn_pre=dg3[0], g_ffn_post=dg4[0], g_q_lat=dgq[0], g_kv_lat=dgkv[0],
                  b_forget=dbf[0, :hd], conv_w=dconv_w)
    return dh_in, dsmall, token


def local_step(cfg, x, target, meta, layer_params, grads_done):
    h = jnp.concatenate([jnp.zeros((cfg.pad, cfg.d), F32), meta, x], axis=0)
    cos, s1, s2 = rope_tables(cfg)
    tabs = ((cos, s1, s2), (cos, -s1, -s2))
    saved = []
    for l in range(cfg.depth):
        w, s = layer_params(l, h)
        h, s, r = layer_fwd(cfg, h, w, s, tabs, f"_{l}")
        saved.append((w, s, r))
    dh, loss = loss_head(cfg, h, target, name="loss_head")
    dsmalls, token = [None] * cfg.depth, None
    for l in reversed(range(cfg.depth)):
        w, s, r = saved[l]
        if token is not None:
            s = {**s, "g_ffn_post": s["g_ffn_post"] + token[0, 0]}
        dh, dsmalls[l], token = layer_bwd(cfg, dh, w, s, r, tabs, f"_{l}", functools.partial(grads_done, l))
    first = cfg.pad + cfg.n_meta
    return loss, dh[first:], dh[cfg.pad:first], dsmalls


def _cols_from_chips(g):
    return jnp.transpose(g, (1, 0, 2)).reshape(g.shape[1], N_CHIPS * g.shape[2])


def _cols_to_chips(w):
    r, c = w.shape
    return jnp.transpose(w.reshape(r, N_CHIPS, c // N_CHIPS), (1, 0, 2))


def _packed_segments(cfg):
    nat = [0] + _cumsum(cfg.nat_splits)
    w = cfg.width
    order = [(10, 0), (0, cfg.off_cq), (1, cfg.off_ckv), (3, cfg.off_conv), (4, cfg.off_conv + w), (5, cfg.off_conv + 2 * w),
             (6, cfg.off_fox), (7, cfg.off_fox + w), (8, cfg.off_fox + 2 * w), (2, cfg.off_kpe), (9, cfg.off_fl)]
    return [(pk, nat[i], cfg.nat_splits[i]) for i, pk in order]


def _chip_cols(cfg):
    n = cfg.d_in // N_CHIPS
    return n, -(-n // LANES) * LANES


def _lane_pieces(cfg, to_packed):
    n, n_pad = _chip_cols(cfg)
    tiles = [[] for _ in range(cfg.d_inp // LANES if to_packed else N_CHIPS * n_pad // LANES)]
    for pk, nat, width in _packed_segments(cfg):
        g = nat
        while g < nat + width:
            k, a = divmod(g, n)
            dst = (pk + g - nat) if to_packed else (k * n_pad + a)
            run = min(nat + width - g, n - a, LANES - dst % LANES)
            src = (k, a) if to_packed else (0, pk + g - nat)
            tiles[dst // LANES].append((dst % LANES, run, *src))
            g += run
    return tiles


def _fill_tiles(pieces, read, write, rows):
    lane = lax.broadcasted_iota(jnp.int32, (rows, LANES), 1)
    for t, parts in enumerate(pieces):
        tile = jnp.zeros((rows, LANES), F32)
        for dl, run, blk, col in parts:
            w0 = col // LANES * LANES
            off = col - w0
            span = LANES if off + run <= LANES else 2 * LANES
            win = read(blk, w0, span)
            shift = (dl - off) % span
            if shift:
                win = pltpu.roll(win, shift, 1)
            win = win[:, :LANES]
            tile = win if (dl == 0 and run == LANES) else jnp.where((lane >= dl) & (lane < dl + run), win, tile)
        write(t, tile)


def pack_w_in_blocks(cfg, lands, own, chip, *, name):
    d = cfg.d
    n, n_pad = _chip_cols(cfg)
    tr = _tile(d, 256, 16)
    pieces = _lane_pieces(cfg, True)

    def body(chip_ref, land_ref, own_ref, o_ref):
        def read(k, w0, span):
            theirs = land_ref[k, :, w0:w0 + span]
            return jnp.where(chip_ref[0] == k, own_ref[:, w0:w0 + span], theirs).astype(F32)

        def write(t, tile):
            o_ref[:, t * LANES:(t + 1) * LANES] = tile.astype(o_ref.dtype)

        _fill_tiles(pieces, read, write, tr)

    return pl.pallas_call(
        body,
        name=name,
        out_shape=jax.ShapeDtypeStruct((d, cfg.d_inp), BF16),
        grid_spec=pltpu.PrefetchScalarGridSpec(
            num_scalar_prefetch=1,
            grid=(d // tr,),
            in_specs=[pl.BlockSpec((N_CHIPS, tr, n_pad), lambda i, chip_ref: (0, i, 0)),
                      pl.BlockSpec((tr, n_pad), lambda i, chip_ref: (i, 0))],
            out_specs=pl.BlockSpec((tr, cfg.d_inp), lambda i, chip_ref: (i, 0)),
        ),
        compiler_params=_cparams("parallel"),
    )(chip, lands, own)


def unpack_w_in_blocks(cfg, dw, *, name):
    d = cfg.d
    n, n_pad = _chip_cols(cfg)
    tr = _tile(d, 256, 16)
    per_blk = n_pad // LANES
    pieces = _lane_pieces(cfg, False)

    def body(dw_ref, o_ref):
        def read(_, w0, span):
            return dw_ref[:, w0:w0 + span].astype(F32)

        def write(t, tile):
            k, i = divmod(t, per_blk)
            o_ref[k, :, i * LANES:(i + 1) * LANES] = tile.astype(o_ref.dtype)

        _fill_tiles(pieces, read, write, tr)

    return pl.pallas_call(
        body,
        name=name,
        out_shape=jax.ShapeDtypeStruct((N_CHIPS, d, n_pad), BF16),
        grid=(d // tr,),
        in_specs=[pl.BlockSpec((tr, cfg.d_inp), lambda i: (i, 0))],
        out_specs=pl.BlockSpec((N_CHIPS, tr, n_pad), lambda i: (0, i, 0)),
        compiler_params=_cparams("parallel"),
    )(dw)


def full_weights(cfg, g, w_in=None):
    make = dict(
        w_uq=lambda a: pack_w_uq(cfg, _cols_from_chips(a)),
        w_ukv=lambda a: pack_w_ukv(cfg, _cols_from_chips(a)),
        w_branch=lambda a: _cols_from_chips(a).reshape(3, cfg.width, cfg.d),
        w_out=lambda a: a.reshape(cfg.d, cfg.d),
        w_ffn_in=_cols_from_chips,
        w_ffn_out=lambda a: a.reshape(cfg.d_ff, cfg.d),
    )
    out = {n: make[n](a) for n, a in g.items()}
    if w_in is not None:
        out["w_in"] = w_in
    return out


def chip_grads(cfg, dw, tag):
    make = dict(
        w_in=lambda a: unpack_w_in_blocks(cfg, a, name=f"unpack_w_in{tag}"),
        w_uq=lambda a: _cols_to_chips(unpack_w_uq(cfg, a)),
        w_ukv=lambda a: _cols_to_chips(unpack_w_ukv(cfg, a)),
        w_branch=lambda a: a.reshape(N_CHIPS, 3 * cfg.width, cfg.d // N_CHIPS),
        w_out=lambda a: a.reshape(N_CHIPS, cfg.d // N_CHIPS, cfg.d),
        w_ffn_in=lambda a: a,
        w_ffn_out=lambda a: a.reshape(N_CHIPS, cfg.d_ff // N_CHIPS, cfg.d),
    )
    return {n: make[n](a) for n, a in dw.items()}


def _small_rows(cfg):
    return dict(g_mix_pre=cfg.d // LANES, g_mix_post=cfg.d // LANES, g_ffn_pre=cfg.d // LANES, g_ffn_post=cfg.d // LANES,
                g_q_lat=cfg.q_rank // LANES, g_kv_lat=cfg.kv_rank // LANES, b_forget=1, conv_w=3 * cfg.width // LANES)


def pack_small(cfg, loss, dmeta, dsmalls):
    parts = [loss[0:1, :], dmeta.reshape(-1, LANES)]
    for ds in dsmalls:
        for k in _small_rows(cfg):
            v = ds[k]
            if k == "b_forget":
                v = jnp.pad(v, (0, LANES - cfg.heads))
            parts.append(v.reshape(-1, LANES))
    rows = sum(p.shape[0] for p in parts)
    parts.append(jnp.zeros((-rows % 8, LANES), F32))
    return jnp.concatenate(parts, axis=0)


def unpack_small(cfg, block):
    loss = block[0, 0]
    n = cfg.n_meta * cfg.d // LANES
    dmeta = block[1:1 + n].reshape(cfg.n_meta, cfg.d)
    at = 1 + n
    out = []
    for _ in range(cfg.depth):
        ds = {}
        for k, rows in _small_rows(cfg).items():
            v = block[at:at + rows]
            at += rows
            if k == "b_forget":
                v = v[0, :cfg.heads]
            elif k == "conv_w":
                v = v.reshape(3, cfg.width)
            else:
                v = v.reshape(-1)
            ds[k] = v
        out.append(ds)
    return loss, dmeta, out


def kernel(x, meta, w_in, b_forget, g_q_lat, g_kv_lat, w_uq, w_ukv, conv_w, w_branch, w_out, w_ffn_in, w_ffn_out, g_mix_pre, g_mix_post, g_ffn_pre, g_ffn_post, loss_target, m_meta, m_w_in, m_b_forget, m_g_q_lat, m_g_kv_lat, m_w_uq, m_w_ukv, m_conv_w, m_w_branch, m_w_out, m_w_ffn_in, m_w_ffn_out, m_g_mix_pre, m_g_mix_post, m_g_ffn_pre, m_g_ffn_post, v_meta, v_w_in, v_b_forget, v_g_q_lat, v_g_kv_lat, v_w_uq, v_w_ukv, v_conv_w, v_w_branch, v_w_out, v_w_ffn_in, v_w_ffn_out, v_g_mix_pre, v_g_mix_post, v_g_ffn_pre, v_g_ffn_post):
    cfg = CFG
    names = ("meta", "w_in", "b_forget", "g_q_lat", "g_kv_lat", "w_uq", "w_ukv", "conv_w", "w_branch", "w_out", "w_ffn_in",
             "w_ffn_out", "g_mix_pre", "g_mix_post", "g_ffn_pre", "g_ffn_post")
    params = dict(zip(names, (meta, w_in, b_forget, g_q_lat, g_kv_lat, w_uq, w_ukv, conv_w, w_branch, w_out, w_ffn_in, w_ffn_out,
                              g_mix_pre, g_mix_post, g_ffn_pre, g_ffn_post)))
    mom1 = dict(zip(names, (m_meta, m_w_in, m_b_forget, m_g_q_lat, m_g_kv_lat, m_w_uq, m_w_ukv, m_conv_w, m_w_branch, m_w_out,
                            m_w_ffn_in, m_w_ffn_out, m_g_mix_pre, m_g_mix_post, m_g_ffn_pre, m_g_ffn_post)))
    mom2 = dict(zip(names, (v_meta, v_w_in, v_b_forget, v_g_q_lat, v_g_kv_lat, v_w_uq, v_w_ukv, v_conv_w, v_w_branch, v_w_out,
                            v_w_ffn_in, v_w_ffn_out, v_g_mix_pre, v_g_mix_post, v_g_ffn_pre, v_g_ffn_post)))
    xi, yi, ci = lax.axis_index("x"), lax.axis_index("y"), lax.axis_index("c")
    chip = 2 * xi + yi
    chip_arr = jnp.reshape(chip, (1,)).astype(jnp.int32)
    core_arr = jnp.reshape(ci, (1,)).astype(jnp.int32)

    meta_all = gather_blocks(meta, reduce=False, name="gather_meta")[0::2]
    meta_full = jnp.transpose(meta_all, (1, 0, 2)).reshape(cfg.n_meta, cfg.d)
    conv_rows = conv_w.reshape(cfg.depth * 3, cfg.width // N_CHIPS)
    conv_all = gather_blocks(conv_rows, reduce=False, name="gather_conv_w")[0::2]
    conv_full = jnp.transpose(conv_all, (1, 0, 2)).reshape(cfg.depth, 3, cfg.width)

    def shard2d(name, l, after=None):
        w = params[name][l]
        if after is not None:
            w = w + after
        w = w.reshape(-1, w.shape[-1]).astype(BF16)
        if name == "w_in":
            w = jnp.pad(w, ((0, 0), (0, _chip_cols(cfg)[1] - w.shape[1])))
        return w

    is_mine = (jnp.arange(N_CHIPS) == chip)[:, None, None]
    shard_shape = {n: shard2d(n, 0).shape for n in WEIGHTS}
    groups = (("w_in", "w_uq", "w_ukv", "w_branch", "w_out"), ("w_ffn_in", "w_ffn_out"))
    gather_plans = [_gather_plan([shard_shape[n][0] // 2 for n in g]) for g in groups]

    def gather_start(l, after):
        started = []
        for gi, g in enumerate(groups):
            started.append(copies_start([shard2d(n, l, after) for n in g], [(N_CHIPS,) + shard_shape[n] for n in g],
                                        gather_plans[gi], 3 * len(g), name=f"gather_start_{gi}_{l}"))
            after = started[-1][4][0, 0]
        return started

    def land(l, gi, started, after):
        send_sems, recv_sems, own, lands, _ = started
        own, lands, landed = copies_wait(send_sems, recv_sems, own, lands, gather_plans[gi], after, name=f"gather_wait_{gi}_{l}")
        return own, forward_halves(lands, name=f"forward_halves_{gi}_{l}"), landed

    in_flight = {0: gather_start(0, None)}

    class LayerWeights(dict):
        def __init__(self, l, h):
            own, lands, _ = land(l, 0, in_flight[l][0], h)
            got = {n: jnp.where(is_mine, o[None], g) for n, o, g in zip(groups[0][1:], own[1:], lands[1:])}
            super().__init__(full_weights(cfg, got, pack_w_in_blocks(cfg, lands[0], own[0], chip_arr, name=f"pack_w_in_{l}")))
            self.layer = l

        def land_ffn(self, after, s):
            l = self.layer
            own, lands, landed = land(l, 1, in_flight.pop(l)[1], after)
            self.update(full_weights(cfg, {n: jnp.where(is_mine, o[None], g) for n, o, g in zip(groups[1], own, lands)}))
            if l + 1 == cfg.depth:
                return s
            in_flight[l + 1] = gather_start(l + 1, landed[0, 0])
            return {**s, "g_ffn_pre": s["g_ffn_pre"] + in_flight[l + 1][1][4][0, 0]}

    def layer_params(l, h):
        s = dict(g_mix_pre=g_mix_pre[l], g_mix_post=g_mix_post[l], g_ffn_pre=g_ffn_pre[l], g_ffn_post=g_ffn_post[l],
                 g_q_lat=g_q_lat[l], g_kv_lat=g_kv_lat[l], b_forget=b_forget[l], conv_w=conv_full[l])
        s["g_mix_pre"] = s["g_mix_pre"] + in_flight[l][1][4][0, 0]
        return LayerWeights(l, h), s

    half_shape = {n: (shard_shape[n][0] // 2, shard_shape[n][1]) for n in WEIGHTS}
    sums_upper = {n: jnp.zeros((cfg.depth - 1,) + half_shape[n], F32) for n in WEIGHTS}
    sums_first = {n: jnp.zeros((1,) + half_shape[n], F32) for n in WEIGHTS}
    swapping, exchanging = [], []

    def finish_exchange(after):
        l, names_, (send_sems, recv_sems, parts, lands, _) = exchanging.pop(0)
        parts, others, _ = copies_wait(send_sems, recv_sems, parts, lands, _exchange_plan(len(names_)), after,
                                       name=f"exchange_wait_{names_[0]}_{l}")
        for n, p, o in zip(names_, parts, others):
            if l == 0:
                sums_first[n] = sum_chips(p, o, chip_arr, sums_first[n], 0, name=f"sum_chips_{n}_{l}")
            else:
                sums_upper[n] = sum_chips(p, o, chip_arr, sums_upper[n], l - 1, name=f"sum_chips_{n}_{l}")

    def finish_swap(after):
        l, names_, (send_sems, recv_sems, mine, lands, _) = swapping.pop(0)
        plan = _swap_plan([g.shape[1] // 2 for g in mine])
        mine, theirs, _ = copies_wait(send_sems, recv_sems, mine, lands, plan, after, name=f"swap_wait_{names_[0]}_{l}")
        parts = [add_own_half(g, t, core_arr, name=f"add_own_half_{n}_{l}") for n, g, t in zip(names_, mine, theirs)]
        started = copies_start(parts, [(3,) + p.shape[1:] for p in parts], _exchange_plan(len(names_)), 3 * len(names_),
                               name=f"exchange_start_{names_[0]}_{l}")
        exchanging.append((l, names_, started))
        return started[4]

    def grads_done(l, dws):
        names_ = [n for n in WEIGHTS if n in dws]
        send = chip_grads(cfg, dws, f"_{l}")
        mine = [send[n] for n in names_]
        halves = [g.shape[1] // 2 for g in mine]
        started = copies_start(mine, [(N_CHIPS, h, g.shape[2]) for g, h in zip(mine, halves)], _swap_plan(halves), len(mine),
                               name=f"swap_start_{names_[0]}_{l}")
        token = started[4]
        if swapping:
            token = finish_swap(token)
            if len(exchanging) > 1:
                finish_exchange(token)
        swapping.append((l, names_, started))
        return token

    loss, grad_x, dmeta, dsmalls = local_step(cfg, x[0], loss_target[0], meta_full, layer_params, grads_done)
    share_plan = _share_plan(len(WEIGHTS))
    sharing = copies_start([sums_upper[n] for n in WEIGHTS], [sums_upper[n].shape for n in WEIGHTS], share_plan, len(WEIGHTS),
                           name="share_start_upper")
    last = finish_swap(sharing[4])
    while exchanging[0][0] > 0:
        finish_exchange(last)

    def update(own, other, first_layer, prev, tag):
        out = {}
        for n, mine_, theirs_ in zip(WEIGHTS, own, other):
            three_d = lambda a: a.reshape(cfg.depth, -1, params[n].shape[-1])
            out[n] = adamw_halves(three_d(params[n]), mine_, theirs_, core_arr, three_d(mom1[n]), three_d(mom2[n]),
                                  first_layer=first_layer, prev=prev and prev[n], name=f"adamw_{tag}_{n}")
        return out

    own_upper, other_upper, _ = copies_wait(sharing[0], sharing[1], sharing[2], sharing[3], share_plan, last, name="share_wait_upper")
    upper = update(own_upper, other_upper, 1, None, "upper")
    busy = sum(upper[n][1][1, 0, :LANES] for n in WEIGHTS)
    while exchanging:
        finish_exchange(busy)
    own_first = [sums_first[n] for n in WEIGHTS]
    done = update(own_first, share_halves(own_first, name="share_halves_first"), 0, upper, "first")
    grad, delta, new_m, new_v = ({n: done[n][k].reshape(params[n].shape) for n in WEIGHTS} for k in range(4))

    total = gather_blocks(pack_small(cfg, loss, dmeta, dsmalls), reduce=True, name="reduce_small")
    loss_sum, dmeta_sum, dsmall_sum = unpack_small(cfg, total)
    for k in _small_rows(cfg):
        grad[k] = jnp.stack([ds[k] for ds in dsmall_sum])
    grad["conv_w"] = lax.dynamic_slice_in_dim(grad["conv_w"], chip * (cfg.width // N_CHIPS), cfg.width // N_CHIPS, axis=2)
    grad["meta"] = lax.dynamic_slice_in_dim(dmeta_sum, chip * (cfg.d // N_CHIPS), cfg.d // N_CHIPS, axis=1)

    for n in names:
        if n in WEIGHTS:
            continue
        shp = params[n].shape
        two_d = lambda a: a.reshape(-1, shp[-1])
        dl, nm, nv = adamw(two_d(params[n]), two_d(grad[n]), two_d(mom1[n]), two_d(mom2[n]), name=f"adamw_{n}")
        delta[n], new_m[n], new_v[n] = dl.reshape(shp), nm.reshape(shp), nv.reshape(shp)

    return (loss_sum, grad_x[None], *[grad[n] for n in names], *[delta[n] for n in names], *[new_m[n] for n in names],
            *[new_v[n] for n in names])
```

```python
import functools
from typing import NamedTuple

import jax
import jax.numpy as jnp
from jax import lax
from jax.experimental import pallas as pl
from jax.experimental.pallas import tpu as pltpu

F32 = jnp.float32
BF16 = jnp.bfloat16
MESH = pl.DeviceIdType.MESH

EPS = 1e-6
NEG_INF = -1e30
ROPE_THETA = 10000.0
LANES = 128
ROPE = 64
N_CHIPS = 4
N_DEV = 8

ADAM_LR = 0.001
ADAM_B1 = 0.9
ADAM_B2 = 0.999
ADAM_EPS = 1e-08
ADAM_WD = 0.01
ADAM_STEP = 10

VMEM_LIMIT_BYTES = 48 * 1024 * 1024


class Cfg(NamedTuple):
    d: int = 2048
    seq: int = 2048
    depth: int = 4
    n_meta: int = 16
    heads: int = 8
    q_rank: int = 512
    kv_rank: int = 512
    d_ff: int = 5632

    @property
    def width(self):
        return self.heads * LANES

    @property
    def pad(self):
        return (-(self.n_meta + self.seq)) % LANES

    @property
    def m(self):
        return self.pad + self.n_meta + self.seq

    @property
    def nat_splits(self):
        w = self.width
        return (self.q_rank, self.kv_rank, ROPE, w, w, w, w, w, w, self.heads, 3 * self.d)

    @property
    def d_in(self):
        return sum(self.nat_splits)

    @property
    def off_cq(self):
        return 3 * self.d

    @property
    def off_ckv(self):
        return self.off_cq + self.q_rank

    @property
    def off_conv(self):
        return self.off_ckv + self.kv_rank

    @property
    def off_fox(self):
        return self.off_conv + 3 * self.width

    @property
    def off_kpe(self):
        return self.off_fox + 3 * self.width

    @property
    def off_fl(self):
        return self.off_kpe + LANES

    @property
    def d_inp(self):
        return -(-(self.off_fl + LANES) // 512) * 512


CFG = Cfg()


def _tile(n, target, mult=LANES):
    best = None
    t = mult
    while t <= min(n, target):
        if n % t == 0:
            best = t
        t += mult
    return best or n


def _cparams(*sem):
    return pltpu.CompilerParams(dimension_semantics=sem, vmem_limit_bytes=VMEM_LIMIT_BYTES)


def _cumsum(xs):
    out, s = [], 0
    for v in xs:
        s += v
        out.append(s)
    return out


def pack_w_uq(cfg, w):
    r = w.shape[0]
    w3 = w.reshape(r, cfg.heads, LANES + ROPE)
    w3 = jnp.pad(w3, ((0, 0), (0, 0), (0, LANES - ROPE)))
    return w3.reshape(r, cfg.heads * 2 * LANES)


def unpack_w_uq(cfg, wp):
    r = wp.shape[0]
    return wp.reshape(r, cfg.heads, 2 * LANES)[:, :, : LANES + ROPE].reshape(r, cfg.heads * (LANES + ROPE))


def pack_w_ukv(cfg, w):
    r = w.shape[0]
    w4 = w.reshape(r, cfg.heads, 2, LANES)
    return jnp.transpose(w4, (0, 2, 1, 3)).reshape(r, 2 * cfg.heads * LANES)


def unpack_w_ukv(cfg, wp):
    r = wp.shape[0]
    w4 = wp.reshape(r, 2, cfg.heads, LANES)
    return jnp.transpose(w4, (0, 2, 1, 3)).reshape(r, 2 * cfg.heads * LANES)


_DIMS = {
    "nn": (((1,), (0,)), ((), ())),
    "nt": (((1,), (1,)), ((), ())),
    "tn": (((0,), (0,)), ((), ())),
}


def matmul(a, b, mode, out_dtype, *, tm, tn, tk, name, chip_cols=False):
    batched = a.ndim == 3
    if mode == "nn":
        (m, kc), n = a.shape[-2:], b.shape[-1]
        a_blk, a_idx = (tm, tk), lambda i, j, k: (i, k)
        b_blk, b_idx = (tk, tn), lambda i, j, k: (k, j)
    elif mode == "nt":
        (m, kc), n = a.shape[-2:], b.shape[-2]
        a_blk, a_idx = (tm, tk), lambda i, j, k: (i, k)
        b_blk, b_idx = (tn, tk), lambda i, j, k: (j, k)
    else:
        (kc, m), n = a.shape[-2:], b.shape[-1]
        a_blk, a_idx = (tk, tm), lambda i, j, k: (k, i)
        b_blk, b_idx = (tk, tn), lambda i, j, k: (k, j)
    assert m % tm == 0 and n % tn == 0 and kc % tk == 0, (name, m, n, kc, tm, tn, tk)
    nk = kc // tk
    dims = _DIMS[mode]
    o_blk, o_idx = (tm, tn), lambda i, j, k: (i, j)
    grid = (m // tm, n // tn, nk)
    per = n // N_CHIPS // tn
    assert not chip_cols or n // N_CHIPS % tn == 0
    if batched:
        nb = a.shape[0]
        grid = (nb,) + grid
        wrap = lambda f: (lambda bb, i, j, k: (bb,) + f(i, j, k))
        a_blk, b_blk, o_blk = (None,) + a_blk, (None,) + b_blk, (None,) + o_blk
        a_idx, b_idx, o_idx = wrap(a_idx), wrap(b_idx), wrap(o_idx)
        out_shape = (nb, m, n)
        if chip_cols:
            o_blk, o_idx = (None,) + o_blk, lambda bb, i, j, k: (j // per, bb, i, j % per)
            out_shape = (N_CHIPS, nb, m, n // N_CHIPS)
        sem = ("parallel", "parallel", "parallel", "arbitrary")
    else:
        out_shape = (m, n)
        if chip_cols:
            o_blk, o_idx = (None,) + o_blk, lambda i, j, k: (j // per, i, j % per)
            out_shape = (N_CHIPS, m, n // N_CHIPS)
        sem = ("parallel", "parallel", "arbitrary")
    k_axis = len(grid) - 1

    def body(a_ref, b_ref, o_ref, *scratch):
        prod = lax.dot_general(a_ref[...], b_ref[...], dims, preferred_element_type=F32)
        if nk == 1:
            o_ref[...] = prod.astype(o_ref.dtype)
        else:
            acc_ref = scratch[0] if scratch else o_ref
            k = pl.program_id(k_axis)

            @pl.when(k == 0)
            def _():
                acc_ref[...] = prod

            @pl.when(k > 0)
            def _():
                acc_ref[...] += prod

            if scratch:
                @pl.when(k == nk - 1)
                def _():
                    o_ref[...] = acc_ref[...].astype(o_ref.dtype)

    return pl.pallas_call(
        body,
        name=name,
        out_shape=jax.ShapeDtypeStruct(out_shape, out_dtype),
        grid=grid,
        in_specs=[pl.BlockSpec(a_blk, a_idx), pl.BlockSpec(b_blk, b_idx)],
        out_specs=pl.BlockSpec(o_blk, o_idx),
        scratch_shapes=[] if nk == 1 or out_dtype == F32 else [pltpu.VMEM((tm, tn), F32)],
        compiler_params=_cparams(*sem),
    )(a, b)


def _row_tile(m):
    return _tile(m, 272, 16)


def rmsnorm_fwd(x, g, out_dtype, *, name, width=None, col_blk=0, res=None):
    m = x.shape[0]
    n = width or x.shape[1]
    tm = _row_tile(m)
    has_res = res is not None

    def body(x_ref, g_ref, *rest):
        o_ref = rest[-1]
        xf = x_ref[...].astype(F32)
        r = lax.rsqrt(jnp.mean(xf * xf, axis=-1, keepdims=True) + EPS)
        y = xf * r * g_ref[...]
        if has_res:
            y = rest[0][...] + y
        o_ref[...] = y.astype(o_ref.dtype)

    in_specs = [pl.BlockSpec((tm, n), lambda i: (i, col_blk)), pl.BlockSpec((1, n), lambda i: (0, 0))]
    args = [x, g.reshape(1, n)]
    if has_res:
        in_specs.append(pl.BlockSpec((tm, n), lambda i: (i, 0)))
        args.append(res)
    return pl.pallas_call(
        body,
        name=name,
        out_shape=jax.ShapeDtypeStruct((m, n), out_dtype),
        grid=(m // tm,),
        in_specs=in_specs,
        out_specs=pl.BlockSpec((tm, n), lambda i: (i, 0)),
        compiler_params=_cparams("parallel"),
    )(*args)


def rmsnorm_bwd(x, g, dy, out_dtype, *, name, width=None, col_blk=0, dres=None):
    m = x.shape[0]
    n = width or x.shape[1]
    tm = _row_tile(m)
    has_res = dres is not None

    def body(x_ref, g_ref, dy_ref, *rest):
        dx_ref, dg_ref = rest[-2:]
        i = pl.program_id(0)
        xf = x_ref[...].astype(F32)
        r = lax.rsqrt(jnp.mean(xf * xf, axis=-1, keepdims=True) + EPS)
        xhat = xf * r
        dyf = dy_ref[...].astype(F32)
        dxh = dyf * g_ref[...]
        dx = r * (dxh - xhat * jnp.mean(dxh * xhat, axis=-1, keepdims=True))
        if has_res:
            dx = dx + rest[0][...]
        dx_ref[...] = dx.astype(dx_ref.dtype)
        part = jnp.sum(dyf * xhat, axis=0, keepdims=True)

        @pl.when(i == 0)
        def _():
            dg_ref[...] = part

        @pl.when(i > 0)
        def _():
            dg_ref[...] += part

    in_specs = [
        pl.BlockSpec((tm, n), lambda i: (i, col_blk)),
        pl.BlockSpec((1, n), lambda i: (0, 0)),
        pl.BlockSpec((tm, n), lambda i: (i, 0)),
    ]
    args = [x, g.reshape(1, n), dy]
    if has_res:
        in_specs.append(pl.BlockSpec((tm, n), lambda i: (i, 0)))
        args.append(dres)
    return pl.pallas_call(
        body,
        name=name,
        out_shape=(jax.ShapeDtypeStruct((m, n), out_dtype), jax.ShapeDtypeStruct((1, n), F32)),
        grid=(m // tm,),
        in_specs=in_specs,
        out_specs=(pl.BlockSpec((tm, n), lambda i: (i, 0)), pl.BlockSpec((1, n), lambda i: (0, 0))),
        compiler_params=_cparams("arbitrary"),
    )(*args)


_NT = (((1,), (1,)), ((), ()))
_NN = (((1,), (0,)), ((), ()))
_TN = (((0,), (0,)), ((), ()))


def _attn_scores(q, k, scale, decay_refs, i, tq, kn, pad):
    s = lax.dot_general(q, k, _NT, preferred_element_type=F32) * scale
    if decay_refs is not None:
        cq_ref, ck_ref = decay_refs
        s = s + (cq_ref[0] - ck_ref[0][:, :kn])
    t_idx = i * tq + lax.broadcasted_iota(jnp.int32, (tq, 1), 0)
    s_idx = lax.broadcasted_iota(jnp.int32, (1, kn), 1)
    mask = (s_idx <= t_idx) & (s_idx >= pad)
    return s, mask, t_idx


def _keys_needed(i, tq, m):
    return min(m, -(-((i + 1) * tq) // LANES) * LANES)


def attn_fwd(q, k, v, *, heads, dk, dv, qblk0, kblk0, vblk0, scale, pad, slot, branches=None, decay=None, name):
    m = q.shape[0]
    tq = _tile(m, 544, 16)
    has_decay = decay is not None

    def body(q_ref, k_ref, v_ref, *rest):
        o_ref, lse_ref = rest[-2:]
        decay_refs = rest[:2] if has_decay else None

        def block(i):
            kn = _keys_needed(i, tq, m)
            s, mask, t_idx = _attn_scores(q_ref[...], k_ref[0:kn, :], scale, decay_refs, i, tq, kn, pad)
            s = jnp.where(mask, s, NEG_INF)
            mx = jnp.max(s, axis=1, keepdims=True)
            p = jnp.exp(s - mx)
            l = jnp.sum(p, axis=1, keepdims=True)
            o = lax.dot_general(p.astype(BF16), v_ref[0:kn, :], _NN, preferred_element_type=F32) / l
            o_ref[...] = jnp.where(t_idx >= pad, o, 0.0).astype(o_ref.dtype)
            lse_ref[0] = mx + jnp.log(l)

        for i in range(m // tq):
            pl.when(pl.program_id(1) == i)(functools.partial(block, i))

    in_specs = [
        pl.BlockSpec((tq, dk), lambda h, i: (i, qblk0 + h)),
        pl.BlockSpec((m, dk), lambda h, i: (0, kblk0 + h)),
        pl.BlockSpec((m, dv), lambda h, i: (0, vblk0 + h)),
    ]
    args = [q, k, v]
    if has_decay:
        in_specs += [pl.BlockSpec((1, tq, 1), lambda h, i: (h, i, 0)), pl.BlockSpec((1, 1, m), lambda h, i: (h, 0, 0))]
        args += list(decay)
    aliases = {}
    if branches is not None:
        aliases = {len(args): 0}
        in_specs.append(pl.BlockSpec(memory_space=pl.ANY))
        args.append(branches)
    return pl.pallas_call(
        body,
        name=name,
        out_shape=(jax.ShapeDtypeStruct((3, m, heads * dv), BF16), jax.ShapeDtypeStruct((heads, m, 1), F32)),
        grid=(heads, m // tq),
        in_specs=in_specs,
        out_specs=(pl.BlockSpec((None, tq, dv), lambda h, i: (slot, i, h)), pl.BlockSpec((1, tq, 1), lambda h, i: (h, i, 0))),
        input_output_aliases=aliases,
        compiler_params=_cparams("parallel", "parallel"),
    )(*args)


def attn_bwd(q, k, v, do, do_sel, lse, *, heads, dk, dv, qblk0, kblk0, vblk0, scale, pad, decay=None, name):
    m = q.shape[0]
    tq = _tile(m, 544, 16)
    nq = m // tq
    has_decay = decay is not None

    def body(q_ref, k_ref, v_ref, do_ref, lse_ref, *rest):
        if has_decay:
            cq_ref, ck_ref, dq_ref, dk_ref, dv_ref, dck_ref, dk_acc, dv_acc = rest
            decay_refs = (cq_ref, ck_ref)
        else:
            dq_ref, dk_ref, dv_ref, dk_acc, dv_acc = rest
            decay_refs = None
        @pl.when(pl.program_id(1) == 0)
        def _():
            dk_acc[...] = jnp.zeros_like(dk_acc)
            dv_acc[...] = jnp.zeros_like(dv_acc)
            if has_decay:
                dck_ref[...] = jnp.zeros_like(dck_ref)

        def block(i):
            kn = _keys_needed(i, tq, m)
            qb, kb, dob = q_ref[...], k_ref[0:kn, :], do_ref[...]
            s, mask, _ = _attn_scores(qb, kb, scale, decay_refs, i, tq, kn, pad)
            p = jnp.where(mask, jnp.exp(s - lse_ref[0]), 0.0)
            dp = lax.dot_general(dob, v_ref[0:kn, :], _NT, preferred_element_type=F32)
            ds = p * (dp - jnp.sum(p * dp, axis=1, keepdims=True))
            dsb = ds.astype(BF16)
            dq_ref[...] = (lax.dot_general(dsb, kb, _NN, preferred_element_type=F32) * scale).astype(dq_ref.dtype)
            dk_acc[0:kn, :] += lax.dot_general(dsb, qb, _TN, preferred_element_type=F32) * scale
            dv_acc[0:kn, :] += lax.dot_general(p.astype(BF16), dob, _TN, preferred_element_type=F32)
            if has_decay:
                dck_ref[0, :, 0:kn] -= jnp.sum(ds, axis=0, keepdims=True)

        for i in range(nq):
            pl.when(pl.program_id(1) == i)(functools.partial(block, i))

        @pl.when(pl.program_id(1) == nq - 1)
        def _():
            dk_ref[...] = dk_acc[...].astype(dk_ref.dtype)
            dv_ref[...] = dv_acc[...].astype(dv_ref.dtype)

    in_specs = [
        pl.BlockSpec((tq, dk), lambda h, i: (i, qblk0 + h)),
        pl.BlockSpec((m, dk), lambda h, i: (0, kblk0 + h)),
        pl.BlockSpec((m, dv), lambda h, i: (0, vblk0 + h)),
        pl.BlockSpec((None, tq, dv), lambda h, i: (do_sel, i, h)),
        pl.BlockSpec((1, tq, 1), lambda h, i: (h, i, 0)),
    ]
    args = [q, k, v, do, lse]
    out_shape = [
        jax.ShapeDtypeStruct((m, heads * dk), BF16),
        jax.ShapeDtypeStruct((m, heads * dk), BF16),
        jax.ShapeDtypeStruct((m, heads * dv), BF16),
    ]
    out_specs = [
        pl.BlockSpec((tq, dk), lambda h, i: (i, h)),
        pl.BlockSpec((m, dk), lambda h, i: (0, h)),
        pl.BlockSpec((m, dv), lambda h, i: (0, h)),
    ]
    if has_decay:
        in_specs += [pl.BlockSpec((1, tq, 1), lambda h, i: (h, i, 0)), pl.BlockSpec((1, 1, m), lambda h, i: (h, 0, 0))]
        args += list(decay)
        out_shape.append(jax.ShapeDtypeStruct((heads, 1, m), F32))
        out_specs.append(pl.BlockSpec((1, 1, m), lambda h, i: (h, 0, 0)))
    return pl.pallas_call(
        body,
        name=name,
        out_shape=tuple(out_shape),
        grid=(heads, nq),
        in_specs=in_specs,
        out_specs=tuple(out_specs),
        scratch_shapes=[pltpu.VMEM((m, dk), F32), pltpu.VMEM((m, dv), F32)],
        compiler_params=_cparams("parallel", "arbitrary"),
    )(*args)


def rope_tables(cfg):
    half = ROPE // 2
    inv_freq = 1.0 / (ROPE_THETA ** (jnp.arange(0, ROPE, 2, dtype=F32) / ROPE))
    pos = (jnp.arange(cfg.m, dtype=jnp.int32) - cfg.pad).astype(F32)
    ang = pos[:, None] * inv_freq[None, :]
    cos, sin = jnp.cos(ang), jnp.sin(ang)
    z = jnp.zeros((cfg.m, half), F32)
    zz = jnp.zeros((cfg.m, LANES - ROPE), F32)
    return (
        jnp.concatenate([cos, cos, zz], axis=1),
        jnp.concatenate([-sin, z, zz], axis=1),
        jnp.concatenate([z, sin, zz], axis=1),
    )


def _rope(x, cos, s1, s2):
    return x * cos + pltpu.roll(x, LANES - ROPE // 2, 1) * s1 + pltpu.roll(x, ROPE // 2, 1) * s2


def mla_prep_fwd(cfg, q, kv, proj, tabs, *, name):
    m, h2 = cfg.m, 2 * LANES
    tm = _tile(m, 544, 16)
    kpe_blk = cfg.off_kpe // LANES

    def body(q_ref, kn_ref, kpe_ref, cos_ref, s1_ref, s2_ref, qf_ref, kf_ref):
        cos, s1, s2 = cos_ref[...], s1_ref[...], s2_ref[...]
        qv = q_ref[...]
        qf_ref[:, :LANES] = qv[:, :LANES]
        qf_ref[:, LANES:] = _rope(qv[:, LANES:].astype(F32), cos, s1, s2).astype(qf_ref.dtype)
        kf_ref[:, :LANES] = kn_ref[...]
        kf_ref[:, LANES:] = _rope(kpe_ref[...].astype(F32), cos, s1, s2).astype(kf_ref.dtype)

    tab = pl.BlockSpec((tm, LANES), lambda i, h: (i, 0))
    return pl.pallas_call(
        body,
        name=name,
        out_shape=(jax.ShapeDtypeStruct((m, cfg.heads * h2), BF16), jax.ShapeDtypeStruct((m, cfg.heads * h2), BF16)),
        grid=(m // tm, cfg.heads),
        in_specs=[
            pl.BlockSpec((tm, h2), lambda i, h: (i, h)),
            pl.BlockSpec((tm, LANES), lambda i, h: (i, h)),
            pl.BlockSpec((tm, LANES), lambda i, h: (i, kpe_blk)),
            tab, tab, tab,
        ],
        out_specs=(pl.BlockSpec((tm, h2), lambda i, h: (i, h)), pl.BlockSpec((tm, h2), lambda i, h: (i, h))),
        compiler_params=_cparams("parallel", "parallel"),
    )(q, kv, proj, *tabs)


def mla_prep_bwd(cfg, dqf, dkf, tabs_t, *, name):
    m, h2 = cfg.m, 2 * LANES
    tm = _tile(m, 544, 16)

    def body(dqf_ref, dkf_ref, cos_ref, s1_ref, s2_ref, dq_ref, dkn_ref, dkpe_ref):
        h = pl.program_id(1)
        cos, s1, s2 = cos_ref[...], s1_ref[...], s2_ref[...]
        dqv, dkv = dqf_ref[...], dkf_ref[...]
        dq_ref[:, :LANES] = dqv[:, :LANES]
        dq_ref[:, LANES:] = _rope(dqv[:, LANES:].astype(F32), cos, s1, s2).astype(dq_ref.dtype)
        dkn_ref[...] = dkv[:, :LANES]
        part = _rope(dkv[:, LANES:].astype(F32), cos, s1, s2)

        @pl.when(h == 0)
        def _():
            dkpe_ref[...] = part

        @pl.when(h > 0)
        def _():
            dkpe_ref[...] += part

    tab = pl.BlockSpec((tm, LANES), lambda i, h: (i, 0))
    return pl.pallas_call(
        body,
        name=name,
        out_shape=(
            jax.ShapeDtypeStruct((m, cfg.heads * h2), BF16),
            jax.ShapeDtypeStruct((m, cfg.heads * LANES), BF16),
            jax.ShapeDtypeStruct((m, LANES), F32),
        ),
        grid=(m // tm, cfg.heads),
        in_specs=[pl.BlockSpec((tm, h2), lambda i, h: (i, h)), pl.BlockSpec((tm, h2), lambda i, h: (i, h)), tab, tab, tab],
        out_specs=(
            pl.BlockSpec((tm, h2), lambda i, h: (i, h)),
            pl.BlockSpec((tm, LANES), lambda i, h: (i, h)),
            pl.BlockSpec((tm, LANES), lambda i, h: (i, 0)),
        ),
        compiler_params=_cparams("parallel", "arbitrary"),
    )(dqf, dkf, *tabs_t)


def _conv_parts(b_ref, c_ref, x_ref, w_ref, m):
    b, c, x = b_ref[...].astype(F32), c_ref[...].astype(F32), x_ref[...].astype(F32)
    u = c * x
    row = lax.broadcasted_iota(jnp.int32, (m, 1), 0)
    u1 = jnp.where(row >= 1, pltpu.roll(u, 1, 0), 0.0)
    u2 = jnp.where(row >= 2, pltpu.roll(u, 2, 0), 0.0)
    w0, w1, w2 = w_ref[0:1, :], w_ref[1:2, :], w_ref[2:3, :]
    uc = w0 * u2 + w1 * u1 + w2 * u
    return b, c, x, u, u1, u2, uc, (w0, w1, w2), row


def _conv_specs(cfg, tn):
    m, nb, blk0 = cfg.m, cfg.width // tn, cfg.off_conv // tn
    return [
        pl.BlockSpec((m, tn), lambda j: (0, blk0 + j)),
        pl.BlockSpec((m, tn), lambda j: (0, blk0 + nb + j)),
        pl.BlockSpec((m, tn), lambda j: (0, blk0 + 2 * nb + j)),
        pl.BlockSpec((3, tn), lambda j: (0, j)),
    ]


def conv_fwd(cfg, proj, conv_w, branches, slot, *, name):
    m, tn = cfg.m, LANES

    def body(b_ref, c_ref, x_ref, w_ref, kept_ref, o_ref):
        b, _, _, _, _, _, uc, _, _ = _conv_parts(b_ref, c_ref, x_ref, w_ref, m)
        o_ref[...] = (b * uc).astype(o_ref.dtype)

    return pl.pallas_call(
        body,
        name=name,
        out_shape=jax.ShapeDtypeStruct((3, m, cfg.width), BF16),
        grid=(cfg.width // tn,),
        in_specs=_conv_specs(cfg, tn) + [pl.BlockSpec(memory_space=pl.ANY)],
        out_specs=pl.BlockSpec((None, m, tn), lambda j: (slot, 0, j)),
        input_output_aliases={4: 0},
        compiler_params=_cparams("parallel"),
    )(proj, proj, proj, conv_w, branches)


def conv_bwd(cfg, proj, conv_w, do, do_sel, *, name):
    m, tn = cfg.m, LANES

    def body(b_ref, c_ref, x_ref, w_ref, do_ref, db_ref, dc_ref, dx_ref, dw_ref):
        b, c, x, u, u1, u2, uc, (w0, w1, w2), row = _conv_parts(b_ref, c_ref, x_ref, w_ref, m)
        dob = do_ref[...].astype(F32)
        db_ref[...] = (dob * uc).astype(db_ref.dtype)
        duc = dob * b
        up1 = jnp.where(row <= m - 2, pltpu.roll(duc, m - 1, 0), 0.0)
        up2 = jnp.where(row <= m - 3, pltpu.roll(duc, m - 2, 0), 0.0)
        du = w2 * duc + w1 * up1 + w0 * up2
        dc_ref[...] = (du * x).astype(dc_ref.dtype)
        dx_ref[...] = (du * c).astype(dx_ref.dtype)
        dw_ref[0:1, :] = jnp.sum(duc * u2, axis=0, keepdims=True)
        dw_ref[1:2, :] = jnp.sum(duc * u1, axis=0, keepdims=True)
        dw_ref[2:3, :] = jnp.sum(duc * u, axis=0, keepdims=True)

    act = jax.ShapeDtypeStruct((m, cfg.width), BF16)
    blk = pl.BlockSpec((m, tn), lambda j: (0, j))
    return pl.pallas_call(
        body,
        name=name,
        out_shape=(act, act, act, jax.ShapeDtypeStruct((3, cfg.width), F32)),
        grid=(cfg.width // tn,),
        in_specs=_conv_specs(cfg, tn) + [pl.BlockSpec((None, m, tn), lambda j: (do_sel, 0, j))],
        out_specs=(blk, blk, blk, pl.BlockSpec((3, tn), lambda j: (0, j))),
        compiler_params=_cparams("parallel"),
    )(proj, proj, proj, conv_w, do)


def _tri(lower):
    r = lax.broadcasted_iota(jnp.int32, (LANES, LANES), 0)
    c = lax.broadcasted_iota(jnp.int32, (LANES, LANES), 1)
    return jnp.where((r >= c) if lower else (r <= c), 1.0, 0.0).astype(F32)


def fox_gate_fwd(cfg, fl, b_pad, *, name):
    m = cfg.m
    nblk = m // LANES

    def body(fl_ref, b_ref, c_ref):
        z = fl_ref[...] + b_ref[...]
        logf = jnp.minimum(z, 0.0) - jnp.log(1.0 + jnp.exp(-jnp.abs(z)))
        row = lax.broadcasted_iota(jnp.int32, (m, 1), 0)
        logf = jnp.where(row >= cfg.pad, logf, 0.0)
        tri = _tri(True)
        carry = jnp.zeros((1, LANES), F32)
        for blk in range(nblk):
            cb = jnp.dot(tri, logf[blk * LANES:(blk + 1) * LANES, :], precision=lax.Precision.HIGHEST,
                         preferred_element_type=F32) + carry
            c_ref[blk * LANES:(blk + 1) * LANES, :] = cb
            carry = cb[LANES - 1:LANES, :]

    full = pl.BlockSpec((m, LANES), lambda: (0, 0))
    return pl.pallas_call(
        body,
        name=name,
        out_shape=jax.ShapeDtypeStruct((m, LANES), F32),
        in_specs=[full, pl.BlockSpec((1, LANES), lambda: (0, 0))],
        out_specs=full,
        compiler_params=pltpu.CompilerParams(vmem_limit_bytes=VMEM_LIMIT_BYTES),
    )(fl, b_pad)


def fox_gate_bwd(cfg, fl, b_pad, dc, *, name):
    m = cfg.m
    nblk = m // LANES

    def body(fl_ref, b_ref, dc_ref, dfl_ref, db_ref):
        z = fl_ref[...] + b_ref[...]
        dlogsig = 1.0 / (1.0 + jnp.exp(z))
        row = lax.broadcasted_iota(jnp.int32, (m, 1), 0)
        gate = jnp.where(row >= cfg.pad, dlogsig, 0.0)
        dcv = dc_ref[...]
        tri = _tri(False)
        carry = jnp.zeros((1, LANES), F32)
        db = jnp.zeros((1, LANES), F32)
        for blk in reversed(range(nblk)):
            sl = slice(blk * LANES, (blk + 1) * LANES)
            rb = jnp.dot(tri, dcv[sl, :], precision=lax.Precision.HIGHEST, preferred_element_type=F32) + carry
            carry = rb[0:1, :]
            dfl = rb * gate[sl, :]
            dfl_ref[sl, :] = dfl
            db = db + jnp.sum(dfl, axis=0, keepdims=True)
        db_ref[...] = db

    full = pl.BlockSpec((m, LANES), lambda: (0, 0))
    one = pl.BlockSpec((1, LANES), lambda: (0, 0))
    return pl.pallas_call(
        body,
        name=name,
        out_shape=(jax.ShapeDtypeStruct((m, LANES), F32), jax.ShapeDtypeStruct((1, LANES), F32)),
        in_specs=[full, one, full],
        out_specs=(full, one),
        compiler_params=pltpu.CompilerParams(vmem_limit_bytes=VMEM_LIMIT_BYTES),
    )(fl, b_pad, dc)


def _sigmoid(x):
    return 1.0 / (1.0 + jnp.exp(-x))


def gate_merge_fwd(cfg, y, proj, *, name):
    m, d = cfg.m, cfg.d
    tm, tn = _tile(m, 1088, 16), _tile(d, 512)
    nd = d // tn

    def body(y_ref, g0_ref, g1_ref, g2_ref, o_ref):
        acc = None
        for n, g_ref in enumerate((g0_ref, g1_ref, g2_ref)):
            t = _sigmoid(g_ref[...].astype(F32)) * y_ref[n].astype(F32)
            acc = t if acc is None else acc + t
        o_ref[...] = acc.astype(o_ref.dtype)

    gate = lambda n: pl.BlockSpec((tm, tn), lambda i, j: (i, n * nd + j))
    return pl.pallas_call(
        body,
        name=name,
        out_shape=jax.ShapeDtypeStruct((m, d), BF16),
        grid=(m // tm, nd),
        in_specs=[pl.BlockSpec((3, tm, tn), lambda i, j: (0, i, j)), gate(0), gate(1), gate(2)],
        out_specs=pl.BlockSpec((tm, tn), lambda i, j: (i, j)),
        compiler_params=_cparams("parallel", "parallel"),
    )(y, proj, proj, proj)


def gate_merge_bwd(cfg, dm, y, proj, *, name):
    m, d = cfg.m, cfg.d
    tm, tn = _tile(m, 1088, 16), _tile(d, 512)
    nd = d // tn

    def body(dm_ref, y_ref, g_ref, dy_ref, dg_ref):
        sg = _sigmoid(g_ref[...].astype(F32))
        dmv = dm_ref[...].astype(F32)
        dy_ref[...] = (sg * dmv).astype(dy_ref.dtype)
        dg_ref[...] = (dmv * y_ref[...].astype(F32) * sg * (1.0 - sg)).astype(dg_ref.dtype)

    return pl.pallas_call(
        body,
        name=name,
        out_shape=(jax.ShapeDtypeStruct((3, m, d), BF16), jax.ShapeDtypeStruct((m, 3 * d), BF16)),
        grid=(m // tm, nd, 3),
        in_specs=[
            pl.BlockSpec((tm, tn), lambda i, j, n: (i, j)),
            pl.BlockSpec((None, tm, tn), lambda i, j, n: (n, i, j)),
            pl.BlockSpec((tm, tn), lambda i, j, n: (i, n * nd + j)),
        ],
        out_specs=(
            pl.BlockSpec((None, tm, tn), lambda i, j, n: (n, i, j)),
            pl.BlockSpec((tm, tn), lambda i, j, n: (i, n * nd + j)),
        ),
        compiler_params=_cparams("parallel", "parallel", "parallel"),
    )(dm, y, proj)


def swiglu_fwd(cfg, gu, *, name):
    m, f = cfg.m, cfg.d_ff
    tm, tn = _tile(m, 1088, 16), _tile(f, 512)
    nf = f // tn

    def body(g_ref, u_ref, o_ref):
        g = g_ref[...].astype(F32)
        o_ref[...] = (g * _sigmoid(g) * u_ref[...].astype(F32)).astype(o_ref.dtype)

    return pl.pallas_call(
        body,
        name=name,
        out_shape=jax.ShapeDtypeStruct((m, f), BF16),
        grid=(m // tm, nf),
        in_specs=[pl.BlockSpec((tm, tn), lambda i, j: (i, j)), pl.BlockSpec((tm, tn), lambda i, j: (i, nf + j))],
        out_specs=pl.BlockSpec((tm, tn), lambda i, j: (i, j)),
        compiler_params=_cparams("parallel", "parallel"),
    )(gu, gu)


def swiglu_bwd(cfg, dact, gu, *, name):
    m, f = cfg.m, cfg.d_ff
    tm, tn = _tile(m, 1088, 16), _tile(f, 512)
    nf = f // tn

    def body(da_ref, g_ref, u_ref, o_ref):
        j = pl.program_id(1)
        g, u, da = g_ref[...].astype(F32), u_ref[...].astype(F32), da_ref[...].astype(F32)
        sg = _sigmoid(g)
        dg = da * u * sg * (1.0 + g * (1.0 - sg))
        du = da * g * sg
        o_ref[...] = jnp.where(j < nf, dg, du).astype(o_ref.dtype)

    return pl.pallas_call(
        body,
        name=name,
        out_shape=jax.ShapeDtypeStruct((m, 2 * f), BF16),
        grid=(m // tm, 2 * nf),
        in_specs=[
            pl.BlockSpec((tm, tn), lambda i, j: (i, j % nf)),
            pl.BlockSpec((tm, tn), lambda i, j: (i, j % nf)),
            pl.BlockSpec((tm, tn), lambda i, j: (i, nf + j % nf)),
        ],
        out_specs=pl.BlockSpec((tm, tn), lambda i, j: (i, j)),
        compiler_params=_cparams("parallel", "parallel"),
    )(dact, gu, gu)


def loss_head(cfg, h, target, *, name):
    m, d = cfg.m, cfg.d
    assert cfg.pad + cfg.n_meta == LANES
    tm = LANES
    inv_d = 1.0 / d

    def body(h_ref, t_ref, dh_ref, loss_ref):
        i = pl.program_id(0)

        @pl.when(i == 0)
        def _():
            dh_ref[...] = jnp.zeros_like(dh_ref)
            loss_ref[...] = jnp.zeros_like(loss_ref)

        @pl.when(i > 0)
        def _():
            err = h_ref[...] - t_ref[...]
            dh_ref[...] = err * inv_d
            loss_ref[...] += 0.5 * inv_d * jnp.sum(err * err)

    return pl.pallas_call(
        body,
        name=name,
        out_shape=(jax.ShapeDtypeStruct((m, d), F32), jax.ShapeDtypeStruct((8, LANES), F32)),
        grid=(m // tm,),
        in_specs=[pl.BlockSpec((tm, d), lambda i: (i, 0)), pl.BlockSpec((tm, d), lambda i: (jnp.maximum(i - 1, 0), 0))],
        out_specs=(pl.BlockSpec((tm, d), lambda i: (i, 0)), pl.BlockSpec((8, LANES), lambda i: (0, 0))),
        compiler_params=_cparams("arbitrary"),
    )(h, target)


def adamw(w, g, m_, v_, *, name):
    r, c = w.shape
    c_pad = -(-c // LANES) * LANES
    tr = r
    if r % 8 == 0:
        tr = _tile(r, max(8, (3 << 19) // (4 * c_pad) // 8 * 8), 8)
    bc1 = 1.0 - ADAM_B1 ** ADAM_STEP
    bc2 = 1.0 - ADAM_B2 ** ADAM_STEP

    def body(w_ref, g_ref, m_ref, v_ref, d_ref, nm_ref, nv_ref):
        gv = g_ref[...]
        nm = ADAM_B1 * m_ref[...] + (1.0 - ADAM_B1) * gv
        nv = ADAM_B2 * v_ref[...] + (1.0 - ADAM_B2) * (gv * gv)
        d_ref[...] = -ADAM_LR * ((nm / bc1) / (jnp.sqrt(nv / bc2) + ADAM_EPS) + ADAM_WD * w_ref[...])
        nm_ref[...] = nm
        nv_ref[...] = nv

    blk = pl.BlockSpec((tr, c), lambda i: (i, 0))
    shp = jax.ShapeDtypeStruct((r, c), F32)
    return pl.pallas_call(
        body,
        name=name,
        out_shape=(shp, shp, shp),
        grid=(r // tr,),
        in_specs=[blk, blk, blk, blk],
        out_specs=(blk, blk, blk),
        compiler_params=_cparams("parallel"),
    )(w, g, m_, v_)


def adamw_halves(w, g_own, g_other, core, m_, v_, *, first_layer=0, prev=None, name):
    _, r, c = w.shape
    nl = g_own.shape[0]
    r2 = r // 2
    c_pad = -(-c // LANES) * LANES
    tr = _tile(r2, max(8, (3 << 19) // (4 * c_pad) // 8 * 8), 8)
    nr = r2 // tr
    bc1 = 1.0 - ADAM_B1 ** ADAM_STEP
    bc2 = 1.0 - ADAM_B2 ** ADAM_STEP

    def body(core_ref, w_ref, go_ref, gr_ref, m_ref, v_ref, *rest):
        g_ref, d_ref, nm_ref, nv_ref = rest[-4:]
        gv = jnp.where(pl.program_id(2) == core_ref[0], go_ref[...], gr_ref[...])[:, :c]
        nm = ADAM_B1 * m_ref[...] + (1.0 - ADAM_B1) * gv
        nv = ADAM_B2 * v_ref[...] + (1.0 - ADAM_B2) * (gv * gv)
        d_ref[...] = -ADAM_LR * ((nm / bc1) / (jnp.sqrt(nv / bc2) + ADAM_EPS) + ADAM_WD * w_ref[...])
        g_ref[...] = gv
        nm_ref[...] = nm
        nv_ref[...] = nv

    full = pl.BlockSpec((None, tr, c), lambda l, i, hf, core_ref: (first_layer + l, hf * nr + i, 0))
    half = pl.BlockSpec((None, tr, g_own.shape[2]), lambda l, i, hf, core_ref: (l, i, 0))
    shp = jax.ShapeDtypeStruct(w.shape, F32)
    kept = list(prev or ())
    return pl.pallas_call(
        body,
        name=name,
        out_shape=(shp, shp, shp, shp),
        grid_spec=pltpu.PrefetchScalarGridSpec(
            num_scalar_prefetch=1,
            grid=(nl, nr, 2),
            in_specs=[full, half, half, full, full] + [pl.BlockSpec(memory_space=pl.ANY)] * len(kept),
            out_specs=(full, full, full, full),
        ),
        input_output_aliases={6 + k: k for k in range(len(kept))},
        compiler_params=_cparams("parallel", "parallel", "arbitrary"),
    )(core, w, g_own, g_other, m_, v_, *kept)


_HBM = pl.BlockSpec(memory_space=pltpu.HBM)


def _place():
    x, y, c = lax.axis_index("x"), lax.axis_index("y"), lax.axis_index("c")
    flips = [(1 - x, y), (x, 1 - y), (1 - x, 1 - y)]
    return x, y, c, flips


_SEM = pl.BlockSpec(memory_space=pltpu.SEMAPHORE)
_EFFECT = pltpu.SideEffectType.DATAFLOW_SIDE_EFFECTING


def _gather_plan(halves):
    def plan(src_refs, land_refs, arrival):
        x, y, c, flips = _place()
        mine = 2 * x + y
        out = []
        for w, h in enumerate(halves):
            for fx, fy in flips:
                slot = (2 * fx + fy) if arrival else mine
                out.append((src_refs[w].at[pl.ds(c * h, h), :], land_refs[w].at[slot, pl.ds(c * h, h), :], (fx, fy, c)))
        return out
    return plan


def _exchange_plan(nw):
    def plan(src_refs, land_refs, arrival):
        _, _, c, flips = _place()
        return [(src_refs[w].at[2 * fx + fy], land_refs[w].at[k], (fx, fy, c)) for w in range(nw) for k, (fx, fy) in enumerate(flips)]
    return plan


def _swap_plan(halves):
    def plan(src_refs, land_refs, arrival):
        x, y, c, _ = _place()
        return [(src_refs[w].at[:, pl.ds((1 - c) * h, h), :], land_refs[w], (x, y, 1 - c)) for w, h in enumerate(halves)]
    return plan


def _share_plan(nw):
    def plan(src_refs, land_refs, arrival):
        x, y, c, _ = _place()
        return [(src_refs[w], land_refs[w], (x, y, 1 - c)) for w in range(nw)]
    return plan


def copies_start(srcs, land_shapes, plan, n_copies, *, name):
    lands = [lax.empty(s, a.dtype) for s, a in zip(land_shapes, srcs)]
    n_in = len(srcs) + len(lands)

    def body(*refs):
        src_refs, land_refs = refs[:len(srcs)], refs[len(srcs):n_in]
        send_sems, recv_sems, token = refs[n_in], refs[n_in + 1], refs[-1]
        for i, (src, dst, to) in enumerate(plan(src_refs, land_refs, False)):
            pltpu.make_async_remote_copy(src_ref=src, dst_ref=dst, send_sem=send_sems.at[i], recv_sem=recv_sems.at[i],
                                         device_id=to, device_id_type=MESH).start()
        token[...] = jnp.zeros_like(token)

    operands = list(srcs) + lands
    out = pl.pallas_call(
        body,
        name=name,
        out_shape=(pltpu.SemaphoreType.DMA((n_copies,)), pltpu.SemaphoreType.DMA((n_copies,)),
                   *[pltpu.HBM(a.shape, a.dtype) for a in operands], jax.ShapeDtypeStruct((8, LANES), F32)),
        in_specs=[_HBM] * n_in,
        out_specs=(_SEM, _SEM, *[_HBM] * n_in, pl.BlockSpec(memory_space=pltpu.VMEM)),
        input_output_aliases={i: 2 + i for i in range(n_in)},
        compiler_params=pltpu.CompilerParams(has_side_effects=_EFFECT),
    )(*[pltpu.with_memory_space_constraint(a, pltpu.HBM) for a in operands])
    return out[0], out[1], list(out[2:2 + len(srcs)]), list(out[2 + len(srcs):2 + n_in]), out[-1]


def copies_wait(send_sems, recv_sems, srcs, lands, plan, after, *, name):
    n_in = len(srcs) + len(lands)
    after = list(after) if isinstance(after, (list, tuple)) else [after]

    def body(*refs):
        src_refs, land_refs = refs[:len(srcs)], refs[len(srcs):n_in]
        send_ref, recv_ref, token = refs[n_in], refs[n_in + 1], refs[-1]
        token[...] = jnp.zeros_like(token)
        for i, (src, dst, to) in enumerate(plan(src_refs, land_refs, True)):
            copy = pltpu.make_async_remote_copy(src_ref=src, dst_ref=dst, send_sem=send_ref.at[i], recv_sem=recv_ref.at[i],
                                                device_id=to, device_id_type=MESH)
            copy.wait_send()
            copy.wait_recv()

    operands = list(srcs) + list(lands)
    out = pl.pallas_call(
        body,
        name=name,
        out_shape=(*[pltpu.HBM(a.shape, a.dtype) for a in operands], jax.ShapeDtypeStruct((8, LANES), F32)),
        in_specs=[_HBM] * n_in + [_SEM, _SEM] + [pl.BlockSpec(memory_space=pl.ANY)] * len(after),
        out_specs=(*[_HBM] * n_in, pl.BlockSpec(memory_space=pltpu.VMEM)),
        input_output_aliases={i: i for i in range(n_in)},
        compiler_params=pltpu.CompilerParams(has_side_effects=_EFFECT),
    )(*operands, send_sems, recv_sems, *after)
    return list(out[:len(srcs)]), list(out[len(srcs):n_in]), out[-1]


def forward_halves(lands, *, name):
    nw = len(lands)
    halves = [a.shape[1] // 2 for a in lands]

    def body(*refs):
        ins, outs = refs[:nw], refs[nw:2 * nw]
        send_sems, recv_sems = refs[2 * nw:]
        x, y, c, flips = _place()
        copies = []
        for w, h in enumerate(halves):
            for k, (fx, fy) in enumerate(flips):
                rows = (2 * fx + fy, pl.ds(c * h, h), slice(None))
                copies.append(pltpu.make_async_remote_copy(src_ref=ins[w].at[rows], dst_ref=outs[w].at[rows], send_sem=send_sems.at[3 * w + k],
                                                           recv_sem=recv_sems.at[3 * w + k], device_id=(x, y, 1 - c), device_id_type=MESH))
        for cp in copies:
            cp.start()
        for cp in copies:
            cp.wait()

    return pl.pallas_call(
        body,
        name=name,
        out_shape=tuple(jax.ShapeDtypeStruct(a.shape, a.dtype) for a in lands),
        in_specs=[_HBM] * nw,
        out_specs=tuple([_HBM] * nw),
        input_output_aliases={w: w for w in range(nw)},
        scratch_shapes=[pltpu.SemaphoreType.DMA((3 * nw,)), pltpu.SemaphoreType.DMA((3 * nw,))],
    )(*lands)


def share_halves(sums, *, name):
    nw = len(sums)

    def body(*refs):
        ins, outs = refs[:nw], refs[nw:2 * nw]
        send_sems, recv_sems = refs[2 * nw:]
        x, y, c, _ = _place()
        copies = [
            pltpu.make_async_remote_copy(src_ref=ins[w], dst_ref=outs[w], send_sem=send_sems.at[w], recv_sem=recv_sems.at[w],
                                         device_id=(x, y, 1 - c), device_id_type=MESH)
            for w in range(nw)
        ]
        for cp in copies:
            cp.start()
        for cp in copies:
            cp.wait()

    return pl.pallas_call(
        body,
        name=name,
        out_shape=tuple(jax.ShapeDtypeStruct(s.shape, s.dtype) for s in sums),
        in_specs=[_HBM] * nw,
        out_specs=tuple([_HBM] * nw),
        scratch_shapes=[pltpu.SemaphoreType.DMA((nw,)), pltpu.SemaphoreType.DMA((nw,))],
    )(*sums)


def gather_blocks(block, *, reduce, name):
    rows, cols = block.shape

    def body(x_ref, out_ref, *rest):
        if reduce:
            buf_ref, send_sems, recv_sems = rest
        else:
            send_sems, recv_sems = rest
            buf_ref = out_ref
        x, y, c, flips = _place()
        me, sibling = (x, y, c), (x, y, 1 - c)

        def slot(px, py, pc):
            return buf_ref.at[4 * px + 2 * py + pc]

        def copy(k, blk, to, src=None):
            return pltpu.make_async_remote_copy(src_ref=slot(*blk) if src is None else src, dst_ref=slot(*blk),
                                                send_sem=send_sems.at[k], recv_sem=recv_sems.at[k], device_id=to,
                                                device_id_type=MESH)

        buf_ref[4 * x + 2 * y + c] = x_ref[...]
        first = [copy(0, me, sibling, src=x_ref)]
        first += [copy(1 + j, me, (*chip, c), src=x_ref) for j, chip in enumerate(flips)]
        for cp in first:
            cp.start()
        passed = [copy(4 + j, (*chip, c), sibling) for j, chip in enumerate(flips)]
        for j, chip in enumerate(flips):
            copy(1 + j, (*chip, c), me).wait_recv()
            passed[j].start()
        copy(0, sibling, me).wait_recv()
        for j, chip in enumerate(flips):
            copy(4 + j, (*chip, 1 - c), me).wait_recv()
        for cp in first + passed:
            cp.wait_send()
        if reduce:
            acc = buf_ref[0]
            for dev in range(1, N_DEV):
                acc = acc + buf_ref[dev]
            out_ref[...] = acc

    vmem = pl.BlockSpec(memory_space=pltpu.VMEM)
    sems = [pltpu.SemaphoreType.DMA((7,)), pltpu.SemaphoreType.DMA((7,))]
    if reduce:
        out_shape = jax.ShapeDtypeStruct((rows, cols), block.dtype)
        scratch = [pltpu.VMEM((N_DEV, rows, cols), block.dtype)] + sems
    else:
        out_shape = jax.ShapeDtypeStruct((N_DEV, rows, cols), block.dtype)
        scratch = sems
    return pl.pallas_call(
        body,
        name=name,
        out_shape=out_shape,
        in_specs=[vmem],
        out_specs=vmem,
        scratch_shapes=scratch,
        compiler_params=pltpu.CompilerParams(vmem_limit_bytes=VMEM_LIMIT_BYTES),
    )(block)


def add_own_half(grad, recv, core, *, name):
    _, r2, cols = recv.shape
    tr = _tile(r2, max(16, (1 << 20) // (2 * cols) // 16 * 16), 16)
    nr = r2 // tr

    def body(core_ref, g_ref, r_ref, o_ref):
        o_ref[...] = (g_ref[...].astype(F32) + r_ref[...].astype(F32)).astype(o_ref.dtype)

    return pl.pallas_call(
        body,
        name=name,
        out_shape=jax.ShapeDtypeStruct(recv.shape, BF16),
        grid_spec=pltpu.PrefetchScalarGridSpec(
            num_scalar_prefetch=1,
            grid=(N_CHIPS, nr),
            in_specs=[
                pl.BlockSpec((None, tr, cols), lambda k, i, core_ref: (k, core_ref[0] * nr + i, 0)),
                pl.BlockSpec((None, tr, cols), lambda k, i, core_ref: (k, i, 0)),
            ],
            out_specs=pl.BlockSpec((None, tr, cols), lambda k, i, core_ref: (k, i, 0)),
        ),
        compiler_params=_cparams("parallel", "parallel"),
    )(core, grad, recv)


def sum_chips(part, recv, chip, sums, layer, *, name):
    _, r2, cols = part.shape
    tr = _tile(r2, max(16, (1 << 20) // (2 * cols) // 16 * 16), 16)

    def body(chip_ref, p_ref, r_ref, sums_ref, o_ref):
        acc = p_ref[...].astype(F32)
        for k in range(3):
            acc = acc + r_ref[k].astype(F32)
        o_ref[...] = acc

    return pl.pallas_call(
        body,
        name=name,
        out_shape=jax.ShapeDtypeStruct(sums.shape, F32),
        grid_spec=pltpu.PrefetchScalarGridSpec(
            num_scalar_prefetch=1,
            grid=(r2 // tr,),
            in_specs=[
                pl.BlockSpec((None, tr, cols), lambda i, chip_ref: (chip_ref[0], i, 0)),
                pl.BlockSpec((3, tr, cols), lambda i, chip_ref: (0, i, 0)),
                pl.BlockSpec(memory_space=pl.ANY),
            ],
            out_specs=pl.BlockSpec((None, tr, cols), lambda i, chip_ref: (layer, i, 0)),
        ),
        input_output_aliases={3: 0},
        compiler_params=_cparams("parallel"),
    )(chip, part, recv, sums)


WEIGHTS = ("w_in", "w_uq", "w_ukv", "w_branch", "w_out", "w_ffn_in", "w_ffn_out")


def layer_fwd(cfg, h, w, s, tabs, tag):
    m, d, hd = cfg.m, cfg.d, cfg.heads
    fox_blk = cfg.off_fox // LANES
    hn = rmsnorm_fwd(h, s["g_mix_pre"], BF16, name=f"norm_mix_pre{tag}")
    proj = matmul(hn, w["w_in"], "nn", BF16, tm=m, tn=_tile(cfg.d_inp, 512), tk=d, name=f"proj{tag}")
    fl = matmul(hn, w["w_in"][:, cfg.off_fl:cfg.off_fl + LANES], "nn", F32, tm=m, tn=LANES, tk=d, name=f"proj_forget{tag}")
    cqn = rmsnorm_fwd(proj, s["g_q_lat"], BF16, width=cfg.q_rank, col_blk=cfg.off_cq // cfg.q_rank, name=f"norm_q{tag}")
    ckvn = rmsnorm_fwd(proj, s["g_kv_lat"], BF16, width=cfg.kv_rank, col_blk=cfg.off_ckv // cfg.kv_rank, name=f"norm_kv{tag}")
    q = matmul(cqn, w["w_uq"], "nn", BF16, tm=m, tn=_tile(2 * cfg.width, 512), tk=cfg.q_rank, name=f"up_q{tag}")
    kv = matmul(ckvn, w["w_ukv"], "nn", BF16, tm=m, tn=_tile(2 * cfg.width, 512), tk=cfg.kv_rank, name=f"up_kv{tag}")
    qf, kf = mla_prep_fwd(cfg, q, kv, proj, tabs[0], name=f"mla_prep{tag}")
    o, lse_a = attn_fwd(qf, kf, kv, heads=hd, dk=2 * LANES, dv=LANES, qblk0=0, kblk0=0, vblk0=hd,
                        scale=(LANES + ROPE) ** -0.5, pad=cfg.pad, slot=0, name=f"mla_attn{tag}")
    o = conv_fwd(cfg, proj, s["conv_w"], o, 1, name=f"conv{tag}")
    b_pad = jnp.pad(s["b_forget"], (0, LANES - hd)).reshape(1, LANES)
    cum = fox_gate_fwd(cfg, fl, b_pad, name=f"fox_gate{tag}")
    cum_t = cum[:, :hd].T
    decay = (cum_t[:, :, None], cum_t[:, None, :])
    o, lse_c = attn_fwd(proj, proj, proj, heads=hd, dk=LANES, dv=LANES, qblk0=fox_blk, kblk0=fox_blk + hd, vblk0=fox_blk + 2 * hd,
                        scale=LANES ** -0.5, pad=cfg.pad, slot=2, branches=o, decay=decay, name=f"fox_attn{tag}")
    y = matmul(o, w["w_branch"], "nn", BF16, tm=m, tn=_tile(d, 512), tk=cfg.width, name=f"branch{tag}")
    merged = gate_merge_fwd(cfg, y, proj, name=f"merge{tag}")
    mix = matmul(merged, w["w_out"], "nn", F32, tm=m, tn=_tile(d, 256), tk=d, name=f"out_proj{tag}")
    h_mid = rmsnorm_fwd(mix, s["g_mix_post"], F32, res=h, name=f"norm_mix_post{tag}")
    if hasattr(w, "land_ffn"):
        s = w.land_ffn(h_mid, s)
    hn2 = rmsnorm_fwd(h_mid, s["g_ffn_pre"], BF16, name=f"norm_ffn_pre{tag}")
    gu = matmul(hn2, w["w_ffn_in"], "nn", BF16, tm=m, tn=_tile(2 * cfg.d_ff, 512), tk=d, name=f"ffn_in{tag}")
    act = swiglu_fwd(cfg, gu, name=f"swiglu{tag}")
    f = matmul(act, w["w_ffn_out"], "nn", F32, tm=m, tn=_tile(d, 512), tk=_tile(cfg.d_ff, 2816), name=f"ffn_out{tag}")
    h_next = rmsnorm_fwd(f, s["g_ffn_post"], F32, res=h_mid, name=f"norm_ffn_post{tag}")
    saved = dict(h=h, hn=hn, proj=proj, fl=fl, cqn=cqn, ckvn=ckvn, kv=kv, qf=qf, kf=kf, lse_a=lse_a,
                 b_pad=b_pad, decay=decay, lse_c=lse_c, o=o, y=y, merged=merged, mix=mix, h_mid=h_mid,
                 hn2=hn2, gu=gu, act=act, f=f)
    return h_next, s, saved


def layer_bwd(cfg, dh, w, s, r, tabs, tag, grads_done):
    m, d, hd = cfg.m, cfg.d, cfg.heads
    fox_blk = cfg.off_fox // LANES
    tk_m = m
    df, dg4 = rmsnorm_bwd(r["f"], s["g_ffn_post"], dh, BF16, name=f"norm_ffn_post_bwd{tag}")
    dact = matmul(df, w["w_ffn_out"], "nt", BF16, tm=m, tn=_tile(cfg.d_ff, 512), tk=d, name=f"ffn_out_dx{tag}")
    dw_fo = matmul(r["act"], df, "tn", BF16, tm=_tile(cfg.d_ff, 512), tn=_tile(d, 1024), tk=tk_m, name=f"ffn_out_dw{tag}")
    dgu = swiglu_bwd(cfg, dact, r["gu"], name=f"swiglu_bwd{tag}")
    dhn2 = matmul(dgu, w["w_ffn_in"], "nt", F32, tm=m, tn=_tile(d, 512), tk=_tile(2 * cfg.d_ff, 2816), name=f"ffn_in_dx{tag}")
    dw_fi = matmul(r["hn2"], dgu, "tn", BF16, tm=_tile(d, 1024), tn=_tile(2 * cfg.d_ff // N_CHIPS, 1408), tk=tk_m, chip_cols=True,
                   name=f"ffn_in_dw{tag}")
    token = grads_done(dict(w_ffn_in=dw_fi, w_ffn_out=dw_fo))
    if token is not None:
        s = {**s, "g_ffn_pre": s["g_ffn_pre"] + token[0, 0]}
    dh_mid, dg3 = rmsnorm_bwd(r["h_mid"], s["g_ffn_pre"], dhn2, F32, dres=dh, name=f"norm_ffn_pre_bwd{tag}")
    dmix, dg2 = rmsnorm_bwd(r["mix"], s["g_mix_post"], dh_mid, BF16, name=f"norm_mix_post_bwd{tag}")
    dmerged = matmul(dmix, w["w_out"], "nt", BF16, tm=m, tn=_tile(d, 512), tk=d, name=f"out_proj_dx{tag}")
    dw_out = matmul(r["merged"], dmix, "tn", BF16, tm=_tile(d, 1024), tn=_tile(d, 512), tk=tk_m, name=f"out_proj_dw{tag}")
    dy, dgl = gate_merge_bwd(cfg, dmerged, r["y"], r["proj"], name=f"merge_bwd{tag}")
    do = matmul(dy, w["w_branch"], "nt", BF16, tm=m, tn=_tile(cfg.width, 512), tk=d, name=f"branch_dx{tag}")
    dw_br = matmul(r["o"], dy, "tn", BF16, tm=_tile(cfg.width, 1024), tn=_tile(d // N_CHIPS, 512), tk=tk_m, chip_cols=True,
                   name=f"branch_dw{tag}")
    dqf, dkf, dv_a = attn_bwd(r["qf"], r["kf"], r["kv"], do, 0, r["lse_a"], heads=hd, dk=2 * LANES, dv=LANES,
                              qblk0=0, kblk0=0, vblk0=hd, scale=(LANES + ROPE) ** -0.5, pad=cfg.pad, name=f"mla_attn_bwd{tag}")
    dq, dkn, dkpe = mla_prep_bwd(cfg, dqf, dkf, tabs[1], name=f"mla_prep_bwd{tag}")
    dkv = jnp.concatenate([dkn, dv_a], axis=1)
    dcqn = matmul(dq, w["w_uq"], "nt", F32, tm=m, tn=cfg.q_rank, tk=2 * cfg.width, name=f"up_q_dx{tag}")
    dw_uq = matmul(r["cqn"], dq, "tn", BF16, tm=cfg.q_rank, tn=_tile(2 * cfg.width, 512), tk=tk_m, name=f"up_q_dw{tag}")
    dckvn = matmul(dkv, w["w_ukv"], "nt", F32, tm=m, tn=cfg.kv_rank, tk=2 * cfg.width, name=f"up_kv_dx{tag}")
    dw_ukv = matmul(r["ckvn"], dkv, "tn", BF16, tm=cfg.kv_rank, tn=_tile(2 * cfg.width, 512), tk=tk_m, name=f"up_kv_dw{tag}")
    dcq, dgq = rmsnorm_bwd(r["proj"], s["g_q_lat"], dcqn, BF16, width=cfg.q_rank, col_blk=cfg.off_cq // cfg.q_rank,
                           name=f"norm_q_bwd{tag}")
    dckv, dgkv = rmsnorm_bwd(r["proj"], s["g_kv_lat"], dckvn, BF16, width=cfg.kv_rank, col_blk=cfg.off_ckv // cfg.kv_rank,
                             name=f"norm_kv_bwd{tag}")
    dcb, dcc, dcx, dconv_w = conv_bwd(cfg, r["proj"], s["conv_w"], do, 1, name=f"conv_bwd{tag}")
    dfq, dfk, dfv, dck = attn_bwd(r["proj"], r["proj"], r["proj"], do, 2, r["lse_c"], heads=hd, dk=LANES, dv=LANES,
                                  qblk0=fox_blk, kblk0=fox_blk + hd, vblk0=fox_blk + 2 * hd, scale=LANES ** -0.5,
                                  pad=cfg.pad, decay=r["decay"], name=f"fox_attn_bwd{tag}")
    dc = jnp.pad(dck[:, 0, :].T, ((0, 0), (0, LANES - hd)))
    dfl, dbf = fox_gate_bwd(cfg, r["fl"], r["b_pad"], dc, name=f"fox_gate_bwd{tag}")
    tail = jnp.zeros((m, cfg.d_inp - cfg.off_fl - LANES), BF16)
    dproj = jnp.concatenate([dgl, dcq, dckv, dcb, dcc, dcx, dfq, dfk, dfv, dkpe.astype(BF16), dfl.astype(BF16), tail], axis=1)
    dhn = matmul(dproj, w["w_in"], "nt", F32, tm=m, tn=_tile(d, 512), tk=_tile(cfg.d_inp, 2304), name=f"proj_dx{tag}")
    dw_in = matmul(r["hn"], dproj, "tn", BF16, tm=_tile(d, 1024), tn=_tile(cfg.d_inp, 512), tk=tk_m, name=f"proj_dw{tag}")
    dh_in, dg1 = rmsnorm_bwd(r["h"], s["g_mix_pre"], dhn, F32, dres=dh_mid, name=f"norm_mix_pre_bwd{tag}")
    token = grads_done(dict(w_in=dw_in, w_uq=dw_uq, w_ukv=dw_ukv, w_branch=dw_br, w_out=dw_out))
    dsmall = dict(g_mix_pre=dg1[0], g_mix_post=dg2[0], g_ffn_pre=dg3[0], g_ffn_post=dg4[0], g_q_lat=dgq[0], g_kv_lat=dgkv[0],
                  b_forget=dbf[0, :hd], conv_w=dconv_w)
    return dh_in, dsmall, token


def local_step(cfg, x, target, meta, layer_params, grads_done):
    h = jnp.concatenate([jnp.zeros((cfg.pad, cfg.d), F32), meta, x], axis=0)
    cos, s1, s2 = rope_tables(cfg)
    tabs = ((cos, s1, s2), (cos, -s1, -s2))
    saved = []
    for l in range(cfg.depth):
        w, s = layer_params(l, h)
        h, s, r = layer_fwd(cfg, h, w, s, tabs, f"_{l}")
        saved.append((w, s, r))
    dh, loss = loss_head(cfg, h, target, name="loss_head")
    dsmalls, token = [None] * cfg.depth, None
    for l in reversed(range(cfg.depth)):
        w, s, r = saved[l]
        if token is not None:
            s = {**s, "g_ffn_post": s["g_ffn_post"] + token[0, 0]}
        dh, dsmalls[l], token = layer_bwd(cfg, dh, w, s, r, tabs, f"_{l}", functools.partial(grads_done, l))
    first = cfg.pad + cfg.n_meta
    return loss, dh[first:], dh[cfg.pad:first], dsmalls


def _cols_from_chips(g):
    return jnp.transpose(g, (1, 0, 2)).reshape(g.shape[1], N_CHIPS * g.shape[2])


def _cols_to_chips(w):
    r, c = w.shape
    return jnp.transpose(w.reshape(r, N_CHIPS, c // N_CHIPS), (1, 0, 2))


def _packed_segments(cfg):
    nat = [0] + _cumsum(cfg.nat_splits)
    w = cfg.width
    order = [(10, 0), (0, cfg.off_cq), (1, cfg.off_ckv), (3, cfg.off_conv), (4, cfg.off_conv + w), (5, cfg.off_conv + 2 * w),
             (6, cfg.off_fox), (7, cfg.off_fox + w), (8, cfg.off_fox + 2 * w), (2, cfg.off_kpe), (9, cfg.off_fl)]
    return [(pk, nat[i], cfg.nat_splits[i]) for i, pk in order]


def _chip_cols(cfg):
    n = cfg.d_in // N_CHIPS
    return n, -(-n // LANES) * LANES


def _lane_pieces(cfg, to_packed):
    n, n_pad = _chip_cols(cfg)
    tiles = [[] for _ in range(cfg.d_inp // LANES if to_packed else N_CHIPS * n_pad // LANES)]
    for pk, nat, width in _packed_segments(cfg):
        g = nat
        while g < nat + width:
            k, a = divmod(g, n)
            dst = (pk + g - nat) if to_packed else (k * n_pad + a)
            run = min(nat + width - g, n - a, LANES - dst % LANES)
            src = (k, a) if to_packed else (0, pk + g - nat)
            tiles[dst // LANES].append((dst % LANES, run, *src))
            g += run
    return tiles


def _fill_tiles(pieces, read, write, rows):
    lane = lax.broadcasted_iota(jnp.int32, (rows, LANES), 1)
    for t, parts in enumerate(pieces):
        tile = jnp.zeros((rows, LANES), F32)
        for dl, run, blk, col in parts:
            w0 = col // LANES * LANES
            off = col - w0
            span = LANES if off + run <= LANES else 2 * LANES
            win = read(blk, w0, span)
            shift = (dl - off) % span
            if shift:
                win = pltpu.roll(win, shift, 1)
            win = win[:, :LANES]
            tile = win if (dl == 0 and run == LANES) else jnp.where((lane >= dl) & (lane < dl + run), win, tile)
        write(t, tile)


def pack_w_in_blocks(cfg, lands, own, chip, *, name):
    d = cfg.d
    n, n_pad = _chip_cols(cfg)
    tr = _tile(d, 256, 16)
    pieces = _lane_pieces(cfg, True)

    def body(chip_ref, land_ref, own_ref, o_ref):
        def read(k, w0, span):
            theirs = land_ref[k, :, w0:w0 + span]
            return jnp.where(chip_ref[0] == k, own_ref[:, w0:w0 + span], theirs).astype(F32)

        def write(t, tile):
            o_ref[:, t * LANES:(t + 1) * LANES] = tile.astype(o_ref.dtype)

        _fill_tiles(pieces, read, write, tr)

    return pl.pallas_call(
        body,
        name=name,
        out_shape=jax.ShapeDtypeStruct((d, cfg.d_inp), BF16),
        grid_spec=pltpu.PrefetchScalarGridSpec(
            num_scalar_prefetch=1,
            grid=(d // tr,),
            in_specs=[pl.BlockSpec((N_CHIPS, tr, n_pad), lambda i, chip_ref: (0, i, 0)),
                      pl.BlockSpec((tr, n_pad), lambda i, chip_ref: (i, 0))],
            out_specs=pl.BlockSpec((tr, cfg.d_inp), lambda i, chip_ref: (i, 0)),
        ),
        compiler_params=_cparams("parallel"),
    )(chip, lands, own)


def unpack_w_in_blocks(cfg, dw, *, name):
    d = cfg.d
    n, n_pad = _chip_cols(cfg)
    tr = _tile(d, 256, 16)
    per_blk = n_pad // LANES
    pieces = _lane_pieces(cfg, False)

    def body(dw_ref, o_ref):
        def read(_, w0, span):
            return dw_ref[:, w0:w0 + span].astype(F32)

        def write(t, tile):
            k, i = divmod(t, per_blk)
            o_ref[k, :, i * LANES:(i + 1) * LANES] = tile.astype(o_ref.dtype)

        _fill_tiles(pieces, read, write, tr)

    return pl.pallas_call(
        body,
        name=name,
        out_shape=jax.ShapeDtypeStruct((N_CHIPS, d, n_pad), BF16),
        grid=(d // tr,),
        in_specs=[pl.BlockSpec((tr, cfg.d_inp), lambda i: (i, 0))],
        out_specs=pl.BlockSpec((N_CHIPS, tr, n_pad), lambda i: (0, i, 0)),
        compiler_params=_cparams("parallel"),
    )(dw)


def full_weights(cfg, g, w_in=None):
    make = dict(
        w_uq=lambda a: pack_w_uq(cfg, _cols_from_chips(a)),
        w_ukv=lambda a: pack_w_ukv(cfg, _cols_from_chips(a)),
        w_branch=lambda a: _cols_from_chips(a).reshape(3, cfg.width, cfg.d),
        w_out=lambda a: a.reshape(cfg.d, cfg.d),
        w_ffn_in=_cols_from_chips,
        w_ffn_out=lambda a: a.reshape(cfg.d_ff, cfg.d),
    )
    out = {n: make[n](a) for n, a in g.items()}
    if w_in is not None:
        out["w_in"] = w_in
    return out


def chip_grads(cfg, dw, tag):
    make = dict(
        w_in=lambda a: unpack_w_in_blocks(cfg, a, name=f"unpack_w_in{tag}"),
        w_uq=lambda a: _cols_to_chips(unpack_w_uq(cfg, a)),
        w_ukv=lambda a: _cols_to_chips(unpack_w_ukv(cfg, a)),
        w_branch=lambda a: a.reshape(N_CHIPS, 3 * cfg.width, cfg.d // N_CHIPS),
        w_out=lambda a: a.reshape(N_CHIPS, cfg.d // N_CHIPS, cfg.d),
        w_ffn_in=lambda a: a,
        w_ffn_out=lambda a: a.reshape(N_CHIPS, cfg.d_ff // N_CHIPS, cfg.d),
    )
    return {n: make[n](a) for n, a in dw.items()}


def _small_rows(cfg):
    return dict(g_mix_pre=cfg.d // LANES, g_mix_post=cfg.d // LANES, g_ffn_pre=cfg.d // LANES, g_ffn_post=cfg.d // LANES,
                g_q_lat=cfg.q_rank // LANES, g_kv_lat=cfg.kv_rank // LANES, b_forget=1, conv_w=3 * cfg.width // LANES)


def pack_small(cfg, loss, dmeta, dsmalls):
    parts = [loss[0:1, :], dmeta.reshape(-1, LANES)]
    for ds in dsmalls:
        for k in _small_rows(cfg):
            v = ds[k]
            if k == "b_forget":
                v = jnp.pad(v, (0, LANES - cfg.heads))
            parts.append(v.reshape(-1, LANES))
    rows = sum(p.shape[0] for p in parts)
    parts.append(jnp.zeros((-rows % 8, LANES), F32))
    return jnp.concatenate(parts, axis=0)


def unpack_small(cfg, block):
    loss = block[0, 0]
    n = cfg.n_meta * cfg.d // LANES
    dmeta = block[1:1 + n].reshape(cfg.n_meta, cfg.d)
    at = 1 + n
    out = []
    for _ in range(cfg.depth):
        ds = {}
        for k, rows in _small_rows(cfg).items():
            v = block[at:at + rows]
            at += rows
            if k == "b_forget":
                v = v[0, :cfg.heads]
            elif k == "conv_w":
                v = v.reshape(3, cfg.width)
            else:
                v = v.reshape(-1)
            ds[k] = v
        out.append(ds)
    return loss, dmeta, out


def kernel(x, meta, w_in, b_forget, g_q_lat, g_kv_lat, w_uq, w_ukv, conv_w, w_branch, w_out, w_ffn_in, w_ffn_out, g_mix_pre, g_mix_post, g_ffn_pre, g_ffn_post, loss_target, m_meta, m_w_in, m_b_forget, m_g_q_lat, m_g_kv_lat, m_w_uq, m_w_ukv, m_conv_w, m_w_branch, m_w_out, m_w_ffn_in, m_w_ffn_out, m_g_mix_pre, m_g_mix_post, m_g_ffn_pre, m_g_ffn_post, v_meta, v_w_in, v_b_forget, v_g_q_lat, v_g_kv_lat, v_w_uq, v_w_ukv, v_conv_w, v_w_branch, v_w_out, v_w_ffn_in, v_w_ffn_out, v_g_mix_pre, v_g_mix_post, v_g_ffn_pre, v_g_ffn_post):
    cfg = CFG
    names = ("meta", "w_in", "b_forget", "g_q_lat", "g_kv_lat", "w_uq", "w_ukv", "conv_w", "w_branch", "w_out", "w_ffn_in",
             "w_ffn_out", "g_mix_pre", "g_mix_post", "g_ffn_pre", "g_ffn_post")
    params = dict(zip(names, (meta, w_in, b_forget, g_q_lat, g_kv_lat, w_uq, w_ukv, conv_w, w_branch, w_out, w_ffn_in, w_ffn_out,
                              g_mix_pre, g_mix_post, g_ffn_pre, g_ffn_post)))
    mom1 = dict(zip(names, (m_meta, m_w_in, m_b_forget, m_g_q_lat, m_g_kv_lat, m_w_uq, m_w_ukv, m_conv_w, m_w_branch, m_w_out,
                            m_w_ffn_in, m_w_ffn_out, m_g_mix_pre, m_g_mix_post, m_g_ffn_pre, m_g_ffn_post)))
    mom2 = dict(zip(names, (v_meta, v_w_in, v_b_forget, v_g_q_lat, v_g_kv_lat, v_w_uq, v_w_ukv, v_conv_w, v_w_branch, v_w_out,
                            v_w_ffn_in, v_w_ffn_out, v_g_mix_pre, v_g_mix_post, v_g_ffn_pre, v_g_ffn_post)))
    xi, yi, ci = lax.axis_index("x"), lax.axis_index("y"), lax.axis_index("c")
    chip = 2 * xi + yi
    chip_arr = jnp.reshape(chip, (1,)).astype(jnp.int32)
    core_arr = jnp.reshape(ci, (1,)).astype(jnp.int32)

    def shard2d(name, l, after=None):
        w = params[name][l]
        if after is not None:
            w = w + after
        w = w.reshape(-1, w.shape[-1]).astype(BF16)
        if name == "w_in":
            w = jnp.pad(w, ((0, 0), (0, _chip_cols(cfg)[1] - w.shape[1])))
        return w

    is_mine = (jnp.arange(N_CHIPS) == chip)[:, None, None]
    shard_shape = {n: shard2d(n, 0).shape for n in WEIGHTS}
    groups = (("w_in", "w_uq", "w_ukv", "w_branch", "w_out"), ("w_ffn_in", "w_ffn_out"))
    gather_plans = [_gather_plan([shard_shape[n][0] // 2 for n in g]) for g in groups]

    def gather_start(l, after):
        started = []
        for gi, g in enumerate(groups):
            started.append(copies_start([shard2d(n, l, after) for n in g], [(N_CHIPS,) + shard_shape[n] for n in g],
                                        gather_plans[gi], 3 * len(g), name=f"gather_start_{gi}_{l}"))
            after = started[-1][4][0, 0]
        return started

    def land(l, gi, started, after):
        send_sems, recv_sems, own, lands, _ = started
        own, lands, landed = copies_wait(send_sems, recv_sems, own, lands, gather_plans[gi], after, name=f"gather_wait_{gi}_{l}")
        return own, forward_halves(lands, name=f"forward_halves_{gi}_{l}"), landed

    in_flight = {0: gather_start(0, None)}
    under_way = in_flight[0][1][4][0, 0]

    meta_all = gather_blocks(meta + under_way, reduce=False, name="gather_meta")[0::2]
    meta_full = jnp.transpose(meta_all, (1, 0, 2)).reshape(cfg.n_meta, cfg.d)
    conv_rows = conv_w.reshape(cfg.depth * 3, cfg.width // N_CHIPS) + under_way
    conv_all = gather_blocks(conv_rows, reduce=False, name="gather_conv_w")[0::2]
    conv_full = jnp.transpose(conv_all, (1, 0, 2)).reshape(cfg.depth, 3, cfg.width)
    three_d = lambda a, n: a.reshape(cfg.depth, -1, params[n].shape[-1])
    adam_in = {n: (three_d(params[n], n), three_d(mom1[n], n), three_d(mom2[n], n)) for n in WEIGHTS}
    adam_in["w_in"] = tuple(a + under_way for a in adam_in["w_in"])

    class LayerWeights(dict):
        def __init__(self, l, h):
            own, lands, _ = land(l, 0, in_flight[l][0], [h, *adam_in["w_in"]] if l == 0 else h)
            got = {n: jnp.where(is_mine, o[None], g) for n, o, g in zip(groups[0][1:], own[1:], lands[1:])}
            super().__init__(full_weights(cfg, got, pack_w_in_blocks(cfg, lands[0], own[0], chip_arr, name=f"pack_w_in_{l}")))
            self.layer = l

        def land_ffn(self, after, s):
            l = self.layer
            own, lands, landed = land(l, 1, in_flight.pop(l)[1], after)
            self.update(full_weights(cfg, {n: jnp.where(is_mine, o[None], g) for n, o, g in zip(groups[1], own, lands)}))
            if l + 1 == cfg.depth:
                return s
            in_flight[l + 1] = gather_start(l + 1, landed[0, 0])
            return {**s, "g_ffn_pre": s["g_ffn_pre"] + in_flight[l + 1][1][4][0, 0]}

    def layer_params(l, h):
        s = dict(g_mix_pre=g_mix_pre[l], g_mix_post=g_mix_post[l], g_ffn_pre=g_ffn_pre[l], g_ffn_post=g_ffn_post[l],
                 g_q_lat=g_q_lat[l], g_kv_lat=g_kv_lat[l], b_forget=b_forget[l], conv_w=conv_full[l])
        s["g_mix_pre"] = s["g_mix_pre"] + in_flight[l][1][4][0, 0]
        return LayerWeights(l, h), s

    half_shape = {n: (shard_shape[n][0] // 2, shard_shape[n][1]) for n in WEIGHTS}
    sums_upper = {n: jnp.zeros((cfg.depth - 1,) + half_shape[n], F32) for n in WEIGHTS}
    sums_first = {n: jnp.zeros((1,) + half_shape[n], F32) for n in WEIGHTS}
    swapping, exchanging = [], []

    def finish_exchange(after):
        l, names_, (send_sems, recv_sems, parts, lands, _) = exchanging.pop(0)
        parts, others, _ = copies_wait(send_sems, recv_sems, parts, lands, _exchange_plan(len(names_)), after,
                                       name=f"exchange_wait_{names_[0]}_{l}")
        for n, p, o in zip(names_, parts, others):
            if l == 0:
                sums_first[n] = sum_chips(p, o, chip_arr, sums_first[n], 0, name=f"sum_chips_{n}_{l}")
            else:
                sums_upper[n] = sum_chips(p, o, chip_arr, sums_upper[n], l - 1, name=f"sum_chips_{n}_{l}")

    def finish_swap(after):
        l, names_, (send_sems, recv_sems, mine, lands, _) = swapping.pop(0)
        plan = _swap_plan([g.shape[1] // 2 for g in mine])
        mine, theirs, _ = copies_wait(send_sems, recv_sems, mine, lands, plan, after, name=f"swap_wait_{names_[0]}_{l}")
        parts = [add_own_half(g, t, core_arr, name=f"add_own_half_{n}_{l}") for n, g, t in zip(names_, mine, theirs)]
        started = copies_start(parts, [(3,) + p.shape[1:] for p in parts], _exchange_plan(len(names_)), 3 * len(names_),
                               name=f"exchange_start_{names_[0]}_{l}")
        exchanging.append((l, names_, started))
        return started[4]

    def grads_done(l, dws):
        names_ = [n for n in WEIGHTS if n in dws]
        send = chip_grads(cfg, dws, f"_{l}")
        mine = [send[n] for n in names_]
        halves = [g.shape[1] // 2 for g in mine]
        started = copies_start(mine, [(N_CHIPS, h, g.shape[2]) for g, h in zip(mine, halves)], _swap_plan(halves), len(mine),
                               name=f"swap_start_{names_[0]}_{l}")
        token = started[4]
        if swapping:
            token = finish_swap(token)
            if len(exchanging) > 1:
                finish_exchange(token)
        swapping.append((l, names_, started))
        return token

    loss, grad_x, dmeta, dsmalls = local_step(cfg, x[0], loss_target[0], meta_full, layer_params, grads_done)
    share_plan = _share_plan(len(WEIGHTS))
    sharing = copies_start([sums_upper[n] for n in WEIGHTS], [sums_upper[n].shape for n in WEIGHTS], share_plan, len(WEIGHTS),
                           name="share_start_upper")
    last = finish_swap(sharing[4])
    while exchanging[0][0] > 0:
        finish_exchange(last)

    def update(own, other, first_layer, prev, tag):
        out = {}
        for n, mine_, theirs_ in zip(WEIGHTS, own, other):
            w3, m3, v3 = adam_in[n]
            out[n] = adamw_halves(w3, mine_, theirs_, core_arr, m3, v3, first_layer=first_layer, prev=prev and prev[n],
                                  name=f"adamw_{tag}_{n}")
        return out

    own_upper, other_upper, _ = copies_wait(sharing[0], sharing[1], sharing[2], sharing[3], share_plan, last, name="share_wait_upper")
    upper = update(own_upper, other_upper, 1, None, "upper")
    busy = sum(upper[n][1][1, 0, :LANES] for n in WEIGHTS)
    while exchanging:
        finish_exchange(busy)
    own_first = [sums_first[n] for n in WEIGHTS]
    done = update(own_first, share_halves(own_first, name="share_halves_first"), 0, upper, "first")
    grad, delta, new_m, new_v = ({n: done[n][k].reshape(params[n].shape) for n in WEIGHTS} for k in range(4))

    total = gather_blocks(pack_small(cfg, loss, dmeta, dsmalls), reduce=True, name="reduce_small")
    loss_sum, dmeta_sum, dsmall_sum = unpack_small(cfg, total)
    for k in _small_rows(cfg):
        grad[k] = jnp.stack([ds[k] for ds in dsmall_sum])
    grad["conv_w"] = lax.dynamic_slice_in_dim(grad["conv_w"], chip * (cfg.width // N_CHIPS), cfg.width // N_CHIPS, axis=2)
    grad["meta"] = lax.dynamic_slice_in_dim(dmeta_sum, chip * (cfg.d // N_CHIPS), cfg.d // N_CHIPS, axis=1)

    for n in names:
        if n in WEIGHTS:
            continue
        shp = params[n].shape
        two_d = lambda a: a.reshape(-1, shp[-1])
        dl, nm, nv = adamw(two_d(params[n]), two_d(grad[n]), two_d(mom1[n]), two_d(mom2[n]), name=f"adamw_{n}")
        delta[n], new_m[n], new_v[n] = dl.reshape(shp), nm.reshape(shp), nv.reshape(shp)

    return (loss_sum, grad_x[None], *[grad[n] for n in names], *[delta[n] for n in names], *[new_m[n] for n in names],
            *[new_v[n] for n in names])
```

```python
import functools
from typing import NamedTuple

import jax
import jax.numpy as jnp
from jax import lax
from jax.experimental import pallas as pl
from jax.experimental.pallas import tpu as pltpu

F32 = jnp.float32
BF16 = jnp.bfloat16
MESH = pl.DeviceIdType.MESH

EPS = 1e-6
NEG_INF = -1e30
ROPE_THETA = 10000.0
LANES = 128
ROPE = 64
N_CHIPS = 4
N_DEV = 8

ADAM_LR = 0.001
ADAM_B1 = 0.9
ADAM_B2 = 0.999
ADAM_EPS = 1e-08
ADAM_WD = 0.01
ADAM_STEP = 10

VMEM_LIMIT_BYTES = 48 * 1024 * 1024


class Cfg(NamedTuple):
    d: int = 2048
    seq: int = 2048
    depth: int = 4
    n_meta: int = 16
    heads: int = 8
    q_rank: int = 512
    kv_rank: int = 512
    d_ff: int = 5632

    @property
    def width(self):
        return self.heads * LANES

    @property
    def pad(self):
        return (-(self.n_meta + self.seq)) % LANES

    @property
    def m(self):
        return self.pad + self.n_meta + self.seq

    @property
    def nat_splits(self):
        w = self.width
        return (self.q_rank, self.kv_rank, ROPE, w, w, w, w, w, w, self.heads, 3 * self.d)

    @property
    def d_in(self):
        return sum(self.nat_splits)

    @property
    def off_cq(self):
        return 3 * self.d

    @property
    def off_ckv(self):
        return self.off_cq + self.q_rank

    @property
    def off_conv(self):
        return self.off_ckv + self.kv_rank

    @property
    def off_fox(self):
        return self.off_conv + 3 * self.width

    @property
    def off_kpe(self):
        return self.off_fox + 3 * self.width

    @property
    def off_fl(self):
        return self.off_kpe + LANES

    @property
    def d_inp(self):
        return -(-(self.off_fl + LANES) // 512) * 512


CFG = Cfg()


def _tile(n, target, mult=LANES):
    best = None
    t = mult
    while t <= min(n, target):
        if n % t == 0:
            best = t
        t += mult
    return best or n


def _cparams(*sem):
    return pltpu.CompilerParams(dimension_semantics=sem, vmem_limit_bytes=VMEM_LIMIT_BYTES)


def _cumsum(xs):
    out, s = [], 0
    for v in xs:
        s += v
        out.append(s)
    return out


def pack_w_uq(cfg, w):
    r = w.shape[0]
    w3 = w.reshape(r, cfg.heads, LANES + ROPE)
    w3 = jnp.pad(w3, ((0, 0), (0, 0), (0, LANES - ROPE)))
    return w3.reshape(r, cfg.heads * 2 * LANES)


def unpack_w_uq(cfg, wp):
    r = wp.shape[0]
    return wp.reshape(r, cfg.heads, 2 * LANES)[:, :, : LANES + ROPE].reshape(r, cfg.heads * (LANES + ROPE))


def pack_w_ukv(cfg, w):
    r = w.shape[0]
    w4 = w.reshape(r, cfg.heads, 2, LANES)
    return jnp.transpose(w4, (0, 2, 1, 3)).reshape(r, 2 * cfg.heads * LANES)


def unpack_w_ukv(cfg, wp):
    r = wp.shape[0]
    w4 = wp.reshape(r, 2, cfg.heads, LANES)
    return jnp.transpose(w4, (0, 2, 1, 3)).reshape(r, 2 * cfg.heads * LANES)


_DIMS = {
    "nn": (((1,), (0,)), ((), ())),
    "nt": (((1,), (1,)), ((), ())),
    "tn": (((0,), (0,)), ((), ())),
}


def matmul(a, b, mode, out_dtype, *, tm, tn, tk, name, chip_cols=False):
    batched = a.ndim == 3
    if mode == "nn":
        (m, kc), n = a.shape[-2:], b.shape[-1]
        a_blk, a_idx = (tm, tk), lambda i, j, k: (i, k)
        b_blk, b_idx = (tk, tn), lambda i, j, k: (k, j)
    elif mode == "nt":
        (m, kc), n = a.shape[-2:], b.shape[-2]
        a_blk, a_idx = (tm, tk), lambda i, j, k: (i, k)
        b_blk, b_idx = (tn, tk), lambda i, j, k: (j, k)
    else:
        (kc, m), n = a.shape[-2:], b.shape[-1]
        a_blk, a_idx = (tk, tm), lambda i, j, k: (k, i)
        b_blk, b_idx = (tk, tn), lambda i, j, k: (k, j)
    assert m % tm == 0 and n % tn == 0 and kc % tk == 0, (name, m, n, kc, tm, tn, tk)
    nk = kc // tk
    dims = _DIMS[mode]
    o_blk, o_idx = (tm, tn), lambda i, j, k: (i, j)
    grid = (m // tm, n // tn, nk)
    per = n // N_CHIPS // tn
    assert not chip_cols or n // N_CHIPS % tn == 0
    if batched:
        nb = a.shape[0]
        grid = (nb,) + grid
        wrap = lambda f: (lambda bb, i, j, k: (bb,) + f(i, j, k))
        a_blk, b_blk, o_blk = (None,) + a_blk, (None,) + b_blk, (None,) + o_blk
        a_idx, b_idx, o_idx = wrap(a_idx), wrap(b_idx), wrap(o_idx)
        out_shape = (nb, m, n)
        if chip_cols:
            o_blk, o_idx = (None,) + o_blk, lambda bb, i, j, k: (j // per, bb, i, j % per)
            out_shape = (N_CHIPS, nb, m, n // N_CHIPS)
        sem = ("parallel", "parallel", "parallel", "arbitrary")
    else:
        out_shape = (m, n)
        if chip_cols:
            o_blk, o_idx = (None,) + o_blk, lambda i, j, k: (j // per, i, j % per)
            out_shape = (N_CHIPS, m, n // N_CHIPS)
        sem = ("parallel", "parallel", "arbitrary")
    k_axis = len(grid) - 1

    def body(a_ref, b_ref, o_ref, *scratch):
        prod = lax.dot_general(a_ref[...], b_ref[...], dims, preferred_element_type=F32)
        if nk == 1:
            o_ref[...] = prod.astype(o_ref.dtype)
        else:
            acc_ref = scratch[0] if scratch else o_ref
            k = pl.program_id(k_axis)

            @pl.when(k == 0)
            def _():
                acc_ref[...] = prod

            @pl.when(k > 0)
            def _():
                acc_ref[...] += prod

            if scratch:
                @pl.when(k == nk - 1)
                def _():
                    o_ref[...] = acc_ref[...].astype(o_ref.dtype)

    return pl.pallas_call(
        body,
        name=name,
        out_shape=jax.ShapeDtypeStruct(out_shape, out_dtype),
        grid=grid,
        in_specs=[pl.BlockSpec(a_blk, a_idx), pl.BlockSpec(b_blk, b_idx)],
        out_specs=pl.BlockSpec(o_blk, o_idx),
        scratch_shapes=[] if nk == 1 or out_dtype == F32 else [pltpu.VMEM((tm, tn), F32)],
        compiler_params=_cparams(*sem),
    )(a, b)


def _row_tile(m):
    return _tile(m, 272, 16)


def rmsnorm_fwd(x, g, out_dtype, *, name, width=None, col_blk=0, res=None):
    m = x.shape[0]
    n = width or x.shape[1]
    tm = _row_tile(m)
    has_res = res is not None

    def body(x_ref, g_ref, *rest):
        o_ref = rest[-1]
        xf = x_ref[...].astype(F32)
        r = lax.rsqrt(jnp.mean(xf * xf, axis=-1, keepdims=True) + EPS)
        y = xf * r * g_ref[...]
        if has_res:
            y = rest[0][...] + y
        o_ref[...] = y.astype(o_ref.dtype)

    in_specs = [pl.BlockSpec((tm, n), lambda i: (i, col_blk)), pl.BlockSpec((1, n), lambda i: (0, 0))]
    args = [x, g.reshape(1, n)]
    if has_res:
        in_specs.append(pl.BlockSpec((tm, n), lambda i: (i, 0)))
        args.append(res)
    return pl.pallas_call(
        body,
        name=name,
        out_shape=jax.ShapeDtypeStruct((m, n), out_dtype),
        grid=(m // tm,),
        in_specs=in_specs,
        out_specs=pl.BlockSpec((tm, n), lambda i: (i, 0)),
        compiler_params=_cparams("parallel"),
    )(*args)


def rmsnorm_bwd(x, g, dy, out_dtype, *, name, width=None, col_blk=0, dres=None):
    m = x.shape[0]
    n = width or x.shape[1]
    tm = _row_tile(m)
    has_res = dres is not None

    def body(x_ref, g_ref, dy_ref, *rest):
        dx_ref, dg_ref = rest[-2:]
        i = pl.program_id(0)
        xf = x_ref[...].astype(F32)
        r = lax.rsqrt(jnp.mean(xf * xf, axis=-1, keepdims=True) + EPS)
        xhat = xf * r
        dyf = dy_ref[...].astype(F32)
        dxh = dyf * g_ref[...]
        dx = r * (dxh - xhat * jnp.mean(dxh * xhat, axis=-1, keepdims=True))
        if has_res:
            dx = dx + rest[0][...]
        dx_ref[...] = dx.astype(dx_ref.dtype)
        part = jnp.sum(dyf * xhat, axis=0, keepdims=True)

        @pl.when(i == 0)
        def _():
            dg_ref[...] = part

        @pl.when(i > 0)
        def _():
            dg_ref[...] += part

    in_specs = [
        pl.BlockSpec((tm, n), lambda i: (i, col_blk)),
        pl.BlockSpec((1, n), lambda i: (0, 0)),
        pl.BlockSpec((tm, n), lambda i: (i, 0)),
    ]
    args = [x, g.reshape(1, n), dy]
    if has_res:
        in_specs.append(pl.BlockSpec((tm, n), lambda i: (i, 0)))
        args.append(dres)
    return pl.pallas_call(
        body,
        name=name,
        out_shape=(jax.ShapeDtypeStruct((m, n), out_dtype), jax.ShapeDtypeStruct((1, n), F32)),
        grid=(m // tm,),
        in_specs=in_specs,
        out_specs=(pl.BlockSpec((tm, n), lambda i: (i, 0)), pl.BlockSpec((1, n), lambda i: (0, 0))),
        compiler_params=_cparams("arbitrary"),
    )(*args)


_NT = (((1,), (1,)), ((), ()))
_NN = (((1,), (0,)), ((), ()))
_TN = (((0,), (0,)), ((), ()))


def _attn_scores(q, k, scale, decay_refs, i, tq, kn, pad):
    s = lax.dot_general(q, k, _NT, preferred_element_type=F32) * scale
    if decay_refs is not None:
        cq_ref, ck_ref = decay_refs
        s = s + (cq_ref[0] - ck_ref[0][:, :kn])
    t_idx = i * tq + lax.broadcasted_iota(jnp.int32, (tq, 1), 0)
    s_idx = lax.broadcasted_iota(jnp.int32, (1, kn), 1)
    mask = (s_idx <= t_idx) & (s_idx >= pad)
    return s, mask, t_idx


def _keys_needed(i, tq, m):
    return min(m, -(-((i + 1) * tq) // LANES) * LANES)


def attn_fwd(q, k, v, *, heads, dk, dv, qblk0, kblk0, vblk0, scale, pad, slot, branches=None, decay=None, name):
    m = q.shape[0]
    tq = _tile(m, 544, 16)
    has_decay = decay is not None

    def body(q_ref, k_ref, v_ref, *rest):
        o_ref, lse_ref = rest[-2:]
        decay_refs = rest[:2] if has_decay else None

        def block(i):
            kn = _keys_needed(i, tq, m)
            s, mask, t_idx = _attn_scores(q_ref[...], k_ref[0:kn, :], scale, decay_refs, i, tq, kn, pad)
            s = jnp.where(mask, s, NEG_INF)
            mx = jnp.max(s, axis=1, keepdims=True)
            p = jnp.exp(s - mx)
            l = jnp.sum(p, axis=1, keepdims=True)
            o = lax.dot_general(p.astype(BF16), v_ref[0:kn, :], _NN, preferred_element_type=F32) / l
            o_ref[...] = jnp.where(t_idx >= pad, o, 0.0).astype(o_ref.dtype)
            lse_ref[0] = mx + jnp.log(l)

        for i in range(m // tq):
            pl.when(pl.program_id(1) == i)(functools.partial(block, i))

    in_specs = [
        pl.BlockSpec((tq, dk), lambda h, i: (i, qblk0 + h)),
        pl.BlockSpec((m, dk), lambda h, i: (0, kblk0 + h)),
        pl.BlockSpec((m, dv), lambda h, i: (0, vblk0 + h)),
    ]
    args = [q, k, v]
    if has_decay:
        in_specs += [pl.BlockSpec((1, tq, 1), lambda h, i: (h, i, 0)), pl.BlockSpec((1, 1, m), lambda h, i: (h, 0, 0))]
        args += list(decay)
    aliases = {}
    if branches is not None:
        aliases = {len(args): 0}
        in_specs.append(pl.BlockSpec(memory_space=pl.ANY))
        args.append(branches)
    return pl.pallas_call(
        body,
        name=name,
        out_shape=(jax.ShapeDtypeStruct((3, m, heads * dv), BF16), jax.ShapeDtypeStruct((heads, m, 1), F32)),
        grid=(heads, m // tq),
        in_specs=in_specs,
        out_specs=(pl.BlockSpec((None, tq, dv), lambda h, i: (slot, i, h)), pl.BlockSpec((1, tq, 1), lambda h, i: (h, i, 0))),
        input_output_aliases=aliases,
        compiler_params=_cparams("parallel", "parallel"),
    )(*args)


def attn_bwd(q, k, v, do, do_sel, lse, *, heads, dk, dv, qblk0, kblk0, vblk0, scale, pad, decay=None, name):
    m = q.shape[0]
    tq = _tile(m, 544, 16)
    nq = m // tq
    has_decay = decay is not None

    def body(q_ref, k_ref, v_ref, do_ref, lse_ref, *rest):
        if has_decay:
            cq_ref, ck_ref, dq_ref, dk_ref, dv_ref, dck_ref, dk_acc, dv_acc = rest
            decay_refs = (cq_ref, ck_ref)
        else:
            dq_ref, dk_ref, dv_ref, dk_acc, dv_acc = rest
            decay_refs = None
        @pl.when(pl.program_id(1) == 0)
        def _():
            dk_acc[...] = jnp.zeros_like(dk_acc)
            dv_acc[...] = jnp.zeros_like(dv_acc)
            if has_decay:
                dck_ref[...] = jnp.zeros_like(dck_ref)

        def block(i):
            kn = _keys_needed(i, tq, m)
            qb, kb, dob = q_ref[...], k_ref[0:kn, :], do_ref[...]
            s, mask, _ = _attn_scores(qb, kb, scale, decay_refs, i, tq, kn, pad)
            p = jnp.where(mask, jnp.exp(s - lse_ref[0]), 0.0)
            dp = lax.dot_general(dob, v_ref[0:kn, :], _NT, preferred_element_type=F32)
            ds = p * (dp - jnp.sum(p * dp, axis=1, keepdims=True))
            dsb = ds.astype(BF16)
            dq_ref[...] = (lax.dot_general(dsb, kb, _NN, preferred_element_type=F32) * scale).astype(dq_ref.dtype)
            dk_acc[0:kn, :] += lax.dot_general(dsb, qb, _TN, preferred_element_type=F32) * scale
            dv_acc[0:kn, :] += lax.dot_general(p.astype(BF16), dob, _TN, preferred_element_type=F32)
            if has_decay:
                dck_ref[0, :, 0:kn] -= jnp.sum(ds, axis=0, keepdims=True)

        for i in range(nq):
            pl.when(pl.program_id(1) == i)(functools.partial(block, i))

        @pl.when(pl.program_id(1) == nq - 1)
        def _():
            dk_ref[...] = dk_acc[...].astype(dk_ref.dtype)
            dv_ref[...] = dv_acc[...].astype(dv_ref.dtype)

    in_specs = [
        pl.BlockSpec((tq, dk), lambda h, i: (i, qblk0 + h)),
        pl.BlockSpec((m, dk), lambda h, i: (0, kblk0 + h)),
        pl.BlockSpec((m, dv), lambda h, i: (0, vblk0 + h)),
        pl.BlockSpec((None, tq, dv), lambda h, i: (do_sel, i, h)),
        pl.BlockSpec((1, tq, 1), lambda h, i: (h, i, 0)),
    ]
    args = [q, k, v, do, lse]
    out_shape = [
        jax.ShapeDtypeStruct((m, heads * dk), BF16),
        jax.ShapeDtypeStruct((m, heads * dk), BF16),
        jax.ShapeDtypeStruct((m, heads * dv), BF16),
    ]
    out_specs = [
        pl.BlockSpec((tq, dk), lambda h, i: (i, h)),
        pl.BlockSpec((m, dk), lambda h, i: (0, h)),
        pl.BlockSpec((m, dv), lambda h, i: (0, h)),
    ]
    if has_decay:
        in_specs += [pl.BlockSpec((1, tq, 1), lambda h, i: (h, i, 0)), pl.BlockSpec((1, 1, m), lambda h, i: (h, 0, 0))]
        args += list(decay)
        out_shape.append(jax.ShapeDtypeStruct((heads, 1, m), F32))
        out_specs.append(pl.BlockSpec((1, 1, m), lambda h, i: (h, 0, 0)))
    return pl.pallas_call(
        body,
        name=name,
        out_shape=tuple(out_shape),
        grid=(heads, nq),
        in_specs=in_specs,
        out_specs=tuple(out_specs),
        scratch_shapes=[pltpu.VMEM((m, dk), F32), pltpu.VMEM((m, dv), F32)],
        compiler_params=_cparams("parallel", "arbitrary"),
    )(*args)


def rope_tables(cfg):
    half = ROPE // 2
    inv_freq = 1.0 / (ROPE_THETA ** (jnp.arange(0, ROPE, 2, dtype=F32) / ROPE))
    pos = (jnp.arange(cfg.m, dtype=jnp.int32) - cfg.pad).astype(F32)
    ang = pos[:, None] * inv_freq[None, :]
    cos, sin = jnp.cos(ang), jnp.sin(ang)
    z = jnp.zeros((cfg.m, half), F32)
    zz = jnp.zeros((cfg.m, LANES - ROPE), F32)
    return (
        jnp.concatenate([cos, cos, zz], axis=1),
        jnp.concatenate([-sin, z, zz], axis=1),
        jnp.concatenate([z, sin, zz], axis=1),
    )


def _rope(x, cos, s1, s2):
    return x * cos + pltpu.roll(x, LANES - ROPE // 2, 1) * s1 + pltpu.roll(x, ROPE // 2, 1) * s2


def mla_prep_fwd(cfg, q, kv, proj, tabs, *, name):
    m, h2 = cfg.m, 2 * LANES
    tm = _tile(m, 544, 16)
    kpe_blk = cfg.off_kpe // LANES

    def body(q_ref, kn_ref, kpe_ref, cos_ref, s1_ref, s2_ref, qf_ref, kf_ref):
        cos, s1, s2 = cos_ref[...], s1_ref[...], s2_ref[...]
        qv = q_ref[...]
        qf_ref[:, :LANES] = qv[:, :LANES]
        qf_ref[:, LANES:] = _rope(qv[:, LANES:].astype(F32), cos, s1, s2).astype(qf_ref.dtype)
        kf_ref[:, :LANES] = kn_ref[...]
        kf_ref[:, LANES:] = _rope(kpe_ref[...].astype(F32), cos, s1, s2).astype(kf_ref.dtype)

    tab = pl.BlockSpec((tm, LANES), lambda i, h: (i, 0))
    return pl.pallas_call(
        body,
        name=name,
        out_shape=(jax.ShapeDtypeStruct((m, cfg.heads * h2), BF16), jax.ShapeDtypeStruct((m, cfg.heads * h2), BF16)),
        grid=(m // tm, cfg.heads),
        in_specs=[
            pl.BlockSpec((tm, h2), lambda i, h: (i, h)),
            pl.BlockSpec((tm, LANES), lambda i, h: (i, h)),
            pl.BlockSpec((tm, LANES), lambda i, h: (i, kpe_blk)),
            tab, tab, tab,
        ],
        out_specs=(pl.BlockSpec((tm, h2), lambda i, h: (i, h)), pl.BlockSpec((tm, h2), lambda i, h: (i, h))),
        compiler_params=_cparams("parallel", "parallel"),
    )(q, kv, proj, *tabs)


def mla_prep_bwd(cfg, dqf, dkf, tabs_t, *, name):
    m, h2 = cfg.m, 2 * LANES
    tm = _tile(m, 544, 16)

    def body(dqf_ref, dkf_ref, cos_ref, s1_ref, s2_ref, dq_ref, dkn_ref, dkpe_ref):
        h = pl.program_id(1)
        cos, s1, s2 = cos_ref[...], s1_ref[...], s2_ref[...]
        dqv, dkv = dqf_ref[...], dkf_ref[...]
        dq_ref[:, :LANES] = dqv[:, :LANES]
        dq_ref[:, LANES:] = _rope(dqv[:, LANES:].astype(F32), cos, s1, s2).astype(dq_ref.dtype)
        dkn_ref[...] = dkv[:, :LANES]
        part = _rope(dkv[:, LANES:].astype(F32), cos, s1, s2)

        @pl.when(h == 0)
        def _():
            dkpe_ref[...] = part

        @pl.when(h > 0)
        def _():
            dkpe_ref[...] += part

    tab = pl.BlockSpec((tm, LANES), lambda i, h: (i, 0))
    return pl.pallas_call(
        body,
        name=name,
        out_shape=(
            jax.ShapeDtypeStruct((m, cfg.heads * h2), BF16),
            jax.ShapeDtypeStruct((m, cfg.heads * LANES), BF16),
            jax.ShapeDtypeStruct((m, LANES), F32),
        ),
        grid=(m // tm, cfg.heads),
        in_specs=[pl.BlockSpec((tm, h2), lambda i, h: (i, h)), pl.BlockSpec((tm, h2), lambda i, h: (i, h)), tab, tab, tab],
        out_specs=(
            pl.BlockSpec((tm, h2), lambda i, h: (i, h)),
            pl.BlockSpec((tm, LANES), lambda i, h: (i, h)),
            pl.BlockSpec((tm, LANES), lambda i, h: (i, 0)),
        ),
        compiler_params=_cparams("parallel", "arbitrary"),
    )(dqf, dkf, *tabs_t)


def _conv_parts(b_ref, c_ref, x_ref, w_ref, m):
    b, c, x = b_ref[...].astype(F32), c_ref[...].astype(F32), x_ref[...].astype(F32)
    u = c * x
    row = lax.broadcasted_iota(jnp.int32, (m, 1), 0)
    u1 = jnp.where(row >= 1, pltpu.roll(u, 1, 0), 0.0)
    u2 = jnp.where(row >= 2, pltpu.roll(u, 2, 0), 0.0)
    w0, w1, w2 = w_ref[0:1, :], w_ref[1:2, :], w_ref[2:3, :]
    uc = w0 * u2 + w1 * u1 + w2 * u
    return b, c, x, u, u1, u2, uc, (w0, w1, w2), row


def _conv_specs(cfg, tn):
    m, nb, blk0 = cfg.m, cfg.width // tn, cfg.off_conv // tn
    return [
        pl.BlockSpec((m, tn), lambda j: (0, blk0 + j)),
        pl.BlockSpec((m, tn), lambda j: (0, blk0 + nb + j)),
        pl.BlockSpec((m, tn), lambda j: (0, blk0 + 2 * nb + j)),
        pl.BlockSpec((3, tn), lambda j: (0, j)),
    ]


def conv_fwd(cfg, proj, conv_w, branches, slot, *, name):
    m, tn = cfg.m, LANES

    def body(b_ref, c_ref, x_ref, w_ref, kept_ref, o_ref):
        b, _, _, _, _, _, uc, _, _ = _conv_parts(b_ref, c_ref, x_ref, w_ref, m)
        o_ref[...] = (b * uc).astype(o_ref.dtype)

    return pl.pallas_call(
        body,
        name=name,
        out_shape=jax.ShapeDtypeStruct((3, m, cfg.width), BF16),
        grid=(cfg.width // tn,),
        in_specs=_conv_specs(cfg, tn) + [pl.BlockSpec(memory_space=pl.ANY)],
        out_specs=pl.BlockSpec((None, m, tn), lambda j: (slot, 0, j)),
        input_output_aliases={4: 0},
        compiler_params=_cparams("parallel"),
    )(proj, proj, proj, conv_w, branches)


def conv_bwd(cfg, proj, conv_w, do, do_sel, *, name):
    m, tn = cfg.m, LANES

    def body(b_ref, c_ref, x_ref, w_ref, do_ref, db_ref, dc_ref, dx_ref, dw_ref):
        b, c, x, u, u1, u2, uc, (w0, w1, w2), row = _conv_parts(b_ref, c_ref, x_ref, w_ref, m)
        dob = do_ref[...].astype(F32)
        db_ref[...] = (dob * uc).astype(db_ref.dtype)
        duc = dob * b
        up1 = jnp.where(row <= m - 2, pltpu.roll(duc, m - 1, 0), 0.0)
        up2 = jnp.where(row <= m - 3, pltpu.roll(duc, m - 2, 0), 0.0)
        du = w2 * duc + w1 * up1 + w0 * up2
        dc_ref[...] = (du * x).astype(dc_ref.dtype)
        dx_ref[...] = (du * c).astype(dx_ref.dtype)
        dw_ref[0:1, :] = jnp.sum(duc * u2, axis=0, keepdims=True)
        dw_ref[1:2, :] = jnp.sum(duc * u1, axis=0, keepdims=True)
        dw_ref[2:3, :] = jnp.sum(duc * u, axis=0, keepdims=True)

    act = jax.ShapeDtypeStruct((m, cfg.width), BF16)
    blk = pl.BlockSpec((m, tn), lambda j: (0, j))
    return pl.pallas_call(
        body,
        name=name,
        out_shape=(act, act, act, jax.ShapeDtypeStruct((3, cfg.width), F32)),
        grid=(cfg.width // tn,),
        in_specs=_conv_specs(cfg, tn) + [pl.BlockSpec((None, m, tn), lambda j: (do_sel, 0, j))],
        out_specs=(blk, blk, blk, pl.BlockSpec((3, tn), lambda j: (0, j))),
        compiler_params=_cparams("parallel"),
    )(proj, proj, proj, conv_w, do)


def _tri(lower):
    r = lax.broadcasted_iota(jnp.int32, (LANES, LANES), 0)
    c = lax.broadcasted_iota(jnp.int32, (LANES, LANES), 1)
    return jnp.where((r >= c) if lower else (r <= c), 1.0, 0.0).astype(F32)


def fox_gate_fwd(cfg, fl, b_pad, *, name):
    m = cfg.m
    nblk = m // LANES

    def body(fl_ref, b_ref, c_ref):
        z = fl_ref[...] + b_ref[...]
        logf = jnp.minimum(z, 0.0) - jnp.log(1.0 + jnp.exp(-jnp.abs(z)))
        row = lax.broadcasted_iota(jnp.int32, (m, 1), 0)
        logf = jnp.where(row >= cfg.pad, logf, 0.0)
        tri = _tri(True)
        carry = jnp.zeros((1, LANES), F32)
        for blk in range(nblk):
            cb = jnp.dot(tri, logf[blk * LANES:(blk + 1) * LANES, :], precision=lax.Precision.HIGHEST,
                         preferred_element_type=F32) + carry
            c_ref[blk * LANES:(blk + 1) * LANES, :] = cb
            carry = cb[LANES - 1:LANES, :]

    full = pl.BlockSpec((m, LANES), lambda: (0, 0))
    return pl.pallas_call(
        body,
        name=name,
        out_shape=jax.ShapeDtypeStruct((m, LANES), F32),
        in_specs=[full, pl.BlockSpec((1, LANES), lambda: (0, 0))],
        out_specs=full,
        compiler_params=pltpu.CompilerParams(vmem_limit_bytes=VMEM_LIMIT_BYTES),
    )(fl, b_pad)


def fox_gate_bwd(cfg, fl, b_pad, dc, *, name):
    m = cfg.m
    nblk = m // LANES

    def body(fl_ref, b_ref, dc_ref, dfl_ref, db_ref):
        z = fl_ref[...] + b_ref[...]
        dlogsig = 1.0 / (1.0 + jnp.exp(z))
        row = lax.broadcasted_iota(jnp.int32, (m, 1), 0)
        gate = jnp.where(row >= cfg.pad, dlogsig, 0.0)
        dcv = dc_ref[...]
        tri = _tri(False)
        carry = jnp.zeros((1, LANES), F32)
        db = jnp.zeros((1, LANES), F32)
        for blk in reversed(range(nblk)):
            sl = slice(blk * LANES, (blk + 1) * LANES)
            rb = jnp.dot(tri, dcv[sl, :], precision=lax.Precision.HIGHEST, preferred_element_type=F32) + carry
            carry = rb[0:1, :]
            dfl = rb * gate[sl, :]
            dfl_ref[sl, :] = dfl
            db = db + jnp.sum(dfl, axis=0, keepdims=True)
        db_ref[...] = db

    full = pl.BlockSpec((m, LANES), lambda: (0, 0))
    one = pl.BlockSpec((1, LANES), lambda: (0, 0))
    return pl.pallas_call(
        body,
        name=name,
        out_shape=(jax.ShapeDtypeStruct((m, LANES), F32), jax.ShapeDtypeStruct((1, LANES), F32)),
        in_specs=[full, one, full],
        out_specs=(full, one),
        compiler_params=pltpu.CompilerParams(vmem_limit_bytes=VMEM_LIMIT_BYTES),
    )(fl, b_pad, dc)


def _sigmoid(x):
    return 1.0 / (1.0 + jnp.exp(-x))


def gate_merge_fwd(cfg, y, proj, *, name):
    m, d = cfg.m, cfg.d
    tm, tn = _tile(m, 1088, 16), _tile(d, 512)
    nd = d // tn

    def body(y_ref, g0_ref, g1_ref, g2_ref, o_ref):
        acc = None
        for n, g_ref in enumerate((g0_ref, g1_ref, g2_ref)):
            t = _sigmoid(g_ref[...].astype(F32)) * y_ref[n].astype(F32)
            acc = t if acc is None else acc + t
        o_ref[...] = acc.astype(o_ref.dtype)

    gate = lambda n: pl.BlockSpec((tm, tn), lambda i, j: (i, n * nd + j))
    return pl.pallas_call(
        body,
        name=name,
        out_shape=jax.ShapeDtypeStruct((m, d), BF16),
        grid=(m // tm, nd),
        in_specs=[pl.BlockSpec((3, tm, tn), lambda i, j: (0, i, j)), gate(0), gate(1), gate(2)],
        out_specs=pl.BlockSpec((tm, tn), lambda i, j: (i, j)),
        compiler_params=_cparams("parallel", "parallel"),
    )(y, proj, proj, proj)


def gate_merge_bwd(cfg, dm, y, proj, *, name):
    m, d = cfg.m, cfg.d
    tm, tn = _tile(m, 1088, 16), _tile(d, 512)
    nd = d // tn

    def body(dm_ref, y_ref, g_ref, dy_ref, dg_ref):
        sg = _sigmoid(g_ref[...].astype(F32))
        dmv = dm_ref[...].astype(F32)
        dy_ref[...] = (sg * dmv).astype(dy_ref.dtype)
        dg_ref[...] = (dmv * y_ref[...].astype(F32) * sg * (1.0 - sg)).astype(dg_ref.dtype)

    return pl.pallas_call(
        body,
        name=name,
        out_shape=(jax.ShapeDtypeStruct((3, m, d), BF16), jax.ShapeDtypeStruct((m, 3 * d), BF16)),
        grid=(m // tm, nd, 3),
        in_specs=[
            pl.BlockSpec((tm, tn), lambda i, j, n: (i, j)),
            pl.BlockSpec((None, tm, tn), lambda i, j, n: (n, i, j)),
            pl.BlockSpec((tm, tn), lambda i, j, n: (i, n * nd + j)),
        ],
        out_specs=(
            pl.BlockSpec((None, tm, tn), lambda i, j, n: (n, i, j)),
            pl.BlockSpec((tm, tn), lambda i, j, n: (i, n * nd + j)),
        ),
        compiler_params=_cparams("parallel", "parallel", "parallel"),
    )(dm, y, proj)


def swiglu_fwd(cfg, gu, *, name):
    m, f = cfg.m, cfg.d_ff
    tm, tn = _tile(m, 1088, 16), _tile(f, 512)
    nf = f // tn

    def body(g_ref, u_ref, o_ref):
        g = g_ref[...].astype(F32)
        o_ref[...] = (g * _sigmoid(g) * u_ref[...].astype(F32)).astype(o_ref.dtype)

    return pl.pallas_call(
        body,
        name=name,
        out_shape=jax.ShapeDtypeStruct((m, f), BF16),
        grid=(m // tm, nf),
        in_specs=[pl.BlockSpec((tm, tn), lambda i, j: (i, j)), pl.BlockSpec((tm, tn), lambda i, j: (i, nf + j))],
        out_specs=pl.BlockSpec((tm, tn), lambda i, j: (i, j)),
        compiler_params=_cparams("parallel", "parallel"),
    )(gu, gu)


def swiglu_bwd(cfg, dact, gu, *, name):
    m, f = cfg.m, cfg.d_ff
    tm, tn = _tile(m, 1088, 16), _tile(f, 512)
    nf = f // tn

    def body(da_ref, g_ref, u_ref, o_ref):
        j = pl.program_id(1)
        g, u, da = g_ref[...].astype(F32), u_ref[...].astype(F32), da_ref[...].astype(F32)
        sg = _sigmoid(g)
        dg = da * u * sg * (1.0 + g * (1.0 - sg))
        du = da * g * sg
        o_ref[...] = jnp.where(j < nf, dg, du).astype(o_ref.dtype)

    return pl.pallas_call(
        body,
        name=name,
        out_shape=jax.ShapeDtypeStruct((m, 2 * f), BF16),
        grid=(m // tm, 2 * nf),
        in_specs=[
            pl.BlockSpec((tm, tn), lambda i, j: (i, j % nf)),
            pl.BlockSpec((tm, tn), lambda i, j: (i, j % nf)),
            pl.BlockSpec((tm, tn), lambda i, j: (i, nf + j % nf)),
        ],
        out_specs=pl.BlockSpec((tm, tn), lambda i, j: (i, j)),
        compiler_params=_cparams("parallel", "parallel"),
    )(dact, gu, gu)


def loss_head(cfg, h, target, *, name):
    m, d = cfg.m, cfg.d
    assert cfg.pad + cfg.n_meta == LANES
    tm = LANES
    inv_d = 1.0 / d

    def body(h_ref, t_ref, dh_ref, loss_ref):
        i = pl.program_id(0)

        @pl.when(i == 0)
        def _():
            dh_ref[...] = jnp.zeros_like(dh_ref)
            loss_ref[...] = jnp.zeros_like(loss_ref)

        @pl.when(i > 0)
        def _():
            err = h_ref[...] - t_ref[...]
            dh_ref[...] = err * inv_d
            loss_ref[...] += 0.5 * inv_d * jnp.sum(err * err)

    return pl.pallas_call(
        body,
        name=name,
        out_shape=(jax.ShapeDtypeStruct((m, d), F32), jax.ShapeDtypeStruct((8, LANES), F32)),
        grid=(m // tm,),
        in_specs=[pl.BlockSpec((tm, d), lambda i: (i, 0)), pl.BlockSpec((tm, d), lambda i: (jnp.maximum(i - 1, 0), 0))],
        out_specs=(pl.BlockSpec((tm, d), lambda i: (i, 0)), pl.BlockSpec((8, LANES), lambda i: (0, 0))),
        compiler_params=_cparams("arbitrary"),
    )(h, target)


def adamw(w, g, m_, v_, *, name):
    r, c = w.shape
    c_pad = -(-c // LANES) * LANES
    tr = r
    if r % 8 == 0:
        tr = _tile(r, max(8, (3 << 19) // (4 * c_pad) // 8 * 8), 8)
    bc1 = 1.0 - ADAM_B1 ** ADAM_STEP
    bc2 = 1.0 - ADAM_B2 ** ADAM_STEP

    def body(w_ref, g_ref, m_ref, v_ref, d_ref, nm_ref, nv_ref):
        gv = g_ref[...]
        nm = ADAM_B1 * m_ref[...] + (1.0 - ADAM_B1) * gv
        nv = ADAM_B2 * v_ref[...] + (1.0 - ADAM_B2) * (gv * gv)
        d_ref[...] = -ADAM_LR * ((nm / bc1) / (jnp.sqrt(nv / bc2) + ADAM_EPS) + ADAM_WD * w_ref[...])
        nm_ref[...] = nm
        nv_ref[...] = nv

    blk = pl.BlockSpec((tr, c), lambda i: (i, 0))
    shp = jax.ShapeDtypeStruct((r, c), F32)
    return pl.pallas_call(
        body,
        name=name,
        out_shape=(shp, shp, shp),
        grid=(r // tr,),
        in_specs=[blk, blk, blk, blk],
        out_specs=(blk, blk, blk),
        compiler_params=_cparams("parallel"),
    )(w, g, m_, v_)


def adamw_halves(w, g_own, g_other, core, m_, v_, *, first_layer=0, prev=None, name):
    _, r, c = w.shape
    nl = g_own.shape[0]
    r2 = r // 2
    c_pad = -(-c // LANES) * LANES
    tr = _tile(r2, max(8, (3 << 19) // (4 * c_pad) // 8 * 8), 8)
    nr = r2 // tr
    bc1 = 1.0 - ADAM_B1 ** ADAM_STEP
    bc2 = 1.0 - ADAM_B2 ** ADAM_STEP

    def body(core_ref, w_ref, go_ref, gr_ref, m_ref, v_ref, *rest):
        g_ref, d_ref, nm_ref, nv_ref = rest[-4:]
        gv = jnp.where(pl.program_id(2) == core_ref[0], go_ref[...], gr_ref[...])[:, :c]
        nm = ADAM_B1 * m_ref[...] + (1.0 - ADAM_B1) * gv
        nv = ADAM_B2 * v_ref[...] + (1.0 - ADAM_B2) * (gv * gv)
        d_ref[...] = -ADAM_LR * ((nm / bc1) / (jnp.sqrt(nv / bc2) + ADAM_EPS) + ADAM_WD * w_ref[...])
        g_ref[...] = gv
        nm_ref[...] = nm
        nv_ref[...] = nv

    full = pl.BlockSpec((None, tr, c), lambda l, i, hf, core_ref: (first_layer + l, hf * nr + i, 0))
    half = pl.BlockSpec((None, tr, g_own.shape[2]), lambda l, i, hf, core_ref: (l, i, 0))
    shp = jax.ShapeDtypeStruct(w.shape, F32)
    kept = list(prev or ())
    return pl.pallas_call(
        body,
        name=name,
        out_shape=(shp, shp, shp, shp),
        grid_spec=pltpu.PrefetchScalarGridSpec(
            num_scalar_prefetch=1,
            grid=(nl, nr, 2),
            in_specs=[full, half, half, full, full] + [pl.BlockSpec(memory_space=pl.ANY)] * len(kept),
            out_specs=(full, full, full, full),
        ),
        input_output_aliases={6 + k: k for k in range(len(kept))},
        compiler_params=_cparams("parallel", "parallel", "arbitrary"),
    )(core, w, g_own, g_other, m_, v_, *kept)


_HBM = pl.BlockSpec(memory_space=pltpu.HBM)


def _place():
    x, y, c = lax.axis_index("x"), lax.axis_index("y"), lax.axis_index("c")
    flips = [(1 - x, y), (x, 1 - y), (1 - x, 1 - y)]
    return x, y, c, flips


_SEM = pl.BlockSpec(memory_space=pltpu.SEMAPHORE)
_EFFECT = pltpu.SideEffectType.DATAFLOW_SIDE_EFFECTING


def _gather_plan(halves):
    def plan(src_refs, land_refs, arrival):
        x, y, c, flips = _place()
        mine = 2 * x + y
        out = []
        for w, h in enumerate(halves):
            for fx, fy in flips:
                slot = (2 * fx + fy) if arrival else mine
                out.append((src_refs[w].at[pl.ds(c * h, h), :], land_refs[w].at[slot, pl.ds(c * h, h), :], (fx, fy, c)))
        return out
    return plan


def _exchange_plan(nw):
    def plan(src_refs, land_refs, arrival):
        _, _, c, flips = _place()
        return [(src_refs[w].at[2 * fx + fy], land_refs[w].at[k], (fx, fy, c)) for w in range(nw) for k, (fx, fy) in enumerate(flips)]
    return plan


def _swap_plan(halves):
    def plan(src_refs, land_refs, arrival):
        x, y, c, _ = _place()
        return [(src_refs[w].at[:, pl.ds((1 - c) * h, h), :], land_refs[w], (x, y, 1 - c)) for w, h in enumerate(halves)]
    return plan


def _share_plan(nw):
    def plan(src_refs, land_refs, arrival):
        x, y, c, _ = _place()
        return [(src_refs[w], land_refs[w], (x, y, 1 - c)) for w in range(nw)]
    return plan


def copies_start(srcs, land_shapes, plan, n_copies, *, name):
    lands = [lax.empty(s, a.dtype) for s, a in zip(land_shapes, srcs)]
    n_in = len(srcs) + len(lands)

    def body(*refs):
        src_refs, land_refs = refs[:len(srcs)], refs[len(srcs):n_in]
        send_sems, recv_sems, token = refs[n_in], refs[n_in + 1], refs[-1]
        for i, (src, dst, to) in enumerate(plan(src_refs, land_refs, False)):
            pltpu.make_async_remote_copy(src_ref=src, dst_ref=dst, send_sem=send_sems.at[i], recv_sem=recv_sems.at[i],
                                         device_id=to, device_id_type=MESH).start()
        token[...] = jnp.zeros_like(token)

    operands = list(srcs) + lands
    out = pl.pallas_call(
        body,
        name=name,
        out_shape=(pltpu.SemaphoreType.DMA((n_copies,)), pltpu.SemaphoreType.DMA((n_copies,)),
                   *[pltpu.HBM(a.shape, a.dtype) for a in operands], jax.ShapeDtypeStruct((8, LANES), F32)),
        in_specs=[_HBM] * n_in,
        out_specs=(_SEM, _SEM, *[_HBM] * n_in, pl.BlockSpec(memory_space=pltpu.VMEM)),
        input_output_aliases={i: 2 + i for i in range(n_in)},
        compiler_params=pltpu.CompilerParams(has_side_effects=_EFFECT),
    )(*[pltpu.with_memory_space_constraint(a, pltpu.HBM) for a in operands])
    return out[0], out[1], list(out[2:2 + len(srcs)]), list(out[2 + len(srcs):2 + n_in]), out[-1]


def copies_wait(send_sems, recv_sems, srcs, lands, plan, after, *, name):
    n_in = len(srcs) + len(lands)
    after = list(after) if isinstance(after, (list, tuple)) else [after]

    def body(*refs):
        src_refs, land_refs = refs[:len(srcs)], refs[len(srcs):n_in]
        send_ref, recv_ref, token = refs[n_in], refs[n_in + 1], refs[-1]
        token[...] = jnp.zeros_like(token)
        for i, (src, dst, to) in enumerate(plan(src_refs, land_refs, True)):
            copy = pltpu.make_async_remote_copy(src_ref=src, dst_ref=dst, send_sem=send_ref.at[i], recv_sem=recv_ref.at[i],
                                                device_id=to, device_id_type=MESH)
            copy.wait_send()
            copy.wait_recv()

    operands = list(srcs) + list(lands)
    out = pl.pallas_call(
        body,
        name=name,
        out_shape=(*[pltpu.HBM(a.shape, a.dtype) for a in operands], jax.ShapeDtypeStruct((8, LANES), F32)),
        in_specs=[_HBM] * n_in + [_SEM, _SEM] + [pl.BlockSpec(memory_space=pl.ANY)] * len(after),
        out_specs=(*[_HBM] * n_in, pl.BlockSpec(memory_space=pltpu.VMEM)),
        input_output_aliases={i: i for i in range(n_in)},
        compiler_params=pltpu.CompilerParams(has_side_effects=_EFFECT),
    )(*operands, send_sems, recv_sems, *after)
    return list(out[:len(srcs)]), list(out[len(srcs):n_in]), out[-1]


def forward_halves(lands, *, name):
    nw = len(lands)
    halves = [a.shape[1] // 2 for a in lands]

    def body(*refs):
        ins, outs = refs[:nw], refs[nw:2 * nw]
        send_sems, recv_sems = refs[2 * nw:]
        x, y, c, flips = _place()
        copies = []
        for w, h in enumerate(halves):
            for k, (fx, fy) in enumerate(flips):
                rows = (2 * fx + fy, pl.ds(c * h, h), slice(None))
                copies.append(pltpu.make_async_remote_copy(src_ref=ins[w].at[rows], dst_ref=outs[w].at[rows], send_sem=send_sems.at[3 * w + k],
                                                           recv_sem=recv_sems.at[3 * w + k], device_id=(x, y, 1 - c), device_id_type=MESH))
        for cp in copies:
            cp.start()
        for cp in copies:
            cp.wait()

    return pl.pallas_call(
        body,
        name=name,
        out_shape=tuple(jax.ShapeDtypeStruct(a.shape, a.dtype) for a in lands),
        in_specs=[_HBM] * nw,
        out_specs=tuple([_HBM] * nw),
        input_output_aliases={w: w for w in range(nw)},
        scratch_shapes=[pltpu.SemaphoreType.DMA((3 * nw,)), pltpu.SemaphoreType.DMA((3 * nw,))],
    )(*lands)


def share_halves(sums, *, name):
    nw = len(sums)

    def body(*refs):
        ins, outs = refs[:nw], refs[nw:2 * nw]
        send_sems, recv_sems = refs[2 * nw:]
        x, y, c, _ = _place()
        copies = [
            pltpu.make_async_remote_copy(src_ref=ins[w], dst_ref=outs[w], send_sem=send_sems.at[w], recv_sem=recv_sems.at[w],
                                         device_id=(x, y, 1 - c), device_id_type=MESH)
            for w in range(nw)
        ]
        for cp in copies:
            cp.start()
        for cp in copies:
            cp.wait()

    return pl.pallas_call(
        body,
        name=name,
        out_shape=tuple(jax.ShapeDtypeStruct(s.shape, s.dtype) for s in sums),
        in_specs=[_HBM] * nw,
        out_specs=tuple([_HBM] * nw),
        scratch_shapes=[pltpu.SemaphoreType.DMA((nw,)), pltpu.SemaphoreType.DMA((nw,))],
    )(*sums)


def gather_blocks(block, *, reduce, name):
    rows, cols = block.shape

    def body(x_ref, out_ref, *rest):
        if reduce:
            buf_ref, send_sems, recv_sems = rest
        else:
            send_sems, recv_sems = rest
            buf_ref = out_ref
        x, y, c, flips = _place()
        me, sibling = (x, y, c), (x, y, 1 - c)

        def slot(px, py, pc):
            return buf_ref.at[4 * px + 2 * py + pc]

        def copy(k, blk, to, src=None):
            return pltpu.make_async_remote_copy(src_ref=slot(*blk) if src is None else src, dst_ref=slot(*blk),
                                                send_sem=send_sems.at[k], recv_sem=recv_sems.at[k], device_id=to,
                                                device_id_type=MESH)

        buf_ref[4 * x + 2 * y + c] = x_ref[...]
        first = [copy(0, me, sibling, src=x_ref)]
        first += [copy(1 + j, me, (*chip, c), src=x_ref) for j, chip in enumerate(flips)]
        for cp in first:
            cp.start()
        passed = [copy(4 + j, (*chip, c), sibling) for j, chip in enumerate(flips)]
        for j, chip in enumerate(flips):
            copy(1 + j, (*chip, c), me).wait_recv()
            passed[j].start()
        copy(0, sibling, me).wait_recv()
        for j, chip in enumerate(flips):
            copy(4 + j, (*chip, 1 - c), me).wait_recv()
        for cp in first + passed:
            cp.wait_send()
        if reduce:
            acc = buf_ref[0]
            for dev in range(1, N_DEV):
                acc = acc + buf_ref[dev]
            out_ref[...] = acc

    vmem = pl.BlockSpec(memory_space=pltpu.VMEM)
    sems = [pltpu.SemaphoreType.DMA((7,)), pltpu.SemaphoreType.DMA((7,))]
    if reduce:
        out_shape = jax.ShapeDtypeStruct((rows, cols), block.dtype)
        scratch = [pltpu.VMEM((N_DEV, rows, cols), block.dtype)] + sems
    else:
        out_shape = jax.ShapeDtypeStruct((N_DEV, rows, cols), block.dtype)
        scratch = sems
    return pl.pallas_call(
        body,
        name=name,
        out_shape=out_shape,
        in_specs=[vmem],
        out_specs=vmem,
        scratch_shapes=scratch,
        compiler_params=pltpu.CompilerParams(vmem_limit_bytes=VMEM_LIMIT_BYTES),
    )(block)


def add_own_half(grad, recv, core, *, name):
    _, r2, cols = recv.shape
    tr = _tile(r2, max(16, (1 << 20) // (2 * cols) // 16 * 16), 16)
    nr = r2 // tr

    def body(core_ref, g_ref, r_ref, o_ref):
        o_ref[...] = (g_ref[...].astype(F32) + r_ref[...].astype(F32)).astype(o_ref.dtype)

    return pl.pallas_call(
        body,
        name=name,
        out_shape=jax.ShapeDtypeStruct(recv.shape, BF16),
        grid_spec=pltpu.PrefetchScalarGridSpec(
            num_scalar_prefetch=1,
            grid=(N_CHIPS, nr),
            in_specs=[
                pl.BlockSpec((None, tr, cols), lambda k, i, core_ref: (k, core_ref[0] * nr + i, 0)),
                pl.BlockSpec((None, tr, cols), lambda k, i, core_ref: (k, i, 0)),
            ],
            out_specs=pl.BlockSpec((None, tr, cols), lambda k, i, core_ref: (k, i, 0)),
        ),
        compiler_params=_cparams("parallel", "parallel"),
    )(core, grad, recv)


def sum_chips(part, recv, chip, sums, layer, *, name):
    _, r2, cols = part.shape
    tr = _tile(r2, max(16, (1 << 20) // (2 * cols) // 16 * 16), 16)

    def body(chip_ref, p_ref, r_ref, sums_ref, o_ref):
        acc = p_ref[...].astype(F32)
        for k in range(3):
            acc = acc + r_ref[k].astype(F32)
        o_ref[...] = acc

    return pl.pallas_call(
        body,
        name=name,
        out_shape=jax.ShapeDtypeStruct(sums.shape, F32),
        grid_spec=pltpu.PrefetchScalarGridSpec(
            num_scalar_prefetch=1,
            grid=(r2 // tr,),
            in_specs=[
                pl.BlockSpec((None, tr, cols), lambda i, chip_ref: (chip_ref[0], i, 0)),
                pl.BlockSpec((3, tr, cols), lambda i, chip_ref: (0, i, 0)),
                pl.BlockSpec(memory_space=pl.ANY),
            ],
            out_specs=pl.BlockSpec((None, tr, cols), lambda i, chip_ref: (layer, i, 0)),
        ),
        input_output_aliases={3: 0},
        compiler_params=_cparams("parallel"),
    )(chip, part, recv, sums)


WEIGHTS = ("w_in", "w_uq", "w_ukv", "w_branch", "w_out", "w_ffn_in", "w_ffn_out")


def layer_fwd(cfg, h, w, s, tabs, tag):
    m, d, hd = cfg.m, cfg.d, cfg.heads
    fox_blk = cfg.off_fox // LANES
    hn = rmsnorm_fwd(h, s["g_mix_pre"], BF16, name=f"norm_mix_pre{tag}")
    proj = matmul(hn, w["w_in"], "nn", BF16, tm=m, tn=_tile(cfg.d_inp, 512), tk=d, name=f"proj{tag}")
    fl = matmul(hn, w["w_in"][:, cfg.off_fl:cfg.off_fl + LANES], "nn", F32, tm=m, tn=LANES, tk=d, name=f"proj_forget{tag}")
    cqn = rmsnorm_fwd(proj, s["g_q_lat"], BF16, width=cfg.q_rank, col_blk=cfg.off_cq // cfg.q_rank, name=f"norm_q{tag}")
    ckvn = rmsnorm_fwd(proj, s["g_kv_lat"], BF16, width=cfg.kv_rank, col_blk=cfg.off_ckv // cfg.kv_rank, name=f"norm_kv{tag}")
    q = matmul(cqn, w["w_uq"], "nn", BF16, tm=m, tn=_tile(2 * cfg.width, 512), tk=cfg.q_rank, name=f"up_q{tag}")
    kv = matmul(ckvn, w["w_ukv"], "nn", BF16, tm=m, tn=_tile(2 * cfg.width, 512), tk=cfg.kv_rank, name=f"up_kv{tag}")
    qf, kf = mla_prep_fwd(cfg, q, kv, proj, tabs[0], name=f"mla_prep{tag}")
    o, lse_a = attn_fwd(qf, kf, kv, heads=hd, dk=2 * LANES, dv=LANES, qblk0=0, kblk0=0, vblk0=hd,
                        scale=(LANES + ROPE) ** -0.5, pad=cfg.pad, slot=0, name=f"mla_attn{tag}")
    o = conv_fwd(cfg, proj, s["conv_w"], o, 1, name=f"conv{tag}")
    b_pad = jnp.pad(s["b_forget"], (0, LANES - hd)).reshape(1, LANES)
    cum = fox_gate_fwd(cfg, fl, b_pad, name=f"fox_gate{tag}")
    cum_t = cum[:, :hd].T
    decay = (cum_t[:, :, None], cum_t[:, None, :])
    o, lse_c = attn_fwd(proj, proj, proj, heads=hd, dk=LANES, dv=LANES, qblk0=fox_blk, kblk0=fox_blk + hd, vblk0=fox_blk + 2 * hd,
                        scale=LANES ** -0.5, pad=cfg.pad, slot=2, branches=o, decay=decay, name=f"fox_attn{tag}")
    y = matmul(o, w["w_branch"], "nn", BF16, tm=m, tn=_tile(d, 512), tk=cfg.width, name=f"branch{tag}")
    merged = gate_merge_fwd(cfg, y, proj, name=f"merge{tag}")
    mix = matmul(merged, w["w_out"], "nn", F32, tm=m, tn=_tile(d, 256), tk=d, name=f"out_proj{tag}")
    h_mid = rmsnorm_fwd(mix, s["g_mix_post"], F32, res=h, name=f"norm_mix_post{tag}")
    if hasattr(w, "land_ffn"):
        s = w.land_ffn(h_mid, s)
    hn2 = rmsnorm_fwd(h_mid, s["g_ffn_pre"], BF16, name=f"norm_ffn_pre{tag}")
    gu = matmul(hn2, w["w_ffn_in"], "nn", BF16, tm=m, tn=_tile(2 * cfg.d_ff, 512), tk=d, name=f"ffn_in{tag}")
    act = swiglu_fwd(cfg, gu, name=f"swiglu{tag}")
    f = matmul(act, w["w_ffn_out"], "nn", F32, tm=m, tn=_tile(d, 512), tk=_tile(cfg.d_ff, 2816), name=f"ffn_out{tag}")
    h_next = rmsnorm_fwd(f, s["g_ffn_post"], F32, res=h_mid, name=f"norm_ffn_post{tag}")
    saved = dict(h=h, hn=hn, proj=proj, fl=fl, cqn=cqn, ckvn=ckvn, kv=kv, qf=qf, kf=kf, lse_a=lse_a,
                 b_pad=b_pad, decay=decay, lse_c=lse_c, o=o, y=y, merged=merged, mix=mix, h_mid=h_mid,
                 hn2=hn2, gu=gu, act=act, f=f)
    return h_next, s, saved


def layer_bwd(cfg, dh, w, s, r, tabs, tag, grads_done):
    m, d, hd = cfg.m, cfg.d, cfg.heads
    fox_blk = cfg.off_fox // LANES
    tk_m = m
    df, dg4 = rmsnorm_bwd(r["f"], s["g_ffn_post"], dh, BF16, name=f"norm_ffn_post_bwd{tag}")
    dact = matmul(df, w["w_ffn_out"], "nt", BF16, tm=m, tn=_tile(cfg.d_ff, 512), tk=d, name=f"ffn_out_dx{tag}")
    dw_fo = matmul(r["act"], df, "tn", BF16, tm=_tile(cfg.d_ff, 512), tn=_tile(d, 1024), tk=tk_m, name=f"ffn_out_dw{tag}")
    dgu = swiglu_bwd(cfg, dact, r["gu"], name=f"swiglu_bwd{tag}")
    dhn2 = matmul(dgu, w["w_ffn_in"], "nt", F32, tm=m, tn=_tile(d, 512), tk=_tile(2 * cfg.d_ff, 2816), name=f"ffn_in_dx{tag}")
    dw_fi = matmul(r["hn2"], dgu, "tn", BF16, tm=_tile(d, 1024), tn=_tile(2 * cfg.d_ff // N_CHIPS, 1408), tk=tk_m, chip_cols=True,
                   name=f"ffn_in_dw{tag}")
    token = grads_done(dict(w_ffn_in=dw_fi, w_ffn_out=dw_fo))
    if token is not None:
        s = {**s, "g_ffn_pre": s["g_ffn_pre"] + token[0, 0]}
    dh_mid, dg3 = rmsnorm_bwd(r["h_mid"], s["g_ffn_pre"], dhn2, F32, dres=dh, name=f"norm_ffn_pre_bwd{tag}")
    dmix, dg2 = rmsnorm_bwd(r["mix"], s["g_mix_post"], dh_mid, BF16, name=f"norm_mix_post_bwd{tag}")
    dmerged = matmul(dmix, w["w_out"], "nt", BF16, tm=m, tn=_tile(d, 512), tk=d, name=f"out_proj_dx{tag}")
    dw_out = matmul(r["merged"], dmix, "tn", BF16, tm=_tile(d, 1024), tn=_tile(d, 512), tk=tk_m, name=f"out_proj_dw{tag}")
    dy, dgl = gate_merge_bwd(cfg, dmerged, r["y"], r["proj"], name=f"merge_bwd{tag}")
    do = matmul(dy, w["w_branch"], "nt", BF16, tm=m, tn=_tile(cfg.width, 512), tk=d, name=f"branch_dx{tag}")
    dw_br = matmul(r["o"], dy, "tn", BF16, tm=_tile(cfg.width, 1024), tn=_tile(d // N_CHIPS, 512), tk=tk_m, chip_cols=True,
                   name=f"branch_dw{tag}")
    dqf, dkf, dv_a = attn_bwd(r["qf"], r["kf"], r["kv"], do, 0, r["lse_a"], heads=hd, dk=2 * LANES, dv=LANES,
                              qblk0=0, kblk0=0, vblk0=hd, scale=(LANES + ROPE) ** -0.5, pad=cfg.pad, name=f"mla_attn_bwd{tag}")
    dq, dkn, dkpe = mla_prep_bwd(cfg, dqf, dkf, tabs[1], name=f"mla_prep_bwd{tag}")
    dkv = jnp.concatenate([dkn, dv_a], axis=1)
    dcqn = matmul(dq, w["w_uq"], "nt", F32, tm=m, tn=cfg.q_rank, tk=2 * cfg.width, name=f"up_q_dx{tag}")
    dw_uq = matmul(r["cqn"], dq, "tn", BF16, tm=cfg.q_rank, tn=_tile(2 * cfg.width, 512), tk=tk_m, name=f"up_q_dw{tag}")
    dckvn = matmul(dkv, w["w_ukv"], "nt", F32, tm=m, tn=cfg.kv_rank, tk=2 * cfg.width, name=f"up_kv_dx{tag}")
    dw_ukv = matmul(r["ckvn"], dkv, "tn", BF16, tm=cfg.kv_rank, tn=_tile(2 * cfg.width, 512), tk=tk_m, name=f"up_kv_dw{tag}")
    dcq, dgq = rmsnorm_bwd(r["proj"], s["g_q_lat"], dcqn, BF16, width=cfg.q_rank, col_blk=cfg.off_cq // cfg.q_rank,
                           name=f"norm_q_bwd{tag}")
    dckv, dgkv = rmsnorm_bwd(r["proj"], s["g_kv_lat"], dckvn, BF16, width=cfg.kv_rank, col_blk=cfg.off_ckv // cfg.kv_rank,
                             name=f"norm_kv_bwd{tag}")
    dcb, dcc, dcx, dconv_w = conv_bwd(cfg, r["proj"], s["conv_w"], do, 1, name=f"conv_bwd{tag}")
    dfq, dfk, dfv, dck = attn_bwd(r["proj"], r["proj"], r["proj"], do, 2, r["lse_c"], heads=hd, dk=LANES, dv=LANES,
                                  qblk0=fox_blk, kblk0=fox_blk + hd, vblk0=fox_blk + 2 * hd, scale=LANES ** -0.5,
                                  pad=cfg.pad, decay=r["decay"], name=f"fox_attn_bwd{tag}")
    dc = jnp.pad(dck[:, 0, :].T, ((0, 0), (0, LANES - hd)))
    dfl, dbf = fox_gate_bwd(cfg, r["fl"], r["b_pad"], dc, name=f"fox_gate_bwd{tag}")
    tail = jnp.zeros((m, cfg.d_inp - cfg.off_fl - LANES), BF16)
    dproj = jnp.concatenate([dgl, dcq, dckv, dcb, dcc, dcx, dfq, dfk, dfv, dkpe.astype(BF16), dfl.astype(BF16), tail], axis=1)
    dhn = matmul(dproj, w["w_in"], "nt", F32, tm=m, tn=_tile(d, 512), tk=_tile(cfg.d_inp, 2304), name=f"proj_dx{tag}")
    dw_in = matmul(r["hn"], dproj, "tn", BF16, tm=_tile(d, 1024), tn=_tile(cfg.d_inp, 512), tk=tk_m, name=f"proj_dw{tag}")
    dh_in, dg1 = rmsnorm_bwd(r["h"], s["g_mix_pre"], dhn, F32, dres=dh_mid, name=f"norm_mix_pre_bwd{tag}")
    token = grads_done(dict(w_in=dw_in, w_uq=dw_uq, w_ukv=dw_ukv, w_branch=dw_br, w_out=dw_out))
    dsmall = dict(g_mix_pre=dg1[0], g_mix_post=dg2[0], g_ffn_pre=dg3[0], g_ffn_post=dg4[0], g_q_lat=dgq[0], g_kv_lat=dgkv[0],
                  b_forget=dbf[0, :hd], conv_w=dconv_w)
    return dh_in, dsmall, token


def local_step(cfg, x, target, meta, layer_params, grads_done):
    h = jnp.concatenate([jnp.zeros((cfg.pad, cfg.d), F32), meta, x], axis=0)
    cos, s1, s2 = rope_tables(cfg)
    tabs = ((cos, s1, s2), (cos, -s1, -s2))
    saved = []
    for l in range(cfg.depth):
        w, s = layer_params(l, h)
        h, s, r = layer_fwd(cfg, h, w, s, tabs, f"_{l}")
        saved.append((w, s, r))
    dh, loss = loss_head(cfg, h, target, name="loss_head")
    dsmalls, token = [None] * cfg.depth, None
    for l in reversed(range(cfg.depth)):
        w, s, r = saved[l]
        if token is not None:
            s = {**s, "g_ffn_post": s["g_ffn_post"] + token[0, 0]}
        dh, dsmalls[l], token = layer_bwd(cfg, dh, w, s, r, tabs, f"_{l}", functools.partial(grads_done, l))
    first = cfg.pad + cfg.n_meta
    return loss, dh[first:], dh[cfg.pad:first], dsmalls


def _cols_from_chips(g):
    return jnp.transpose(g, (1, 0, 2)).reshape(g.shape[1], N_CHIPS * g.shape[2])


def _cols_to_chips(w):
    r, c = w.shape
    return jnp.transpose(w.reshape(r, N_CHIPS, c // N_CHIPS), (1, 0, 2))


def _packed_segments(cfg):
    nat = [0] + _cumsum(cfg.nat_splits)
    w = cfg.width
    order = [(10, 0), (0, cfg.off_cq), (1, cfg.off_ckv), (3, cfg.off_conv), (4, cfg.off_conv + w), (5, cfg.off_conv + 2 * w),
             (6, cfg.off_fox), (7, cfg.off_fox + w), (8, cfg.off_fox + 2 * w), (2, cfg.off_kpe), (9, cfg.off_fl)]
    return [(pk, nat[i], cfg.nat_splits[i]) for i, pk in order]


def _chip_cols(cfg):
    n = cfg.d_in // N_CHIPS
    return n, -(-n // LANES) * LANES


def _lane_pieces(cfg, to_packed):
    n, n_pad = _chip_cols(cfg)
    tiles = [[] for _ in range(cfg.d_inp // LANES if to_packed else N_CHIPS * n_pad // LANES)]
    for pk, nat, width in _packed_segments(cfg):
        g = nat
        while g < nat + width:
            k, a = divmod(g, n)
            dst = (pk + g - nat) if to_packed else (k * n_pad + a)
            run = min(nat + width - g, n - a, LANES - dst % LANES)
            src = (k, a) if to_packed else (0, pk + g - nat)
            tiles[dst // LANES].append((dst % LANES, run, *src))
            g += run
    return tiles


def _fill_tiles(pieces, read, write, rows):
    lane = lax.broadcasted_iota(jnp.int32, (rows, LANES), 1)
    for t, parts in enumerate(pieces):
        tile = jnp.zeros((rows, LANES), F32)
        for dl, run, blk, col in parts:
            w0 = col // LANES * LANES
            off = col - w0
            span = LANES if off + run <= LANES else 2 * LANES
            win = read(blk, w0, span)
            shift = (dl - off) % span
            if shift:
                win = pltpu.roll(win, shift, 1)
            win = win[:, :LANES]
            tile = win if (dl == 0 and run == LANES) else jnp.where((lane >= dl) & (lane < dl + run), win, tile)
        write(t, tile)


def pack_w_in_blocks(cfg, lands, own, chip, *, name):
    d = cfg.d
    n, n_pad = _chip_cols(cfg)
    tr = _tile(d, 256, 16)
    pieces = _lane_pieces(cfg, True)

    def body(chip_ref, land_ref, own_ref, o_ref):
        def read(k, w0, span):
            theirs = land_ref[k, :, w0:w0 + span]
            return jnp.where(chip_ref[0] == k, own_ref[:, w0:w0 + span], theirs).astype(F32)

        def write(t, tile):
            o_ref[:, t * LANES:(t + 1) * LANES] = tile.astype(o_ref.dtype)

        _fill_tiles(pieces, read, write, tr)

    return pl.pallas_call(
        body,
        name=name,
        out_shape=jax.ShapeDtypeStruct((d, cfg.d_inp), BF16),
        grid_spec=pltpu.PrefetchScalarGridSpec(
            num_scalar_prefetch=1,
            grid=(d // tr,),
            in_specs=[pl.BlockSpec((N_CHIPS, tr, n_pad), lambda i, chip_ref: (0, i, 0)),
                      pl.BlockSpec((tr, n_pad), lambda i, chip_ref: (i, 0))],
            out_specs=pl.BlockSpec((tr, cfg.d_inp), lambda i, chip_ref: (i, 0)),
        ),
        compiler_params=_cparams("parallel"),
    )(chip, lands, own)


def unpack_w_in_blocks(cfg, dw, *, name):
    d = cfg.d
    n, n_pad = _chip_cols(cfg)
    tr = _tile(d, 256, 16)
    per_blk = n_pad // LANES
    pieces = _lane_pieces(cfg, False)

    def body(dw_ref, o_ref):
        def read(_, w0, span):
            return dw_ref[:, w0:w0 + span].astype(F32)

        def write(t, tile):
            k, i = divmod(t, per_blk)
            o_ref[k, :, i * LANES:(i + 1) * LANES] = tile.astype(o_ref.dtype)

        _fill_tiles(pieces, read, write, tr)

    return pl.pallas_call(
        body,
        name=name,
        out_shape=jax.ShapeDtypeStruct((N_CHIPS, d, n_pad), BF16),
        grid=(d // tr,),
        in_specs=[pl.BlockSpec((tr, cfg.d_inp), lambda i: (i, 0))],
        out_specs=pl.BlockSpec((N_CHIPS, tr, n_pad), lambda i: (0, i, 0)),
        compiler_params=_cparams("parallel"),
    )(dw)


def full_weights(cfg, g, w_in=None):
    make = dict(
        w_uq=lambda a: pack_w_uq(cfg, _cols_from_chips(a)),
        w_ukv=lambda a: pack_w_ukv(cfg, _cols_from_chips(a)),
        w_branch=lambda a: _cols_from_chips(a).reshape(3, cfg.width, cfg.d),
        w_out=lambda a: a.reshape(cfg.d, cfg.d),
        w_ffn_in=_cols_from_chips,
        w_ffn_out=lambda a: a.reshape(cfg.d_ff, cfg.d),
    )
    out = {n: make[n](a) for n, a in g.items()}
    if w_in is not None:
        out["w_in"] = w_in
    return out


def chip_grads(cfg, dw, tag):
    make = dict(
        w_in=lambda a: unpack_w_in_blocks(cfg, a, name=f"unpack_w_in{tag}"),
        w_uq=lambda a: _cols_to_chips(unpack_w_uq(cfg, a)),
        w_ukv=lambda a: _cols_to_chips(unpack_w_ukv(cfg, a)),
        w_branch=lambda a: a.reshape(N_CHIPS, 3 * cfg.width, cfg.d // N_CHIPS),
        w_out=lambda a: a.reshape(N_CHIPS, cfg.d // N_CHIPS, cfg.d),
        w_ffn_in=lambda a: a,
        w_ffn_out=lambda a: a.reshape(N_CHIPS, cfg.d_ff // N_CHIPS, cfg.d),
    )
    return {n: make[n](a) for n, a in dw.items()}


def _small_rows(cfg):
    return dict(g_mix_pre=cfg.d // LANES, g_mix_post=cfg.d // LANES, g_ffn_pre=cfg.d // LANES, g_ffn_post=cfg.d // LANES,
                g_q_lat=cfg.q_rank // LANES, g_kv_lat=cfg.kv_rank // LANES, b_forget=1, conv_w=3 * cfg.width // LANES)


def pack_small(cfg, loss, dmeta, dsmalls):
    parts = [loss[0:1, :], dmeta.reshape(-1, LANES)]
    for ds in dsmalls:
        for k in _small_rows(cfg):
            v = ds[k]
            if k == "b_forget":
                v = jnp.pad(v, (0, LANES - cfg.heads))
            parts.append(v.reshape(-1, LANES))
    rows = sum(p.shape[0] for p in parts)
    parts.append(jnp.zeros((-rows % 8, LANES), F32))
    return jnp.concatenate(parts, axis=0)


def unpack_small(cfg, block):
    loss = block[0, 0]
    n = cfg.n_meta * cfg.d // LANES
    dmeta = block[1:1 + n].reshape(cfg.n_meta, cfg.d)
    at = 1 + n
    out = []
    for _ in range(cfg.depth):
        ds = {}
        for k, rows in _small_rows(cfg).items():
            v = block[at:at + rows]
            at += rows
            if k == "b_forget":
                v = v[0, :cfg.heads]
            elif k == "conv_w":
                v = v.reshape(3, cfg.width)
            else:
                v = v.reshape(-1)
            ds[k] = v
        out.append(ds)
    return loss, dmeta, out


def kernel(x, meta, w_in, b_forget, g_q_lat, g_kv_lat, w_uq, w_ukv, conv_w, w_branch, w_out, w_ffn_in, w_ffn_out, g_mix_pre, g_mix_post, g_ffn_pre, g_ffn_post, loss_target, m_meta, m_w_in, m_b_forget, m_g_q_lat, m_g_kv_lat, m_w_uq, m_w_ukv, m_conv_w, m_w_branch, m_w_out, m_w_ffn_in, m_w_ffn_out, m_g_mix_pre, m_g_mix_post, m_g_ffn_pre, m_g_ffn_post, v_meta, v_w_in, v_b_forget, v_g_q_lat, v_g_kv_lat, v_w_uq, v_w_ukv, v_conv_w, v_w_branch, v_w_out, v_w_ffn_in, v_w_ffn_out, v_g_mix_pre, v_g_mix_post, v_g_ffn_pre, v_g_ffn_post):
    cfg = CFG
    names = ("meta", "w_in", "b_forget", "g_q_lat", "g_kv_lat", "w_uq", "w_ukv", "conv_w", "w_branch", "w_out", "w_ffn_in",
             "w_ffn_out", "g_mix_pre", "g_mix_post", "g_ffn_pre", "g_ffn_post")
    params = dict(zip(names, (meta, w_in, b_forget, g_q_lat, g_kv_lat, w_uq, w_ukv, conv_w, w_branch, w_out, w_ffn_in, w_ffn_out,
                              g_mix_pre, g_mix_post, g_ffn_pre, g_ffn_post)))
    mom1 = dict(zip(names, (m_meta, m_w_in, m_b_forget, m_g_q_lat, m_g_kv_lat, m_w_uq, m_w_ukv, m_conv_w, m_w_branch, m_w_out,
                            m_w_ffn_in, m_w_ffn_out, m_g_mix_pre, m_g_mix_post, m_g_ffn_pre, m_g_ffn_post)))
    mom2 = dict(zip(names, (v_meta, v_w_in, v_b_forget, v_g_q_lat, v_g_kv_lat, v_w_uq, v_w_ukv, v_conv_w, v_w_branch, v_w_out,
                            v_w_ffn_in, v_w_ffn_out, v_g_mix_pre, v_g_mix_post, v_g_ffn_pre, v_g_ffn_post)))
    xi, yi, ci = lax.axis_index("x"), lax.axis_index("y"), lax.axis_index("c")
    chip = 2 * xi + yi
    chip_arr = jnp.reshape(chip, (1,)).astype(jnp.int32)
    core_arr = jnp.reshape(ci, (1,)).astype(jnp.int32)

    def shard2d(name, l, after=None):
        w = params[name][l]
        if after is not None:
            w = w + after
        w = w.reshape(-1, w.shape[-1]).astype(BF16)
        if name == "w_in":
            w = jnp.pad(w, ((0, 0), (0, _chip_cols(cfg)[1] - w.shape[1])))
        return w

    is_mine = (jnp.arange(N_CHIPS) == chip)[:, None, None]
    shard_shape = {n: shard2d(n, 0).shape for n in WEIGHTS}
    groups = (("w_in", "w_uq", "w_ukv", "w_branch", "w_out"), ("w_ffn_in", "w_ffn_out"))
    gather_plans = [_gather_plan([shard_shape[n][0] // 2 for n in g]) for g in groups]

    def gather_start(l, after):
        started = []
        for gi, g in enumerate(groups):
            started.append(copies_start([shard2d(n, l, after) for n in g], [(N_CHIPS,) + shard_shape[n] for n in g],
                                        gather_plans[gi], 3 * len(g), name=f"gather_start_{gi}_{l}"))
            after = started[-1][4][0, 0]
        return started

    def land(l, gi, started, after):
        send_sems, recv_sems, own, lands, _ = started
        own, lands, landed = copies_wait(send_sems, recv_sems, own, lands, gather_plans[gi], after, name=f"gather_wait_{gi}_{l}")
        return own, forward_halves(lands, name=f"forward_halves_{gi}_{l}"), landed

    meta_all = gather_blocks(meta, reduce=False, name="gather_meta")[0::2]
    meta_full = jnp.transpose(meta_all, (1, 0, 2)).reshape(cfg.n_meta, cfg.d)
    conv_rows = conv_w.reshape(cfg.depth * 3, cfg.width // N_CHIPS)
    conv_all = gather_blocks(conv_rows, reduce=False, name="gather_conv_w")[0::2]
    conv_full = jnp.transpose(conv_all, (1, 0, 2)).reshape(cfg.depth, 3, cfg.width)

    in_flight = {0: gather_start(0, None)}
    under_way = in_flight[0][1][4][0, 0]
    three_d = lambda a, n: a.reshape(cfg.depth, -1, params[n].shape[-1])
    adam_in = {n: (three_d(params[n], n), three_d(mom1[n], n), three_d(mom2[n], n)) for n in WEIGHTS}
    adam_in["w_in"] = tuple(a + under_way for a in adam_in["w_in"])

    class LayerWeights(dict):
        def __init__(self, l, h):
            own, lands, _ = land(l, 0, in_flight[l][0], [h, *adam_in["w_in"]] if l == 0 else h)
            got = {n: jnp.where(is_mine, o[None], g) for n, o, g in zip(groups[0][1:], own[1:], lands[1:])}
            super().__init__(full_weights(cfg, got, pack_w_in_blocks(cfg, lands[0], own[0], chip_arr, name=f"pack_w_in_{l}")))
            self.layer = l

        def land_ffn(self, after, s):
            l = self.layer
            own, lands, landed = land(l, 1, in_flight.pop(l)[1], after)
            self.update(full_weights(cfg, {n: jnp.where(is_mine, o[None], g) for n, o, g in zip(groups[1], own, lands)}))
            if l + 1 == cfg.depth:
                return s
            in_flight[l + 1] = gather_start(l + 1, landed[0, 0])
            return {**s, "g_ffn_pre": s["g_ffn_pre"] + in_flight[l + 1][1][4][0, 0]}

    def layer_params(l, h):
        s = dict(g_mix_pre=g_mix_pre[l], g_mix_post=g_mix_post[l], g_ffn_pre=g_ffn_pre[l], g_ffn_post=g_ffn_post[l],
                 g_q_lat=g_q_lat[l], g_kv_lat=g_kv_lat[l], b_forget=b_forget[l], conv_w=conv_full[l])
        s["g_mix_pre"] = s["g_mix_pre"] + in_flight[l][1][4][0, 0]
        return LayerWeights(l, h), s

    half_shape = {n: (shard_shape[n][0] // 2, shard_shape[n][1]) for n in WEIGHTS}
    sums_upper = {n: jnp.zeros((cfg.depth - 1,) + half_shape[n], F32) for n in WEIGHTS}
    sums_first = {n: jnp.zeros((1,) + half_shape[n], F32) for n in WEIGHTS}
    swapping, exchanging = [], []

    def finish_exchange(after):
        l, names_, (send_sems, recv_sems, parts, lands, _) = exchanging.pop(0)
        parts, others, _ = copies_wait(send_sems, recv_sems, parts, lands, _exchange_plan(len(names_)), after,
                                       name=f"exchange_wait_{names_[0]}_{l}")
        for n, p, o in zip(names_, parts, others):
            if l == 0:
                sums_first[n] = sum_chips(p, o, chip_arr, sums_first[n], 0, name=f"sum_chips_{n}_{l}")
            else:
                sums_upper[n] = sum_chips(p, o, chip_arr, sums_upper[n], l - 1, name=f"sum_chips_{n}_{l}")

    def finish_swap(after):
        l, names_, (send_sems, recv_sems, mine, lands, _) = swapping.pop(0)
        plan = _swap_plan([g.shape[1] // 2 for g in mine])
        mine, theirs, _ = copies_wait(send_sems, recv_sems, mine, lands, plan, after, name=f"swap_wait_{names_[0]}_{l}")
        parts = [add_own_half(g, t, core_arr, name=f"add_own_half_{n}_{l}") for n, g, t in zip(names_, mine, theirs)]
        started = copies_start(parts, [(3,) + p.shape[1:] for p in parts], _exchange_plan(len(names_)), 3 * len(names_),
                               name=f"exchange_start_{names_[0]}_{l}")
        exchanging.append((l, names_, started))
        return started[4]

    def grads_done(l, dws):
        names_ = [n for n in WEIGHTS if n in dws]
        send = chip_grads(cfg, dws, f"_{l}")
        mine = [send[n] for n in names_]
        halves = [g.shape[1] // 2 for g in mine]
        started = copies_start(mine, [(N_CHIPS, h, g.shape[2]) for g, h in zip(mine, halves)], _swap_plan(halves), len(mine),
                               name=f"swap_start_{names_[0]}_{l}")
        token = started[4]
        if swapping:
            token = finish_swap(token)
            if len(exchanging) > 1:
                finish_exchange(token)
        swapping.append((l, names_, started))
        return token

    loss, grad_x, dmeta, dsmalls = local_step(cfg, x[0], loss_target[0], meta_full, layer_params, grads_done)
    share_plan = _share_plan(len(WEIGHTS))
    sharing = copies_start([sums_upper[n] for n in WEIGHTS], [sums_upper[n].shape for n in WEIGHTS], share_plan, len(WEIGHTS),
                           name="share_start_upper")
    last = finish_swap(sharing[4])
    while exchanging[0][0] > 0:
        finish_exchange(last)

    def update(own, other, first_layer, prev, tag):
        out = {}
        for n, mine_, theirs_ in zip(WEIGHTS, own, other):
            w3, m3, v3 = adam_in[n]
            out[n] = adamw_halves(w3, mine_, theirs_, core_arr, m3, v3, first_layer=first_layer, prev=prev and prev[n],
                                  name=f"adamw_{tag}_{n}")
        return out

    own_upper, other_upper, _ = copies_wait(sharing[0], sharing[1], sharing[2], sharing[3], share_plan, last, name="share_wait_upper")
    upper = update(own_upper, other_upper, 1, None, "upper")
    busy = sum(upper[n][1][1, 0, :LANES] for n in WEIGHTS)
    while exchanging:
        finish_exchange(busy)
    own_first = [sums_first[n] for n in WEIGHTS]
    done = update(own_first, share_halves(own_first, name="share_halves_first"), 0, upper, "first")
    grad, delta, new_m, new_v = ({n: done[n][k].reshape(params[n].shape) for n in WEIGHTS} for k in range(4))

    total = gather_blocks(pack_small(cfg, loss, dmeta, dsmalls), reduce=True, name="reduce_small")
    loss_sum, dmeta_sum, dsmall_sum = unpack_small(cfg, total)
    for k in _small_rows(cfg):
        grad[k] = jnp.stack([ds[k] for ds in dsmall_sum])
    grad["conv_w"] = lax.dynamic_slice_in_dim(grad["conv_w"], chip * (cfg.width // N_CHIPS), cfg.width // N_CHIPS, axis=2)
    grad["meta"] = lax.dynamic_slice_in_dim(dmeta_sum, chip * (cfg.d // N_CHIPS), cfg.d // N_CHIPS, axis=1)

    for n in names:
        if n in WEIGHTS:
            continue
        shp = params[n].shape
        two_d = lambda a: a.reshape(-1, shp[-1])
        dl, nm, nv = adamw(two_d(params[n]), two_d(grad[n]), two_d(mom1[n]), two_d(mom2[n]), name=f"adamw_{n}")
        delta[n], new_m[n], new_v[n] = dl.reshape(shp), nm.reshape(shp), nv.reshape(shp)

    return (loss_sum, grad_x[None], *[grad[n] for n in names], *[delta[n] for n in names], *[new_m[n] for n in names],
            *[new_v[n] for n in names])
```
